```python
import math
import jax
import jax.numpy as jnp
from jax import lax
import numpy as np

D_MODEL = 2048
BATCH = 8
SEQ = 2048
DEPTH = 4

A_HEADS = 4
A_NOPE = 128
A_ROPE = 64
A_V = 128
A_KV_RANK = 256
IDX_HEADS = 8
IDX_DIM = 64
INDEX_TOPK = 256
B_HEADS = 4
B_QK = 64
B_V = 128
RET_CHUNK = 128
C_HEADS = 4
C_DH = 64
D_HEADS = 4
D_K = 128
D_V = 128
CONV_K = 4
GDN_CHUNK = 64
N_EXPERTS = 32
TOP_K = 4
D_FF = 512
SWIGLU_ALPHA = 1.702
SWIGLU_LIMIT = 7.0
MOE_BLOCK = 256

ROPE_THETA = 10000.0
Q_BLOCK = 128
MAX_POS_OFFSET = 4096
DEEPNORM_ALPHA = (2 * DEPTH) ** 0.25
DEEPNORM_BETA = (8 * DEPTH) ** -0.25
LN_EPS = 1e-5
RMS_EPS = 1e-6

D_MIX = A_HEADS * A_V + B_HEADS * B_V + C_HEADS * 2 * C_DH + D_HEADS * D_V
IN_WIDTHS = (
    A_HEADS * (A_NOPE + A_ROPE), A_KV_RANK, A_ROPE, IDX_HEADS * IDX_DIM, IDX_DIM, IDX_HEADS,
    B_HEADS * B_QK, B_HEADS * B_QK, B_HEADS * B_V, B_HEADS * B_V,
    C_HEADS * 2 * C_DH, C_HEADS * 2 * C_DH, C_HEADS * 2 * C_DH,
    D_HEADS * (2 * D_K + D_V), D_HEADS * D_V, D_HEADS, D_HEADS,
)
IN_TOTAL = sum(IN_WIDTHS)

kernel_name = 'hybrid_dsa_ret_diff_gdn_moe_deepnorm'


def layer_norm(t, g, b):
    tf = t.astype(jnp.float32)
    mu = jnp.mean(tf, -1, keepdims=True)
    var = jnp.mean(jnp.square(tf - mu), -1, keepdims=True)
    return ((tf - mu) * lax.rsqrt(var + LN_EPS) * g + b).astype(t.dtype)


def rms_normalize(t):
    tf = t.astype(jnp.float32)
    return tf * lax.rsqrt(jnp.mean(tf * tf, -1, keepdims=True) + RMS_EPS)


def rms_norm(t, g):
    return (rms_normalize(t) * g).astype(t.dtype)


def l2_normalize(t):
    tf = t.astype(jnp.float32)
    return tf * lax.rsqrt(jnp.sum(tf * tf, -1, keepdims=True) + RMS_EPS)


def rotary(t, positions):
    half = t.shape[-1] // 2
    inv_freq = ROPE_THETA ** (-jnp.arange(half, dtype=jnp.float32) / half)
    ang = positions.astype(jnp.float32)[:, :, None, None] * inv_freq
    cos, sin = jnp.cos(ang), jnp.sin(ang)
    tf = t.astype(jnp.float32)
    t1, t2 = tf[..., :half], tf[..., half:]
    return jnp.concatenate([t1 * cos - t2 * sin, t2 * cos + t1 * sin], -1).astype(t.dtype)


def to_chunks(t, c):
    return jnp.moveaxis(t.reshape((t.shape[0], t.shape[1] // c, c) + t.shape[2:]), 2, 3)


def from_chunks(t):
    t = jnp.moveaxis(t, 3, 2)
    return t.reshape((t.shape[0], t.shape[1] * t.shape[2]) + t.shape[3:])


def to_qblocks(t):
    return jnp.moveaxis(t.reshape((t.shape[0], t.shape[1] // Q_BLOCK, Q_BLOCK) + t.shape[2:]), 1, 0)


def from_qblocks(t):
    t = jnp.moveaxis(t, 0, 1)
    return t.reshape((t.shape[0], t.shape[1] * t.shape[2]) + t.shape[3:])


def dsa_mla_mixer(a_q, a_ckv, a_kr, a_iq, a_ik, a_iw, positions, kv_norm, w_uk, w_uv):
    bsz, seq = a_q.shape[:2]
    q = a_q.reshape(bsz, seq, A_HEADS, A_NOPE + A_ROPE)
    q_rope = rotary(q[..., A_NOPE:], positions)
    q_lat = jnp.einsum('bshd,rhd->bshr', q[..., :A_NOPE], w_uk)
    q_all = jnp.concatenate([q_lat, q_rope], -1)
    c_kv = rms_norm(a_ckv, kv_norm)
    k_rope = rotary(a_kr[:, :, None, :], positions)[:, :, 0]
    kv_lat = jnp.concatenate([c_kv, k_rope], -1)
    iq = rotary(a_iq.reshape(bsz, seq, IDX_HEADS, IDX_DIM), positions)
    ik = rotary(a_ik[:, :, None, :], positions)[:, :, 0]
    iw = a_iw * (IDX_HEADS * IDX_DIM) ** -0.5
    n_keep = min(INDEX_TOPK, seq // 4)
    key_pos = jnp.arange(seq)
    scale = (A_NOPE + A_ROPE) ** -0.5

    def block(args):
        qb, iqb, iwb, qpos = args
        idx_logits = jnp.einsum('bqhd,bsd->bqhs', iqb, ik)
        index = jnp.einsum('bqhs,bqh->bqs', jax.nn.relu(idx_logits), iwb).astype(jnp.float32)
        index = jnp.where(key_pos[None, None, :] <= qpos[None, :, None], index, -jnp.inf)
        _, sel = lax.top_k(index, n_keep)
        valid = sel <= qpos[None, :, None]
        kv_sel = jax.vmap(lambda kv, ix: kv[ix])(kv_lat, sel)
        logits = jnp.einsum('bqhr,bqkr->bqhk', qb, kv_sel).astype(jnp.float32) * scale
        logits = jnp.where(valid[:, :, None, :], logits, -jnp.inf)
        p = jax.nn.softmax(logits, axis=-1).astype(kv_sel.dtype)
        return jnp.einsum('bqhk,bqkr->bqhr', p, kv_sel[..., :A_KV_RANK])

    qpos_blocks = jnp.arange(seq).reshape(seq // Q_BLOCK, Q_BLOCK)
    o_lat = from_qblocks(lax.map(block, (to_qblocks(q_all), to_qblocks(iq), to_qblocks(iw), qpos_blocks)))
    o = jnp.einsum('bshr,rhd->bshd', o_lat, w_uv)
    return o.reshape(bsz, seq, A_HEADS * A_V)


def retention_mixer(b_q, b_k, b_v, b_g, positions):
    bsz, seq = b_q.shape[:2]
    f32 = jnp.float32
    q = rotary(b_q.reshape(bsz, seq, B_HEADS, B_QK), positions).astype(f32)
    k = rotary(b_k.reshape(bsz, seq, B_HEADS, B_QK), positions).astype(f32) * B_QK ** -0.5
    v = b_v.reshape(bsz, seq, B_HEADS, B_V).astype(f32)
    log_gamma = jnp.log(1.0 - 2.0 ** (-5.0 - jnp.arange(B_HEADS, dtype=f32)))
    c = RET_CHUNK
    qc, kc, vc = to_chunks(q, c), to_chunks(k, c), to_chunks(v, c)
    i = jnp.arange(c, dtype=f32)
    rel = i[:, None] - i[None, :]
    intra_decay = jnp.where(rel >= 0, jnp.exp(log_gamma[:, None, None] * jnp.maximum(rel, 0.0)), 0.0)
    scores = jnp.einsum('bnhid,bnhjd->bnhij', qc, kc) * intra_decay
    o_intra = jnp.einsum('bnhij,bnhje->bnhie', scores, vc)
    k_to_end = kc * jnp.exp(log_gamma[:, None] * (c - 1 - i))[:, :, None]
    chunk_kv = jnp.einsum('bnhcd,bnhce->bnhde', k_to_end, vc)
    chunk_decay = jnp.exp(log_gamma * c)[None, :, None, None]

    def step(state, kv_n):
        return state * chunk_decay + kv_n, state

    _, prev = lax.scan(step, jnp.zeros((bsz, B_HEADS, B_QK, B_V), f32), jnp.moveaxis(chunk_kv, 1, 0))
    prev = jnp.moveaxis(prev, 0, 1)
    q_from_start = qc * jnp.exp(log_gamma[:, None] * (i + 1.0))[:, :, None]
    o = o_intra + jnp.einsum('bnhid,bnhde->bnhie', q_from_start, prev)
    o = rms_normalize(from_chunks(o))
    gate = jax.nn.silu(b_g.astype(f32)).reshape(bsz, seq, B_HEADS, B_V)
    return (o * gate).reshape(bsz, seq, B_HEADS * B_V).astype(b_q.dtype)


def diff_attention_mixer(c_q, c_k, c_v, positions, lam, subln_g, lambda_init):
    bsz, seq = c_q.shape[:2]
    q = rotary(c_q.reshape(bsz, seq, C_HEADS * 2, C_DH), positions).reshape(bsz, seq, C_HEADS, 2, C_DH)
    k = rotary(c_k.reshape(bsz, seq, C_HEADS * 2, C_DH), positions).reshape(bsz, seq, C_HEADS, 2, C_DH)
    v = c_v.reshape(bsz, seq, C_HEADS, 2 * C_DH)
    lf = lam.astype(jnp.float32)
    lam_full = jnp.exp(jnp.sum(lf[0] * lf[1])) - jnp.exp(jnp.sum(lf[2] * lf[3])) + lambda_init
    key_pos = jnp.arange(seq)
    scale = C_DH ** -0.5

    def block(args):
        qb, qpos = args
        logits = jnp.einsum('bqhmd,bshmd->bhmqs', qb, k).astype(jnp.float32) * scale
        logits = jnp.where(key_pos[None, :] <= qpos[:, None], logits, -jnp.inf)
        p = jax.nn.softmax(logits, axis=-1)
        diff = (p[:, :, 0] - lam_full * p[:, :, 1]).astype(v.dtype)
        return jnp.einsum('bhqs,bshe->bqhe', diff, v)

    qpos_blocks = jnp.arange(seq).reshape(seq // Q_BLOCK, Q_BLOCK)
    o = from_qblocks(lax.map(block, (to_qblocks(q), qpos_blocks)))
    o = rms_norm(o, subln_g) * (1.0 - lambda_init)
    return o.reshape(bsz, seq, C_HEADS * 2 * C_DH)


def causal_depthwise_conv(t, w):
    return lax.conv_general_dilated(t, w[:, None, :], window_strides=(1,), padding=((CONV_K - 1, 0),),
                                    dimension_numbers=('NWC', 'WIO', 'NWC'), feature_group_count=t.shape[-1])


def chunk_gated_delta_rule(q, k, v, g, beta):
    f32 = jnp.float32
    bsz = q.shape[0]
    c = GDN_CHUNK
    qc, kc, vc = (to_chunks(t.astype(f32), c) for t in (q, k, v))
    gc, bc = to_chunks(g.astype(f32), c), to_chunks(beta.astype(f32), c)
    G = jnp.cumsum(gc, axis=-1)
    idx = jnp.arange(c)
    tril = idx[:, None] >= idx[None, :]
    strict = idx[:, None] > idx[None, :]
    diffG = G[..., :, None] - G[..., None, :]
    gam = jnp.where(tril, jnp.exp(jnp.where(tril, diffG, 0.0)), 0.0)
    kb = kc * bc[..., None]
    lower = jnp.where(strict, jnp.einsum('bnhid,bnhjd->bnhij', kb, kc) * gam, 0.0)
    rhs = jnp.concatenate([vc * bc[..., None], kb * jnp.exp(G)[..., None]], -1)
    sol = lax.linalg.triangular_solve(lower + jnp.eye(c, dtype=f32), rhs, left_side=True, lower=True,
                                      unit_diagonal=True)
    u, w = sol[..., :D_V], sol[..., D_V:]
    attn = jnp.where(tril, jnp.einsum('bnhid,bnhjd->bnhij', qc, kc) * gam, 0.0)
    qg = qc * jnp.exp(G)[..., None]
    kd = kc * jnp.exp(G[..., -1:] - G)[..., None]
    decay_last = jnp.exp(G[..., -1])

    def step(state, xs):
        qg_n, w_n, u_n, a_n, kd_n, dl_n = xs
        v_new = u_n - jnp.einsum('bhcd,bhde->bhce', w_n, state)
        o_n = jnp.einsum('bhcd,bhde->bhce', qg_n, state) + jnp.einsum('bhij,bhje->bhie', a_n, v_new)
        state = state * dl_n[..., None, None] + jnp.einsum('bhcd,bhce->bhde', kd_n, v_new)
        return state, o_n

    xs = tuple(jnp.moveaxis(t, 1, 0) for t in (qg, w, u, attn, kd, decay_last))
    _, o = lax.scan(step, jnp.zeros((bsz, D_HEADS, D_K, D_V), f32), xs)
    return from_chunks(jnp.moveaxis(o, 0, 1))


def gated_deltanet_mixer(d_qkv, d_z, d_b, d_a, conv_w, a_log, dt_bias, norm_g):
    bsz, seq = d_qkv.shape[:2]
    qkv = jax.nn.silu(causal_depthwise_conv(d_qkv, conv_w))
    q, k, v = jnp.split(qkv, [D_HEADS * D_K, 2 * D_HEADS * D_K], axis=-1)
    q = l2_normalize(q.reshape(bsz, seq, D_HEADS, D_K)) * D_K ** -0.5
    k = l2_normalize(k.reshape(bsz, seq, D_HEADS, D_K))
    v = v.reshape(bsz, seq, D_HEADS, D_V)
    beta = jax.nn.sigmoid(d_b.astype(jnp.float32))
    g = -jnp.exp(a_log.astype(jnp.float32)) * jax.nn.softplus(d_a.astype(jnp.float32) + dt_bias.astype(jnp.float32))
    o = chunk_gated_delta_rule(q, k, v, g, beta)
    o = rms_norm(o, norm_g) * jax.nn.silu(d_z.reshape(bsz, seq, D_HEADS, D_V).astype(jnp.float32))
    return o.reshape(bsz, seq, D_HEADS * D_V).astype(d_qkv.dtype)


def clamped_swiglu(hg):
    glu, lin = hg[..., :D_FF], hg[..., D_FF:]
    glu = jnp.minimum(glu, SWIGLU_LIMIT)
    lin = jnp.clip(lin, -SWIGLU_LIMIT, SWIGLU_LIMIT)
    return glu * jax.nn.sigmoid(SWIGLU_ALPHA * glu) * (lin + 1.0)


def moe_ffn(h, w_router, b_router, w_gu, b_gu, w_dn, b_dn):
    bsz, seq, d = h.shape
    n_tok = bsz * seq
    hf = h.reshape(n_tok, d)
    logits = hf @ w_router + b_router
    top_logits, top_e = lax.top_k(logits, TOP_K)
    gates = jax.nn.softmax(top_logits.astype(jnp.float32), axis=-1).astype(h.dtype)
    n_assign = n_tok * TOP_K
    e_flat = top_e.reshape(n_assign)
    order = jnp.argsort(e_flat)
    e_sorted = e_flat[order]
    tok_sorted = (order // TOP_K).astype(jnp.int32)
    gate_sorted = gates.reshape(n_assign)[order]
    counts = jnp.bincount(e_flat, length=N_EXPERTS)
    padded = (counts + MOE_BLOCK - 1) // MOE_BLOCK * MOE_BLOCK
    pad_end = jnp.cumsum(padded)
    pad_start = pad_end - padded
    start = jnp.cumsum(counts) - counts
    row = pad_start[e_sorted] + jnp.arange(n_assign) - start[e_sorted]
    n_blocks = -(-n_assign // MOE_BLOCK) + N_EXPERTS
    n_rows = n_blocks * MOE_BLOCK
    row_tok = jnp.zeros((n_rows,), jnp.int32).at[row].set(tok_sorted)
    row_gate = jnp.zeros((n_rows,), h.dtype).at[row].set(gate_sorted)
    block_e = jnp.minimum(jnp.searchsorted(pad_end, jnp.arange(n_blocks) * MOE_BLOCK, side='right'), N_EXPERTS - 1)

    def expert_block(args):
        tok, gate, e = args
        act = clamped_swiglu(hf[tok] @ w_gu[e] + b_gu[e])
        return (act @ w_dn[e] + b_dn[e]) * gate[:, None]

    y_rows = lax.map(expert_block, (row_tok.reshape(n_blocks, MOE_BLOCK), row_gate.reshape(n_blocks, MOE_BLOCK), block_e))
    y = jnp.zeros_like(hf).at[row_tok].add(y_rows.reshape(n_rows, d))
    return y.reshape(bsz, seq, d)


def setup_inputs(seed: int = 0) -> dict:
    key = jax.random.key(seed)
    ks = jax.random.split(key, 24)
    f32 = jnp.float32
    L, D = DEPTH, D_MODEL

    def nrm(k, shape, scale):
        return jax.random.normal(k, shape, f32) * scale

    x = nrm(ks[0], (BATCH, SEQ, D), 1.0)
    offsets = jax.random.randint(ks[1], (BATCH, 1), 0, MAX_POS_OFFSET, dtype=jnp.int32)
    positions = (offsets + jnp.arange(SEQ, dtype=jnp.int32)[None, :]).astype(jnp.int32)
    dt = jnp.exp(jax.random.uniform(ks[11], (L, D_HEADS), f32, math.log(1e-3), math.log(1e-1)))
    return {
        'x': x,
        'positions': positions,
        'w_in': nrm(ks[2], (L, D, IN_TOTAL), D ** -0.5),
        'w_out': nrm(ks[3], (L, D_MIX, D), D_MIX ** -0.5 * DEEPNORM_BETA),
        'a_kv_norm': 1.0 + nrm(ks[4], (L, A_KV_RANK), 0.02),
        'a_w_uk': nrm(ks[5], (L, A_KV_RANK, A_HEADS, A_NOPE), A_KV_RANK ** -0.5),
        'a_w_uv': nrm(ks[6], (L, A_KV_RANK, A_HEADS, A_V), A_KV_RANK ** -0.5),
        'c_lambda': nrm(ks[7], (L, 4, C_DH), 0.1),
        'c_subln': 1.0 + nrm(ks[8], (L, 2 * C_DH), 0.02),
        'd_conv': nrm(ks[9], (L, CONV_K, D_HEADS * (2 * D_K + D_V)), CONV_K ** -0.5),
        'd_a_log': jnp.log(jax.random.uniform(ks[10], (L, D_HEADS), f32, 1.0, 16.0)),
        'd_dt_bias': dt + jnp.log(-jnp.expm1(-dt)),
        'd_norm': 1.0 + nrm(ks[12], (L, D_V), 0.02),
        'ln1_g': 1.0 + nrm(ks[13], (L, D), 0.02),
        'ln1_b': nrm(ks[14], (L, D), 0.02),
        'w_router': nrm(ks[15], (L, D, N_EXPERTS), D ** -0.5),
        'b_router': nrm(ks[16], (L, N_EXPERTS), 0.01),
        'w_gu': nrm(ks[17], (L, N_EXPERTS, D, 2 * D_FF), D ** -0.5),
        'b_gu': nrm(ks[18], (L, N_EXPERTS, 2 * D_FF), 0.01),
        'w_dn': nrm(ks[19], (L, N_EXPERTS, D_FF, D), D_FF ** -0.5 * DEEPNORM_BETA),
        'b_dn': nrm(ks[20], (L, N_EXPERTS, D), 0.01),
        'ln2_g': 1.0 + nrm(ks[21], (L, D), 0.02),
        'ln2_b': nrm(ks[22], (L, D), 0.02),
    }


def reference(x, positions, w_in, w_out, a_kv_norm, a_w_uk, a_w_uv, c_lambda, c_subln, d_conv, d_a_log,
              d_dt_bias, d_norm, ln1_g, ln1_b, w_router, b_router, w_gu, b_gu, w_dn, b_dn, ln2_g, ln2_b):
    splits = np.cumsum(IN_WIDTHS)[:-1].tolist()
    for l in range(DEPTH):
        (a_q, a_ckv, a_kr, a_iq, a_ik, a_iw, b_q, b_k, b_v, b_g, c_q, c_k, c_v,
         d_qkv, d_z, d_b, d_a) = jnp.split(x @ w_in[l], splits, axis=-1)
        lambda_init = 0.8 - 0.6 * math.exp(-0.3 * l)
        heads = jnp.concatenate([
            dsa_mla_mixer(a_q, a_ckv, a_kr, a_iq, a_ik, a_iw, positions, a_kv_norm[l], a_w_uk[l], a_w_uv[l]),
            retention_mixer(b_q, b_k, b_v, b_g, positions),
            diff_attention_mixer(c_q, c_k, c_v, positions, c_lambda[l], c_subln[l], lambda_init),
            gated_deltanet_mixer(d_qkv, d_z, d_b, d_a, d_conv[l], d_a_log[l], d_dt_bias[l], d_norm[l]),
        ], axis=-1)
        h = layer_norm(DEEPNORM_ALPHA * x + heads @ w_out[l], ln1_g[l], ln1_b[l])
        y = moe_ffn(h, w_router[l], b_router[l], w_gu[l], b_gu[l], w_dn[l], b_dn[l])
        x = layer_norm(DEEPNORM_ALPHA * h + y, ln2_g[l], ln2_b[l])
    return x
```

```python
import functools
import math

import jax
import jax.numpy as jnp
import numpy as np
from jax import lax
from jax.experimental import pallas as pl
from jax.experimental.pallas import tpu as pltpu

D_MODEL = 2048
DEPTH = 4
A_HEADS, A_NOPE, A_ROPE, A_V, A_KV_RANK = 4, 128, 64, 128, 256
IDX_HEADS, IDX_DIM, INDEX_TOPK = 8, 64, 256
B_HEADS, B_QK, B_V, RET_CHUNK = 4, 64, 128, 128
C_HEADS, C_DH = 4, 64
D_HEADS, D_K, D_V, CONV_K, GDN_CHUNK = 4, 128, 128, 4, 64
N_EXPERTS, TOP_K, D_FF = 32, 4, 512
SWIGLU_ALPHA, SWIGLU_LIMIT, MOE_BLOCK = 1.702, 7.0, 256
ROPE_THETA = 10000.0
Q_BLOCK = 128
DEEPNORM_ALPHA = (2 * DEPTH) ** 0.25
LN_EPS = 1e-5
RMS_EPS = 1e-6
D_MIX = A_HEADS * A_V + B_HEADS * B_V + C_HEADS * 2 * C_DH + D_HEADS * D_V

IN_WIDTHS = (
    A_HEADS * (A_NOPE + A_ROPE), A_KV_RANK, A_ROPE, IDX_HEADS * IDX_DIM, IDX_DIM, IDX_HEADS,
    B_HEADS * B_QK, B_HEADS * B_QK, B_HEADS * B_V, B_HEADS * B_V,
    C_HEADS * 2 * C_DH, C_HEADS * 2 * C_DH, C_HEADS * 2 * C_DH,
    D_HEADS * (2 * D_K + D_V), D_HEADS * D_V, D_HEADS, D_HEADS,
)
IN_TOTAL = sum(IN_WIDTHS)
_IN_OFFS = np.concatenate([[0], np.cumsum(IN_WIDTHS)]).tolist()

LANE = 128
VMEM_LIMIT = 56 * 1024 * 1024

_NARROW = (5, 15, 16)
_WIDE = tuple(i for i in range(len(IN_WIDTHS)) if i not in _NARROW)
_PACK_ORDER = _WIDE + _NARROW
_PACK_OFFS = {}
_o = 0
for _i in _PACK_ORDER:
    _PACK_OFFS[_i] = _o
    _o += IN_WIDTHS[_i]
IN_PACKED = -(-_o // (2 * LANE)) * (2 * LANE)


def _pack_w_in(w_in_l):
    cols = [w_in_l[:, _IN_OFFS[i]:_IN_OFFS[i + 1]] for i in _PACK_ORDER]
    cols.append(jnp.zeros((w_in_l.shape[0], IN_PACKED - IN_TOTAL), w_in_l.dtype))
    return jnp.concatenate(cols, axis=1).astype(jnp.bfloat16)


def _unpack_proj(proj):
    return [proj[..., _PACK_OFFS[i]:_PACK_OFFS[i] + IN_WIDTHS[i]] for i in range(len(IN_WIDTHS))]


def _in_proj_body(x_ref, w_ref, o_ref, xb_ref):
    @pl.when(pl.program_id(1) == 0)
    def _():
        xb_ref[...] = x_ref[...].astype(jnp.bfloat16)

    o_ref[...] = jnp.dot(xb_ref[...], w_ref[...], preferred_element_type=jnp.float32)


def _in_proj(x2d, w_packed, tm=1024, tn=768):
    m, k = x2d.shape
    n = w_packed.shape[1]
    return pl.pallas_call(
        _in_proj_body,
        grid=(m // tm, n // tn),
        in_specs=[pl.BlockSpec((tm, k), lambda i, j: (i, 0)),
                  pl.BlockSpec((k, tn), lambda i, j: (0, j))],
        out_specs=pl.BlockSpec((tm, tn), lambda i, j: (i, j)),
        out_shape=jax.ShapeDtypeStruct((m, n), jnp.float32),
        scratch_shapes=[pltpu.VMEM((tm, k), jnp.bfloat16)],
        compiler_params=pltpu.CompilerParams(
            dimension_semantics=("arbitrary", "arbitrary"), vmem_limit_bytes=VMEM_LIMIT),
        name="in_proj",
    )(x2d, w_packed)


def _ln_rows(t, g, b):
    mu = jnp.mean(t, -1, keepdims=True)
    d = t - mu
    var = jnp.mean(d * d, -1, keepdims=True)
    return d * lax.rsqrt(var + LN_EPS) * g + b


def _out_ln_body(heads_ref, w_ref, x_ref, g_ref, b_ref, o_ref):
    acc = jnp.dot(heads_ref[...].astype(jnp.bfloat16), w_ref[...], preferred_element_type=jnp.float32)
    o_ref[...] = _ln_rows(DEEPNORM_ALPHA * x_ref[...] + acc, g_ref[...], b_ref[...])


def _out_ln(heads2d, w_out_bf, x2d, g, b, tm=512):
    m, k = heads2d.shape
    n = w_out_bf.shape[1]
    return pl.pallas_call(
        _out_ln_body,
        grid=(m // tm,),
        in_specs=[pl.BlockSpec((tm, k), lambda i: (i, 0)),
                  pl.BlockSpec((k, n), lambda i: (0, 0)),
                  pl.BlockSpec((tm, n), lambda i: (i, 0)),
                  pl.BlockSpec((1, n), lambda i: (0, 0)),
                  pl.BlockSpec((1, n), lambda i: (0, 0))],
        out_specs=pl.BlockSpec((tm, n), lambda i: (i, 0)),
        out_shape=jax.ShapeDtypeStruct((m, n), jnp.float32),
        compiler_params=pltpu.CompilerParams(
            dimension_semantics=("arbitrary",), vmem_limit_bytes=VMEM_LIMIT),
        name="out_proj_ln",
    )(heads2d, w_out_bf, x2d, g.reshape(1, n), b.reshape(1, n))


def _layer_norm(t, g, b):
    mu = jnp.mean(t, -1, keepdims=True)
    var = jnp.mean(jnp.square(t - mu), -1, keepdims=True)
    return (t - mu) * lax.rsqrt(var + LN_EPS) * g + b


def _rms_normalize(t):
    return t * lax.rsqrt(jnp.mean(t * t, -1, keepdims=True) + RMS_EPS)


def _l2_normalize(t):
    return t * lax.rsqrt(jnp.sum(t * t, -1, keepdims=True) + RMS_EPS)


def _rotary(t, positions):
    half = t.shape[-1] // 2
    inv_freq = ROPE_THETA ** (-jnp.arange(half, dtype=jnp.float32) / half)
    ang = positions.astype(jnp.float32)[:, :, None, None] * inv_freq
    cos, sin = jnp.cos(ang), jnp.sin(ang)
    t1, t2 = t[..., :half], t[..., half:]
    return jnp.concatenate([t1 * cos - t2 * sin, t2 * cos + t1 * sin], -1)


def _to_chunks(t, c):
    return jnp.moveaxis(t.reshape((t.shape[0], t.shape[1] // c, c) + t.shape[2:]), 2, 3)


def _from_chunks(t):
    t = jnp.moveaxis(t, 3, 2)
    return t.reshape((t.shape[0], t.shape[1] * t.shape[2]) + t.shape[3:])


def _to_qblocks(t):
    return jnp.moveaxis(t.reshape((t.shape[0], t.shape[1] // Q_BLOCK, Q_BLOCK) + t.shape[2:]), 1, 0)


def _from_qblocks(t):
    t = jnp.moveaxis(t, 0, 1)
    return t.reshape((t.shape[0], t.shape[1] * t.shape[2]) + t.shape[3:])


def _dsa_mla_mixer(a_q, a_ckv, a_kr, a_iq, a_ik, a_iw, positions, kv_norm, w_uk, w_uv):
    bsz, seq = a_q.shape[:2]
    q = a_q.reshape(bsz, seq, A_HEADS, A_NOPE + A_ROPE)
    q_rope = _rotary(q[..., A_NOPE:], positions)
    q_lat = jnp.einsum('bshd,rhd->bshr', q[..., :A_NOPE], w_uk)
    q_all = jnp.concatenate([q_lat, q_rope], -1)
    c_kv = _rms_normalize(a_ckv) * kv_norm
    k_rope = _rotary(a_kr[:, :, None, :], positions)[:, :, 0]
    kv_lat = jnp.concatenate([c_kv, k_rope], -1)
    iq = _rotary(a_iq.reshape(bsz, seq, IDX_HEADS, IDX_DIM), positions)
    ik = _rotary(a_ik[:, :, None, :], positions)[:, :, 0]
    iw = a_iw * (IDX_HEADS * IDX_DIM) ** -0.5
    n_keep = min(INDEX_TOPK, seq // 4)
    key_pos = jnp.arange(seq)
    scale = (A_NOPE + A_ROPE) ** -0.5

    def block(args):
        qb, iqb, iwb, qpos = args
        idx_logits = jnp.einsum('bqhd,bsd->bqhs', iqb, ik)
        index = jnp.einsum('bqhs,bqh->bqs', jax.nn.relu(idx_logits), iwb).astype(jnp.float32)
        index = jnp.where(key_pos[None, None, :] <= qpos[None, :, None], index, -jnp.inf)
        _, sel = lax.top_k(index, n_keep)
        valid = sel <= qpos[None, :, None]
        kv_sel = jax.vmap(lambda kv, ix: kv[ix])(kv_lat, sel)
        logits = jnp.einsum('bqhr,bqkr->bqhk', qb, kv_sel).astype(jnp.float32) * scale
        logits = jnp.where(valid[:, :, None, :], logits, -jnp.inf)
        p = jax.nn.softmax(logits, axis=-1)
        return jnp.einsum('bqhk,bqkr->bqhr', p, kv_sel[..., :A_KV_RANK])

    qpos_blocks = jnp.arange(seq).reshape(seq // Q_BLOCK, Q_BLOCK)
    o_lat = _from_qblocks(lax.map(block, (_to_qblocks(q_all), _to_qblocks(iq), _to_qblocks(iw), qpos_blocks)))
    o = jnp.einsum('bshr,rhd->bshd', o_lat, w_uv)
    return o.reshape(bsz, seq, A_HEADS * A_V)


def _retention_mixer(b_q, b_k, b_v, b_g, positions):
    bsz, seq = b_q.shape[:2]
    f32 = jnp.float32
    q = _rotary(b_q.reshape(bsz, seq, B_HEADS, B_QK), positions)
    k = _rotary(b_k.reshape(bsz, seq, B_HEADS, B_QK), positions) * B_QK ** -0.5
    v = b_v.reshape(bsz, seq, B_HEADS, B_V)
    log_gamma = jnp.log(1.0 - 2.0 ** (-5.0 - jnp.arange(B_HEADS, dtype=f32)))
    c = RET_CHUNK
    qc, kc, vc = _to_chunks(q, c), _to_chunks(k, c), _to_chunks(v, c)
    i = jnp.arange(c, dtype=f32)
    rel = i[:, None] - i[None, :]
    intra_decay = jnp.where(rel >= 0, jnp.exp(log_gamma[:, None, None] * jnp.maximum(rel, 0.0)), 0.0)
    scores = jnp.einsum('bnhid,bnhjd->bnhij', qc, kc) * intra_decay
    o_intra = jnp.einsum('bnhij,bnhje->bnhie', scores, vc)
    k_to_end = kc * jnp.exp(log_gamma[:, None] * (c - 1 - i))[:, :, None]
    chunk_kv = jnp.einsum('bnhcd,bnhce->bnhde', k_to_end, vc)
    chunk_decay = jnp.exp(log_gamma * c)[None, :, None, None]

    def step(state, kv_n):
        return state * chunk_decay + kv_n, state

    _, prev = lax.scan(step, jnp.zeros((bsz, B_HEADS, B_QK, B_V), f32), jnp.moveaxis(chunk_kv, 1, 0))
    prev = jnp.moveaxis(prev, 0, 1)
    q_from_start = qc * jnp.exp(log_gamma[:, None] * (i + 1.0))[:, :, None]
    o = o_intra + jnp.einsum('bnhid,bnhde->bnhie', q_from_start, prev)
    o = _rms_normalize(_from_chunks(o))
    gate = jax.nn.silu(b_g).reshape(bsz, seq, B_HEADS, B_V)
    return (o * gate).reshape(bsz, seq, B_HEADS * B_V)


def _diff_attention_mixer(c_q, c_k, c_v, positions, lam, subln_g, lambda_init):
    bsz, seq = c_q.shape[:2]
    q = _rotary(c_q.reshape(bsz, seq, C_HEADS * 2, C_DH), positions).reshape(bsz, seq, C_HEADS, 2, C_DH)
    k = _rotary(c_k.reshape(bsz, seq, C_HEADS * 2, C_DH), positions).reshape(bsz, seq, C_HEADS, 2, C_DH)
    v = c_v.reshape(bsz, seq, C_HEADS, 2 * C_DH)
    lam_full = jnp.exp(jnp.sum(lam[0] * lam[1])) - jnp.exp(jnp.sum(lam[2] * lam[3])) + lambda_init
    key_pos = jnp.arange(seq)
    scale = C_DH ** -0.5

    def block(args):
        qb, qpos = args
        logits = jnp.einsum('bqhmd,bshmd->bhmqs', qb, k) * scale
        logits = jnp.where(key_pos[None, :] <= qpos[:, None], logits, -jnp.inf)
        p = jax.nn.softmax(logits, axis=-1)
        diff = p[:, :, 0] - lam_full * p[:, :, 1]
        return jnp.einsum('bhqs,bshe->bqhe', diff, v)

    qpos_blocks = jnp.arange(seq).reshape(seq // Q_BLOCK, Q_BLOCK)
    o = _from_qblocks(lax.map(block, (_to_qblocks(q), qpos_blocks)))
    o = _rms_normalize(o) * subln_g * (1.0 - lambda_init)
    return o.reshape(bsz, seq, C_HEADS * 2 * C_DH)


def _causal_depthwise_conv(t, w):
    return lax.conv_general_dilated(t, w[:, None, :], window_strides=(1,), padding=((CONV_K - 1, 0),),
                                    dimension_numbers=('NWC', 'WIO', 'NWC'), feature_group_count=t.shape[-1])


def _chunk_gated_delta_rule(q, k, v, g, beta):
    f32 = jnp.float32
    bsz = q.shape[0]
    c = GDN_CHUNK
    qc, kc, vc = (_to_chunks(t, c) for t in (q, k, v))
    gc, bc = _to_chunks(g, c), _to_chunks(beta, c)
    G = jnp.cumsum(gc, axis=-1)
    idx = jnp.arange(c)
    tril = idx[:, None] >= idx[None, :]
    strict = idx[:, None] > idx[None, :]
    diffG = G[..., :, None] - G[..., None, :]
    gam = jnp.where(tril, jnp.exp(jnp.where(tril, diffG, 0.0)), 0.0)
    kb = kc * bc[..., None]
    lower = jnp.where(strict, jnp.einsum('bnhid,bnhjd->bnhij', kb, kc) * gam, 0.0)
    rhs = jnp.concatenate([vc * bc[..., None], kb * jnp.exp(G)[..., None]], -1)
    sol = lax.linalg.triangular_solve(lower + jnp.eye(c, dtype=f32), rhs, left_side=True, lower=True,
                                      unit_diagonal=True)
    u, w = sol[..., :D_V], sol[..., D_V:]
    attn = jnp.where(tril, jnp.einsum('bnhid,bnhjd->bnhij', qc, kc) * gam, 0.0)
    qg = qc * jnp.exp(G)[..., None]
    kd = kc * jnp.exp(G[..., -1:] - G)[..., None]
    decay_last = jnp.exp(G[..., -1])

    def step(state, xs):
        qg_n, w_n, u_n, a_n, kd_n, dl_n = xs
        v_new = u_n - jnp.einsum('bhcd,bhde->bhce', w_n, state)
        o_n = jnp.einsum('bhcd,bhde->bhce', qg_n, state) + jnp.einsum('bhij,bhje->bhie', a_n, v_new)
        state = state * dl_n[..., None, None] + jnp.einsum('bhcd,bhce->bhde', kd_n, v_new)
        return state, o_n

    xs = tuple(jnp.moveaxis(t, 1, 0) for t in (qg, w, u, attn, kd, decay_last))
    _, o = lax.scan(step, jnp.zeros((bsz, D_HEADS, D_K, D_V), f32), xs)
    return _from_chunks(jnp.moveaxis(o, 0, 1))


def _gated_deltanet_mixer(d_qkv, d_z, d_b, d_a, conv_w, a_log, dt_bias, norm_g):
    bsz, seq = d_qkv.shape[:2]
    qkv = jax.nn.silu(_causal_depthwise_conv(d_qkv, conv_w))
    q, k, v = jnp.split(qkv, [D_HEADS * D_K, 2 * D_HEADS * D_K], axis=-1)
    q = _l2_normalize(q.reshape(bsz, seq, D_HEADS, D_K)) * D_K ** -0.5
    k = _l2_normalize(k.reshape(bsz, seq, D_HEADS, D_K))
    v = v.reshape(bsz, seq, D_HEADS, D_V)
    beta = jax.nn.sigmoid(d_b)
    g = -jnp.exp(a_log) * jax.nn.softplus(d_a + dt_bias)
    o = _chunk_gated_delta_rule(q, k, v, g, beta)
    o = _rms_normalize(o) * norm_g * jax.nn.silu(d_z.reshape(bsz, seq, D_HEADS, D_V))
    return o.reshape(bsz, seq, D_HEADS * D_V)


def _clamped_swiglu(hg):
    glu, lin = hg[..., :D_FF], hg[..., D_FF:]
    glu = jnp.minimum(glu, SWIGLU_LIMIT)
    lin = jnp.clip(lin, -SWIGLU_LIMIT, SWIGLU_LIMIT)
    return glu * jax.nn.sigmoid(SWIGLU_ALPHA * glu) * (lin + 1.0)


def _moe_ffn(hf, w_router, b_router, w_gu, b_gu, w_dn, b_dn):
    n_tok, d = hf.shape
    logits = hf @ w_router + b_router
    top_logits, top_e = lax.top_k(logits, TOP_K)
    gates = jax.nn.softmax(top_logits, axis=-1)
    n_assign = n_tok * TOP_K
    e_flat = top_e.reshape(n_assign)
    order = jnp.argsort(e_flat)
    e_sorted = e_flat[order]
    tok_sorted = (order // TOP_K).astype(jnp.int32)
    gate_sorted = gates.reshape(n_assign)[order]
    counts = jnp.bincount(e_flat, length=N_EXPERTS)
    padded = (counts + MOE_BLOCK - 1) // MOE_BLOCK * MOE_BLOCK
    pad_end = jnp.cumsum(padded)
    pad_start = pad_end - padded
    start = jnp.cumsum(counts) - counts
    row = pad_start[e_sorted] + jnp.arange(n_assign) - start[e_sorted]
    n_blocks = -(-n_assign // MOE_BLOCK) + N_EXPERTS
    n_rows = n_blocks * MOE_BLOCK
    row_tok = jnp.zeros((n_rows,), jnp.int32).at[row].set(tok_sorted)
    row_gate = jnp.zeros((n_rows,), hf.dtype).at[row].set(gate_sorted)
    block_e = jnp.minimum(jnp.searchsorted(pad_end, jnp.arange(n_blocks) * MOE_BLOCK, side='right'), N_EXPERTS - 1)

    def expert_block(args):
        tok, gate, e = args
        act = _clamped_swiglu(hf[tok] @ w_gu[e] + b_gu[e])
        return (act @ w_dn[e] + b_dn[e]) * gate[:, None]

    y_rows = lax.map(expert_block, (row_tok.reshape(n_blocks, MOE_BLOCK), row_gate.reshape(n_blocks, MOE_BLOCK), block_e))
    return jnp.zeros_like(hf).at[row_tok].add(y_rows.reshape(n_rows, d))


def kernel(x, positions, w_in, w_out, a_kv_norm, a_w_uk, a_w_uv, c_lambda, c_subln, d_conv, d_a_log, d_dt_bias, d_norm, ln1_g, ln1_b, w_router, b_router, w_gu, b_gu, w_dn, b_dn, ln2_g, ln2_b):
    bsz, seq, d = x.shape
    x2d = x.reshape(bsz * seq, d)
    for l in range(DEPTH):
        proj = _in_proj(x2d, _pack_w_in(w_in[l])).reshape(bsz, seq, IN_PACKED)
        (a_q, a_ckv, a_kr, a_iq, a_ik, a_iw, b_q, b_k, b_v, b_g, c_q, c_k, c_v,
         d_qkv, d_z, d_b, d_a) = _unpack_proj(proj)
        lambda_init = 0.8 - 0.6 * math.exp(-0.3 * l)
        heads = jnp.concatenate([
            _dsa_mla_mixer(a_q, a_ckv, a_kr, a_iq, a_ik, a_iw, positions, a_kv_norm[l], a_w_uk[l], a_w_uv[l]),
            _retention_mixer(b_q, b_k, b_v, b_g, positions),
            _diff_attention_mixer(c_q, c_k, c_v, positions, c_lambda[l], c_subln[l], lambda_init),
            _gated_deltanet_mixer(d_qkv, d_z, d_b, d_a, d_conv[l], d_a_log[l], d_dt_bias[l], d_norm[l]),
        ], axis=-1)
        h2d = _out_ln(heads.reshape(bsz * seq, D_MIX), w_out[l].astype(jnp.bfloat16), x2d, ln1_g[l], ln1_b[l])
        y = _moe_ffn(h2d, w_router[l], b_router[l], w_gu[l], b_gu[l], w_dn[l], b_dn[l])
        x2d = _layer_norm(DEEPNORM_ALPHA * h2d + y, ln2_g[l], ln2_b[l])
    return x2d.reshape(bsz, seq, d)
```

```python
import functools
import math

import jax
import jax.numpy as jnp
import numpy as np
from jax import lax
from jax.experimental import pallas as pl
from jax.experimental.pallas import tpu as pltpu

D_MODEL = 2048
DEPTH = 4
A_HEADS, A_NOPE, A_ROPE, A_V, A_KV_RANK = 4, 128, 64, 128, 256
IDX_HEADS, IDX_DIM, INDEX_TOPK = 8, 64, 256
B_HEADS, B_QK, B_V, RET_CHUNK = 4, 64, 128, 128
C_HEADS, C_DH = 4, 64
D_HEADS, D_K, D_V, CONV_K, GDN_CHUNK = 4, 128, 128, 4, 64
N_EXPERTS, TOP_K, D_FF = 32, 4, 512
SWIGLU_ALPHA, SWIGLU_LIMIT, MOE_BLOCK = 1.702, 7.0, 256
ROPE_THETA = 10000.0
Q_BLOCK = 128
DEEPNORM_ALPHA = (2 * DEPTH) ** 0.25
LN_EPS = 1e-5
RMS_EPS = 1e-6
D_MIX = A_HEADS * A_V + B_HEADS * B_V + C_HEADS * 2 * C_DH + D_HEADS * D_V

IN_WIDTHS = (
    A_HEADS * (A_NOPE + A_ROPE), A_KV_RANK, A_ROPE, IDX_HEADS * IDX_DIM, IDX_DIM, IDX_HEADS,
    B_HEADS * B_QK, B_HEADS * B_QK, B_HEADS * B_V, B_HEADS * B_V,
    C_HEADS * 2 * C_DH, C_HEADS * 2 * C_DH, C_HEADS * 2 * C_DH,
    D_HEADS * (2 * D_K + D_V), D_HEADS * D_V, D_HEADS, D_HEADS,
)
IN_TOTAL = sum(IN_WIDTHS)
_IN_OFFS = np.concatenate([[0], np.cumsum(IN_WIDTHS)]).tolist()

LANE = 128
VMEM_LIMIT = 56 * 1024 * 1024

_NARROW = (5, 15, 16)
_WIDE = tuple(i for i in range(len(IN_WIDTHS)) if i not in _NARROW)
_PACK_ORDER = _WIDE + _NARROW
_PACK_OFFS = {}
_o = 0
for _i in _PACK_ORDER:
    _PACK_OFFS[_i] = _o
    _o += IN_WIDTHS[_i]
IN_PACKED = -(-_o // (2 * LANE)) * (2 * LANE)


def _pack_w_in(w_in_l):
    cols = [w_in_l[:, _IN_OFFS[i]:_IN_OFFS[i + 1]] for i in _PACK_ORDER]
    cols.append(jnp.zeros((w_in_l.shape[0], IN_PACKED - IN_TOTAL), w_in_l.dtype))
    return jnp.concatenate(cols, axis=1).astype(jnp.bfloat16)


def _unpack_proj(proj):
    return [proj[..., _PACK_OFFS[i]:_PACK_OFFS[i] + IN_WIDTHS[i]] for i in range(len(IN_WIDTHS))]


def _in_proj_body(x_ref, w_ref, o_ref, xb_ref):
    @pl.when(pl.program_id(1) == 0)
    def _():
        xb_ref[...] = x_ref[...].astype(jnp.bfloat16)

    o_ref[...] = jnp.dot(xb_ref[...], w_ref[...], preferred_element_type=jnp.float32)


def _in_proj(x2d, w_packed, tm=1024, tn=768):
    m, k = x2d.shape
    n = w_packed.shape[1]
    return pl.pallas_call(
        _in_proj_body,
        grid=(m // tm, n // tn),
        in_specs=[pl.BlockSpec((tm, k), lambda i, j: (i, 0)),
                  pl.BlockSpec((k, tn), lambda i, j: (0, j))],
        out_specs=pl.BlockSpec((tm, tn), lambda i, j: (i, j)),
        out_shape=jax.ShapeDtypeStruct((m, n), jnp.float32),
        scratch_shapes=[pltpu.VMEM((tm, k), jnp.bfloat16)],
        compiler_params=pltpu.CompilerParams(
            dimension_semantics=("arbitrary", "arbitrary"), vmem_limit_bytes=VMEM_LIMIT),
        name="in_proj",
    )(x2d, w_packed)


def _ln_rows(t, g, b):
    mu = jnp.mean(t, -1, keepdims=True)
    d = t - mu
    var = jnp.mean(d * d, -1, keepdims=True)
    return d * lax.rsqrt(var + LN_EPS) * g + b


def _out_ln_body(heads_ref, w_ref, x_ref, g_ref, b_ref, o_ref):
    acc = jnp.dot(heads_ref[...].astype(jnp.bfloat16), w_ref[...], preferred_element_type=jnp.float32)
    o_ref[...] = _ln_rows(DEEPNORM_ALPHA * x_ref[...] + acc, g_ref[...], b_ref[...])


def _out_ln(heads2d, w_out_bf, x2d, g, b, tm=512):
    m, k = heads2d.shape
    n = w_out_bf.shape[1]
    return pl.pallas_call(
        _out_ln_body,
        grid=(m // tm,),
        in_specs=[pl.BlockSpec((tm, k), lambda i: (i, 0)),
                  pl.BlockSpec((k, n), lambda i: (0, 0)),
                  pl.BlockSpec((tm, n), lambda i: (i, 0)),
                  pl.BlockSpec((1, n), lambda i: (0, 0)),
                  pl.BlockSpec((1, n), lambda i: (0, 0))],
        out_specs=pl.BlockSpec((tm, n), lambda i: (i, 0)),
        out_shape=jax.ShapeDtypeStruct((m, n), jnp.float32),
        compiler_params=pltpu.CompilerParams(
            dimension_semantics=("arbitrary",), vmem_limit_bytes=VMEM_LIMIT),
        name="out_proj_ln",
    )(heads2d, w_out_bf, x2d, g.reshape(1, n), b.reshape(1, n))


def _layer_norm(t, g, b):
    mu = jnp.mean(t, -1, keepdims=True)
    var = jnp.mean(jnp.square(t - mu), -1, keepdims=True)
    return (t - mu) * lax.rsqrt(var + LN_EPS) * g + b


def _rms_normalize(t):
    return t * lax.rsqrt(jnp.mean(t * t, -1, keepdims=True) + RMS_EPS)


def _l2_normalize(t):
    return t * lax.rsqrt(jnp.sum(t * t, -1, keepdims=True) + RMS_EPS)


def _rotary(t, positions):
    half = t.shape[-1] // 2
    inv_freq = ROPE_THETA ** (-jnp.arange(half, dtype=jnp.float32) / half)
    ang = positions.astype(jnp.float32)[:, :, None, None] * inv_freq
    cos, sin = jnp.cos(ang), jnp.sin(ang)
    t1, t2 = t[..., :half], t[..., half:]
    return jnp.concatenate([t1 * cos - t2 * sin, t2 * cos + t1 * sin], -1)


def _to_chunks(t, c):
    return jnp.moveaxis(t.reshape((t.shape[0], t.shape[1] // c, c) + t.shape[2:]), 2, 3)


def _from_chunks(t):
    t = jnp.moveaxis(t, 3, 2)
    return t.reshape((t.shape[0], t.shape[1] * t.shape[2]) + t.shape[3:])


def _to_qblocks(t):
    return jnp.moveaxis(t.reshape((t.shape[0], t.shape[1] // Q_BLOCK, Q_BLOCK) + t.shape[2:]), 1, 0)


def _from_qblocks(t):
    t = jnp.moveaxis(t, 0, 1)
    return t.reshape((t.shape[0], t.shape[1] * t.shape[2]) + t.shape[3:])


def _dsa_mla_mixer(a_q, a_ckv, a_kr, a_iq, a_ik, a_iw, positions, kv_norm, w_uk, w_uv):
    bsz, seq = a_q.shape[:2]
    q = a_q.reshape(bsz, seq, A_HEADS, A_NOPE + A_ROPE)
    q_rope = _rotary(q[..., A_NOPE:], positions)
    q_lat = jnp.einsum('bshd,rhd->bshr', q[..., :A_NOPE], w_uk)
    q_all = jnp.concatenate([q_lat, q_rope], -1)
    c_kv = _rms_normalize(a_ckv) * kv_norm
    k_rope = _rotary(a_kr[:, :, None, :], positions)[:, :, 0]
    kv_lat = jnp.concatenate([c_kv, k_rope], -1)
    iq = _rotary(a_iq.reshape(bsz, seq, IDX_HEADS, IDX_DIM), positions)
    ik = _rotary(a_ik[:, :, None, :], positions)[:, :, 0]
    iw = a_iw * (IDX_HEADS * IDX_DIM) ** -0.5
    n_keep = min(INDEX_TOPK, seq // 4)
    nq, nc = seq // Q_BLOCK, seq // A_KCHUNK
    cdt = MXU_DTYPE
    qT = q_all.reshape(bsz, nq, Q_BLOCK, A_HEADS, A_LAT).transpose(0, 1, 4, 3, 2)
    qT = qT.reshape(bsz, nq, A_LAT, A_HEADS * Q_BLOCK).astype(cdt)
    iqT = iq.reshape(bsz, nq, Q_BLOCK, IDX_HEADS, IDX_DIM).transpose(0, 1, 4, 3, 2)
    iqT = iqT.reshape(bsz, nq, IDX_DIM, IDX_HEADS * Q_BLOCK).astype(cdt)
    iwT = iw.reshape(bsz, nq, Q_BLOCK, IDX_HEADS).transpose(0, 1, 3, 2).reshape(bsz * nq, 1, IDX_HEADS * Q_BLOCK)
    ikc = ik.reshape(bsz, nc, A_KCHUNK, IDX_DIM).astype(cdt)
    kvc = kv_lat.reshape(bsz, nc, A_KCHUNK, A_LAT).astype(cdt)
    ckvT = c_kv.reshape(bsz, nc, A_KCHUNK, A_KV_RANK).transpose(0, 1, 3, 2).astype(cdt)
    wuv = w_uv.transpose(1, 0, 2).astype(cdt)
    return _dsa_attention(iqT, iwT, qT, ikc, kvc, ckvT, wuv, n_keep=n_keep)


A_LAT = A_KV_RANK + A_ROPE
A_KCHUNK = 256
MXU_DTYPE = jnp.bfloat16
_SIGN_BIT = np.int32(-2 ** 31)
_LOW31 = np.int32(2 ** 31 - 1)
_KEY_NEG_INF = np.int32(np.array(-np.inf, np.float32).view(np.int32) ^ _LOW31)
_MASKED = -1e30


def _dsa_body(iqT_ref, iw_ref, qT_ref, ik_ref, kv_ref, ckvT_ref, wuv_ref, o_ref,
              key_ref, acc_ref, thr_ref, need_ref, *, n_keep):
    f32 = jnp.float32
    qi = pl.program_id(1)
    nk = (qi * Q_BLOCK + Q_BLOCK + A_KCHUNK - 1) // A_KCHUNK
    iqT = iqT_ref[0, 0]
    iw = iw_ref[0]
    qpos = qi * Q_BLOCK + lax.broadcasted_iota(jnp.int32, (A_KCHUNK, Q_BLOCK), 1)
    krow = lax.broadcasted_iota(jnp.int32, (A_KCHUNK, Q_BLOCK), 0)

    def index_chunk(c, carry):
        lg = jnp.dot(ik_ref[0, c], iqT, preferred_element_type=f32)
        w = jnp.maximum(lg, 0.0) * iw
        idx = w[:, :Q_BLOCK]
        for h in range(1, IDX_HEADS):
            idx = idx + w[:, h * Q_BLOCK:(h + 1) * Q_BLOCK]
        idx = jnp.where(idx == 0.0, 0.0, idx)
        idx = jnp.where(krow + c * A_KCHUNK <= qpos, idx, -jnp.inf)
        bits = pltpu.bitcast(idx, jnp.int32)
        key_ref[c] = bits ^ ((bits >> 31) & _LOW31)
        return carry

    lax.fori_loop(0, nk, index_chunk, 0)

    def count(cmp, t):
        def body(c, cnt):
            hit = cmp(key_ref[c], t).astype(jnp.int32)
            return cnt + hit.reshape(A_KCHUNK // 8, 8, Q_BLOCK).sum(0)
        cnt = lax.fori_loop(0, nk, body, jnp.zeros((8, Q_BLOCK), jnp.int32))
        return cnt.sum(0, keepdims=True)

    @pl.when(qi * Q_BLOCK + Q_BLOCK <= n_keep)
    def _():
        thr_ref[...] = jnp.full((1, Q_BLOCK), _KEY_NEG_INF, jnp.int32)
        need_ref[...] = jnp.zeros((1, Q_BLOCK), jnp.int32)

    @pl.when(qi * Q_BLOCK + Q_BLOCK > n_keep)
    def _():
        def bit_step(i, prefix):
            cand = prefix | jnp.left_shift(jnp.int32(1), 31 - i)
            cnt = count(lambda k, t: k >= t, cand ^ _SIGN_BIT)
            return jnp.where(cnt >= n_keep, cand, prefix)
        prefix = lax.fori_loop(0, 32, bit_step, jnp.zeros((1, Q_BLOCK), jnp.int32))
        thr = prefix ^ _SIGN_BIT
        thr_ref[...] = thr
        need_ref[...] = n_keep - count(lambda k, t: k > t, thr)

    thr = thr_ref[...]
    need = need_ref[...].astype(f32)
    qT = qT_ref[0, 0]
    scale = (A_NOPE + A_ROPE) ** -0.5
    r_i = lax.broadcasted_iota(jnp.int32, (A_KCHUNK, A_KCHUNK), 0)
    c_i = lax.broadcasted_iota(jnp.int32, (A_KCHUNK, A_KCHUNK), 1)
    before = (c_i < r_i).astype(jnp.bfloat16)
    acc_ref[...] = jnp.zeros_like(acc_ref)

    def attend_chunk(c, carry):
        m, l, ties_seen = carry
        s = jnp.dot(kv_ref[0, c], qT, preferred_element_type=f32) * scale
        key = key_ref[c]
        tie = key == thr
        tie_f = tie.astype(f32)
        rank = jnp.dot(before, tie_f.astype(jnp.bfloat16), preferred_element_type=f32) + ties_seen
        keep = (key > thr) | (tie & (rank < need))
        bias = jnp.where(keep, 0.0, _MASKED)
        s = s + jnp.concatenate([bias] * A_HEADS, axis=1)
        m_new = jnp.maximum(m, s.max(0, keepdims=True))
        alpha = jnp.exp(m - m_new)
        p = jnp.exp(s - m_new)
        l = alpha * l + p.sum(0, keepdims=True)
        pv = jnp.dot(ckvT_ref[0, c], p.astype(ckvT_ref.dtype), preferred_element_type=f32)
        acc_ref[...] = acc_ref[...] * alpha + pv
        return m_new, l, ties_seen + tie_f.sum(0, keepdims=True)

    lanes = A_HEADS * Q_BLOCK
    init = (jnp.full((1, lanes), _MASKED, f32), jnp.zeros((1, lanes), f32), jnp.zeros((1, Q_BLOCK), f32))
    _, l, _ = lax.fori_loop(0, nk, attend_chunk, init)
    o_latT = acc_ref[...] / l
    for h in range(A_HEADS):
        o_lat = o_latT[:, h * Q_BLOCK:(h + 1) * Q_BLOCK].T.astype(wuv_ref.dtype)
        o_ref[0, :, h * A_V:(h + 1) * A_V] = jnp.dot(o_lat, wuv_ref[h], preferred_element_type=f32)


def _dsa_attention(iqT, iwT, qT, ikc, kvc, ckvT, wuv, *, n_keep):
    bsz, nq = qT.shape[:2]
    nc = ikc.shape[1]
    seq = nq * Q_BLOCK
    per_q = lambda b, q: (b, q, 0, 0)
    per_b = lambda b, q: (b, 0, 0, 0)
    return pl.pallas_call(
        functools.partial(_dsa_body, n_keep=n_keep),
        grid=(bsz, nq),
        in_specs=[pl.BlockSpec((1, 1) + iqT.shape[2:], per_q),
                  pl.BlockSpec((1, 1, iwT.shape[2]), lambda b, q: (b * nq + q, 0, 0)),
                  pl.BlockSpec((1, 1) + qT.shape[2:], per_q),
                  pl.BlockSpec((1,) + ikc.shape[1:], per_b),
                  pl.BlockSpec((1,) + kvc.shape[1:], per_b),
                  pl.BlockSpec((1,) + ckvT.shape[1:], per_b),
                  pl.BlockSpec(wuv.shape, lambda b, q: (0, 0, 0))],
        out_specs=pl.BlockSpec((1, Q_BLOCK, A_HEADS * A_V), lambda b, q: (b, q, 0)),
        out_shape=jax.ShapeDtypeStruct((bsz, seq, A_HEADS * A_V), jnp.float32),
        scratch_shapes=[pltpu.VMEM((nc, A_KCHUNK, Q_BLOCK), jnp.int32),
                        pltpu.VMEM((A_KV_RANK, A_HEADS * Q_BLOCK), jnp.float32),
                        pltpu.VMEM((1, Q_BLOCK), jnp.int32),
                        pltpu.VMEM((1, Q_BLOCK), jnp.int32)],
        compiler_params=pltpu.CompilerParams(
            dimension_semantics=("arbitrary", "arbitrary"), vmem_limit_bytes=VMEM_LIMIT),
        name="dsa_attention",
    )(iqT, iwT, qT, ikc, kvc, ckvT, wuv)


def _retention_mixer(b_q, b_k, b_v, b_g, positions):
    bsz, seq = b_q.shape[:2]
    f32 = jnp.float32
    q = _rotary(b_q.reshape(bsz, seq, B_HEADS, B_QK), positions)
    k = _rotary(b_k.reshape(bsz, seq, B_HEADS, B_QK), positions) * B_QK ** -0.5
    v = b_v.reshape(bsz, seq, B_HEADS, B_V)
    log_gamma = jnp.log(1.0 - 2.0 ** (-5.0 - jnp.arange(B_HEADS, dtype=f32)))
    c = RET_CHUNK
    qc, kc, vc = _to_chunks(q, c), _to_chunks(k, c), _to_chunks(v, c)
    i = jnp.arange(c, dtype=f32)
    rel = i[:, None] - i[None, :]
    intra_decay = jnp.where(rel >= 0, jnp.exp(log_gamma[:, None, None] * jnp.maximum(rel, 0.0)), 0.0)
    scores = jnp.einsum('bnhid,bnhjd->bnhij', qc, kc) * intra_decay
    o_intra = jnp.einsum('bnhij,bnhje->bnhie', scores, vc)
    k_to_end = kc * jnp.exp(log_gamma[:, None] * (c - 1 - i))[:, :, None]
    chunk_kv = jnp.einsum('bnhcd,bnhce->bnhde', k_to_end, vc)
    chunk_decay = jnp.exp(log_gamma * c)[None, :, None, None]

    def step(state, kv_n):
        return state * chunk_decay + kv_n, state

    _, prev = lax.scan(step, jnp.zeros((bsz, B_HEADS, B_QK, B_V), f32), jnp.moveaxis(chunk_kv, 1, 0))
    prev = jnp.moveaxis(prev, 0, 1)
    q_from_start = qc * jnp.exp(log_gamma[:, None] * (i + 1.0))[:, :, None]
    o = o_intra + jnp.einsum('bnhid,bnhde->bnhie', q_from_start, prev)
    o = _rms_normalize(_from_chunks(o))
    gate = jax.nn.silu(b_g).reshape(bsz, seq, B_HEADS, B_V)
    return (o * gate).reshape(bsz, seq, B_HEADS * B_V)


def _diff_attention_mixer(c_q, c_k, c_v, positions, lam, subln_g, lambda_init):
    bsz, seq = c_q.shape[:2]
    q = _rotary(c_q.reshape(bsz, seq, C_HEADS * 2, C_DH), positions).reshape(bsz, seq, C_HEADS, 2, C_DH)
    k = _rotary(c_k.reshape(bsz, seq, C_HEADS * 2, C_DH), positions).reshape(bsz, seq, C_HEADS, 2, C_DH)
    v = c_v.reshape(bsz, seq, C_HEADS, 2 * C_DH)
    lam_full = jnp.exp(jnp.sum(lam[0] * lam[1])) - jnp.exp(jnp.sum(lam[2] * lam[3])) + lambda_init
    key_pos = jnp.arange(seq)
    scale = C_DH ** -0.5

    def block(args):
        qb, qpos = args
        logits = jnp.einsum('bqhmd,bshmd->bhmqs', qb, k) * scale
        logits = jnp.where(key_pos[None, :] <= qpos[:, None], logits, -jnp.inf)
        p = jax.nn.softmax(logits, axis=-1)
        diff = p[:, :, 0] - lam_full * p[:, :, 1]
        return jnp.einsum('bhqs,bshe->bqhe', diff, v)

    qpos_blocks = jnp.arange(seq).reshape(seq // Q_BLOCK, Q_BLOCK)
    o = _from_qblocks(lax.map(block, (_to_qblocks(q), qpos_blocks)))
    o = _rms_normalize(o) * subln_g * (1.0 - lambda_init)
    return o.reshape(bsz, seq, C_HEADS * 2 * C_DH)


def _causal_depthwise_conv(t, w):
    return lax.conv_general_dilated(t, w[:, None, :], window_strides=(1,), padding=((CONV_K - 1, 0),),
                                    dimension_numbers=('NWC', 'WIO', 'NWC'), feature_group_count=t.shape[-1])


def _chunk_gated_delta_rule(q, k, v, g, beta):
    f32 = jnp.float32
    bsz = q.shape[0]
    c = GDN_CHUNK
    qc, kc, vc = (_to_chunks(t, c) for t in (q, k, v))
    gc, bc = _to_chunks(g, c), _to_chunks(beta, c)
    G = jnp.cumsum(gc, axis=-1)
    idx = jnp.arange(c)
    tril = idx[:, None] >= idx[None, :]
    strict = idx[:, None] > idx[None, :]
    diffG = G[..., :, None] - G[..., None, :]
    gam = jnp.where(tril, jnp.exp(jnp.where(tril, diffG, 0.0)), 0.0)
    kb = kc * bc[..., None]
    lower = jnp.where(strict, jnp.einsum('bnhid,bnhjd->bnhij', kb, kc) * gam, 0.0)
    rhs = jnp.concatenate([vc * bc[..., None], kb * jnp.exp(G)[..., None]], -1)
    sol = lax.linalg.triangular_solve(lower + jnp.eye(c, dtype=f32), rhs, left_side=True, lower=True,
                                      unit_diagonal=True)
    u, w = sol[..., :D_V], sol[..., D_V:]
    attn = jnp.where(tril, jnp.einsum('bnhid,bnhjd->bnhij', qc, kc) * gam, 0.0)
    qg = qc * jnp.exp(G)[..., None]
    kd = kc * jnp.exp(G[..., -1:] - G)[..., None]
    decay_last = jnp.exp(G[..., -1])

    def step(state, xs):
        qg_n, w_n, u_n, a_n, kd_n, dl_n = xs
        v_new = u_n - jnp.einsum('bhcd,bhde->bhce', w_n, state)
        o_n = jnp.einsum('bhcd,bhde->bhce', qg_n, state) + jnp.einsum('bhij,bhje->bhie', a_n, v_new)
        state = state * dl_n[..., None, None] + jnp.einsum('bhcd,bhce->bhde', kd_n, v_new)
        return state, o_n

    xs = tuple(jnp.moveaxis(t, 1, 0) for t in (qg, w, u, attn, kd, decay_last))
    _, o = lax.scan(step, jnp.zeros((bsz, D_HEADS, D_K, D_V), f32), xs)
    return _from_chunks(jnp.moveaxis(o, 0, 1))


def _gated_deltanet_mixer(d_qkv, d_z, d_b, d_a, conv_w, a_log, dt_bias, norm_g):
    bsz, seq = d_qkv.shape[:2]
    qkv = jax.nn.silu(_causal_depthwise_conv(d_qkv, conv_w))
    q, k, v = jnp.split(qkv, [D_HEADS * D_K, 2 * D_HEADS * D_K], axis=-1)
    q = _l2_normalize(q.reshape(bsz, seq, D_HEADS, D_K)) * D_K ** -0.5
    k = _l2_normalize(k.reshape(bsz, seq, D_HEADS, D_K))
    v = v.reshape(bsz, seq, D_HEADS, D_V)
    beta = jax.nn.sigmoid(d_b)
    g = -jnp.exp(a_log) * jax.nn.softplus(d_a + dt_bias)
    o = _chunk_gated_delta_rule(q, k, v, g, beta)
    o = _rms_normalize(o) * norm_g * jax.nn.silu(d_z.reshape(bsz, seq, D_HEADS, D_V))
    return o.reshape(bsz, seq, D_HEADS * D_V)


def _clamped_swiglu(hg):
    glu, lin = hg[..., :D_FF], hg[..., D_FF:]
    glu = jnp.minimum(glu, SWIGLU_LIMIT)
    lin = jnp.clip(lin, -SWIGLU_LIMIT, SWIGLU_LIMIT)
    return glu * jax.nn.sigmoid(SWIGLU_ALPHA * glu) * (lin + 1.0)


def _moe_ffn(hf, w_router, b_router, w_gu, b_gu, w_dn, b_dn):
    n_tok, d = hf.shape
    logits = hf @ w_router + b_router
    top_logits, top_e = lax.top_k(logits, TOP_K)
    gates = jax.nn.softmax(top_logits, axis=-1)
    n_assign = n_tok * TOP_K
    e_flat = top_e.reshape(n_assign)
    order = jnp.argsort(e_flat)
    e_sorted = e_flat[order]
    tok_sorted = (order // TOP_K).astype(jnp.int32)
    gate_sorted = gates.reshape(n_assign)[order]
    counts = jnp.bincount(e_flat, length=N_EXPERTS)
    padded = (counts + MOE_BLOCK - 1) // MOE_BLOCK * MOE_BLOCK
    pad_end = jnp.cumsum(padded)
    pad_start = pad_end - padded
    start = jnp.cumsum(counts) - counts
    row = pad_start[e_sorted] + jnp.arange(n_assign) - start[e_sorted]
    n_blocks = -(-n_assign // MOE_BLOCK) + N_EXPERTS
    n_rows = n_blocks * MOE_BLOCK
    row_tok = jnp.zeros((n_rows,), jnp.int32).at[row].set(tok_sorted)
    row_gate = jnp.zeros((n_rows,), hf.dtype).at[row].set(gate_sorted)
    block_e = jnp.minimum(jnp.searchsorted(pad_end, jnp.arange(n_blocks) * MOE_BLOCK, side='right'), N_EXPERTS - 1)

    def expert_block(args):
        tok, gate, e = args
        act = _clamped_swiglu(hf[tok] @ w_gu[e] + b_gu[e])
        return (act @ w_dn[e] + b_dn[e]) * gate[:, None]

    y_rows = lax.map(expert_block, (row_tok.reshape(n_blocks, MOE_BLOCK), row_gate.reshape(n_blocks, MOE_BLOCK), block_e))
    return jnp.zeros_like(hf).at[row_tok].add(y_rows.reshape(n_rows, d))


def kernel(x, positions, w_in, w_out, a_kv_norm, a_w_uk, a_w_uv, c_lambda, c_subln, d_conv, d_a_log, d_dt_bias, d_norm, ln1_g, ln1_b, w_router, b_router, w_gu, b_gu, w_dn, b_dn, ln2_g, ln2_b):
    bsz, seq, d = x.shape
    x2d = x.reshape(bsz * seq, d)
    for l in range(DEPTH):
        proj = _in_proj(x2d, _pack_w_in(w_in[l])).reshape(bsz, seq, IN_PACKED)
        (a_q, a_ckv, a_kr, a_iq, a_ik, a_iw, b_q, b_k, b_v, b_g, c_q, c_k, c_v,
         d_qkv, d_z, d_b, d_a) = _unpack_proj(proj)
        lambda_init = 0.8 - 0.6 * math.exp(-0.3 * l)
        heads = jnp.concatenate([
            _dsa_mla_mixer(a_q, a_ckv, a_kr, a_iq, a_ik, a_iw, positions, a_kv_norm[l], a_w_uk[l], a_w_uv[l]),
            _retention_mixer(b_q, b_k, b_v, b_g, positions),
            _diff_attention_mixer(c_q, c_k, c_v, positions, c_lambda[l], c_subln[l], lambda_init),
            _gated_deltanet_mixer(d_qkv, d_z, d_b, d_a, d_conv[l], d_a_log[l], d_dt_bias[l], d_norm[l]),
        ], axis=-1)
        h2d = _out_ln(heads.reshape(bsz * seq, D_MIX), w_out[l].astype(jnp.bfloat16), x2d, ln1_g[l], ln1_b[l])
        y = _moe_ffn(h2d, w_router[l], b_router[l], w_gu[l], b_gu[l], w_dn[l], b_dn[l])
        x2d = _layer_norm(DEEPNORM_ALPHA * h2d + y, ln2_g[l], ln2_b[l])
    return x2d.reshape(bsz, seq, d)
```

```python
import functools
import math

import jax
import jax.numpy as jnp
import numpy as np
from jax import lax
from jax.experimental import pallas as pl
from jax.experimental.pallas import tpu as pltpu

D_MODEL = 2048
DEPTH = 4
A_HEADS, A_NOPE, A_ROPE, A_V, A_KV_RANK = 4, 128, 64, 128, 256
IDX_HEADS, IDX_DIM, INDEX_TOPK = 8, 64, 256
B_HEADS, B_QK, B_V, RET_CHUNK = 4, 64, 128, 128
C_HEADS, C_DH = 4, 64
D_HEADS, D_K, D_V, CONV_K, GDN_CHUNK = 4, 128, 128, 4, 64
N_EXPERTS, TOP_K, D_FF = 32, 4, 512
SWIGLU_ALPHA, SWIGLU_LIMIT, MOE_BLOCK = 1.702, 7.0, 256
ROPE_THETA = 10000.0
Q_BLOCK = 128
DEEPNORM_ALPHA = (2 * DEPTH) ** 0.25
LN_EPS = 1e-5
RMS_EPS = 1e-6
D_MIX = A_HEADS * A_V + B_HEADS * B_V + C_HEADS * 2 * C_DH + D_HEADS * D_V

IN_WIDTHS = (
    A_HEADS * (A_NOPE + A_ROPE), A_KV_RANK, A_ROPE, IDX_HEADS * IDX_DIM, IDX_DIM, IDX_HEADS,
    B_HEADS * B_QK, B_HEADS * B_QK, B_HEADS * B_V, B_HEADS * B_V,
    C_HEADS * 2 * C_DH, C_HEADS * 2 * C_DH, C_HEADS * 2 * C_DH,
    D_HEADS * (2 * D_K + D_V), D_HEADS * D_V, D_HEADS, D_HEADS,
)
IN_TOTAL = sum(IN_WIDTHS)
_IN_OFFS = np.concatenate([[0], np.cumsum(IN_WIDTHS)]).tolist()

LANE = 128
VMEM_LIMIT = 56 * 1024 * 1024

_NARROW = (5, 15, 16)
_WIDE = tuple(i for i in range(len(IN_WIDTHS)) if i not in _NARROW)
_PACK_ORDER = _WIDE + _NARROW
_PACK_OFFS = {}
_o = 0
for _i in _PACK_ORDER:
    _PACK_OFFS[_i] = _o
    _o += IN_WIDTHS[_i]
IN_PACKED = -(-_o // (2 * LANE)) * (2 * LANE)


def _pack_w_in(w_in_l):
    cols = [w_in_l[:, _IN_OFFS[i]:_IN_OFFS[i + 1]] for i in _PACK_ORDER]
    cols.append(jnp.zeros((w_in_l.shape[0], IN_PACKED - IN_TOTAL), w_in_l.dtype))
    return jnp.concatenate(cols, axis=1).astype(jnp.bfloat16)


def _unpack_proj(proj):
    return [proj[..., _PACK_OFFS[i]:_PACK_OFFS[i] + IN_WIDTHS[i]] for i in range(len(IN_WIDTHS))]


def _in_proj_body(x_ref, w_ref, o_ref, xb_ref):
    @pl.when(pl.program_id(1) == 0)
    def _():
        xb_ref[...] = x_ref[...].astype(jnp.bfloat16)

    o_ref[...] = jnp.dot(xb_ref[...], w_ref[...], preferred_element_type=jnp.float32)


def _in_proj(x2d, w_packed, tm=1024, tn=768):
    m, k = x2d.shape
    n = w_packed.shape[1]
    return pl.pallas_call(
        _in_proj_body,
        grid=(m // tm, n // tn),
        in_specs=[pl.BlockSpec((tm, k), lambda i, j: (i, 0)),
                  pl.BlockSpec((k, tn), lambda i, j: (0, j))],
        out_specs=pl.BlockSpec((tm, tn), lambda i, j: (i, j)),
        out_shape=jax.ShapeDtypeStruct((m, n), jnp.float32),
        scratch_shapes=[pltpu.VMEM((tm, k), jnp.bfloat16)],
        compiler_params=pltpu.CompilerParams(
            dimension_semantics=("arbitrary", "arbitrary"), vmem_limit_bytes=VMEM_LIMIT),
        name="in_proj",
    )(x2d, w_packed)


def _ln_rows(t, g, b):
    mu = jnp.mean(t, -1, keepdims=True)
    d = t - mu
    var = jnp.mean(d * d, -1, keepdims=True)
    return d * lax.rsqrt(var + LN_EPS) * g + b


def _out_ln_router_body(heads_ref, w_ref, x_ref, g_ref, b_ref, wr_ref, br_ref, h_ref, te_ref, tg_ref):
    f32 = jnp.float32
    acc = jnp.dot(heads_ref[...].astype(MXU_DTYPE), w_ref[...], preferred_element_type=f32)
    h = _ln_rows(DEEPNORM_ALPHA * x_ref[...] + acc, g_ref[...], b_ref[...])
    h_ref[...] = h
    logits = jnp.dot(h.astype(MXU_DTYPE), wr_ref[...], preferred_element_type=f32) + br_ref[...]
    lane = lax.broadcasted_iota(jnp.int32, logits.shape, 1)
    vals, te = [], jnp.zeros(logits.shape, jnp.int32)
    for k in range(TOP_K):
        m = logits.max(-1, keepdims=True)
        e = jnp.min(jnp.where(logits == m, lane, LANE), -1, keepdims=True)
        vals.append(m)
        te = jnp.where(lane == k, e, te)
        logits = jnp.where(lane == e, -jnp.inf, logits)
    ex = [jnp.exp(v - vals[0]) for v in vals]
    den = ex[0] + ex[1] + ex[2] + ex[3]
    tg = jnp.zeros(logits.shape, f32)
    for k in range(TOP_K):
        tg = jnp.where(lane == k, ex[k] / den, tg)
    te_ref[...] = te
    tg_ref[...] = tg


def _out_ln_router(heads2d, w_out_c, x2d, g, b, w_router, b_router, tm=512):
    m, k = heads2d.shape
    n = w_out_c.shape[1]
    wr = jnp.zeros((n, LANE), MXU_DTYPE).at[:, :N_EXPERTS].set(w_router.astype(MXU_DTYPE))
    br = jnp.full((1, LANE), -jnp.inf, jnp.float32).at[0, :N_EXPERTS].set(b_router)
    row_blk = lambda c: pl.BlockSpec((tm, c), lambda i: (i, 0))
    whole = lambda r, c: pl.BlockSpec((r, c), lambda i: (0, 0))
    return pl.pallas_call(
        _out_ln_router_body,
        grid=(m // tm,),
        in_specs=[row_blk(k), whole(k, n), row_blk(n), whole(1, n), whole(1, n), whole(n, LANE), whole(1, LANE)],
        out_specs=[row_blk(n), row_blk(LANE), row_blk(LANE)],
        out_shape=[jax.ShapeDtypeStruct((m, n), jnp.float32),
                   jax.ShapeDtypeStruct((m, LANE), jnp.int32),
                   jax.ShapeDtypeStruct((m, LANE), jnp.float32)],
        compiler_params=pltpu.CompilerParams(
            dimension_semantics=("arbitrary",), vmem_limit_bytes=VMEM_LIMIT),
        name="out_proj_ln_router",
    )(heads2d, w_out_c, x2d, g.reshape(1, n), b.reshape(1, n), wr, br)


def _layer_norm(t, g, b):
    mu = jnp.mean(t, -1, keepdims=True)
    var = jnp.mean(jnp.square(t - mu), -1, keepdims=True)
    return (t - mu) * lax.rsqrt(var + LN_EPS) * g + b


def _rms_normalize(t):
    return t * lax.rsqrt(jnp.mean(t * t, -1, keepdims=True) + RMS_EPS)


def _l2_normalize(t):
    return t * lax.rsqrt(jnp.sum(t * t, -1, keepdims=True) + RMS_EPS)


def _rotary(t, positions):
    half = t.shape[-1] // 2
    inv_freq = ROPE_THETA ** (-jnp.arange(half, dtype=jnp.float32) / half)
    ang = positions.astype(jnp.float32)[:, :, None, None] * inv_freq
    cos, sin = jnp.cos(ang), jnp.sin(ang)
    t1, t2 = t[..., :half], t[..., half:]
    return jnp.concatenate([t1 * cos - t2 * sin, t2 * cos + t1 * sin], -1)


def _to_chunks(t, c):
    return jnp.moveaxis(t.reshape((t.shape[0], t.shape[1] // c, c) + t.shape[2:]), 2, 3)


def _from_chunks(t):
    t = jnp.moveaxis(t, 3, 2)
    return t.reshape((t.shape[0], t.shape[1] * t.shape[2]) + t.shape[3:])


def _to_qblocks(t):
    return jnp.moveaxis(t.reshape((t.shape[0], t.shape[1] // Q_BLOCK, Q_BLOCK) + t.shape[2:]), 1, 0)


def _from_qblocks(t):
    t = jnp.moveaxis(t, 0, 1)
    return t.reshape((t.shape[0], t.shape[1] * t.shape[2]) + t.shape[3:])


def _dsa_mla_mixer(a_q, a_ckv, a_kr, a_iq, a_ik, a_iw, positions, kv_norm, w_uk, w_uv):
    bsz, seq = a_q.shape[:2]
    q = a_q.reshape(bsz, seq, A_HEADS, A_NOPE + A_ROPE)
    q_rope = _rotary(q[..., A_NOPE:], positions)
    q_lat = jnp.einsum('bshd,rhd->bshr', q[..., :A_NOPE], w_uk)
    q_all = jnp.concatenate([q_lat, q_rope], -1)
    c_kv = _rms_normalize(a_ckv) * kv_norm
    k_rope = _rotary(a_kr[:, :, None, :], positions)[:, :, 0]
    kv_lat = jnp.concatenate([c_kv, k_rope], -1)
    iq = _rotary(a_iq.reshape(bsz, seq, IDX_HEADS, IDX_DIM), positions)
    ik = _rotary(a_ik[:, :, None, :], positions)[:, :, 0]
    iw = a_iw * (IDX_HEADS * IDX_DIM) ** -0.5
    n_keep = min(INDEX_TOPK, seq // 4)
    nq, nc = seq // Q_BLOCK, seq // A_KCHUNK
    cdt = MXU_DTYPE
    qT = q_all.reshape(bsz, nq, Q_BLOCK, A_HEADS, A_LAT).transpose(0, 1, 4, 3, 2)
    qT = qT.reshape(bsz, nq, A_LAT, A_HEADS * Q_BLOCK).astype(cdt)
    iqT = iq.reshape(bsz, nq, Q_BLOCK, IDX_HEADS, IDX_DIM).transpose(0, 1, 4, 3, 2)
    iqT = iqT.reshape(bsz, nq, IDX_DIM, IDX_HEADS * Q_BLOCK).astype(cdt)
    iwT = iw.reshape(bsz, nq, Q_BLOCK, IDX_HEADS).transpose(0, 1, 3, 2).reshape(bsz * nq, 1, IDX_HEADS * Q_BLOCK)
    ikc = ik.reshape(bsz, nc, A_KCHUNK, IDX_DIM).astype(cdt)
    kvc = kv_lat.reshape(bsz, nc, A_KCHUNK, A_LAT).astype(cdt)
    ckvT = c_kv.reshape(bsz, nc, A_KCHUNK, A_KV_RANK).transpose(0, 1, 3, 2).astype(cdt)
    wuv = w_uv.transpose(1, 0, 2).astype(cdt)
    return _dsa_attention(iqT, iwT, qT, ikc, kvc, ckvT, wuv, n_keep=n_keep)


A_LAT = A_KV_RANK + A_ROPE
A_KCHUNK = 256
MXU_DTYPE = jnp.bfloat16
_SIGN_BIT = np.int32(-2 ** 31)
_LOW31 = np.int32(2 ** 31 - 1)
_KEY_NEG_INF = np.int32(np.array(-np.inf, np.float32).view(np.int32) ^ _LOW31)
_MASKED = -1e30


def _dsa_body(iqT_ref, iw_ref, qT_ref, ik_ref, kv_ref, ckvT_ref, wuv_ref, o_ref,
              key_ref, acc_ref, thr_ref, need_ref, *, n_keep):
    f32 = jnp.float32
    qi = pl.program_id(1)
    nk = (qi * Q_BLOCK + Q_BLOCK + A_KCHUNK - 1) // A_KCHUNK
    iqT = iqT_ref[0, 0]
    iw = iw_ref[0]
    qpos = qi * Q_BLOCK + lax.broadcasted_iota(jnp.int32, (A_KCHUNK, Q_BLOCK), 1)
    krow = lax.broadcasted_iota(jnp.int32, (A_KCHUNK, Q_BLOCK), 0)

    def index_chunk(c, carry):
        lg = jnp.dot(ik_ref[0, c], iqT, preferred_element_type=f32)
        w = jnp.maximum(lg, 0.0) * iw
        idx = w[:, :Q_BLOCK]
        for h in range(1, IDX_HEADS):
            idx = idx + w[:, h * Q_BLOCK:(h + 1) * Q_BLOCK]
        idx = jnp.where(idx == 0.0, 0.0, idx)
        idx = jnp.where(krow + c * A_KCHUNK <= qpos, idx, -jnp.inf)
        bits = pltpu.bitcast(idx, jnp.int32)
        key_ref[c] = bits ^ ((bits >> 31) & _LOW31)
        return carry

    lax.fori_loop(0, nk, index_chunk, 0)

    def count(cmp, t):
        def body(c, cnt):
            hit = cmp(key_ref[c], t).astype(jnp.int32)
            return cnt + hit.reshape(A_KCHUNK // 8, 8, Q_BLOCK).sum(0)
        cnt = lax.fori_loop(0, nk, body, jnp.zeros((8, Q_BLOCK), jnp.int32))
        return cnt.sum(0, keepdims=True)

    @pl.when(qi * Q_BLOCK + Q_BLOCK <= n_keep)
    def _():
        thr_ref[...] = jnp.full((1, Q_BLOCK), _KEY_NEG_INF, jnp.int32)
        need_ref[...] = jnp.zeros((1, Q_BLOCK), jnp.int32)

    @pl.when(qi * Q_BLOCK + Q_BLOCK > n_keep)
    def _():
        def bit_step(i, prefix):
            cand = prefix | jnp.left_shift(jnp.int32(1), 31 - i)
            cnt = count(lambda k, t: k >= t, cand ^ _SIGN_BIT)
            return jnp.where(cnt >= n_keep, cand, prefix)
        prefix = lax.fori_loop(0, 32, bit_step, jnp.zeros((1, Q_BLOCK), jnp.int32))
        thr = prefix ^ _SIGN_BIT
        thr_ref[...] = thr
        need_ref[...] = n_keep - count(lambda k, t: k > t, thr)

    thr = thr_ref[...]
    need = need_ref[...].astype(f32)
    qT = qT_ref[0, 0]
    scale = (A_NOPE + A_ROPE) ** -0.5
    r_i = lax.broadcasted_iota(jnp.int32, (A_KCHUNK, A_KCHUNK), 0)
    c_i = lax.broadcasted_iota(jnp.int32, (A_KCHUNK, A_KCHUNK), 1)
    before = (c_i < r_i).astype(jnp.bfloat16)
    acc_ref[...] = jnp.zeros_like(acc_ref)

    def attend_chunk(c, carry):
        m, l, ties_seen = carry
        s = jnp.dot(kv_ref[0, c], qT, preferred_element_type=f32) * scale
        key = key_ref[c]
        tie = key == thr
        tie_f = tie.astype(f32)
        rank = jnp.dot(before, tie_f.astype(jnp.bfloat16), preferred_element_type=f32) + ties_seen
        keep = (key > thr) | (tie & (rank < need))
        bias = jnp.where(keep, 0.0, _MASKED)
        s = s + jnp.concatenate([bias] * A_HEADS, axis=1)
        m_new = jnp.maximum(m, s.max(0, keepdims=True))
        alpha = jnp.exp(m - m_new)
        p = jnp.exp(s - m_new)
        l = alpha * l + p.sum(0, keepdims=True)
        pv = jnp.dot(ckvT_ref[0, c], p.astype(ckvT_ref.dtype), preferred_element_type=f32)
        acc_ref[...] = acc_ref[...] * alpha + pv
        return m_new, l, ties_seen + tie_f.sum(0, keepdims=True)

    lanes = A_HEADS * Q_BLOCK
    init = (jnp.full((1, lanes), _MASKED, f32), jnp.zeros((1, lanes), f32), jnp.zeros((1, Q_BLOCK), f32))
    _, l, _ = lax.fori_loop(0, nk, attend_chunk, init)
    o_latT = acc_ref[...] / l
    for h in range(A_HEADS):
        o_lat = o_latT[:, h * Q_BLOCK:(h + 1) * Q_BLOCK].T.astype(wuv_ref.dtype)
        o_ref[0, :, h * A_V:(h + 1) * A_V] = jnp.dot(o_lat, wuv_ref[h], preferred_element_type=f32)


def _dsa_attention(iqT, iwT, qT, ikc, kvc, ckvT, wuv, *, n_keep):
    bsz, nq = qT.shape[:2]
    nc = ikc.shape[1]
    seq = nq * Q_BLOCK
    per_q = lambda b, q: (b, q, 0, 0)
    per_b = lambda b, q: (b, 0, 0, 0)
    return pl.pallas_call(
        functools.partial(_dsa_body, n_keep=n_keep),
        grid=(bsz, nq),
        in_specs=[pl.BlockSpec((1, 1) + iqT.shape[2:], per_q),
                  pl.BlockSpec((1, 1, iwT.shape[2]), lambda b, q: (b * nq + q, 0, 0)),
                  pl.BlockSpec((1, 1) + qT.shape[2:], per_q),
                  pl.BlockSpec((1,) + ikc.shape[1:], per_b),
                  pl.BlockSpec((1,) + kvc.shape[1:], per_b),
                  pl.BlockSpec((1,) + ckvT.shape[1:], per_b),
                  pl.BlockSpec(wuv.shape, lambda b, q: (0, 0, 0))],
        out_specs=pl.BlockSpec((1, Q_BLOCK, A_HEADS * A_V), lambda b, q: (b, q, 0)),
        out_shape=jax.ShapeDtypeStruct((bsz, seq, A_HEADS * A_V), jnp.float32),
        scratch_shapes=[pltpu.VMEM((nc, A_KCHUNK, Q_BLOCK), jnp.int32),
                        pltpu.VMEM((A_KV_RANK, A_HEADS * Q_BLOCK), jnp.float32),
                        pltpu.VMEM((1, Q_BLOCK), jnp.int32),
                        pltpu.VMEM((1, Q_BLOCK), jnp.int32)],
        compiler_params=pltpu.CompilerParams(
            dimension_semantics=("arbitrary", "arbitrary"), vmem_limit_bytes=VMEM_LIMIT),
        name="dsa_attention",
    )(iqT, iwT, qT, ikc, kvc, ckvT, wuv)


def _retention_mixer(b_q, b_k, b_v, b_g, positions):
    bsz, seq = b_q.shape[:2]
    f32 = jnp.float32
    q = _rotary(b_q.reshape(bsz, seq, B_HEADS, B_QK), positions)
    k = _rotary(b_k.reshape(bsz, seq, B_HEADS, B_QK), positions) * B_QK ** -0.5
    v = b_v.reshape(bsz, seq, B_HEADS, B_V)
    log_gamma = jnp.log(1.0 - 2.0 ** (-5.0 - jnp.arange(B_HEADS, dtype=f32)))
    c = RET_CHUNK
    qc, kc, vc = _to_chunks(q, c), _to_chunks(k, c), _to_chunks(v, c)
    i = jnp.arange(c, dtype=f32)
    rel = i[:, None] - i[None, :]
    intra_decay = jnp.where(rel >= 0, jnp.exp(log_gamma[:, None, None] * jnp.maximum(rel, 0.0)), 0.0)
    scores = jnp.einsum('bnhid,bnhjd->bnhij', qc, kc) * intra_decay
    o_intra = jnp.einsum('bnhij,bnhje->bnhie', scores, vc)
    k_to_end = kc * jnp.exp(log_gamma[:, None] * (c - 1 - i))[:, :, None]
    chunk_kv = jnp.einsum('bnhcd,bnhce->bnhde', k_to_end, vc)
    chunk_decay = jnp.exp(log_gamma * c)[None, :, None, None]

    def step(state, kv_n):
        return state * chunk_decay + kv_n, state

    _, prev = lax.scan(step, jnp.zeros((bsz, B_HEADS, B_QK, B_V), f32), jnp.moveaxis(chunk_kv, 1, 0))
    prev = jnp.moveaxis(prev, 0, 1)
    q_from_start = qc * jnp.exp(log_gamma[:, None] * (i + 1.0))[:, :, None]
    o = o_intra + jnp.einsum('bnhid,bnhde->bnhie', q_from_start, prev)
    o = _rms_normalize(_from_chunks(o))
    gate = jax.nn.silu(b_g).reshape(bsz, seq, B_HEADS, B_V)
    return (o * gate).reshape(bsz, seq, B_HEADS * B_V)


def _diff_attention_mixer(c_q, c_k, c_v, positions, lam, subln_g, lambda_init):
    bsz, seq = c_q.shape[:2]
    q = _rotary(c_q.reshape(bsz, seq, C_HEADS * 2, C_DH), positions).reshape(bsz, seq, C_HEADS, 2, C_DH)
    k = _rotary(c_k.reshape(bsz, seq, C_HEADS * 2, C_DH), positions).reshape(bsz, seq, C_HEADS, 2, C_DH)
    v = c_v.reshape(bsz, seq, C_HEADS, 2 * C_DH)
    lam_full = jnp.exp(jnp.sum(lam[0] * lam[1])) - jnp.exp(jnp.sum(lam[2] * lam[3])) + lambda_init
    key_pos = jnp.arange(seq)
    scale = C_DH ** -0.5

    def block(args):
        qb, qpos = args
        logits = jnp.einsum('bqhmd,bshmd->bhmqs', qb, k) * scale
        logits = jnp.where(key_pos[None, :] <= qpos[:, None], logits, -jnp.inf)
        p = jax.nn.softmax(logits, axis=-1)
        diff = p[:, :, 0] - lam_full * p[:, :, 1]
        return jnp.einsum('bhqs,bshe->bqhe', diff, v)

    qpos_blocks = jnp.arange(seq).reshape(seq // Q_BLOCK, Q_BLOCK)
    o = _from_qblocks(lax.map(block, (_to_qblocks(q), qpos_blocks)))
    o = _rms_normalize(o) * subln_g * (1.0 - lambda_init)
    return o.reshape(bsz, seq, C_HEADS * 2 * C_DH)


def _causal_depthwise_conv(t, w):
    return lax.conv_general_dilated(t, w[:, None, :], window_strides=(1,), padding=((CONV_K - 1, 0),),
                                    dimension_numbers=('NWC', 'WIO', 'NWC'), feature_group_count=t.shape[-1])


def _chunk_gated_delta_rule(q, k, v, g, beta):
    f32 = jnp.float32
    bsz = q.shape[0]
    c = GDN_CHUNK
    qc, kc, vc = (_to_chunks(t, c) for t in (q, k, v))
    gc, bc = _to_chunks(g, c), _to_chunks(beta, c)
    G = jnp.cumsum(gc, axis=-1)
    idx = jnp.arange(c)
    tril = idx[:, None] >= idx[None, :]
    strict = idx[:, None] > idx[None, :]
    diffG = G[..., :, None] - G[..., None, :]
    gam = jnp.where(tril, jnp.exp(jnp.where(tril, diffG, 0.0)), 0.0)
    kb = kc * bc[..., None]
    lower = jnp.where(strict, jnp.einsum('bnhid,bnhjd->bnhij', kb, kc) * gam, 0.0)
    rhs = jnp.concatenate([vc * bc[..., None], kb * jnp.exp(G)[..., None]], -1)
    sol = lax.linalg.triangular_solve(lower + jnp.eye(c, dtype=f32), rhs, left_side=True, lower=True,
                                      unit_diagonal=True)
    u, w = sol[..., :D_V], sol[..., D_V:]
    attn = jnp.where(tril, jnp.einsum('bnhid,bnhjd->bnhij', qc, kc) * gam, 0.0)
    qg = qc * jnp.exp(G)[..., None]
    kd = kc * jnp.exp(G[..., -1:] - G)[..., None]
    decay_last = jnp.exp(G[..., -1])

    def step(state, xs):
        qg_n, w_n, u_n, a_n, kd_n, dl_n = xs
        v_new = u_n - jnp.einsum('bhcd,bhde->bhce', w_n, state)
        o_n = jnp.einsum('bhcd,bhde->bhce', qg_n, state) + jnp.einsum('bhij,bhje->bhie', a_n, v_new)
        state = state * dl_n[..., None, None] + jnp.einsum('bhcd,bhce->bhde', kd_n, v_new)
        return state, o_n

    xs = tuple(jnp.moveaxis(t, 1, 0) for t in (qg, w, u, attn, kd, decay_last))
    _, o = lax.scan(step, jnp.zeros((bsz, D_HEADS, D_K, D_V), f32), xs)
    return _from_chunks(jnp.moveaxis(o, 0, 1))


def _gated_deltanet_mixer(d_qkv, d_z, d_b, d_a, conv_w, a_log, dt_bias, norm_g):
    bsz, seq = d_qkv.shape[:2]
    qkv = jax.nn.silu(_causal_depthwise_conv(d_qkv, conv_w))
    q, k, v = jnp.split(qkv, [D_HEADS * D_K, 2 * D_HEADS * D_K], axis=-1)
    q = _l2_normalize(q.reshape(bsz, seq, D_HEADS, D_K)) * D_K ** -0.5
    k = _l2_normalize(k.reshape(bsz, seq, D_HEADS, D_K))
    v = v.reshape(bsz, seq, D_HEADS, D_V)
    beta = jax.nn.sigmoid(d_b)
    g = -jnp.exp(a_log) * jax.nn.softplus(d_a + dt_bias)
    o = _chunk_gated_delta_rule(q, k, v, g, beta)
    o = _rms_normalize(o) * norm_g * jax.nn.silu(d_z.reshape(bsz, seq, D_HEADS, D_V))
    return o.reshape(bsz, seq, D_HEADS * D_V)


def _clamped_swiglu(hg):
    glu, lin = hg[..., :D_FF], hg[..., D_FF:]
    glu = jnp.minimum(glu, SWIGLU_LIMIT)
    lin = jnp.clip(lin, -SWIGLU_LIMIT, SWIGLU_LIMIT)
    return glu * jax.nn.sigmoid(SWIGLU_ALPHA * glu) * (lin + 1.0)


def _moe_plan(top_e):
    n_tok = top_e.shape[0]
    n_blocks = -(-n_tok * TOP_K // MOE_BLOCK) + N_EXPERTS
    member = (top_e[:, :, None] == jnp.arange(N_EXPERTS, dtype=jnp.int32)).astype(jnp.int32).sum(1)
    rank = jnp.cumsum(member, axis=0) - member
    counts = member.sum(0)
    padded = (counts + MOE_BLOCK - 1) // MOE_BLOCK * MOE_BLOCK
    pad_end = jnp.cumsum(padded)
    dest = jnp.take_along_axis(rank + (pad_end - padded)[None, :], top_e, axis=1).astype(jnp.int32)
    tok = jnp.broadcast_to(jnp.arange(n_tok, dtype=jnp.int32)[:, None], dest.shape)
    row_tok = jnp.zeros((n_blocks * MOE_BLOCK,), jnp.int32).at[dest.reshape(-1)].set(tok.reshape(-1))
    block_e = jnp.minimum(jnp.searchsorted(pad_end, jnp.arange(n_blocks) * MOE_BLOCK, side='right'),
                          N_EXPERTS - 1).astype(jnp.int32)
    n_used = (pad_end[-1:] // MOE_BLOCK).astype(jnp.int32)
    return dest, row_tok, block_e, n_used


def _row_gather_start(src_hbm, rows_smem, n_rows, dst, sem):
    def body(r, carry):
        pltpu.make_async_copy(src_hbm.at[pl.ds(rows_smem[0, 0, r], 1)], dst.at[pl.ds(r, 1)], sem).start()
        return carry
    lax.fori_loop(0, n_rows, body, 0, unroll=8)


def _row_gather_wait(src_hbm, n_rows, dst, sem):
    pltpu.make_async_copy(src_hbm.at[pl.ds(0, n_rows)], dst, sem).wait()


def _moe_expert_body(be_ref, nused_ref, tok_ref, tok_next_ref, h_hbm, wgu_ref, bgu_ref, wdn_ref, bdn_ref,
                     y_ref, buf, sem):
    f32 = jnp.float32
    i = pl.program_id(0)
    n_used = nused_ref[0]
    slot = i % 2

    @pl.when(jnp.logical_and(i == 0, n_used > 0))
    def _():
        _row_gather_start(h_hbm, tok_ref, MOE_BLOCK, buf.at[0], sem.at[0])

    @pl.when(i + 1 < n_used)
    def _():
        _row_gather_start(h_hbm, tok_next_ref, MOE_BLOCK, buf.at[1 - slot], sem.at[1 - slot])

    @pl.when(i < n_used)
    def _():
        _row_gather_wait(h_hbm, MOE_BLOCK, buf.at[slot], sem.at[slot])
        xb = buf[slot].astype(MXU_DTYPE)
        hg = jnp.dot(xb, wgu_ref[0], preferred_element_type=f32) + bgu_ref[0]
        act = _clamped_swiglu(hg)
        y_ref[...] = jnp.dot(act.astype(MXU_DTYPE), wdn_ref[0], preferred_element_type=f32) + bdn_ref[0]

    @pl.when(i >= n_used)
    def _():
        y_ref[...] = jnp.zeros_like(y_ref)


def _moe_experts(h2d, row_tok, block_e, n_used, w_gu_c, b_gu, w_dn_c, b_dn):
    n_tok, d = h2d.shape
    n_blocks = block_e.shape[0]
    tok3 = row_tok.reshape(n_blocks, 1, MOE_BLOCK)
    smem_blk = lambda imap: pl.BlockSpec((1, 1, MOE_BLOCK), imap, memory_space=pltpu.SMEM)
    by_expert = lambda shape: pl.BlockSpec((1,) + shape, lambda i, be, nu: (be[i], 0, 0))
    return pl.pallas_call(
        _moe_expert_body,
        grid_spec=pltpu.PrefetchScalarGridSpec(
            num_scalar_prefetch=2,
            grid=(n_blocks,),
            in_specs=[smem_blk(lambda i, be, nu: (i, 0, 0)),
                      smem_blk(lambda i, be, nu: (jnp.minimum(i + 1, n_blocks - 1), 0, 0)),
                      pl.BlockSpec(memory_space=pl.ANY),
                      by_expert((d, 2 * D_FF)), by_expert((1, 2 * D_FF)),
                      by_expert((D_FF, d)), by_expert((1, d))],
            out_specs=pl.BlockSpec((MOE_BLOCK, d), lambda i, be, nu: (i, 0)),
            scratch_shapes=[pltpu.VMEM((2, MOE_BLOCK, d), jnp.float32), pltpu.SemaphoreType.DMA((2,))]),
        out_shape=jax.ShapeDtypeStruct((n_blocks * MOE_BLOCK, d), jnp.float32),
        compiler_params=pltpu.CompilerParams(
            dimension_semantics=("arbitrary",), vmem_limit_bytes=VMEM_LIMIT),
        name="moe_experts",
    )(block_e, n_used, tok3, tok3, h2d, w_gu_c, b_gu.reshape(N_EXPERTS, 1, 2 * D_FF),
      w_dn_c, b_dn.reshape(N_EXPERTS, 1, d))


MOE_COMBINE_ROWS = 128


def _moe_combine_ln_body(dest_ref, dest_next_ref, h_ref, tg_ref, g_ref, b_ref, y_hbm, o_ref, buf, sem):
    tm = MOE_COMBINE_ROWS
    i = pl.program_id(0)
    n = pl.num_programs(0)
    slot = i % 2

    def start(rows_smem, s):
        for k in range(TOP_K):
            def body(r, carry):
                pltpu.make_async_copy(y_hbm.at[pl.ds(rows_smem[0, 0, k * tm + r], 1)],
                                      buf.at[s, k, pl.ds(r, 1)], sem.at[s]).start()
                return carry
            lax.fori_loop(0, tm, body, 0, unroll=8)

    @pl.when(i == 0)
    def _():
        start(dest_ref, 0)

    @pl.when(i + 1 < n)
    def _():
        start(dest_next_ref, 1 - slot)

    tg = tg_ref[...]
    y = jnp.zeros(h_ref.shape, jnp.float32)
    for k in range(TOP_K):
        _row_gather_wait(y_hbm, tm, buf.at[slot, k], sem.at[slot])
    for k in range(TOP_K):
        y = y + tg[:, k:k + 1] * buf[slot, k]
    o_ref[...] = _ln_rows(DEEPNORM_ALPHA * h_ref[...] + y, g_ref[...], b_ref[...])


def _moe_combine_ln(h2d, y_rows, dest, tg, g, b):
    n_tok, d = h2d.shape
    tm = MOE_COMBINE_ROWS
    nt = n_tok // tm
    dest3 = dest.reshape(nt, tm, TOP_K).transpose(0, 2, 1).reshape(nt, 1, TOP_K * tm)
    smem_blk = lambda imap: pl.BlockSpec((1, 1, TOP_K * tm), imap, memory_space=pltpu.SMEM)
    return pl.pallas_call(
        _moe_combine_ln_body,
        grid=(nt,),
        in_specs=[smem_blk(lambda i: (i, 0, 0)),
                  smem_blk(lambda i: (jnp.minimum(i + 1, nt - 1), 0, 0)),
                  pl.BlockSpec((tm, d), lambda i: (i, 0)),
                  pl.BlockSpec((tm, LANE), lambda i: (i, 0)),
                  pl.BlockSpec((1, d), lambda i: (0, 0)),
                  pl.BlockSpec((1, d), lambda i: (0, 0)),
                  pl.BlockSpec(memory_space=pl.ANY)],
        out_specs=pl.BlockSpec((tm, d), lambda i: (i, 0)),
        out_shape=jax.ShapeDtypeStruct((n_tok, d), jnp.float32),
        scratch_shapes=[pltpu.VMEM((2, TOP_K, tm, d), jnp.float32), pltpu.SemaphoreType.DMA((2,))],
        compiler_params=pltpu.CompilerParams(
            dimension_semantics=("arbitrary",), vmem_limit_bytes=VMEM_LIMIT),
        name="moe_combine_ln",
    )(dest3, dest3, h2d, tg, g.reshape(1, d), b.reshape(1, d), y_rows)


def _moe_layer(h2d, te, tg, w_gu, b_gu, w_dn, b_dn, g, b):
    dest, row_tok, block_e, n_used = _moe_plan(te[:, :TOP_K])
    y_rows = _moe_experts(h2d, row_tok, block_e, n_used, w_gu.astype(MXU_DTYPE), b_gu, w_dn.astype(MXU_DTYPE), b_dn)
    return _moe_combine_ln(h2d, y_rows, dest, tg, g, b)


def kernel(x, positions, w_in, w_out, a_kv_norm, a_w_uk, a_w_uv, c_lambda, c_subln, d_conv, d_a_log, d_dt_bias, d_norm, ln1_g, ln1_b, w_router, b_router, w_gu, b_gu, w_dn, b_dn, ln2_g, ln2_b):
    bsz, seq, d = x.shape
    x2d = x.reshape(bsz * seq, d)
    for l in range(DEPTH):
        proj = _in_proj(x2d, _pack_w_in(w_in[l])).reshape(bsz, seq, IN_PACKED)
        (a_q, a_ckv, a_kr, a_iq, a_ik, a_iw, b_q, b_k, b_v, b_g, c_q, c_k, c_v,
         d_qkv, d_z, d_b, d_a) = _unpack_proj(proj)
        lambda_init = 0.8 - 0.6 * math.exp(-0.3 * l)
        heads = jnp.concatenate([
            _dsa_mla_mixer(a_q, a_ckv, a_kr, a_iq, a_ik, a_iw, positions, a_kv_norm[l], a_w_uk[l], a_w_uv[l]),
            _retention_mixer(b_q, b_k, b_v, b_g, positions),
            _diff_attention_mixer(c_q, c_k, c_v, positions, c_lambda[l], c_subln[l], lambda_init),
            _gated_deltanet_mixer(d_qkv, d_z, d_b, d_a, d_conv[l], d_a_log[l], d_dt_bias[l], d_norm[l]),
        ], axis=-1)
        h2d, te, tg = _out_ln_router(heads.reshape(bsz * seq, D_MIX), w_out[l].astype(MXU_DTYPE), x2d,
                                     ln1_g[l], ln1_b[l], w_router[l], b_router[l])
        x2d = _moe_layer(h2d, te, tg, w_gu[l], b_gu[l], w_dn[l], b_dn[l], ln2_g[l], ln2_b[l])
    return x2d.reshape(bsz, seq, d)
```

```python
import functools
import math

import jax
import jax.numpy as jnp
import numpy as np
from jax import lax
from jax.experimental import pallas as pl
from jax.experimental.pallas import tpu as pltpu

D_MODEL = 2048
DEPTH = 4
A_HEADS, A_NOPE, A_ROPE, A_V, A_KV_RANK = 4, 128, 64, 128, 256
IDX_HEADS, IDX_DIM, INDEX_TOPK = 8, 64, 256
B_HEADS, B_QK, B_V, RET_CHUNK = 4, 64, 128, 128
C_HEADS, C_DH = 4, 64
D_HEADS, D_K, D_V, CONV_K, GDN_CHUNK = 4, 128, 128, 4, 64
N_EXPERTS, TOP_K, D_FF = 32, 4, 512
SWIGLU_ALPHA, SWIGLU_LIMIT, MOE_BLOCK = 1.702, 7.0, 256
ROPE_THETA = 10000.0
Q_BLOCK = 128
DEEPNORM_ALPHA = (2 * DEPTH) ** 0.25
LN_EPS = 1e-5
RMS_EPS = 1e-6
D_MIX = A_HEADS * A_V + B_HEADS * B_V + C_HEADS * 2 * C_DH + D_HEADS * D_V

IN_WIDTHS = (
    A_HEADS * (A_NOPE + A_ROPE), A_KV_RANK, A_ROPE, IDX_HEADS * IDX_DIM, IDX_DIM, IDX_HEADS,
    B_HEADS * B_QK, B_HEADS * B_QK, B_HEADS * B_V, B_HEADS * B_V,
    C_HEADS * 2 * C_DH, C_HEADS * 2 * C_DH, C_HEADS * 2 * C_DH,
    D_HEADS * (2 * D_K + D_V), D_HEADS * D_V, D_HEADS, D_HEADS,
)
IN_TOTAL = sum(IN_WIDTHS)
_IN_OFFS = np.concatenate([[0], np.cumsum(IN_WIDTHS)]).tolist()

LANE = 128
VMEM_LIMIT = 56 * 1024 * 1024

_NARROW = (5, 15, 16)
_WIDE = tuple(i for i in range(len(IN_WIDTHS)) if i not in _NARROW)
_PACK_ORDER = _WIDE + _NARROW
_PACK_OFFS = {}
_o = 0
for _i in _PACK_ORDER:
    _PACK_OFFS[_i] = _o
    _o += IN_WIDTHS[_i]
IN_PACKED = -(-_o // (2 * LANE)) * (2 * LANE)


def _pack_w_in(w_in_l):
    cols = [w_in_l[:, _IN_OFFS[i]:_IN_OFFS[i + 1]] for i in _PACK_ORDER]
    cols.append(jnp.zeros((w_in_l.shape[0], IN_PACKED - IN_TOTAL), w_in_l.dtype))
    return jnp.concatenate(cols, axis=1).astype(jnp.bfloat16)


def _unpack_proj(proj):
    return [proj[..., _PACK_OFFS[i]:_PACK_OFFS[i] + IN_WIDTHS[i]] for i in range(len(IN_WIDTHS))]


def _in_proj_body(x_ref, w_ref, o_ref, xb_ref):
    @pl.when(pl.program_id(1) == 0)
    def _():
        xb_ref[...] = x_ref[...].astype(jnp.bfloat16)

    o_ref[...] = jnp.dot(xb_ref[...], w_ref[...], preferred_element_type=jnp.float32)


def _in_proj(x2d, w_packed, tm=1024, tn=768):
    m, k = x2d.shape
    n = w_packed.shape[1]
    return pl.pallas_call(
        _in_proj_body,
        grid=(m // tm, n // tn),
        in_specs=[pl.BlockSpec((tm, k), lambda i, j: (i, 0)),
                  pl.BlockSpec((k, tn), lambda i, j: (0, j))],
        out_specs=pl.BlockSpec((tm, tn), lambda i, j: (i, j)),
        out_shape=jax.ShapeDtypeStruct((m, n), jnp.float32),
        scratch_shapes=[pltpu.VMEM((tm, k), jnp.bfloat16)],
        compiler_params=pltpu.CompilerParams(
            dimension_semantics=("arbitrary", "arbitrary"), vmem_limit_bytes=VMEM_LIMIT),
        name="in_proj",
    )(x2d, w_packed)


def _ln_rows(t, g, b):
    mu = jnp.mean(t, -1, keepdims=True)
    d = t - mu
    var = jnp.mean(d * d, -1, keepdims=True)
    return d * lax.rsqrt(var + LN_EPS) * g + b


def _out_ln_router_body(heads_ref, w_ref, x_ref, g_ref, b_ref, wr_ref, br_ref, h_ref, te_ref, tg_ref):
    f32 = jnp.float32
    acc = jnp.dot(heads_ref[...].astype(MXU_DTYPE), w_ref[...], preferred_element_type=f32)
    h = _ln_rows(DEEPNORM_ALPHA * x_ref[...] + acc, g_ref[...], b_ref[...])
    h_ref[...] = h
    logits = jnp.dot(h.astype(MXU_DTYPE), wr_ref[...], preferred_element_type=f32) + br_ref[...]
    lane = lax.broadcasted_iota(jnp.int32, logits.shape, 1)
    vals, te = [], jnp.zeros(logits.shape, jnp.int32)
    for k in range(TOP_K):
        m = logits.max(-1, keepdims=True)
        e = jnp.min(jnp.where(logits == m, lane, LANE), -1, keepdims=True)
        vals.append(m)
        te = jnp.where(lane == k, e, te)
        logits = jnp.where(lane == e, -jnp.inf, logits)
    ex = [jnp.exp(v - vals[0]) for v in vals]
    den = ex[0] + ex[1] + ex[2] + ex[3]
    tg = jnp.zeros(logits.shape, f32)
    for k in range(TOP_K):
        tg = jnp.where(lane == k, ex[k] / den, tg)
    te_ref[...] = te
    tg_ref[...] = tg


def _out_ln_router(heads2d, w_out_c, x2d, g, b, w_router, b_router, tm=512):
    m, k = heads2d.shape
    n = w_out_c.shape[1]
    wr = jnp.zeros((n, LANE), MXU_DTYPE).at[:, :N_EXPERTS].set(w_router.astype(MXU_DTYPE))
    br = jnp.full((1, LANE), -jnp.inf, jnp.float32).at[0, :N_EXPERTS].set(b_router)
    row_blk = lambda c: pl.BlockSpec((tm, c), lambda i: (i, 0))
    whole = lambda r, c: pl.BlockSpec((r, c), lambda i: (0, 0))
    return pl.pallas_call(
        _out_ln_router_body,
        grid=(m // tm,),
        in_specs=[row_blk(k), whole(k, n), row_blk(n), whole(1, n), whole(1, n), whole(n, LANE), whole(1, LANE)],
        out_specs=[row_blk(n), row_blk(LANE), row_blk(LANE)],
        out_shape=[jax.ShapeDtypeStruct((m, n), jnp.float32),
                   jax.ShapeDtypeStruct((m, LANE), jnp.int32),
                   jax.ShapeDtypeStruct((m, LANE), jnp.float32)],
        compiler_params=pltpu.CompilerParams(
            dimension_semantics=("arbitrary",), vmem_limit_bytes=VMEM_LIMIT),
        name="out_proj_ln_router",
    )(heads2d, w_out_c, x2d, g.reshape(1, n), b.reshape(1, n), wr, br)


def _layer_norm(t, g, b):
    mu = jnp.mean(t, -1, keepdims=True)
    var = jnp.mean(jnp.square(t - mu), -1, keepdims=True)
    return (t - mu) * lax.rsqrt(var + LN_EPS) * g + b


def _rms_normalize(t):
    return t * lax.rsqrt(jnp.mean(t * t, -1, keepdims=True) + RMS_EPS)


def _l2_normalize(t):
    return t * lax.rsqrt(jnp.sum(t * t, -1, keepdims=True) + RMS_EPS)


def _rotary(t, positions):
    half = t.shape[-1] // 2
    inv_freq = ROPE_THETA ** (-jnp.arange(half, dtype=jnp.float32) / half)
    ang = positions.astype(jnp.float32)[:, :, None, None] * inv_freq
    cos, sin = jnp.cos(ang), jnp.sin(ang)
    t1, t2 = t[..., :half], t[..., half:]
    return jnp.concatenate([t1 * cos - t2 * sin, t2 * cos + t1 * sin], -1)


def _to_chunks(t, c):
    return jnp.moveaxis(t.reshape((t.shape[0], t.shape[1] // c, c) + t.shape[2:]), 2, 3)


def _from_chunks(t):
    t = jnp.moveaxis(t, 3, 2)
    return t.reshape((t.shape[0], t.shape[1] * t.shape[2]) + t.shape[3:])


def _to_qblocks(t):
    return jnp.moveaxis(t.reshape((t.shape[0], t.shape[1] // Q_BLOCK, Q_BLOCK) + t.shape[2:]), 1, 0)


def _from_qblocks(t):
    t = jnp.moveaxis(t, 0, 1)
    return t.reshape((t.shape[0], t.shape[1] * t.shape[2]) + t.shape[3:])


def _dsa_mla_mixer(a_q, a_ckv, a_kr, a_iq, a_ik, a_iw, positions, kv_norm, w_uk, w_uv):
    bsz, seq = a_q.shape[:2]
    q = a_q.reshape(bsz, seq, A_HEADS, A_NOPE + A_ROPE)
    q_rope = _rotary(q[..., A_NOPE:], positions)
    q_lat = jnp.einsum('bshd,rhd->bshr', q[..., :A_NOPE], w_uk)
    q_all = jnp.concatenate([q_lat, q_rope], -1)
    c_kv = _rms_normalize(a_ckv) * kv_norm
    k_rope = _rotary(a_kr[:, :, None, :], positions)[:, :, 0]
    kv_lat = jnp.concatenate([c_kv, k_rope], -1)
    iq = _rotary(a_iq.reshape(bsz, seq, IDX_HEADS, IDX_DIM), positions)
    ik = _rotary(a_ik[:, :, None, :], positions)[:, :, 0]
    iw = a_iw * (IDX_HEADS * IDX_DIM) ** -0.5
    n_keep = min(INDEX_TOPK, seq // 4)
    nq, nc = seq // Q_BLOCK, seq // A_KCHUNK
    cdt = MXU_DTYPE
    qT = q_all.reshape(bsz, nq, Q_BLOCK, A_HEADS, A_LAT).transpose(0, 1, 4, 3, 2)
    qT = qT.reshape(bsz, nq, A_LAT, A_HEADS * Q_BLOCK).astype(cdt)
    iqT = iq.reshape(bsz, nq, Q_BLOCK, IDX_HEADS, IDX_DIM).transpose(0, 1, 4, 3, 2)
    iqT = iqT.reshape(bsz, nq, IDX_DIM, IDX_HEADS * Q_BLOCK).astype(cdt)
    iwT = iw.reshape(bsz, nq, Q_BLOCK, IDX_HEADS).transpose(0, 1, 3, 2).reshape(bsz * nq, 1, IDX_HEADS * Q_BLOCK)
    ikc = ik.reshape(bsz, nc, A_KCHUNK, IDX_DIM).astype(cdt)
    kvc = kv_lat.reshape(bsz, nc, A_KCHUNK, A_LAT).astype(cdt)
    ckvT = c_kv.reshape(bsz, nc, A_KCHUNK, A_KV_RANK).transpose(0, 1, 3, 2).astype(cdt)
    wuv = w_uv.transpose(1, 0, 2).astype(cdt)
    return _dsa_attention(iqT, iwT, qT, ikc, kvc, ckvT, wuv, n_keep=n_keep)


A_LAT = A_KV_RANK + A_ROPE
A_KCHUNK = 256
MXU_DTYPE = jnp.bfloat16
_SIGN_BIT = np.int32(-2 ** 31)
_LOW31 = np.int32(2 ** 31 - 1)
_KEY_NEG_INF = np.int32(np.array(-np.inf, np.float32).view(np.int32) ^ _LOW31)
_MASKED = -1e30


def _dsa_body(iqT_ref, iw_ref, qT_ref, ik_ref, kv_ref, ckvT_ref, wuv_ref, o_ref,
              key_ref, acc_ref, thr_ref, need_ref, *, n_keep):
    f32 = jnp.float32
    qi = pl.program_id(1)
    nk = (qi * Q_BLOCK + Q_BLOCK + A_KCHUNK - 1) // A_KCHUNK
    iqT = iqT_ref[0, 0]
    iw = iw_ref[0]
    qpos = qi * Q_BLOCK + lax.broadcasted_iota(jnp.int32, (A_KCHUNK, Q_BLOCK), 1)
    krow = lax.broadcasted_iota(jnp.int32, (A_KCHUNK, Q_BLOCK), 0)

    def index_chunk(c, carry):
        lg = jnp.dot(ik_ref[0, c], iqT, preferred_element_type=f32)
        w = jnp.maximum(lg, 0.0) * iw
        idx = w[:, :Q_BLOCK]
        for h in range(1, IDX_HEADS):
            idx = idx + w[:, h * Q_BLOCK:(h + 1) * Q_BLOCK]
        idx = jnp.where(idx == 0.0, 0.0, idx)
        idx = jnp.where(krow + c * A_KCHUNK <= qpos, idx, -jnp.inf)
        bits = pltpu.bitcast(idx, jnp.int32)
        key_ref[c] = bits ^ ((bits >> 31) & _LOW31)
        return carry

    lax.fori_loop(0, nk, index_chunk, 0)

    def count(cmp, t):
        def body(c, cnt):
            hit = cmp(key_ref[c], t).astype(jnp.int32)
            return cnt + hit.reshape(A_KCHUNK // 8, 8, Q_BLOCK).sum(0)
        cnt = lax.fori_loop(0, nk, body, jnp.zeros((8, Q_BLOCK), jnp.int32))
        return cnt.sum(0, keepdims=True)

    @pl.when(qi * Q_BLOCK + Q_BLOCK <= n_keep)
    def _():
        thr_ref[...] = jnp.full((1, Q_BLOCK), _KEY_NEG_INF, jnp.int32)
        need_ref[...] = jnp.zeros((1, Q_BLOCK), jnp.int32)

    @pl.when(qi * Q_BLOCK + Q_BLOCK > n_keep)
    def _():
        def bit_step(i, prefix):
            cand = prefix | jnp.left_shift(jnp.int32(1), 31 - i)
            cnt = count(lambda k, t: k >= t, cand ^ _SIGN_BIT)
            return jnp.where(cnt >= n_keep, cand, prefix)
        prefix = lax.fori_loop(0, 32, bit_step, jnp.zeros((1, Q_BLOCK), jnp.int32))
        thr = prefix ^ _SIGN_BIT
        thr_ref[...] = thr
        need_ref[...] = n_keep - count(lambda k, t: k > t, thr)

    thr = thr_ref[...]
    need = need_ref[...].astype(f32)
    qT = qT_ref[0, 0]
    scale = (A_NOPE + A_ROPE) ** -0.5
    r_i = lax.broadcasted_iota(jnp.int32, (A_KCHUNK, A_KCHUNK), 0)
    c_i = lax.broadcasted_iota(jnp.int32, (A_KCHUNK, A_KCHUNK), 1)
    before = (c_i < r_i).astype(jnp.bfloat16)
    acc_ref[...] = jnp.zeros_like(acc_ref)

    def attend_chunk(c, carry):
        m, l, ties_seen = carry
        s = jnp.dot(kv_ref[0, c], qT, preferred_element_type=f32) * scale
        key = key_ref[c]
        tie = key == thr
        tie_f = tie.astype(f32)
        rank = jnp.dot(before, tie_f.astype(jnp.bfloat16), preferred_element_type=f32) + ties_seen
        keep = (key > thr) | (tie & (rank < need))
        bias = jnp.where(keep, 0.0, _MASKED)
        s = s + jnp.concatenate([bias] * A_HEADS, axis=1)
        m_new = jnp.maximum(m, s.max(0, keepdims=True))
        alpha = jnp.exp(m - m_new)
        p = jnp.exp(s - m_new)
        l = alpha * l + p.sum(0, keepdims=True)
        pv = jnp.dot(ckvT_ref[0, c], p.astype(ckvT_ref.dtype), preferred_element_type=f32)
        acc_ref[...] = acc_ref[...] * alpha + pv
        return m_new, l, ties_seen + tie_f.sum(0, keepdims=True)

    lanes = A_HEADS * Q_BLOCK
    init = (jnp.full((1, lanes), _MASKED, f32), jnp.zeros((1, lanes), f32), jnp.zeros((1, Q_BLOCK), f32))
    _, l, _ = lax.fori_loop(0, nk, attend_chunk, init)
    o_latT = acc_ref[...] / l
    for h in range(A_HEADS):
        o_lat = o_latT[:, h * Q_BLOCK:(h + 1) * Q_BLOCK].T.astype(wuv_ref.dtype)
        o_ref[0, :, h * A_V:(h + 1) * A_V] = jnp.dot(o_lat, wuv_ref[h], preferred_element_type=f32)


def _dsa_attention(iqT, iwT, qT, ikc, kvc, ckvT, wuv, *, n_keep):
    bsz, nq = qT.shape[:2]
    nc = ikc.shape[1]
    seq = nq * Q_BLOCK
    per_q = lambda b, q: (b, q, 0, 0)
    per_b = lambda b, q: (b, 0, 0, 0)
    return pl.pallas_call(
        functools.partial(_dsa_body, n_keep=n_keep),
        grid=(bsz, nq),
        in_specs=[pl.BlockSpec((1, 1) + iqT.shape[2:], per_q),
                  pl.BlockSpec((1, 1, iwT.shape[2]), lambda b, q: (b * nq + q, 0, 0)),
                  pl.BlockSpec((1, 1) + qT.shape[2:], per_q),
                  pl.BlockSpec((1,) + ikc.shape[1:], per_b),
                  pl.BlockSpec((1,) + kvc.shape[1:], per_b),
                  pl.BlockSpec((1,) + ckvT.shape[1:], per_b),
                  pl.BlockSpec(wuv.shape, lambda b, q: (0, 0, 0))],
        out_specs=pl.BlockSpec((1, Q_BLOCK, A_HEADS * A_V), lambda b, q: (b, q, 0)),
        out_shape=jax.ShapeDtypeStruct((bsz, seq, A_HEADS * A_V), jnp.float32),
        scratch_shapes=[pltpu.VMEM((nc, A_KCHUNK, Q_BLOCK), jnp.int32),
                        pltpu.VMEM((A_KV_RANK, A_HEADS * Q_BLOCK), jnp.float32),
                        pltpu.VMEM((1, Q_BLOCK), jnp.int32),
                        pltpu.VMEM((1, Q_BLOCK), jnp.int32)],
        compiler_params=pltpu.CompilerParams(
            dimension_semantics=("arbitrary", "arbitrary"), vmem_limit_bytes=VMEM_LIMIT),
        name="dsa_attention",
    )(iqT, iwT, qT, ikc, kvc, ckvT, wuv)


RET_TILE = 256


def _retention_body(q_ref, kT_ref, v_ref, g_ref, o_ref, s_ref):
    f32 = jnp.float32
    cdt = MXU_DTYPE
    c = RET_CHUNK

    @pl.when(pl.program_id(1) == 0)
    def _():
        s_ref[...] = jnp.zeros_like(s_ref)

    rel = (lax.broadcasted_iota(jnp.int32, (c, c), 0) - lax.broadcasted_iota(jnp.int32, (c, c), 1)).astype(f32)
    pos_c = lax.broadcasted_iota(jnp.int32, (c, 1), 0).astype(f32)
    pos_r = lax.broadcasted_iota(jnp.int32, (1, c), 1).astype(f32)

    def mm(a, b):
        return jnp.dot(a.astype(cdt), b.astype(cdt), preferred_element_type=f32)

    for h in range(B_HEADS):
        log_gamma = float(np.log(np.float32(1.0) - np.float32(2.0) ** np.float32(-5 - h)))
        intra = jnp.where(rel >= 0, jnp.exp(log_gamma * jnp.maximum(rel, 0.0)), 0.0)
        to_end = jnp.exp(log_gamma * (c - 1 - pos_r))
        from_start = jnp.exp(log_gamma * (pos_c + 1.0))
        chunk_decay = float(np.exp(np.float32(log_gamma) * np.float32(c)))
        lanes = slice(h * B_V, (h + 1) * B_V)
        for ci in range(RET_TILE // c):
            rows = slice(ci * c, (ci + 1) * c)
            q = q_ref[0, h, rows]
            kT = kT_ref[0, h, :, rows] * B_QK ** -0.5
            v = v_ref[0, rows, lanes]
            state = s_ref[h]
            o = mm(mm(q, kT) * intra, v) + mm(q * from_start, state)
            s_ref[h] = state * chunk_decay + mm(kT * to_end, v)
            gate = g_ref[0, rows, lanes]
            o = o * lax.rsqrt(jnp.mean(o * o, -1, keepdims=True) + RMS_EPS)
            o_ref[0, rows, lanes] = o * (gate * _sigmoid(gate))


def _retention_mixer(b_q, b_k, b_v, b_g, positions):
    bsz, seq = b_q.shape[:2]
    q = _rotary(b_q.reshape(bsz, seq, B_HEADS, B_QK), positions).transpose(0, 2, 1, 3)
    kT = _rotary(b_k.reshape(bsz, seq, B_HEADS, B_QK), positions).transpose(0, 2, 3, 1)
    width = B_HEADS * B_V
    tile = pl.BlockSpec((1, RET_TILE, width), lambda b, t: (b, t, 0))
    return pl.pallas_call(
        _retention_body,
        grid=(bsz, seq // RET_TILE),
        in_specs=[pl.BlockSpec((1, B_HEADS, RET_TILE, B_QK), lambda b, t: (b, 0, t, 0)),
                  pl.BlockSpec((1, B_HEADS, B_QK, RET_TILE), lambda b, t: (b, 0, 0, t)),
                  tile, tile],
        out_specs=tile,
        out_shape=jax.ShapeDtypeStruct((bsz, seq, width), jnp.float32),
        scratch_shapes=[pltpu.VMEM((B_HEADS, B_QK, B_V), jnp.float32)],
        compiler_params=pltpu.CompilerParams(
            dimension_semantics=("arbitrary", "arbitrary"), vmem_limit_bytes=VMEM_LIMIT),
        name="retention",
    )(q, kT, b_v, b_g)


def _diff_attention_mixer(c_q, c_k, c_v, positions, lam, subln_g, lambda_init):
    bsz, seq = c_q.shape[:2]
    q = _rotary(c_q.reshape(bsz, seq, C_HEADS * 2, C_DH), positions).reshape(bsz, seq, C_HEADS, 2, C_DH)
    k = _rotary(c_k.reshape(bsz, seq, C_HEADS * 2, C_DH), positions).reshape(bsz, seq, C_HEADS, 2, C_DH)
    v = c_v.reshape(bsz, seq, C_HEADS, 2 * C_DH)
    lam_full = jnp.exp(jnp.sum(lam[0] * lam[1])) - jnp.exp(jnp.sum(lam[2] * lam[3])) + lambda_init
    nq, nc = seq // Q_BLOCK, seq // C_KCHUNK
    cdt = MXU_DTYPE
    qs = (q * C_DH ** -0.5).reshape(bsz, nq, Q_BLOCK, C_HEADS, 2, C_DH).transpose(0, 1, 3, 4, 5, 2)
    zero = jnp.zeros_like(qs[:, :, :, 0])
    qbd = jnp.concatenate([jnp.concatenate([qs[:, :, :, 0], zero], -1),
                           jnp.concatenate([zero, qs[:, :, :, 1]], -1)], -2).astype(cdt)
    kc = k.reshape(bsz, nc, C_KCHUNK, C_HEADS, 2 * C_DH).transpose(0, 1, 3, 2, 4).astype(cdt)
    vT = v.reshape(bsz, nc, C_KCHUNK, C_HEADS, 2 * C_DH).transpose(0, 1, 3, 4, 2).astype(cdt)
    gain = (subln_g * (1.0 - lambda_init)).reshape(1, 2 * C_DH)
    return _diff_attention(lam_full.reshape(1), qbd, kc, vT, gain)


C_KCHUNK = 256


def _diff_attention_body(lam_ref, qbd_ref, k_ref, vT_ref, gain_ref, o_ref, acc_ref):
    f32 = jnp.float32
    qi = pl.program_id(1)
    nk = (qi * Q_BLOCK + Q_BLOCK + C_KCHUNK - 1) // C_KCHUNK
    qpos = qi * Q_BLOCK + lax.broadcasted_iota(jnp.int32, (C_KCHUNK, Q_BLOCK), 1)
    krow = lax.broadcasted_iota(jnp.int32, (C_KCHUNK, Q_BLOCK), 0)
    acc_ref[...] = jnp.zeros_like(acc_ref)

    def chunk(c, carry):
        ms, ls = carry
        causal = jnp.where(krow + c * C_KCHUNK <= qpos, 0.0, _MASKED)
        bias = jnp.concatenate([causal, causal], axis=1)
        new_ms, new_ls = [], []
        for h in range(C_HEADS):
            s = jnp.dot(k_ref[0, c, h], qbd_ref[0, 0, h], preferred_element_type=f32) + bias
            m_new = jnp.maximum(ms[h], s.max(0, keepdims=True))
            alpha = jnp.exp(ms[h] - m_new)
            p = jnp.exp(s - m_new)
            new_ls.append(alpha * ls[h] + p.sum(0, keepdims=True))
            new_ms.append(m_new)
            pv = jnp.dot(vT_ref[0, c, h], p.astype(vT_ref.dtype), preferred_element_type=f32)
            acc_ref[h] = acc_ref[h] * alpha + pv
        return tuple(new_ms), tuple(new_ls)

    lanes = 2 * Q_BLOCK
    init = (tuple(jnp.full((1, lanes), _MASKED, f32) for _ in range(C_HEADS)),
            tuple(jnp.zeros((1, lanes), f32) for _ in range(C_HEADS)))
    _, ls = lax.fori_loop(0, nk, chunk, init)
    lam = lam_ref[0]
    for h in range(C_HEADS):
        o = acc_ref[h] / ls[h]
        o = o[:, :Q_BLOCK] - lam * o[:, Q_BLOCK:]
        o = o * lax.rsqrt(jnp.mean(o * o, 0, keepdims=True) + RMS_EPS)
        o_ref[0, :, h * 2 * C_DH:(h + 1) * 2 * C_DH] = o.T * gain_ref[...]


def _diff_attention(lam, qbd, kc, vT, gain):
    bsz, nq = qbd.shape[:2]
    seq = nq * Q_BLOCK
    width = C_HEADS * 2 * C_DH
    return pl.pallas_call(
        _diff_attention_body,
        grid=(bsz, nq),
        in_specs=[pl.BlockSpec(memory_space=pltpu.SMEM),
                  pl.BlockSpec((1, 1) + qbd.shape[2:], lambda b, q: (b, q, 0, 0, 0)),
                  pl.BlockSpec((1,) + kc.shape[1:], lambda b, q: (b, 0, 0, 0, 0)),
                  pl.BlockSpec((1,) + vT.shape[1:], lambda b, q: (b, 0, 0, 0, 0)),
                  pl.BlockSpec(gain.shape, lambda b, q: (0, 0))],
        out_specs=pl.BlockSpec((1, Q_BLOCK, width), lambda b, q: (b, q, 0)),
        out_shape=jax.ShapeDtypeStruct((bsz, seq, width), jnp.float32),
        scratch_shapes=[pltpu.VMEM((C_HEADS, 2 * C_DH, 2 * Q_BLOCK), jnp.float32)],
        compiler_params=pltpu.CompilerParams(
            dimension_semantics=("arbitrary", "arbitrary"), vmem_limit_bytes=VMEM_LIMIT),
        name="diff_attention",
    )(lam, qbd, kc, vT, gain)


def _causal_depthwise_conv(t, w):
    return lax.conv_general_dilated(t, w[:, None, :], window_strides=(1,), padding=((CONV_K - 1, 0),),
                                    dimension_numbers=('NWC', 'WIO', 'NWC'), feature_group_count=t.shape[-1])


def _chunk_gated_delta_rule(q, k, v, g, beta):
    f32 = jnp.float32
    bsz = q.shape[0]
    c = GDN_CHUNK
    qc, kc, vc = (_to_chunks(t, c) for t in (q, k, v))
    gc, bc = _to_chunks(g, c), _to_chunks(beta, c)
    G = jnp.cumsum(gc, axis=-1)
    idx = jnp.arange(c)
    tril = idx[:, None] >= idx[None, :]
    strict = idx[:, None] > idx[None, :]
    diffG = G[..., :, None] - G[..., None, :]
    gam = jnp.where(tril, jnp.exp(jnp.where(tril, diffG, 0.0)), 0.0)
    kb = kc * bc[..., None]
    lower = jnp.where(strict, jnp.einsum('bnhid,bnhjd->bnhij', kb, kc) * gam, 0.0)
    rhs = jnp.concatenate([vc * bc[..., None], kb * jnp.exp(G)[..., None]], -1)
    sol = lax.linalg.triangular_solve(lower + jnp.eye(c, dtype=f32), rhs, left_side=True, lower=True,
                                      unit_diagonal=True)
    u, w = sol[..., :D_V], sol[..., D_V:]
    attn = jnp.where(tril, jnp.einsum('bnhid,bnhjd->bnhij', qc, kc) * gam, 0.0)
    qg = qc * jnp.exp(G)[..., None]
    kd = kc * jnp.exp(G[..., -1:] - G)[..., None]
    decay_last = jnp.exp(G[..., -1])

    def step(state, xs):
        qg_n, w_n, u_n, a_n, kd_n, dl_n = xs
        v_new = u_n - jnp.einsum('bhcd,bhde->bhce', w_n, state)
        o_n = jnp.einsum('bhcd,bhde->bhce', qg_n, state) + jnp.einsum('bhij,bhje->bhie', a_n, v_new)
        state = state * dl_n[..., None, None] + jnp.einsum('bhcd,bhce->bhde', kd_n, v_new)
        return state, o_n

    xs = tuple(jnp.moveaxis(t, 1, 0) for t in (qg, w, u, attn, kd, decay_last))
    _, o = lax.scan(step, jnp.zeros((bsz, D_HEADS, D_K, D_V), f32), xs)
    return _from_chunks(jnp.moveaxis(o, 0, 1))


GDN_TILE = 256
GDN_SUB = 16
SUBLANE = 8


def _sigmoid(t):
    return 1.0 / (1.0 + jnp.exp(-t))


def _softplus(t):
    return jnp.maximum(t, 0.0) + jnp.log1p(jnp.exp(-jnp.abs(t)))


def _gdn_body(qkv_ref, z_ref, gate_ref, gateT_ref, conv_ref, alog_r_ref, dtb_r_ref, alog_c_ref, dtb_c_ref,
              ng_ref, o_ref, halo_ref, s_ref):
    f32 = jnp.float32
    cdt = MXU_DTYPE
    c = GDN_CHUNK

    @pl.when(pl.program_id(1) == 0)
    def _():
        halo_ref[...] = jnp.zeros_like(halo_ref)
        s_ref[...] = jnp.zeros_like(s_ref)

    x = qkv_ref[0]
    xp = jnp.concatenate([halo_ref[...], x], 0)
    w = conv_ref[...]
    pre = xp[SUBLANE - CONV_K + 1:SUBLANE - CONV_K + 1 + GDN_TILE] * w[0:1]
    for j in range(1, CONV_K):
        off = SUBLANE - CONV_K + 1 + j
        pre = pre + xp[off:off + GDN_TILE] * w[j:j + 1]
    halo_ref[...] = x[GDN_TILE - SUBLANE:]
    qkv = pre * _sigmoid(pre)

    gates = gate_ref[0]
    beta_c = _sigmoid(gates[:, :D_HEADS])
    g_c = -jnp.exp(alog_r_ref[...]) * _softplus(gates[:, D_HEADS:] + dtb_r_ref[...])
    g_r = -jnp.exp(alog_c_ref[...]) * _softplus(gateT_ref[0][D_HEADS:] + dtb_c_ref[...])

    row = lax.broadcasted_iota(jnp.int32, (c, c), 0)
    col = lax.broadcasted_iota(jnp.int32, (c, c), 1)
    tril = row >= col
    strict = row > col
    same_sub = (row // GDN_SUB) == (col // GDN_SUB)
    eye = (row == col).astype(f32)
    lower_ones = tril.astype(f32)
    upper_ones = (row <= col).astype(f32)

    def mm(a, b):
        return jnp.dot(a, b, preferred_element_type=f32)

    for ci in range(GDN_TILE // c):
        rows = slice(ci * c, (ci + 1) * c)
        g_cum_c = mm(lower_ones, g_c[rows])
        g_cum_r = mm(g_r[:, rows], upper_ones)
        for h in range(D_HEADS):
            lanes = slice(h * D_K, (h + 1) * D_K)
            q = qkv[rows, h * D_K:(h + 1) * D_K]
            k = qkv[rows, D_HEADS * D_K + h * D_K:D_HEADS * D_K + (h + 1) * D_K]
            v = qkv[rows, 2 * D_HEADS * D_K + h * D_V:2 * D_HEADS * D_K + (h + 1) * D_V]
            q = q * lax.rsqrt(jnp.sum(q * q, -1, keepdims=True) + RMS_EPS) * D_K ** -0.5
            k = k * lax.rsqrt(jnp.sum(k * k, -1, keepdims=True) + RMS_EPS)
            gc = g_cum_c[:, h:h + 1]
            gr = g_cum_r[h:h + 1, :]
            g_last = gc[c - 1:c]
            gam = jnp.where(tril, jnp.exp(jnp.where(tril, gc - gr, 0.0)), 0.0)
            e_g = jnp.exp(gc)
            beta = beta_c[rows, h:h + 1]
            kT = k.T
            kb = k * beta
            low = jnp.where(strict, mm(kb.astype(cdt), kT.astype(cdt)) * gam, 0.0)
            attn = jnp.where(tril, mm(q.astype(cdt), kT.astype(cdt)) * gam, 0.0)
            low_d = jnp.where(same_sub, low, 0.0)
            xk = -low_d
            td = eye + xk
            for _ in range(3):
                xk = mm(xk, xk)
                td = td + mm(td, xk)
            m1 = mm(td, low - low_d)
            m2 = mm(m1, m1)
            sol = mm(td, jnp.concatenate([v * beta, kb * e_g], 1))
            sol = sol + mm(m2, sol)
            sol = sol - mm(m1, sol)
            u, wy = sol[:, :D_V], sol[:, D_V:]
            state = s_ref[h]
            state_c = state.astype(cdt)
            v_new = u - mm(wy.astype(cdt), state_c)
            v_new_c = v_new.astype(cdt)
            o = mm((q * e_g).astype(cdt), state_c) + mm(attn.astype(cdt), v_new_c)
            kdT = kT * jnp.exp(g_last - gr)
            s_ref[h] = state * jnp.exp(g_last) + mm(kdT.astype(cdt), v_new_c)
            zz = z_ref[0, rows, lanes]
            o = o * lax.rsqrt(jnp.mean(o * o, -1, keepdims=True) + RMS_EPS) * ng_ref[...]
            o_ref[0, rows, lanes] = o * (zz * _sigmoid(zz))


def _gated_deltanet_mixer(d_qkv, d_z, d_b, d_a, conv_w, a_log, dt_bias, norm_g):
    bsz, seq, width = d_qkv.shape
    gates = jnp.concatenate([d_b, d_a], -1)
    gatesT = gates.transpose(0, 2, 1)
    tile = lambda last: pl.BlockSpec((1, GDN_TILE, last), lambda b, t: (b, t, 0))
    whole = lambda r, cc: pl.BlockSpec((r, cc), lambda b, t: (0, 0))
    return pl.pallas_call(
        _gdn_body,
        grid=(bsz, seq // GDN_TILE),
        in_specs=[tile(width), tile(D_HEADS * D_V), tile(2 * D_HEADS),
                  pl.BlockSpec((1, 2 * D_HEADS, GDN_TILE), lambda b, t: (b, 0, t)),
                  whole(CONV_K, width), whole(1, D_HEADS), whole(1, D_HEADS), whole(D_HEADS, 1), whole(D_HEADS, 1),
                  whole(1, D_V)],
        out_specs=tile(D_HEADS * D_V),
        out_shape=jax.ShapeDtypeStruct((bsz, seq, D_HEADS * D_V), jnp.float32),
        scratch_shapes=[pltpu.VMEM((SUBLANE, width), jnp.float32),
                        pltpu.VMEM((D_HEADS, D_K, D_V), jnp.float32)],
        compiler_params=pltpu.CompilerParams(
            dimension_semantics=("arbitrary", "arbitrary"), vmem_limit_bytes=VMEM_LIMIT),
        name="gated_deltanet",
    )(d_qkv, d_z, gates, gatesT, conv_w, a_log.reshape(1, D_HEADS), dt_bias.reshape(1, D_HEADS),
      a_log.reshape(D_HEADS, 1), dt_bias.reshape(D_HEADS, 1), norm_g.reshape(1, D_V))


def _clamped_swiglu(hg):
    glu, lin = hg[..., :D_FF], hg[..., D_FF:]
    glu = jnp.minimum(glu, SWIGLU_LIMIT)
    lin = jnp.clip(lin, -SWIGLU_LIMIT, SWIGLU_LIMIT)
    return glu * jax.nn.sigmoid(SWIGLU_ALPHA * glu) * (lin + 1.0)


def _moe_plan(top_e):
    n_tok = top_e.shape[0]
    n_blocks = -(-n_tok * TOP_K // MOE_BLOCK) + N_EXPERTS
    member = (top_e[:, :, None] == jnp.arange(N_EXPERTS, dtype=jnp.int32)).astype(jnp.int32).sum(1)
    rank = jnp.cumsum(member, axis=0) - member
    counts = member.sum(0)
    padded = (counts + MOE_BLOCK - 1) // MOE_BLOCK * MOE_BLOCK
    pad_end = jnp.cumsum(padded)
    dest = jnp.take_along_axis(rank + (pad_end - padded)[None, :], top_e, axis=1).astype(jnp.int32)
    tok = jnp.broadcast_to(jnp.arange(n_tok, dtype=jnp.int32)[:, None], dest.shape)
    row_tok = jnp.zeros((n_blocks * MOE_BLOCK,), jnp.int32).at[dest.reshape(-1)].set(tok.reshape(-1))
    block_e = jnp.minimum(jnp.searchsorted(pad_end, jnp.arange(n_blocks) * MOE_BLOCK, side='right'),
                          N_EXPERTS - 1).astype(jnp.int32)
    n_used = (pad_end[-1:] // MOE_BLOCK).astype(jnp.int32)
    return dest, row_tok, block_e, n_used


def _row_gather_start(src_hbm, rows_smem, n_rows, dst, sem):
    def body(r, carry):
        pltpu.make_async_copy(src_hbm.at[pl.ds(rows_smem[0, 0, r], 1)], dst.at[pl.ds(r, 1)], sem).start()
        return carry
    lax.fori_loop(0, n_rows, body, 0, unroll=8)


def _row_gather_wait(src_hbm, n_rows, dst, sem):
    pltpu.make_async_copy(src_hbm.at[pl.ds(0, n_rows)], dst, sem).wait()


def _moe_expert_body(be_ref, nused_ref, tok_ref, tok_next_ref, h_hbm, wgu_ref, bgu_ref, wdn_ref, bdn_ref,
                     y_ref, buf, sem):
    f32 = jnp.float32
    i = pl.program_id(0)
    n_used = nused_ref[0]
    slot = i % 2

    @pl.when(jnp.logical_and(i == 0, n_used > 0))
    def _():
        _row_gather_start(h_hbm, tok_ref, MOE_BLOCK, buf.at[0], sem.at[0])

    @pl.when(i + 1 < n_used)
    def _():
        _row_gather_start(h_hbm, tok_next_ref, MOE_BLOCK, buf.at[1 - slot], sem.at[1 - slot])

    @pl.when(i < n_used)
    def _():
        _row_gather_wait(h_hbm, MOE_BLOCK, buf.at[slot], sem.at[slot])
        xb = buf[slot].astype(MXU_DTYPE)
        hg = jnp.dot(xb, wgu_ref[0], preferred_element_type=f32) + bgu_ref[0]
        act = _clamped_swiglu(hg)
        y_ref[...] = jnp.dot(act.astype(MXU_DTYPE), wdn_ref[0], preferred_element_type=f32) + bdn_ref[0]

    @pl.when(i >= n_used)
    def _():
        y_ref[...] = jnp.zeros_like(y_ref)


def _moe_experts(h2d, row_tok, block_e, n_used, w_gu_c, b_gu, w_dn_c, b_dn):
    n_tok, d = h2d.shape
    n_blocks = block_e.shape[0]
    tok3 = row_tok.reshape(n_blocks, 1, MOE_BLOCK)
    smem_blk = lambda imap: pl.BlockSpec((1, 1, MOE_BLOCK), imap, memory_space=pltpu.SMEM)
    by_expert = lambda shape: pl.BlockSpec((1,) + shape, lambda i, be, nu: (be[i], 0, 0))
    return pl.pallas_call(
        _moe_expert_body,
        grid_spec=pltpu.PrefetchScalarGridSpec(
            num_scalar_prefetch=2,
            grid=(n_blocks,),
            in_specs=[smem_blk(lambda i, be, nu: (i, 0, 0)),
                      smem_blk(lambda i, be, nu: (jnp.minimum(i + 1, n_blocks - 1), 0, 0)),
                      pl.BlockSpec(memory_space=pl.ANY),
                      by_expert((d, 2 * D_FF)), by_expert((1, 2 * D_FF)),
                      by_expert((D_FF, d)), by_expert((1, d))],
            out_specs=pl.BlockSpec((MOE_BLOCK, d), lambda i, be, nu: (i, 0)),
            scratch_shapes=[pltpu.VMEM((2, MOE_BLOCK, d), jnp.float32), pltpu.SemaphoreType.DMA((2,))]),
        out_shape=jax.ShapeDtypeStruct((n_blocks * MOE_BLOCK, d), jnp.float32),
        compiler_params=pltpu.CompilerParams(
            dimension_semantics=("arbitrary",), vmem_limit_bytes=VMEM_LIMIT),
        name="moe_experts",
    )(block_e, n_used, tok3, tok3, h2d, w_gu_c, b_gu.reshape(N_EXPERTS, 1, 2 * D_FF),
      w_dn_c, b_dn.reshape(N_EXPERTS, 1, d))


MOE_COMBINE_ROWS = 128


def _moe_combine_ln_body(dest_ref, dest_next_ref, h_ref, tg_ref, g_ref, b_ref, y_hbm, o_ref, buf, sem):
    tm = MOE_COMBINE_ROWS
    i = pl.program_id(0)
    n = pl.num_programs(0)
    slot = i % 2

    def start(rows_smem, s):
        for k in range(TOP_K):
            def body(r, carry):
                pltpu.make_async_copy(y_hbm.at[pl.ds(rows_smem[0, 0, k * tm + r], 1)],
                                      buf.at[s, k, pl.ds(r, 1)], sem.at[s]).start()
                return carry
            lax.fori_loop(0, tm, body, 0, unroll=8)

    @pl.when(i == 0)
    def _():
        start(dest_ref, 0)

    @pl.when(i + 1 < n)
    def _():
        start(dest_next_ref, 1 - slot)

    tg = tg_ref[...]
    y = jnp.zeros(h_ref.shape, jnp.float32)
    for k in range(TOP_K):
        _row_gather_wait(y_hbm, tm, buf.at[slot, k], sem.at[slot])
    for k in range(TOP_K):
        y = y + tg[:, k:k + 1] * buf[slot, k]
    o_ref[...] = _ln_rows(DEEPNORM_ALPHA * h_ref[...] + y, g_ref[...], b_ref[...])


def _moe_combine_ln(h2d, y_rows, dest, tg, g, b):
    n_tok, d = h2d.shape
    tm = MOE_COMBINE_ROWS
    nt = n_tok // tm
    dest3 = dest.reshape(nt, tm, TOP_K).transpose(0, 2, 1).reshape(nt, 1, TOP_K * tm)
    smem_blk = lambda imap: pl.BlockSpec((1, 1, TOP_K * tm), imap, memory_space=pltpu.SMEM)
    return pl.pallas_call(
        _moe_combine_ln_body,
        grid=(nt,),
        in_specs=[smem_blk(lambda i: (i, 0, 0)),
                  smem_blk(lambda i: (jnp.minimum(i + 1, nt - 1), 0, 0)),
                  pl.BlockSpec((tm, d), lambda i: (i, 0)),
                  pl.BlockSpec((tm, LANE), lambda i: (i, 0)),
                  pl.BlockSpec((1, d), lambda i: (0, 0)),
                  pl.BlockSpec((1, d), lambda i: (0, 0)),
                  pl.BlockSpec(memory_space=pl.ANY)],
        out_specs=pl.BlockSpec((tm, d), lambda i: (i, 0)),
        out_shape=jax.ShapeDtypeStruct((n_tok, d), jnp.float32),
        scratch_shapes=[pltpu.VMEM((2, TOP_K, tm, d), jnp.float32), pltpu.SemaphoreType.DMA((2,))],
        compiler_params=pltpu.CompilerParams(
            dimension_semantics=("arbitrary",), vmem_limit_bytes=VMEM_LIMIT),
        name="moe_combine_ln",
    )(dest3, dest3, h2d, tg, g.reshape(1, d), b.reshape(1, d), y_rows)


def _moe_layer(h2d, te, tg, w_gu, b_gu, w_dn, b_dn, g, b):
    dest, row_tok, block_e, n_used = _moe_plan(te[:, :TOP_K])
    y_rows = _moe_experts(h2d, row_tok, block_e, n_used, w_gu.astype(MXU_DTYPE), b_gu, w_dn.astype(MXU_DTYPE), b_dn)
    return _moe_combine_ln(h2d, y_rows, dest, tg, g, b)


def kernel(x, positions, w_in, w_out, a_kv_norm, a_w_uk, a_w_uv, c_lambda, c_subln, d_conv, d_a_log, d_dt_bias, d_norm, ln1_g, ln1_b, w_router, b_router, w_gu, b_gu, w_dn, b_dn, ln2_g, ln2_b):
    bsz, seq, d = x.shape
    x2d = x.reshape(bsz * seq, d)
    for l in range(DEPTH):
        proj = _in_proj(x2d, _pack_w_in(w_in[l])).reshape(bsz, seq, IN_PACKED)
        (a_q, a_ckv, a_kr, a_iq, a_ik, a_iw, b_q, b_k, b_v, b_g, c_q, c_k, c_v,
         d_qkv, d_z, d_b, d_a) = _unpack_proj(proj)
        lambda_init = 0.8 - 0.6 * math.exp(-0.3 * l)
        heads = jnp.concatenate([
            _dsa_mla_mixer(a_q, a_ckv, a_kr, a_iq, a_ik, a_iw, positions, a_kv_norm[l], a_w_uk[l], a_w_uv[l]),
            _retention_mixer(b_q, b_k, b_v, b_g, positions),
            _diff_attention_mixer(c_q, c_k, c_v, positions, c_lambda[l], c_subln[l], lambda_init),
            _gated_deltanet_mixer(d_qkv, d_z, d_b, d_a, d_conv[l], d_a_log[l], d_dt_bias[l], d_norm[l]),
        ], axis=-1)
        h2d, te, tg = _out_ln_router(heads.reshape(bsz * seq, D_MIX), w_out[l].astype(MXU_DTYPE), x2d,
                                     ln1_g[l], ln1_b[l], w_router[l], b_router[l])
        x2d = _moe_layer(h2d, te, tg, w_gu[l], b_gu[l], w_dn[l], b_dn[l], ln2_g[l], ln2_b[l])
    return x2d.reshape(bsz, seq, d)
```

```python
import functools
import math

import jax
import jax.numpy as jnp
import numpy as np
from jax import lax
from jax.experimental import pallas as pl
from jax.experimental.pallas import tpu as pltpu

D_MODEL = 2048
DEPTH = 4
A_HEADS, A_NOPE, A_ROPE, A_V, A_KV_RANK = 4, 128, 64, 128, 256
IDX_HEADS, IDX_DIM, INDEX_TOPK = 8, 64, 256
B_HEADS, B_QK, B_V, RET_CHUNK = 4, 64, 128, 128
C_HEADS, C_DH = 4, 64
D_HEADS, D_K, D_V, CONV_K, GDN_CHUNK = 4, 128, 128, 4, 64
N_EXPERTS, TOP_K, D_FF = 32, 4, 512
SWIGLU_ALPHA, SWIGLU_LIMIT, MOE_BLOCK = 1.702, 7.0, 256
ROPE_THETA = 10000.0
ROPE_DIM = 64
Q_BLOCK = 128
DEEPNORM_ALPHA = (2 * DEPTH) ** 0.25
LN_EPS = 1e-5
RMS_EPS = 1e-6
D_MIX = A_HEADS * A_V + B_HEADS * B_V + C_HEADS * 2 * C_DH + D_HEADS * D_V
A_LAT = A_KV_RANK + A_ROPE

IN_WIDTHS = (
    A_HEADS * (A_NOPE + A_ROPE), A_KV_RANK, A_ROPE, IDX_HEADS * IDX_DIM, IDX_DIM, IDX_HEADS,
    B_HEADS * B_QK, B_HEADS * B_QK, B_HEADS * B_V, B_HEADS * B_V,
    C_HEADS * 2 * C_DH, C_HEADS * 2 * C_DH, C_HEADS * 2 * C_DH,
    D_HEADS * (2 * D_K + D_V), D_HEADS * D_V, D_HEADS, D_HEADS,
)
_IN_OFFS = np.concatenate([[0], np.cumsum(IN_WIDTHS)]).tolist()

LANE = 128
SUBLANE = 8
VMEM_LIMIT = 56 * 1024 * 1024

MXU_DTYPE = jnp.bfloat16
_MASKED = -1e30

COL = 512
COL_D_QKV = 0
COL_D_Z = 3
COL_B_V = 4
COL_B_G = 5
COL_B_QK = 6
COL_C_Q = 7
COL_C_K = 8
COL_C_V = 9
COL_A_CKV_ROPE = 10
COL_A_NOPE = 11
COL_A_IQ = 12
NARROW = 256
COL_NARROW = 26
IN_PACKED = (COL_NARROW + 1) * NARROW
NARROW_IW = 0
NARROW_DB = IDX_HEADS
NARROW_DA = IDX_HEADS + D_HEADS


def _pack_w_in(w):
    g = lambda i: w[:, _IN_OFFS[i]:_IN_OFFS[i + 1]]
    a_q = g(0).reshape(w.shape[0], A_HEADS, A_NOPE + A_ROPE)
    nope = a_q[:, :, :A_NOPE].reshape(w.shape[0], A_HEADS * A_NOPE)
    rope = a_q[:, :, A_NOPE:].reshape(w.shape[0], A_HEADS * A_ROPE)
    cols = [g(13), g(14), g(8), g(9), g(6), g(7), g(10), g(11), g(12), g(1), rope, nope, g(3),
            g(2), g(4), g(5), g(15), g(16)]
    used = sum(c.shape[1] for c in cols)
    cols.append(jnp.zeros((w.shape[0], IN_PACKED - used), w.dtype))
    return jnp.concatenate(cols, axis=1).astype(MXU_DTYPE)


def _sigmoid(t):
    return 1.0 / (1.0 + jnp.exp(-t))


def _softplus(t):
    return jnp.maximum(t, 0.0) + jnp.log1p(jnp.exp(-jnp.abs(t)))


def _params(*sem):
    return pltpu.CompilerParams(dimension_semantics=sem, vmem_limit_bytes=VMEM_LIMIT)


def _in_proj_body(x_ref, w_ref, o_ref, xb_ref):
    @pl.when(pl.program_id(1) == 0)
    def _():
        xb_ref[...] = x_ref[...].astype(MXU_DTYPE)

    o_ref[...] = jnp.dot(xb_ref[...], w_ref[...], preferred_element_type=jnp.float32)


def _in_proj(x2d, w_packed, tm=1024, tn=768):
    m, k = x2d.shape
    n = w_packed.shape[1]
    return pl.pallas_call(
        _in_proj_body,
        grid=(m // tm, n // tn),
        in_specs=[pl.BlockSpec((tm, k), lambda i, j: (i, 0)),
                  pl.BlockSpec((k, tn), lambda i, j: (0, j))],
        out_specs=pl.BlockSpec((tm, tn), lambda i, j: (i, j)),
        out_shape=jax.ShapeDtypeStruct((m, n), jnp.float32),
        scratch_shapes=[pltpu.VMEM((tm, k), MXU_DTYPE)],
        compiler_params=_params("arbitrary", "arbitrary"),
        name="in_proj",
    )(x2d, w_packed)


PREP_TILE = 256
KCHUNK = 256


def _prep_body(bqk_ref, cq_ref, ck_ref, cv_ref, ckvrope_ref, nope_ref, iq_ref, nar_ref, cos_ref, sin_ref,
               kvn_ref, wukT_ref,
               qT_ref, iqT_ref, iw_ref, ik_ref, kv_ref, ckvT_ref, qbd_ref, kc_ref, vT_ref, rq_ref, rkT_ref, gT_ref):
    f32 = jnp.float32
    cdt = MXU_DTYPE
    cos, sin = cos_ref[0], sin_ref[0]
    lane = lax.broadcasted_iota(jnp.int32, (PREP_TILE, LANE), 1)
    first_half = (lane % ROPE_DIM) < ROPE_DIM // 2
    eye = (lax.broadcasted_iota(jnp.int32, (LANE, LANE), 0)
           == lax.broadcasted_iota(jnp.int32, (LANE, LANE), 1)).astype(cdt)
    top = lax.broadcasted_iota(jnp.int32, (LANE, LANE), 0) < LANE // 2
    top_wide = lax.broadcasted_iota(jnp.int32, (LANE, PREP_TILE), 0) < LANE // 2

    def rope(t):
        outs = []
        for j in range(t.shape[1] // LANE):
            ts = t[:, j * LANE:(j + 1) * LANE]
            partner = jnp.where(first_half, pltpu.roll(ts, LANE - ROPE_DIM // 2, 1), pltpu.roll(ts, ROPE_DIM // 2, 1))
            outs.append(ts * cos + partner * sin)
        return outs[0] if len(outs) == 1 else jnp.concatenate(outs, axis=1)

    def tr(t):
        return lax.dot_general(eye, t.astype(cdt), (((1,), (1,)), ((), ())), preferred_element_type=f32)

    ckv_rope = ckvrope_ref[...]
    a_ckv = ckv_rope[:, :A_KV_RANK]
    c_kv = a_ckv * lax.rsqrt(jnp.mean(a_ckv * a_ckv, -1, keepdims=True) + RMS_EPS) * kvn_ref[...]
    nar = nar_ref[...]
    kr_ik = rope(nar[:, :LANE])
    kv_ref[0, 0, :, :A_KV_RANK] = c_kv.astype(cdt)
    kv_ref[0, 0, :, A_KV_RANK:] = kr_ik[:, :A_ROPE].astype(cdt)
    ik_ref[0, 0] = kr_ik[:, A_ROPE:].astype(cdt)
    for rb in range(A_KV_RANK // LANE):
        ckvT_ref[0, 0, rb * LANE:(rb + 1) * LANE, :] = tr(c_kv[:, rb * LANE:(rb + 1) * LANE]).astype(cdt)
    q_rope = rope(ckv_rope[:, A_KV_RANK:])
    nope = nope_ref[...]
    iq = rope(iq_ref[...])
    narT = nar[:, LANE:].T
    gT_ref[0] = narT[NARROW_DB:NARROW_DB + 2 * D_HEADS]
    cq = rope(cq_ref[...]) * C_DH ** -0.5
    for j in range(PREP_TILE // Q_BLOCK):
        rows = slice(j * Q_BLOCK, (j + 1) * Q_BLOCK)
        for h in range(A_HEADS):
            q_latT = lax.dot_general(wukT_ref[h], nope[rows, h * A_NOPE:(h + 1) * A_NOPE].astype(cdt),
                                     (((1,), (1,)), ((), ())), preferred_element_type=f32)
            qT_ref[0, j, :A_KV_RANK, h * Q_BLOCK:(h + 1) * Q_BLOCK] = q_latT.astype(cdt)
        for p in range(A_HEADS * A_ROPE // LANE):
            t = tr(q_rope[rows, p * LANE:(p + 1) * LANE])
            qT_ref[0, j, A_KV_RANK:, (2 * p) * Q_BLOCK:(2 * p + 1) * Q_BLOCK] = t[:A_ROPE].astype(cdt)
            qT_ref[0, j, A_KV_RANK:, (2 * p + 1) * Q_BLOCK:(2 * p + 2) * Q_BLOCK] = t[A_ROPE:].astype(cdt)
        for p in range(IDX_HEADS * IDX_DIM // LANE):
            t = tr(iq[rows, p * LANE:(p + 1) * LANE])
            iqT_ref[0, j, :, (2 * p) * Q_BLOCK:(2 * p + 1) * Q_BLOCK] = t[:IDX_DIM].astype(cdt)
            iqT_ref[0, j, :, (2 * p + 1) * Q_BLOCK:(2 * p + 2) * Q_BLOCK] = t[IDX_DIM:].astype(cdt)
        iw_ref[0, j] = narT[NARROW_IW:NARROW_IW + IDX_HEADS, rows] * (IDX_HEADS * IDX_DIM) ** -0.5
        for h in range(C_HEADS):
            t = tr(cq[rows, h * LANE:(h + 1) * LANE])
            qbd_ref[0, j, h, :, :Q_BLOCK] = jnp.where(top, t, 0.0).astype(cdt)
            qbd_ref[0, j, h, :, Q_BLOCK:] = jnp.where(top, 0.0, t).astype(cdt)
    ck = rope(ck_ref[...])
    cv = cv_ref[...]
    for h in range(C_HEADS):
        kc_ref[0, 0, h] = ck[:, h * LANE:(h + 1) * LANE].astype(cdt)
        vT_ref[0, 0, h] = tr(cv[:, h * LANE:(h + 1) * LANE]).astype(cdt)
    bqk = rope(bqk_ref[...])
    rq_ref[0] = bqk[:, :B_HEADS * B_QK].astype(cdt)
    for p in range(B_HEADS * B_QK // LANE):
        t = tr(bqk[:, B_HEADS * B_QK + p * LANE:B_HEADS * B_QK + (p + 1) * LANE]) * B_QK ** -0.5
        rkT_ref[0, 2 * p] = jnp.where(top_wide, t, 0.0).astype(cdt)
        rkT_ref[0, 2 * p + 1] = jnp.where(top_wide, 0.0, t).astype(cdt)


def _mixer_prep(proj, cos, sin, kv_norm, wukT, bsz, seq):
    nt = seq // PREP_TILE
    nq = seq // Q_BLOCK
    cdt = MXU_DTYPE
    col = lambda width, c: pl.BlockSpec((PREP_TILE, width), lambda b, t: (b * nt + t, c))
    table = pl.BlockSpec((1, PREP_TILE, LANE), lambda b, t: (b, t, 0))
    per_q = lambda *tail: pl.BlockSpec((1, PREP_TILE // Q_BLOCK) + tail, lambda b, t: (b, t) + (0,) * len(tail))
    per_c = lambda *tail: pl.BlockSpec((1, 1) + tail, lambda b, t: (b, t) + (0,) * len(tail))
    sds = jax.ShapeDtypeStruct
    outs = [
        (sds((bsz, nq, A_LAT, A_HEADS * Q_BLOCK), cdt), per_q(A_LAT, A_HEADS * Q_BLOCK)),
        (sds((bsz, nq, IDX_DIM, IDX_HEADS * Q_BLOCK), cdt), per_q(IDX_DIM, IDX_HEADS * Q_BLOCK)),
        (sds((bsz, nq, IDX_HEADS, Q_BLOCK), jnp.float32), per_q(IDX_HEADS, Q_BLOCK)),
        (sds((bsz, nt, KCHUNK, IDX_DIM), cdt), per_c(KCHUNK, IDX_DIM)),
        (sds((bsz, nt, KCHUNK, A_LAT), cdt), per_c(KCHUNK, A_LAT)),
        (sds((bsz, nt, A_KV_RANK, KCHUNK), cdt), per_c(A_KV_RANK, KCHUNK)),
        (sds((bsz, nq, C_HEADS, 2 * C_DH, 2 * Q_BLOCK), cdt), per_q(C_HEADS, 2 * C_DH, 2 * Q_BLOCK)),
        (sds((bsz, nt, C_HEADS, KCHUNK, 2 * C_DH), cdt), per_c(C_HEADS, KCHUNK, 2 * C_DH)),
        (sds((bsz, nt, C_HEADS, 2 * C_DH, KCHUNK), cdt), per_c(C_HEADS, 2 * C_DH, KCHUNK)),
        (sds((bsz, seq, B_HEADS * B_QK), cdt), pl.BlockSpec((1, PREP_TILE, B_HEADS * B_QK), lambda b, t: (b, t, 0))),
        (sds((bsz, B_HEADS, LANE, seq), cdt), pl.BlockSpec((1, B_HEADS, LANE, PREP_TILE), lambda b, t: (b, 0, 0, t))),
        (sds((bsz, 2 * D_HEADS, seq), jnp.float32), pl.BlockSpec((1, 2 * D_HEADS, PREP_TILE), lambda b, t: (b, 0, t))),
    ]
    return pl.pallas_call(
        _prep_body,
        grid=(bsz, nt),
        in_specs=[col(COL, COL_B_QK), col(COL, COL_C_Q), col(COL, COL_C_K), col(COL, COL_C_V),
                  col(COL, COL_A_CKV_ROPE), col(COL, COL_A_NOPE), col(COL, COL_A_IQ), col(NARROW, COL_NARROW),
                  table, table,
                  pl.BlockSpec((1, A_KV_RANK), lambda b, t: (0, 0)),
                  pl.BlockSpec((A_HEADS, A_KV_RANK, A_NOPE), lambda b, t: (0, 0, 0))],
        out_specs=[o[1] for o in outs],
        out_shape=[o[0] for o in outs],
        compiler_params=_params("arbitrary", "arbitrary"),
        name="mixer_prep",
    )(proj, proj, proj, proj, proj, proj, proj, proj, cos, sin, kv_norm.reshape(1, A_KV_RANK), wukT)


def _rope_tables(positions):
    half = ROPE_DIM // 2
    inv_freq = ROPE_THETA ** (-jnp.arange(half, dtype=jnp.float32) / half)
    ang = positions.astype(jnp.float32)[:, :, None] * inv_freq
    cos, sin = jnp.cos(ang), jnp.sin(ang)
    reps = LANE // ROPE_DIM
    return (jnp.concatenate([cos, cos] * reps, -1), jnp.concatenate([-sin, sin] * reps, -1))


_SIGN_BIT = np.int32(-2 ** 31)
_LOW31 = np.int32(2 ** 31 - 1)
_KEY_NEG_INF = np.int32(np.array(-np.inf, np.float32).view(np.int32) ^ _LOW31)


def _dsa_body(iqT_ref, iw_ref, qT_ref, ik_ref, kv_ref, ckvT_ref, wuv_ref, o_ref,
              key_ref, acc_ref, thr_ref, need_ref, *, n_keep):
    f32 = jnp.float32
    qi = pl.program_id(1)
    nk = (qi * Q_BLOCK + Q_BLOCK + KCHUNK - 1) // KCHUNK
    iqT = iqT_ref[0, 0]
    iw8 = iw_ref[0, 0]
    iw = jnp.concatenate([iw8[h:h + 1] for h in range(IDX_HEADS)], axis=1)
    qpos = qi * Q_BLOCK + lax.broadcasted_iota(jnp.int32, (KCHUNK, Q_BLOCK), 1)
    krow = lax.broadcasted_iota(jnp.int32, (KCHUNK, Q_BLOCK), 0)

    def index_chunk(c, carry):
        lg = jnp.dot(ik_ref[0, c], iqT, preferred_element_type=f32)
        w = jnp.maximum(lg, 0.0) * iw
        idx = w[:, :Q_BLOCK]
        for h in range(1, IDX_HEADS):
            idx = idx + w[:, h * Q_BLOCK:(h + 1) * Q_BLOCK]
        idx = jnp.where(idx == 0.0, 0.0, idx)
        idx = jnp.where(krow + c * KCHUNK <= qpos, idx, -jnp.inf)
        bits = pltpu.bitcast(idx, jnp.int32)
        key_ref[c] = bits ^ ((bits >> 31) & _LOW31)
        return carry

    lax.fori_loop(0, nk, index_chunk, 0)

    def count(cmp, t):
        def body(c, cnt):
            hit = cmp(key_ref[c], t).astype(jnp.int32)
            return cnt + hit.reshape(KCHUNK // SUBLANE, SUBLANE, Q_BLOCK).sum(0)
        cnt = lax.fori_loop(0, nk, body, jnp.zeros((SUBLANE, Q_BLOCK), jnp.int32))
        return cnt.sum(0, keepdims=True)

    @pl.when(qi * Q_BLOCK + Q_BLOCK <= n_keep)
    def _():
        thr_ref[...] = jnp.full((1, Q_BLOCK), _KEY_NEG_INF, jnp.int32)
        need_ref[...] = jnp.zeros((1, Q_BLOCK), jnp.int32)

    @pl.when(qi * Q_BLOCK + Q_BLOCK > n_keep)
    def _():
        def bit_step(i, prefix):
            cand = prefix | jnp.left_shift(jnp.int32(1), 31 - i)
            cnt = count(lambda k, t: k >= t, cand ^ _SIGN_BIT)
            return jnp.where(cnt >= n_keep, cand, prefix)
        prefix = lax.fori_loop(0, 32, bit_step, jnp.zeros((1, Q_BLOCK), jnp.int32))
        thr = prefix ^ _SIGN_BIT
        thr_ref[...] = thr
        need_ref[...] = n_keep - count(lambda k, t: k > t, thr)

    thr = thr_ref[...]
    need = need_ref[...].astype(f32)
    qT = qT_ref[0, 0]
    scale = (A_NOPE + A_ROPE) ** -0.5
    r_i = lax.broadcasted_iota(jnp.int32, (KCHUNK, KCHUNK), 0)
    c_i = lax.broadcasted_iota(jnp.int32, (KCHUNK, KCHUNK), 1)
    before = (c_i < r_i).astype(jnp.bfloat16)
    acc_ref[...] = jnp.zeros_like(acc_ref)

    def attend_chunk(c, carry):
        m, l, ties_seen = carry
        s = jnp.dot(kv_ref[0, c], qT, preferred_element_type=f32) * scale
        key = key_ref[c]
        tie = key == thr
        tie_f = tie.astype(f32)
        rank = jnp.dot(before, tie_f.astype(jnp.bfloat16), preferred_element_type=f32) + ties_seen
        keep = (key > thr) | (tie & (rank < need))
        bias = jnp.where(keep, 0.0, _MASKED)
        s = s + jnp.concatenate([bias] * A_HEADS, axis=1)
        m_new = jnp.maximum(m, s.max(0, keepdims=True))
        alpha = jnp.exp(m - m_new)
        p = jnp.exp(s - m_new)
        l = alpha * l + p.sum(0, keepdims=True)
        pv = jnp.dot(ckvT_ref[0, c], p.astype(ckvT_ref.dtype), preferred_element_type=f32)
        acc_ref[...] = acc_ref[...] * alpha + pv
        return m_new, l, ties_seen + tie_f.sum(0, keepdims=True)

    lanes = A_HEADS * Q_BLOCK
    init = (jnp.full((1, lanes), _MASKED, f32), jnp.zeros((1, lanes), f32), jnp.zeros((1, Q_BLOCK), f32))
    _, l, _ = lax.fori_loop(0, nk, attend_chunk, init)
    o_latT = acc_ref[...] / l
    for h in range(A_HEADS):
        o_lat = o_latT[:, h * Q_BLOCK:(h + 1) * Q_BLOCK].T.astype(wuv_ref.dtype)
        o_ref[0, :, h * A_V:(h + 1) * A_V] = jnp.dot(o_lat, wuv_ref[h], preferred_element_type=f32)


def _dsa_attention(iqT, iw, qT, ikc, kvc, ckvT, wuv, *, n_keep):
    bsz, nq = qT.shape[:2]
    nc = ikc.shape[1]
    seq = nq * Q_BLOCK
    per_q = lambda b, q: (b, q, 0, 0)
    per_b = lambda b, q: (b, 0, 0, 0)
    return pl.pallas_call(
        functools.partial(_dsa_body, n_keep=n_keep),
        grid=(bsz, nq),
        in_specs=[pl.BlockSpec((1, 1) + iqT.shape[2:], per_q),
                  pl.BlockSpec((1, 1) + iw.shape[2:], per_q),
                  pl.BlockSpec((1, 1) + qT.shape[2:], per_q),
                  pl.BlockSpec((1,) + ikc.shape[1:], per_b),
                  pl.BlockSpec((1,) + kvc.shape[1:], per_b),
                  pl.BlockSpec((1,) + ckvT.shape[1:], per_b),
                  pl.BlockSpec(wuv.shape, lambda b, q: (0, 0, 0))],
        out_specs=pl.BlockSpec((1, Q_BLOCK, A_HEADS * A_V), lambda b, q: (b, q, 0)),
        out_shape=jax.ShapeDtypeStruct((bsz, seq, A_HEADS * A_V), jnp.float32),
        scratch_shapes=[pltpu.VMEM((nc, KCHUNK, Q_BLOCK), jnp.int32),
                        pltpu.VMEM((A_KV_RANK, A_HEADS * Q_BLOCK), jnp.float32),
                        pltpu.VMEM((1, Q_BLOCK), jnp.int32),
                        pltpu.VMEM((1, Q_BLOCK), jnp.int32)],
        compiler_params=_params("arbitrary", "arbitrary"),
        name="dsa_attention",
    )(iqT, iw, qT, ikc, kvc, ckvT, wuv)


RET_TILE = 256


def _retention_body(q_ref, kT_ref, v_ref, g_ref, o_ref, s_ref):
    f32 = jnp.float32
    cdt = MXU_DTYPE
    c = RET_CHUNK

    @pl.when(pl.program_id(1) == 0)
    def _():
        s_ref[...] = jnp.zeros_like(s_ref)

    rel = (lax.broadcasted_iota(jnp.int32, (c, c), 0) - lax.broadcasted_iota(jnp.int32, (c, c), 1)).astype(f32)
    pos_c = lax.broadcasted_iota(jnp.int32, (c, 1), 0).astype(f32)
    pos_r = lax.broadcasted_iota(jnp.int32, (1, c), 1).astype(f32)

    def mm(a, b):
        return jnp.dot(a.astype(cdt), b.astype(cdt), preferred_element_type=f32)

    for h in range(B_HEADS):
        log_gamma = float(np.log(np.float32(1.0) - np.float32(2.0) ** np.float32(-5 - h)))
        intra = jnp.where(rel >= 0, jnp.exp(log_gamma * jnp.maximum(rel, 0.0)), 0.0)
        to_end = jnp.exp(log_gamma * (c - 1 - pos_r))
        from_start = jnp.exp(log_gamma * (pos_c + 1.0))
        chunk_decay = float(np.exp(np.float32(log_gamma) * np.float32(c)))
        lanes = slice(h * B_V, (h + 1) * B_V)
        pair = slice((h // 2) * LANE, (h // 2 + 1) * LANE)
        for ci in range(RET_TILE // c):
            rows = slice(ci * c, (ci + 1) * c)
            q = q_ref[0, rows, pair].astype(f32)
            kT = kT_ref[0, h, :, rows].astype(f32)
            v = v_ref[rows, lanes]
            state = s_ref[h]
            o = mm(mm(q, kT) * intra, v) + mm(q * from_start, state)
            s_ref[h] = state * chunk_decay + mm(kT * to_end, v)
            gate = g_ref[rows, lanes]
            o = o * lax.rsqrt(jnp.mean(o * o, -1, keepdims=True) + RMS_EPS)
            o_ref[0, rows, lanes] = o * (gate * _sigmoid(gate))


def _retention(rq, rkT, proj, bsz, seq):
    nt = seq // RET_TILE
    width = B_HEADS * B_V
    col = lambda c: pl.BlockSpec((RET_TILE, COL), lambda b, t: (b * nt + t, c))
    return pl.pallas_call(
        _retention_body,
        grid=(bsz, nt),
        in_specs=[pl.BlockSpec((1, RET_TILE, B_HEADS * B_QK), lambda b, t: (b, t, 0)),
                  pl.BlockSpec((1, B_HEADS, LANE, RET_TILE), lambda b, t: (b, 0, 0, t)),
                  col(COL_B_V), col(COL_B_G)],
        out_specs=pl.BlockSpec((1, RET_TILE, width), lambda b, t: (b, t, 0)),
        out_shape=jax.ShapeDtypeStruct((bsz, seq, width), jnp.float32),
        scratch_shapes=[pltpu.VMEM((B_HEADS, LANE, B_V), jnp.float32)],
        compiler_params=_params("arbitrary", "arbitrary"),
        name="retention",
    )(rq, rkT, proj, proj)


def _diff_attention_body(lam_ref, qbd_ref, k_ref, vT_ref, gain_ref, o_ref, acc_ref):
    f32 = jnp.float32
    qi = pl.program_id(1)
    nk = (qi * Q_BLOCK + Q_BLOCK + KCHUNK - 1) // KCHUNK
    qpos = qi * Q_BLOCK + lax.broadcasted_iota(jnp.int32, (KCHUNK, Q_BLOCK), 1)
    krow = lax.broadcasted_iota(jnp.int32, (KCHUNK, Q_BLOCK), 0)
    acc_ref[...] = jnp.zeros_like(acc_ref)

    def chunk(c, carry):
        ms, ls = carry
        causal = jnp.where(krow + c * KCHUNK <= qpos, 0.0, _MASKED)
        bias = jnp.concatenate([causal, causal], axis=1)
        new_ms, new_ls = [], []
        for h in range(C_HEADS):
            s = jnp.dot(k_ref[0, c, h], qbd_ref[0, 0, h], preferred_element_type=f32) + bias
            m_new = jnp.maximum(ms[h], s.max(0, keepdims=True))
            alpha = jnp.exp(ms[h] - m_new)
            p = jnp.exp(s - m_new)
            new_ls.append(alpha * ls[h] + p.sum(0, keepdims=True))
            new_ms.append(m_new)
            pv = jnp.dot(vT_ref[0, c, h], p.astype(vT_ref.dtype), preferred_element_type=f32)
            acc_ref[h] = acc_ref[h] * alpha + pv
        return tuple(new_ms), tuple(new_ls)

    lanes = 2 * Q_BLOCK
    init = (tuple(jnp.full((1, lanes), _MASKED, f32) for _ in range(C_HEADS)),
            tuple(jnp.zeros((1, lanes), f32) for _ in range(C_HEADS)))
    _, ls = lax.fori_loop(0, nk, chunk, init)
    lam = lam_ref[0]
    for h in range(C_HEADS):
        o = acc_ref[h] / ls[h]
        o = o[:, :Q_BLOCK] - lam * o[:, Q_BLOCK:]
        o = o * lax.rsqrt(jnp.mean(o * o, 0, keepdims=True) + RMS_EPS)
        o_ref[0, :, h * 2 * C_DH:(h + 1) * 2 * C_DH] = o.T * gain_ref[...]


def _diff_attention(lam, qbd, kc, vT, gain):
    bsz, nq = qbd.shape[:2]
    seq = nq * Q_BLOCK
    width = C_HEADS * 2 * C_DH
    return pl.pallas_call(
        _diff_attention_body,
        grid=(bsz, nq),
        in_specs=[pl.BlockSpec(memory_space=pltpu.SMEM),
                  pl.BlockSpec((1, 1) + qbd.shape[2:], lambda b, q: (b, q, 0, 0, 0)),
                  pl.BlockSpec((1,) + kc.shape[1:], lambda b, q: (b, 0, 0, 0, 0)),
                  pl.BlockSpec((1,) + vT.shape[1:], lambda b, q: (b, 0, 0, 0, 0)),
                  pl.BlockSpec(gain.shape, lambda b, q: (0, 0))],
        out_specs=pl.BlockSpec((1, Q_BLOCK, width), lambda b, q: (b, q, 0)),
        out_shape=jax.ShapeDtypeStruct((bsz, seq, width), jnp.float32),
        scratch_shapes=[pltpu.VMEM((C_HEADS, 2 * C_DH, 2 * Q_BLOCK), jnp.float32)],
        compiler_params=_params("arbitrary", "arbitrary"),
        name="diff_attention",
    )(lam, qbd, kc, vT, gain)


GDN_TILE = 256
GDN_SUB = 16


def _gdn_body(qkv_ref, z_ref, nar_ref, gateT_ref, conv_ref, alog_r_ref, dtb_r_ref, alog_c_ref, dtb_c_ref,
              ng_ref, o_ref, halo_ref, s_ref):
    f32 = jnp.float32
    cdt = MXU_DTYPE
    c = GDN_CHUNK

    @pl.when(pl.program_id(1) == 0)
    def _():
        halo_ref[...] = jnp.zeros_like(halo_ref)
        s_ref[...] = jnp.zeros_like(s_ref)

    x = qkv_ref[...]
    xp = jnp.concatenate([halo_ref[...], x], 0)
    w = conv_ref[...]
    pre = xp[SUBLANE - CONV_K + 1:SUBLANE - CONV_K + 1 + GDN_TILE] * w[0:1]
    for j in range(1, CONV_K):
        off = SUBLANE - CONV_K + 1 + j
        pre = pre + xp[off:off + GDN_TILE] * w[j:j + 1]
    halo_ref[...] = x[GDN_TILE - SUBLANE:]
    qkv = pre * _sigmoid(pre)

    gates = nar_ref[:, LANE:]
    beta_c = _sigmoid(gates[:, NARROW_DB:NARROW_DB + D_HEADS])
    g_c = -jnp.exp(alog_r_ref[...]) * _softplus(gates[:, NARROW_DA:NARROW_DA + D_HEADS] + dtb_r_ref[...])
    g_r = -jnp.exp(alog_c_ref[...]) * _softplus(gateT_ref[0][D_HEADS:] + dtb_c_ref[...])

    row = lax.broadcasted_iota(jnp.int32, (c, c), 0)
    col = lax.broadcasted_iota(jnp.int32, (c, c), 1)
    tril = row >= col
    strict = row > col
    same_sub = (row // GDN_SUB) == (col // GDN_SUB)
    eye = (row == col).astype(f32)
    lower_ones = tril.astype(f32)
    upper_ones = (row <= col).astype(f32)

    def mm(a, b):
        return jnp.dot(a, b, preferred_element_type=f32)

    for ci in range(GDN_TILE // c):
        rows = slice(ci * c, (ci + 1) * c)
        g_cum_c = mm(lower_ones, g_c[rows])
        g_cum_r = mm(g_r[:, rows], upper_ones)
        for h in range(D_HEADS):
            lanes = slice(h * D_K, (h + 1) * D_K)
            q = qkv[rows, h * D_K:(h + 1) * D_K]
            k = qkv[rows, D_HEADS * D_K + h * D_K:D_HEADS * D_K + (h + 1) * D_K]
            v = qkv[rows, 2 * D_HEADS * D_K + h * D_V:2 * D_HEADS * D_K + (h + 1) * D_V]
            q = q * lax.rsqrt(jnp.sum(q * q, -1, keepdims=True) + RMS_EPS) * D_K ** -0.5
            k = k * lax.rsqrt(jnp.sum(k * k, -1, keepdims=True) + RMS_EPS)
            gc = g_cum_c[:, h:h + 1]
            gr = g_cum_r[h:h + 1, :]
            g_last = gc[c - 1:c]
            gam = jnp.where(tril, jnp.exp(jnp.where(tril, gc - gr, 0.0)), 0.0)
            e_g = jnp.exp(gc)
            beta = beta_c[rows, h:h + 1]
            kT = k.T
            kb = k * beta
            low = jnp.where(strict, mm(kb.astype(cdt), kT.astype(cdt)) * gam, 0.0)
            attn = jnp.where(tril, mm(q.astype(cdt), kT.astype(cdt)) * gam, 0.0)
            low_d = jnp.where(same_sub, low, 0.0)
            xk = -low_d
            td = eye + xk
            for _ in range(3):
                xk = mm(xk, xk)
                td = td + mm(td, xk)
            m1 = mm(td, low - low_d)
            m2 = mm(m1, m1)
            sol = mm(td, jnp.concatenate([v * beta, kb * e_g], 1))
            sol = sol + mm(m2, sol)
            sol = sol - mm(m1, sol)
            u, wy = sol[:, :D_V], sol[:, D_V:]
            state = s_ref[h]
            state_c = state.astype(cdt)
            v_new = u - mm(wy.astype(cdt), state_c)
            v_new_c = v_new.astype(cdt)
            o = mm((q * e_g).astype(cdt), state_c) + mm(attn.astype(cdt), v_new_c)
            kdT = kT * jnp.exp(g_last - gr)
            s_ref[h] = state * jnp.exp(g_last) + mm(kdT.astype(cdt), v_new_c)
            zz = z_ref[rows, lanes]
            o = o * lax.rsqrt(jnp.mean(o * o, -1, keepdims=True) + RMS_EPS) * ng_ref[...]
            o_ref[0, rows, lanes] = o * (zz * _sigmoid(zz))


def _gated_deltanet(proj, gatesT, conv_w, a_log, dt_bias, norm_g, bsz, seq):
    nt = seq // GDN_TILE
    width = D_HEADS * (2 * D_K + D_V)
    whole = lambda r, cc: pl.BlockSpec((r, cc), lambda b, t: (0, 0))
    return pl.pallas_call(
        _gdn_body,
        grid=(bsz, nt),
        in_specs=[pl.BlockSpec((GDN_TILE, width), lambda b, t: (b * nt + t, COL_D_QKV)),
                  pl.BlockSpec((GDN_TILE, COL), lambda b, t: (b * nt + t, COL_D_Z)),
                  pl.BlockSpec((GDN_TILE, NARROW), lambda b, t: (b * nt + t, COL_NARROW)),
                  pl.BlockSpec((1, 2 * D_HEADS, GDN_TILE), lambda b, t: (b, 0, t)),
                  whole(CONV_K, width), whole(1, D_HEADS), whole(1, D_HEADS), whole(D_HEADS, 1), whole(D_HEADS, 1),
                  whole(1, D_V)],
        out_specs=pl.BlockSpec((1, GDN_TILE, D_HEADS * D_V), lambda b, t: (b, t, 0)),
        out_shape=jax.ShapeDtypeStruct((bsz, seq, D_HEADS * D_V), jnp.float32),
        scratch_shapes=[pltpu.VMEM((SUBLANE, width), jnp.float32),
                        pltpu.VMEM((D_HEADS, D_K, D_V), jnp.float32)],
        compiler_params=_params("arbitrary", "arbitrary"),
        name="gated_deltanet",
    )(proj, proj, proj, gatesT, conv_w, a_log.reshape(1, D_HEADS), dt_bias.reshape(1, D_HEADS),
      a_log.reshape(D_HEADS, 1), dt_bias.reshape(D_HEADS, 1), norm_g.reshape(1, D_V))


def _ln_rows(t, g, b):
    mu = jnp.mean(t, -1, keepdims=True)
    d = t - mu
    var = jnp.mean(d * d, -1, keepdims=True)
    return d * lax.rsqrt(var + LN_EPS) * g + b


def _out_ln_router_body(ha_ref, hb_ref, hc_ref, hd_ref, w_ref, x_ref, g_ref, b_ref, wr_ref, br_ref,
                        h_ref, te_ref, tg_ref):
    f32 = jnp.float32
    acc = DEEPNORM_ALPHA * x_ref[...]
    row0 = 0
    for part in (ha_ref, hb_ref, hc_ref, hd_ref):
        width = part.shape[-1]
        acc = acc + jnp.dot(part[...].astype(MXU_DTYPE), w_ref[row0:row0 + width, :], preferred_element_type=f32)
        row0 += width
    h = _ln_rows(acc, g_ref[...], b_ref[...])
    h_ref[...] = h
    logits = jnp.dot(h.astype(MXU_DTYPE), wr_ref[...], preferred_element_type=f32) + br_ref[...]
    lane = lax.broadcasted_iota(jnp.int32, logits.shape, 1)
    vals, te = [], jnp.zeros(logits.shape, jnp.int32)
    for k in range(TOP_K):
        m = logits.max(-1, keepdims=True)
        e = jnp.min(jnp.where(logits == m, lane, LANE), -1, keepdims=True)
        vals.append(m)
        te = jnp.where(lane == k, e, te)
        logits = jnp.where(lane == e, -jnp.inf, logits)
    ex = [jnp.exp(v - vals[0]) for v in vals]
    den = ex[0] + ex[1] + ex[2] + ex[3]
    tg = jnp.zeros(logits.shape, f32)
    for k in range(TOP_K):
        tg = jnp.where(lane == k, ex[k] / den, tg)
    te_ref[...] = te
    tg_ref[...] = tg


def _out_ln_router(head_parts, w_out_c, x2d, g, b, w_router, b_router, tm=512):
    m = x2d.shape[0]
    k, n = w_out_c.shape
    wr = jnp.zeros((n, LANE), MXU_DTYPE).at[:, :N_EXPERTS].set(w_router.astype(MXU_DTYPE))
    br = jnp.full((1, LANE), -jnp.inf, jnp.float32).at[0, :N_EXPERTS].set(b_router)
    row_blk = lambda c: pl.BlockSpec((tm, c), lambda i: (i, 0))
    whole = lambda r, c: pl.BlockSpec((r, c), lambda i: (0, 0))
    return pl.pallas_call(
        _out_ln_router_body,
        grid=(m // tm,),
        in_specs=[row_blk(p.shape[1]) for p in head_parts]
        + [whole(k, n), row_blk(n), whole(1, n), whole(1, n), whole(n, LANE), whole(1, LANE)],
        out_specs=[row_blk(n), row_blk(LANE), row_blk(LANE)],
        out_shape=[jax.ShapeDtypeStruct((m, n), jnp.float32),
                   jax.ShapeDtypeStruct((m, LANE), jnp.int32),
                   jax.ShapeDtypeStruct((m, LANE), jnp.float32)],
        compiler_params=_params("arbitrary"),
        name="out_proj_ln_router",
    )(*head_parts, w_out_c, x2d, g.reshape(1, n), b.reshape(1, n), wr, br)


def _clamped_swiglu(hg):
    glu, lin = hg[..., :D_FF], hg[..., D_FF:]
    glu = jnp.minimum(glu, SWIGLU_LIMIT)
    lin = jnp.clip(lin, -SWIGLU_LIMIT, SWIGLU_LIMIT)
    return glu * _sigmoid(SWIGLU_ALPHA * glu) * (lin + 1.0)


def _moe_plan(top_e):
    n_tok = top_e.shape[0]
    n_blocks = -(-n_tok * TOP_K // MOE_BLOCK) + N_EXPERTS
    member = (top_e[:, :, None] == jnp.arange(N_EXPERTS, dtype=jnp.int32)).astype(jnp.int32).sum(1)
    rank = jnp.cumsum(member, axis=0) - member
    counts = member.sum(0)
    padded = (counts + MOE_BLOCK - 1) // MOE_BLOCK * MOE_BLOCK
    pad_end = jnp.cumsum(padded)
    dest = jnp.take_along_axis(rank + (pad_end - padded)[None, :], top_e, axis=1).astype(jnp.int32)
    tok = jnp.broadcast_to(jnp.arange(n_tok, dtype=jnp.int32)[:, None], dest.shape)
    row_tok = jnp.zeros((n_blocks * MOE_BLOCK,), jnp.int32).at[dest.reshape(-1)].set(tok.reshape(-1))
    block_e = jnp.minimum(jnp.searchsorted(pad_end, jnp.arange(n_blocks) * MOE_BLOCK, side='right'),
                          N_EXPERTS - 1).astype(jnp.int32)
    n_used = (pad_end[-1:] // MOE_BLOCK).astype(jnp.int32)
    return dest, row_tok, block_e, n_used


def _row_gather_start(src_hbm, rows_smem, n_rows, dst, sem):
    def body(r, carry):
        pltpu.make_async_copy(src_hbm.at[pl.ds(rows_smem[0, 0, r], 1)], dst.at[pl.ds(r, 1)], sem).start()
        return carry
    lax.fori_loop(0, n_rows, body, 0, unroll=8)


def _row_gather_wait(src_hbm, n_rows, dst, sem):
    pltpu.make_async_copy(src_hbm.at[pl.ds(0, n_rows)], dst, sem).wait()


def _moe_expert_body(be_ref, nused_ref, tok_ref, tok_next_ref, h_hbm, wgu_ref, bgu_ref, wdn_ref, bdn_ref,
                     y_ref, buf, sem, wgu_c, wdn_c):
    f32 = jnp.float32
    i = pl.program_id(0)
    n_used = nused_ref[0]
    slot = i % 2

    @pl.when(jnp.logical_and(i == 0, n_used > 0))
    def _():
        _row_gather_start(h_hbm, tok_ref, MOE_BLOCK, buf.at[0], sem.at[0])

    @pl.when(i + 1 < n_used)
    def _():
        _row_gather_start(h_hbm, tok_next_ref, MOE_BLOCK, buf.at[1 - slot], sem.at[1 - slot])

    @pl.when(jnp.logical_or(i == 0, be_ref[i] != be_ref[jnp.maximum(i - 1, 0)]))
    def _():
        wgu_c[...] = wgu_ref[0].astype(MXU_DTYPE)
        wdn_c[...] = wdn_ref[0].astype(MXU_DTYPE)

    @pl.when(i < n_used)
    def _():
        _row_gather_wait(h_hbm, MOE_BLOCK, buf.at[slot], sem.at[slot])
        xb = buf[slot].astype(MXU_DTYPE)
        hg = jnp.dot(xb, wgu_c[...], preferred_element_type=f32) + bgu_ref[0]
        act = _clamped_swiglu(hg)
        y_ref[...] = jnp.dot(act.astype(MXU_DTYPE), wdn_c[...], preferred_element_type=f32) + bdn_ref[0]

    @pl.when(i >= n_used)
    def _():
        y_ref[...] = jnp.zeros_like(y_ref)


def _moe_experts(h2d, row_tok, block_e, n_used, w_gu, b_gu, w_dn, b_dn):
    n_tok, d = h2d.shape
    n_blocks = block_e.shape[0]
    tok3 = row_tok.reshape(n_blocks, 1, MOE_BLOCK)
    smem_blk = lambda imap: pl.BlockSpec((1, 1, MOE_BLOCK), imap, memory_space=pltpu.SMEM)
    by_expert = lambda shape: pl.BlockSpec((1,) + shape, lambda i, be, nu: (be[i], 0, 0))
    return pl.pallas_call(
        _moe_expert_body,
        grid_spec=pltpu.PrefetchScalarGridSpec(
            num_scalar_prefetch=2,
            grid=(n_blocks,),
            in_specs=[smem_blk(lambda i, be, nu: (i, 0, 0)),
                      smem_blk(lambda i, be, nu: (jnp.minimum(i + 1, n_blocks - 1), 0, 0)),
                      pl.BlockSpec(memory_space=pl.ANY),
                      by_expert((d, 2 * D_FF)), by_expert((1, 2 * D_FF)),
                      by_expert((D_FF, d)), by_expert((1, d))],
            out_specs=pl.BlockSpec((MOE_BLOCK, d), lambda i, be, nu: (i, 0)),
            scratch_shapes=[pltpu.VMEM((2, MOE_BLOCK, d), jnp.float32), pltpu.SemaphoreType.DMA((2,)),
                            pltpu.VMEM((d, 2 * D_FF), MXU_DTYPE), pltpu.VMEM((D_FF, d), MXU_DTYPE)]),
        out_shape=jax.ShapeDtypeStruct((n_blocks * MOE_BLOCK, d), jnp.float32),
        compiler_params=_params("arbitrary"),
        name="moe_experts",
    )(block_e, n_used, tok3, tok3, h2d, w_gu, b_gu.reshape(N_EXPERTS, 1, 2 * D_FF),
      w_dn, b_dn.reshape(N_EXPERTS, 1, d))


MOE_COMBINE_ROWS = 128


def _moe_combine_ln_body(dest_ref, dest_next_ref, h_ref, tg_ref, g_ref, b_ref, y_hbm, o_ref, buf, sem):
    tm = MOE_COMBINE_ROWS
    i = pl.program_id(0)
    n = pl.num_programs(0)
    slot = i % 2

    def start(rows_smem, s):
        for k in range(TOP_K):
            def body(r, carry):
                pltpu.make_async_copy(y_hbm.at[pl.ds(rows_smem[0, 0, k * tm + r], 1)],
                                      buf.at[s, k, pl.ds(r, 1)], sem.at[s]).start()
                return carry
            lax.fori_loop(0, tm, body, 0, unroll=8)

    @pl.when(i == 0)
    def _():
        start(dest_ref, 0)

    @pl.when(i + 1 < n)
    def _():
        start(dest_next_ref, 1 - slot)

    tg = tg_ref[...]
    y = jnp.zeros(h_ref.shape, jnp.float32)
    for k in range(TOP_K):
        _row_gather_wait(y_hbm, tm, buf.at[slot, k], sem.at[slot])
    for k in range(TOP_K):
        y = y + tg[:, k:k + 1] * buf[slot, k]
    o_ref[...] = _ln_rows(DEEPNORM_ALPHA * h_ref[...] + y, g_ref[...], b_ref[...])


def _moe_combine_ln(h2d, y_rows, dest, tg, g, b):
    n_tok, d = h2d.shape
    tm = MOE_COMBINE_ROWS
    nt = n_tok // tm
    dest3 = dest.reshape(nt, tm, TOP_K).transpose(0, 2, 1).reshape(nt, 1, TOP_K * tm)
    smem_blk = lambda imap: pl.BlockSpec((1, 1, TOP_K * tm), imap, memory_space=pltpu.SMEM)
    return pl.pallas_call(
        _moe_combine_ln_body,
        grid=(nt,),
        in_specs=[smem_blk(lambda i: (i, 0, 0)),
                  smem_blk(lambda i: (jnp.minimum(i + 1, nt - 1), 0, 0)),
                  pl.BlockSpec((tm, d), lambda i: (i, 0)),
                  pl.BlockSpec((tm, LANE), lambda i: (i, 0)),
                  pl.BlockSpec((1, d), lambda i: (0, 0)),
                  pl.BlockSpec((1, d), lambda i: (0, 0)),
                  pl.BlockSpec(memory_space=pl.ANY)],
        out_specs=pl.BlockSpec((tm, d), lambda i: (i, 0)),
        out_shape=jax.ShapeDtypeStruct((n_tok, d), jnp.float32),
        scratch_shapes=[pltpu.VMEM((2, TOP_K, tm, d), jnp.float32), pltpu.SemaphoreType.DMA((2,))],
        compiler_params=_params("arbitrary"),
        name="moe_combine_ln",
    )(dest3, dest3, h2d, tg, g.reshape(1, d), b.reshape(1, d), y_rows)


def _moe_layer(h2d, te, tg, w_gu, b_gu, w_dn, b_dn, g, b):
    dest, row_tok, block_e, n_used = _moe_plan(te[:, :TOP_K])
    y_rows = _moe_experts(h2d, row_tok, block_e, n_used, w_gu, b_gu, w_dn, b_dn)
    return _moe_combine_ln(h2d, y_rows, dest, tg, g, b)


def _mixers(proj, cos, sin, bsz, seq, lambda_init, a_kv_norm, a_w_uk, a_w_uv, c_lambda, c_subln,
            d_conv, d_a_log, d_dt_bias, d_norm):
    cdt = MXU_DTYPE
    (qT, iqT, iw, ikc, kvc, ckvT, qbd, kc, vT, rq, rkT, gatesT) = _mixer_prep(
        proj, cos, sin, a_kv_norm, a_w_uk.transpose(1, 0, 2).astype(cdt), bsz, seq)
    head_a = _dsa_attention(iqT, iw, qT, ikc, kvc, ckvT, a_w_uv.transpose(1, 0, 2).astype(cdt),
                            n_keep=min(INDEX_TOPK, seq // 4))
    head_b = _retention(rq, rkT, proj, bsz, seq)
    lam_full = (jnp.exp(jnp.sum(c_lambda[0] * c_lambda[1])) - jnp.exp(jnp.sum(c_lambda[2] * c_lambda[3]))
                + lambda_init)
    gain = (c_subln * (1.0 - lambda_init)).reshape(1, 2 * C_DH)
    head_c = _diff_attention(lam_full.reshape(1), qbd, kc, vT, gain)
    head_d = _gated_deltanet(proj, gatesT, d_conv, d_a_log, d_dt_bias, d_norm, bsz, seq)
    return [t.reshape(bsz * seq, t.shape[-1]) for t in (head_a, head_b, head_c, head_d)]


def kernel(x, positions, w_in, w_out, a_kv_norm, a_w_uk, a_w_uv, c_lambda, c_subln, d_conv, d_a_log, d_dt_bias, d_norm, ln1_g, ln1_b, w_router, b_router, w_gu, b_gu, w_dn, b_dn, ln2_g, ln2_b):
    bsz, seq, d = x.shape
    x2d = x.reshape(bsz * seq, d)
    cos, sin = _rope_tables(positions)
    for l in range(DEPTH):
        proj = _in_proj(x2d, _pack_w_in(w_in[l]))
        lambda_init = 0.8 - 0.6 * math.exp(-0.3 * l)
        heads = _mixers(proj, cos, sin, bsz, seq, lambda_init, a_kv_norm[l], a_w_uk[l], a_w_uv[l], c_lambda[l],
                        c_subln[l], d_conv[l], d_a_log[l], d_dt_bias[l], d_norm[l])
        h2d, te, tg = _out_ln_router(heads, w_out[l].astype(MXU_DTYPE), x2d, ln1_g[l], ln1_b[l],
                                     w_router[l], b_router[l])
        x2d = _moe_layer(h2d, te, tg, w_gu[l], b_gu[l], w_dn[l], b_dn[l], ln2_g[l], ln2_b[l])
    return x2d.reshape(bsz, seq, d)
```

```python
import functools
import math

import jax
import jax.numpy as jnp
import numpy as np
from jax import lax
from jax.experimental import pallas as pl
from jax.experimental.pallas import tpu as pltpu

D_MODEL = 2048
DEPTH = 4
A_HEADS, A_NOPE, A_ROPE, A_V, A_KV_RANK = 4, 128, 64, 128, 256
IDX_HEADS, IDX_DIM, INDEX_TOPK = 8, 64, 256
B_HEADS, B_QK, B_V, RET_CHUNK = 4, 64, 128, 128
C_HEADS, C_DH = 4, 64
D_HEADS, D_K, D_V, CONV_K, GDN_CHUNK = 4, 128, 128, 4, 64
N_EXPERTS, TOP_K, D_FF = 32, 4, 512
SWIGLU_ALPHA, SWIGLU_LIMIT, MOE_BLOCK = 1.702, 7.0, 256
ROPE_THETA = 10000.0
ROPE_DIM = 64
Q_BLOCK = 128
DEEPNORM_ALPHA = (2 * DEPTH) ** 0.25
LN_EPS = 1e-5
RMS_EPS = 1e-6
D_MIX = A_HEADS * A_V + B_HEADS * B_V + C_HEADS * 2 * C_DH + D_HEADS * D_V
A_LAT = A_KV_RANK + A_ROPE

IN_WIDTHS = (
    A_HEADS * (A_NOPE + A_ROPE), A_KV_RANK, A_ROPE, IDX_HEADS * IDX_DIM, IDX_DIM, IDX_HEADS,
    B_HEADS * B_QK, B_HEADS * B_QK, B_HEADS * B_V, B_HEADS * B_V,
    C_HEADS * 2 * C_DH, C_HEADS * 2 * C_DH, C_HEADS * 2 * C_DH,
    D_HEADS * (2 * D_K + D_V), D_HEADS * D_V, D_HEADS, D_HEADS,
)
_IN_OFFS = np.concatenate([[0], np.cumsum(IN_WIDTHS)]).tolist()

LANE = 128
SUBLANE = 8
VMEM_LIMIT = 56 * 1024 * 1024

MXU_DTYPE = jnp.bfloat16
_MASKED = -1e30

COL = 512
COL_D_QKV = 0
COL_D_Z = 3
COL_B_V = 4
COL_B_G = 5
COL_B_QK = 6
COL_C_Q = 7
COL_C_K = 8
COL_C_V = 9
COL_A_CKV_ROPE = 10
COL_A_NOPE = 11
COL_A_IQ = 12
NARROW = 256
COL_NARROW = 26
IN_PACKED = (COL_NARROW + 1) * NARROW
NARROW_IW = 0
NARROW_DB = IDX_HEADS
NARROW_DA = IDX_HEADS + D_HEADS


def _pack_w_in(w):
    g = lambda i: w[:, _IN_OFFS[i]:_IN_OFFS[i + 1]]
    a_q = g(0).reshape(w.shape[0], A_HEADS, A_NOPE + A_ROPE)
    nope = a_q[:, :, :A_NOPE].reshape(w.shape[0], A_HEADS * A_NOPE)
    rope = a_q[:, :, A_NOPE:].reshape(w.shape[0], A_HEADS * A_ROPE)
    cols = [g(13), g(14), g(8), g(9), g(6), g(7), g(10), g(11), g(12), g(1), rope, nope, g(3),
            g(2), g(4), g(5), g(15), g(16)]
    used = sum(c.shape[1] for c in cols)
    cols.append(jnp.zeros((w.shape[0], IN_PACKED - used), w.dtype))
    return jnp.concatenate(cols, axis=1).astype(MXU_DTYPE)


def _sigmoid(t):
    return 1.0 / (1.0 + jnp.exp(-t))


def _softplus(t):
    return jnp.maximum(t, 0.0) + jnp.log1p(jnp.exp(-jnp.abs(t)))


def _params(*sem):
    return pltpu.CompilerParams(dimension_semantics=sem, vmem_limit_bytes=VMEM_LIMIT)


def _in_proj_body(x_ref, w_ref, o_ref, xb_ref):
    @pl.when(pl.program_id(1) == 0)
    def _():
        xb_ref[...] = x_ref[...].astype(MXU_DTYPE)

    o_ref[...] = jnp.dot(xb_ref[...], w_ref[...], preferred_element_type=jnp.float32)


def _in_proj(x2d, w_packed, tm=1024, tn=768):
    m, k = x2d.shape
    n = w_packed.shape[1]
    return pl.pallas_call(
        _in_proj_body,
        grid=(m // tm, n // tn),
        in_specs=[pl.BlockSpec((tm, k), lambda i, j: (i, 0)),
                  pl.BlockSpec((k, tn), lambda i, j: (0, j))],
        out_specs=pl.BlockSpec((tm, tn), lambda i, j: (i, j)),
        out_shape=jax.ShapeDtypeStruct((m, n), jnp.float32),
        scratch_shapes=[pltpu.VMEM((tm, k), MXU_DTYPE)],
        compiler_params=_params("arbitrary", "arbitrary"),
        name="in_proj",
    )(x2d, w_packed)


PREP_TILE = 256
KCHUNK = 256


def _prep_body(bqk_ref, cq_ref, ck_ref, cv_ref, ckvrope_ref, nope_ref, iq_ref, nar_ref, cos_ref, sin_ref,
               kvn_ref, wukT_ref,
               qT_ref, iqT_ref, iw_ref, ik_ref, kv_ref, ckvT_ref, qbd_ref, kc_ref, vT_ref, rq_ref, rkT_ref, gT_ref):
    f32 = jnp.float32
    cdt = MXU_DTYPE
    cos, sin = cos_ref[0], sin_ref[0]
    lane = lax.broadcasted_iota(jnp.int32, (PREP_TILE, LANE), 1)
    first_half = (lane % ROPE_DIM) < ROPE_DIM // 2
    eye = (lax.broadcasted_iota(jnp.int32, (LANE, LANE), 0)
           == lax.broadcasted_iota(jnp.int32, (LANE, LANE), 1)).astype(cdt)
    top = lax.broadcasted_iota(jnp.int32, (LANE, LANE), 0) < LANE // 2
    top_wide = lax.broadcasted_iota(jnp.int32, (LANE, PREP_TILE), 0) < LANE // 2

    def rope(t):
        outs = []
        for j in range(t.shape[1] // LANE):
            ts = t[:, j * LANE:(j + 1) * LANE]
            partner = jnp.where(first_half, pltpu.roll(ts, LANE - ROPE_DIM // 2, 1), pltpu.roll(ts, ROPE_DIM // 2, 1))
            outs.append(ts * cos + partner * sin)
        return outs[0] if len(outs) == 1 else jnp.concatenate(outs, axis=1)

    def tr(t):
        return lax.dot_general(eye, t.astype(cdt), (((1,), (1,)), ((), ())), preferred_element_type=f32)

    ckv_rope = ckvrope_ref[...]
    a_ckv = ckv_rope[:, :A_KV_RANK]
    c_kv = a_ckv * lax.rsqrt(jnp.mean(a_ckv * a_ckv, -1, keepdims=True) + RMS_EPS) * kvn_ref[...]
    nar = nar_ref[...]
    kr_ik = rope(nar[:, :LANE])
    kv_ref[0, 0, :, :A_KV_RANK] = c_kv.astype(cdt)
    kv_ref[0, 0, :, A_KV_RANK:] = kr_ik[:, :A_ROPE].astype(cdt)
    ik_ref[0, 0] = kr_ik[:, A_ROPE:].astype(cdt)
    for rb in range(A_KV_RANK // LANE):
        ckvT_ref[0, 0, rb * LANE:(rb + 1) * LANE, :] = tr(c_kv[:, rb * LANE:(rb + 1) * LANE]).astype(cdt)
    q_rope = rope(ckv_rope[:, A_KV_RANK:])
    nope = nope_ref[...]
    iq = rope(iq_ref[...])
    narT = nar[:, LANE:].T
    gT_ref[0] = narT[NARROW_DB:NARROW_DB + 2 * D_HEADS]
    cq = rope(cq_ref[...]) * C_DH ** -0.5
    for j in range(PREP_TILE // Q_BLOCK):
        rows = slice(j * Q_BLOCK, (j + 1) * Q_BLOCK)
        for h in range(A_HEADS):
            q_latT = lax.dot_general(wukT_ref[h], nope[rows, h * A_NOPE:(h + 1) * A_NOPE].astype(cdt),
                                     (((1,), (1,)), ((), ())), preferred_element_type=f32)
            qT_ref[0, j, :A_KV_RANK, h * Q_BLOCK:(h + 1) * Q_BLOCK] = q_latT.astype(cdt)
        for p in range(A_HEADS * A_ROPE // LANE):
            t = tr(q_rope[rows, p * LANE:(p + 1) * LANE])
            qT_ref[0, j, A_KV_RANK:, (2 * p) * Q_BLOCK:(2 * p + 1) * Q_BLOCK] = t[:A_ROPE].astype(cdt)
            qT_ref[0, j, A_KV_RANK:, (2 * p + 1) * Q_BLOCK:(2 * p + 2) * Q_BLOCK] = t[A_ROPE:].astype(cdt)
        for p in range(IDX_HEADS * IDX_DIM // LANE):
            t = tr(iq[rows, p * LANE:(p + 1) * LANE])
            iqT_ref[0, j, :, (2 * p) * Q_BLOCK:(2 * p + 1) * Q_BLOCK] = t[:IDX_DIM].astype(cdt)
            iqT_ref[0, j, :, (2 * p + 1) * Q_BLOCK:(2 * p + 2) * Q_BLOCK] = t[IDX_DIM:].astype(cdt)
        iw_ref[0, j] = narT[NARROW_IW:NARROW_IW + IDX_HEADS, rows] * (IDX_HEADS * IDX_DIM) ** -0.5
        for h in range(C_HEADS):
            t = tr(cq[rows, h * LANE:(h + 1) * LANE])
            qbd_ref[0, j, h, :, :Q_BLOCK] = jnp.where(top, t, 0.0).astype(cdt)
            qbd_ref[0, j, h, :, Q_BLOCK:] = jnp.where(top, 0.0, t).astype(cdt)
    ck = rope(ck_ref[...])
    cv = cv_ref[...]
    for h in range(C_HEADS):
        kc_ref[0, 0, h] = ck[:, h * LANE:(h + 1) * LANE].astype(cdt)
        vT_ref[0, 0, h] = tr(cv[:, h * LANE:(h + 1) * LANE]).astype(cdt)
    bqk = rope(bqk_ref[...])
    rq_ref[0] = bqk[:, :B_HEADS * B_QK].astype(cdt)
    for p in range(B_HEADS * B_QK // LANE):
        t = tr(bqk[:, B_HEADS * B_QK + p * LANE:B_HEADS * B_QK + (p + 1) * LANE]) * B_QK ** -0.5
        rkT_ref[0, 2 * p] = jnp.where(top_wide, t, 0.0).astype(cdt)
        rkT_ref[0, 2 * p + 1] = jnp.where(top_wide, 0.0, t).astype(cdt)


def _mixer_prep(proj, cos, sin, kv_norm, wukT, bsz, seq):
    nt = seq // PREP_TILE
    nq = seq // Q_BLOCK
    cdt = MXU_DTYPE
    col = lambda width, c: pl.BlockSpec((PREP_TILE, width), lambda b, t: (b * nt + t, c))
    table = pl.BlockSpec((1, PREP_TILE, LANE), lambda b, t: (b, t, 0))
    per_q = lambda *tail: pl.BlockSpec((1, PREP_TILE // Q_BLOCK) + tail, lambda b, t: (b, t) + (0,) * len(tail))
    per_c = lambda *tail: pl.BlockSpec((1, 1) + tail, lambda b, t: (b, t) + (0,) * len(tail))
    sds = jax.ShapeDtypeStruct
    outs = [
        (sds((bsz, nq, A_LAT, A_HEADS * Q_BLOCK), cdt), per_q(A_LAT, A_HEADS * Q_BLOCK)),
        (sds((bsz, nq, IDX_DIM, IDX_HEADS * Q_BLOCK), cdt), per_q(IDX_DIM, IDX_HEADS * Q_BLOCK)),
        (sds((bsz, nq, IDX_HEADS, Q_BLOCK), jnp.float32), per_q(IDX_HEADS, Q_BLOCK)),
        (sds((bsz, nt, KCHUNK, IDX_DIM), cdt), per_c(KCHUNK, IDX_DIM)),
        (sds((bsz, nt, KCHUNK, A_LAT), cdt), per_c(KCHUNK, A_LAT)),
        (sds((bsz, nt, A_KV_RANK, KCHUNK), cdt), per_c(A_KV_RANK, KCHUNK)),
        (sds((bsz, nq, C_HEADS, 2 * C_DH, 2 * Q_BLOCK), cdt), per_q(C_HEADS, 2 * C_DH, 2 * Q_BLOCK)),
        (sds((bsz, nt, C_HEADS, KCHUNK, 2 * C_DH), cdt), per_c(C_HEADS, KCHUNK, 2 * C_DH)),
        (sds((bsz, nt, C_HEADS, 2 * C_DH, KCHUNK), cdt), per_c(C_HEADS, 2 * C_DH, KCHUNK)),
        (sds((bsz, seq, B_HEADS * B_QK), cdt), pl.BlockSpec((1, PREP_TILE, B_HEADS * B_QK), lambda b, t: (b, t, 0))),
        (sds((bsz, B_HEADS, LANE, seq), cdt), pl.BlockSpec((1, B_HEADS, LANE, PREP_TILE), lambda b, t: (b, 0, 0, t))),
        (sds((bsz, 2 * D_HEADS, seq), jnp.float32), pl.BlockSpec((1, 2 * D_HEADS, PREP_TILE), lambda b, t: (b, 0, t))),
    ]
    return pl.pallas_call(
        _prep_body,
        grid=(bsz, nt),
        in_specs=[col(COL, COL_B_QK), col(COL, COL_C_Q), col(COL, COL_C_K), col(COL, COL_C_V),
                  col(COL, COL_A_CKV_ROPE), col(COL, COL_A_NOPE), col(COL, COL_A_IQ), col(NARROW, COL_NARROW),
                  table, table,
                  pl.BlockSpec((1, A_KV_RANK), lambda b, t: (0, 0)),
                  pl.BlockSpec((A_HEADS, A_KV_RANK, A_NOPE), lambda b, t: (0, 0, 0))],
        out_specs=[o[1] for o in outs],
        out_shape=[o[0] for o in outs],
        compiler_params=_params("arbitrary", "arbitrary"),
        name="mixer_prep",
    )(proj, proj, proj, proj, proj, proj, proj, proj, cos, sin, kv_norm.reshape(1, A_KV_RANK), wukT)


def _rope_tables(positions):
    half = ROPE_DIM // 2
    inv_freq = ROPE_THETA ** (-jnp.arange(half, dtype=jnp.float32) / half)
    ang = positions.astype(jnp.float32)[:, :, None] * inv_freq
    cos, sin = jnp.cos(ang), jnp.sin(ang)
    reps = LANE // ROPE_DIM
    return (jnp.concatenate([cos, cos] * reps, -1), jnp.concatenate([-sin, sin] * reps, -1))


_SIGN_BIT = np.int32(-2 ** 31)
_LOW31 = np.int32(2 ** 31 - 1)
_KEY_NEG_INF = np.int32(np.array(-np.inf, np.float32).view(np.int32) ^ _LOW31)


def _dsa_body(iqT_ref, iw_ref, qT_ref, ik_ref, kv_ref, ckvT_ref, wuv_ref, o_ref,
              key_ref, acc_ref, thr_ref, need_ref, *, n_keep):
    f32 = jnp.float32
    qi = pl.program_id(1)
    nk = (qi * Q_BLOCK + Q_BLOCK + KCHUNK - 1) // KCHUNK
    iqT = iqT_ref[0, 0]
    iw8 = iw_ref[0, 0]
    iw = jnp.concatenate([iw8[h:h + 1] for h in range(IDX_HEADS)], axis=1)
    qpos = qi * Q_BLOCK + lax.broadcasted_iota(jnp.int32, (KCHUNK, Q_BLOCK), 1)
    krow = lax.broadcasted_iota(jnp.int32, (KCHUNK, Q_BLOCK), 0)

    def index_chunk(c, carry):
        lg = jnp.dot(ik_ref[0, c], iqT, preferred_element_type=f32)
        w = jnp.maximum(lg, 0.0) * iw
        idx = w[:, :Q_BLOCK]
        for h in range(1, IDX_HEADS):
            idx = idx + w[:, h * Q_BLOCK:(h + 1) * Q_BLOCK]
        idx = jnp.where(idx == 0.0, 0.0, idx)
        idx = jnp.where(krow + c * KCHUNK <= qpos, idx, -jnp.inf)
        bits = pltpu.bitcast(idx, jnp.int32)
        key_ref[c] = bits ^ ((bits >> 31) & _LOW31)
        return carry

    lax.fori_loop(0, nk, index_chunk, 0)

    def count(cmp, t):
        def body(c, cnt):
            hit = cmp(key_ref[c], t).astype(jnp.int32)
            return cnt + hit.reshape(KCHUNK // SUBLANE, SUBLANE, Q_BLOCK).sum(0)
        cnt = lax.fori_loop(0, nk, body, jnp.zeros((SUBLANE, Q_BLOCK), jnp.int32))
        return cnt.sum(0, keepdims=True)

    @pl.when(qi * Q_BLOCK + Q_BLOCK <= n_keep)
    def _():
        thr_ref[...] = jnp.full((1, Q_BLOCK), _KEY_NEG_INF, jnp.int32)
        need_ref[...] = jnp.zeros((1, Q_BLOCK), jnp.int32)

    @pl.when(qi * Q_BLOCK + Q_BLOCK > n_keep)
    def _():
        def bit_step(i, prefix):
            cand = prefix | jnp.left_shift(jnp.int32(1), 31 - i)
            cnt = count(lambda k, t: k >= t, cand ^ _SIGN_BIT)
            return jnp.where(cnt >= n_keep, cand, prefix)
        prefix = lax.fori_loop(0, 32, bit_step, jnp.zeros((1, Q_BLOCK), jnp.int32))
        thr = prefix ^ _SIGN_BIT
        thr_ref[...] = thr
        need_ref[...] = n_keep - count(lambda k, t: k > t, thr)

    thr = thr_ref[...]
    need = need_ref[...].astype(f32)
    qT = qT_ref[0, 0]
    scale = (A_NOPE + A_ROPE) ** -0.5
    r_i = lax.broadcasted_iota(jnp.int32, (KCHUNK, KCHUNK), 0)
    c_i = lax.broadcasted_iota(jnp.int32, (KCHUNK, KCHUNK), 1)
    before = (c_i < r_i).astype(jnp.bfloat16)
    acc_ref[...] = jnp.zeros_like(acc_ref)

    def attend_chunk(c, carry):
        m, l, ties_seen = carry
        s = jnp.dot(kv_ref[0, c], qT, preferred_element_type=f32) * scale
        key = key_ref[c]
        tie = key == thr
        tie_f = tie.astype(f32)
        rank = jnp.dot(before, tie_f.astype(jnp.bfloat16), preferred_element_type=f32) + ties_seen
        keep = (key > thr) | (tie & (rank < need))
        bias = jnp.where(keep, 0.0, _MASKED)
        s = s + jnp.concatenate([bias] * A_HEADS, axis=1)
        m_new = jnp.maximum(m, s.max(0, keepdims=True))
        alpha = jnp.exp(m - m_new)
        p = jnp.exp(s - m_new)
        l = alpha * l + p.sum(0, keepdims=True)
        pv = jnp.dot(ckvT_ref[0, c], p.astype(ckvT_ref.dtype), preferred_element_type=f32)
        acc_ref[...] = acc_ref[...] * alpha + pv
        return m_new, l, ties_seen + tie_f.sum(0, keepdims=True)

    lanes = A_HEADS * Q_BLOCK
    init = (jnp.full((1, lanes), _MASKED, f32), jnp.zeros((1, lanes), f32), jnp.zeros((1, Q_BLOCK), f32))
    _, l, _ = lax.fori_loop(0, nk, attend_chunk, init)
    o_latT = acc_ref[...] / l
    for h in range(A_HEADS):
        o_lat = o_latT[:, h * Q_BLOCK:(h + 1) * Q_BLOCK].T.astype(wuv_ref.dtype)
        o_ref[0, :, h * A_V:(h + 1) * A_V] = jnp.dot(o_lat, wuv_ref[h], preferred_element_type=f32)


def _dsa_attention(iqT, iw, qT, ikc, kvc, ckvT, wuv, *, n_keep):
    bsz, nq = qT.shape[:2]
    nc = ikc.shape[1]
    seq = nq * Q_BLOCK
    per_q = lambda b, q: (b, q, 0, 0)
    per_b = lambda b, q: (b, 0, 0, 0)
    return pl.pallas_call(
        functools.partial(_dsa_body, n_keep=n_keep),
        grid=(bsz, nq),
        in_specs=[pl.BlockSpec((1, 1) + iqT.shape[2:], per_q),
                  pl.BlockSpec((1, 1) + iw.shape[2:], per_q),
                  pl.BlockSpec((1, 1) + qT.shape[2:], per_q),
                  pl.BlockSpec((1,) + ikc.shape[1:], per_b),
                  pl.BlockSpec((1,) + kvc.shape[1:], per_b),
                  pl.BlockSpec((1,) + ckvT.shape[1:], per_b),
                  pl.BlockSpec(wuv.shape, lambda b, q: (0, 0, 0))],
        out_specs=pl.BlockSpec((1, Q_BLOCK, A_HEADS * A_V), lambda b, q: (b, q, 0)),
        out_shape=jax.ShapeDtypeStruct((bsz, seq, A_HEADS * A_V), jnp.float32),
        scratch_shapes=[pltpu.VMEM((nc, KCHUNK, Q_BLOCK), jnp.int32),
                        pltpu.VMEM((A_KV_RANK, A_HEADS * Q_BLOCK), jnp.float32),
                        pltpu.VMEM((1, Q_BLOCK), jnp.int32),
                        pltpu.VMEM((1, Q_BLOCK), jnp.int32)],
        compiler_params=_params("arbitrary", "arbitrary"),
        name="dsa_attention",
    )(iqT, iw, qT, ikc, kvc, ckvT, wuv)


RET_TILE = 256


def _retention_body(q_ref, kT_ref, v_ref, g_ref, o_ref, s_ref):
    f32 = jnp.float32
    cdt = MXU_DTYPE
    c = RET_CHUNK

    @pl.when(pl.program_id(1) == 0)
    def _():
        s_ref[...] = jnp.zeros_like(s_ref)

    rel = (lax.broadcasted_iota(jnp.int32, (c, c), 0) - lax.broadcasted_iota(jnp.int32, (c, c), 1)).astype(f32)
    pos_c = lax.broadcasted_iota(jnp.int32, (c, 1), 0).astype(f32)
    pos_r = lax.broadcasted_iota(jnp.int32, (1, c), 1).astype(f32)

    def mm(a, b):
        return jnp.dot(a.astype(cdt), b.astype(cdt), preferred_element_type=f32)

    for h in range(B_HEADS):
        log_gamma = float(np.log(np.float32(1.0) - np.float32(2.0) ** np.float32(-5 - h)))
        intra = jnp.where(rel >= 0, jnp.exp(log_gamma * jnp.maximum(rel, 0.0)), 0.0)
        to_end = jnp.exp(log_gamma * (c - 1 - pos_r))
        from_start = jnp.exp(log_gamma * (pos_c + 1.0))
        chunk_decay = float(np.exp(np.float32(log_gamma) * np.float32(c)))
        lanes = slice(h * B_V, (h + 1) * B_V)
        pair = slice((h // 2) * LANE, (h // 2 + 1) * LANE)
        for ci in range(RET_TILE // c):
            rows = slice(ci * c, (ci + 1) * c)
            q = q_ref[0, rows, pair].astype(f32)
            kT = kT_ref[0, h, :, rows].astype(f32)
            v = v_ref[rows, lanes]
            state = s_ref[h]
            o = mm(mm(q, kT) * intra, v) + mm(q * from_start, state)
            s_ref[h] = state * chunk_decay + mm(kT * to_end, v)
            gate = g_ref[rows, lanes]
            o = o * lax.rsqrt(jnp.mean(o * o, -1, keepdims=True) + RMS_EPS)
            o_ref[0, rows, lanes] = o * (gate * _sigmoid(gate))


def _retention(rq, rkT, proj, bsz, seq):
    nt = seq // RET_TILE
    width = B_HEADS * B_V
    col = lambda c: pl.BlockSpec((RET_TILE, COL), lambda b, t: (b * nt + t, c))
    return pl.pallas_call(
        _retention_body,
        grid=(bsz, nt),
        in_specs=[pl.BlockSpec((1, RET_TILE, B_HEADS * B_QK), lambda b, t: (b, t, 0)),
                  pl.BlockSpec((1, B_HEADS, LANE, RET_TILE), lambda b, t: (b, 0, 0, t)),
                  col(COL_B_V), col(COL_B_G)],
        out_specs=pl.BlockSpec((1, RET_TILE, width), lambda b, t: (b, t, 0)),
        out_shape=jax.ShapeDtypeStruct((bsz, seq, width), jnp.float32),
        scratch_shapes=[pltpu.VMEM((B_HEADS, LANE, B_V), jnp.float32)],
        compiler_params=_params("arbitrary", "arbitrary"),
        name="retention",
    )(rq, rkT, proj, proj)


def _diff_attention_body(lam_ref, qbd_ref, k_ref, vT_ref, gain_ref, o_ref, acc_ref):
    f32 = jnp.float32
    qi = pl.program_id(1)
    nk = (qi * Q_BLOCK + Q_BLOCK + KCHUNK - 1) // KCHUNK
    qpos = qi * Q_BLOCK + lax.broadcasted_iota(jnp.int32, (KCHUNK, Q_BLOCK), 1)
    krow = lax.broadcasted_iota(jnp.int32, (KCHUNK, Q_BLOCK), 0)
    acc_ref[...] = jnp.zeros_like(acc_ref)

    def chunk(c, carry):
        ms, ls = carry
        causal = jnp.where(krow + c * KCHUNK <= qpos, 0.0, _MASKED)
        bias = jnp.concatenate([causal, causal], axis=1)
        heads = range(C_HEADS)
        s = [jnp.dot(k_ref[0, c, h], qbd_ref[0, 0, h], preferred_element_type=f32) + bias for h in heads]
        new_ms = [jnp.maximum(ms[h], s[h].max(0, keepdims=True)) for h in heads]
        alpha = [jnp.exp(ms[h] - new_ms[h]) for h in heads]
        p = [jnp.exp(s[h] - new_ms[h]) for h in heads]
        new_ls = [alpha[h] * ls[h] + p[h].sum(0, keepdims=True) for h in heads]
        pv = [jnp.dot(vT_ref[0, c, h], p[h].astype(vT_ref.dtype), preferred_element_type=f32) for h in heads]
        for h in heads:
            acc_ref[h] = acc_ref[h] * alpha[h] + pv[h]
        return tuple(new_ms), tuple(new_ls)

    lanes = 2 * Q_BLOCK
    init = (tuple(jnp.full((1, lanes), _MASKED, f32) for _ in range(C_HEADS)),
            tuple(jnp.zeros((1, lanes), f32) for _ in range(C_HEADS)))
    _, ls = lax.fori_loop(0, nk, chunk, init)
    lam = lam_ref[0]
    for h in range(C_HEADS):
        o = acc_ref[h] / ls[h]
        o = o[:, :Q_BLOCK] - lam * o[:, Q_BLOCK:]
        o = o * lax.rsqrt(jnp.mean(o * o, 0, keepdims=True) + RMS_EPS)
        o_ref[0, :, h * 2 * C_DH:(h + 1) * 2 * C_DH] = o.T * gain_ref[...]


def _diff_attention(lam, qbd, kc, vT, gain):
    bsz, nq = qbd.shape[:2]
    seq = nq * Q_BLOCK
    width = C_HEADS * 2 * C_DH
    return pl.pallas_call(
        _diff_attention_body,
        grid=(bsz, nq),
        in_specs=[pl.BlockSpec(memory_space=pltpu.SMEM),
                  pl.BlockSpec((1, 1) + qbd.shape[2:], lambda b, q: (b, q, 0, 0, 0)),
                  pl.BlockSpec((1,) + kc.shape[1:], lambda b, q: (b, 0, 0, 0, 0)),
                  pl.BlockSpec((1,) + vT.shape[1:], lambda b, q: (b, 0, 0, 0, 0)),
                  pl.BlockSpec(gain.shape, lambda b, q: (0, 0))],
        out_specs=pl.BlockSpec((1, Q_BLOCK, width), lambda b, q: (b, q, 0)),
        out_shape=jax.ShapeDtypeStruct((bsz, seq, width), jnp.float32),
        scratch_shapes=[pltpu.VMEM((C_HEADS, 2 * C_DH, 2 * Q_BLOCK), jnp.float32)],
        compiler_params=_params("arbitrary", "arbitrary"),
        name="diff_attention",
    )(lam, qbd, kc, vT, gain)


GDN_TILE = 256
GDN_SUB = 16


def _gdn_body(qkv_ref, z_ref, nar_ref, gateT_ref, conv_ref, alog_r_ref, dtb_r_ref, alog_c_ref, dtb_c_ref,
              ng_ref, o_ref, halo_ref, s_ref):
    f32 = jnp.float32
    cdt = MXU_DTYPE
    c = GDN_CHUNK

    @pl.when(pl.program_id(1) == 0)
    def _():
        halo_ref[...] = jnp.zeros_like(halo_ref)
        s_ref[...] = jnp.zeros_like(s_ref)

    x = qkv_ref[...]
    xp = jnp.concatenate([halo_ref[...], x], 0)
    w = conv_ref[...]
    pre = xp[SUBLANE - CONV_K + 1:SUBLANE - CONV_K + 1 + GDN_TILE] * w[0:1]
    for j in range(1, CONV_K):
        off = SUBLANE - CONV_K + 1 + j
        pre = pre + xp[off:off + GDN_TILE] * w[j:j + 1]
    halo_ref[...] = x[GDN_TILE - SUBLANE:]
    qkv = pre * _sigmoid(pre)

    gates = nar_ref[:, LANE:]
    beta_c = _sigmoid(gates[:, NARROW_DB:NARROW_DB + D_HEADS])
    g_c = -jnp.exp(alog_r_ref[...]) * _softplus(gates[:, NARROW_DA:NARROW_DA + D_HEADS] + dtb_r_ref[...])
    g_r = -jnp.exp(alog_c_ref[...]) * _softplus(gateT_ref[0][D_HEADS:] + dtb_c_ref[...])

    row = lax.broadcasted_iota(jnp.int32, (c, c), 0)
    col = lax.broadcasted_iota(jnp.int32, (c, c), 1)
    tril = row >= col
    strict = row > col
    same_sub = (row // GDN_SUB) == (col // GDN_SUB)
    eye = (row == col).astype(f32)
    lower_ones = tril.astype(f32)
    upper_ones = (row <= col).astype(f32)

    def mm(a, b):
        return jnp.dot(a, b, preferred_element_type=f32)

    n_chunks = GDN_TILE // c
    pairs = [(ci, h) for ci in range(n_chunks) for h in range(D_HEADS)]
    rows_of = lambda ci: slice(ci * c, (ci + 1) * c)
    g_cum_c = [mm(lower_ones, g_c[rows_of(ci)]) for ci in range(n_chunks)]
    g_cum_r = [mm(g_r[:, rows_of(ci)], upper_ones) for ci in range(n_chunks)]
    st = []
    for ci, h in pairs:
        rows = rows_of(ci)
        q = qkv[rows, h * D_K:(h + 1) * D_K]
        k = qkv[rows, D_HEADS * D_K + h * D_K:D_HEADS * D_K + (h + 1) * D_K]
        v = qkv[rows, 2 * D_HEADS * D_K + h * D_V:2 * D_HEADS * D_K + (h + 1) * D_V]
        q = q * lax.rsqrt(jnp.sum(q * q, -1, keepdims=True) + RMS_EPS) * D_K ** -0.5
        k = k * lax.rsqrt(jnp.sum(k * k, -1, keepdims=True) + RMS_EPS)
        gc = g_cum_c[ci][:, h:h + 1]
        gr = g_cum_r[ci][h:h + 1, :]
        g_last = gc[c - 1:c]
        gam = jnp.where(tril, jnp.exp(jnp.where(tril, gc - gr, 0.0)), 0.0)
        e_g = jnp.exp(gc)
        beta = beta_c[rows, h:h + 1]
        kT = k.T
        kb = k * beta
        st.append(dict(gam=gam, kT_c=kT.astype(cdt), kb_c=kb.astype(cdt), q_c=q.astype(cdt),
                       qg_c=(q * e_g).astype(cdt), rhs=jnp.concatenate([v * beta, kb * e_g], 1),
                       kdT_c=(kT * jnp.exp(g_last - gr)).astype(cdt), decay=jnp.exp(g_last)))
    for p in st:
        p["low"] = jnp.where(strict, mm(p["kb_c"], p["kT_c"]) * p["gam"], 0.0)
        p["attn_c"] = jnp.where(tril, mm(p["q_c"], p["kT_c"]) * p["gam"], 0.0).astype(cdt)
        p["low_d"] = jnp.where(same_sub, p["low"], 0.0)
        p["xk"] = -p["low_d"]
        p["td"] = eye + p["xk"]
    for _ in range(3):
        for p in st:
            p["xk"] = mm(p["xk"], p["xk"])
        for p in st:
            p["td"] = p["td"] + mm(p["td"], p["xk"])
    for p in st:
        p["m1"] = mm(p["td"], p["low"] - p["low_d"])
        p["sol"] = mm(p["td"], p["rhs"])
    for p in st:
        p["m2"] = mm(p["m1"], p["m1"])
    for p in st:
        p["sol"] = p["sol"] + mm(p["m2"], p["sol"])
    for p in st:
        p["sol"] = p["sol"] - mm(p["m1"], p["sol"])
    for ci in range(n_chunks):
        rows = rows_of(ci)
        cur = [st[ci * D_HEADS + h] for h in range(D_HEADS)]
        states = [s_ref[h] for h in range(D_HEADS)]
        states_c = [s.astype(cdt) for s in states]
        v_new_c = [(p["sol"][:, :D_V] - mm(p["sol"][:, D_V:].astype(cdt), states_c[h])).astype(cdt)
                   for h, p in enumerate(cur)]
        for h, p in enumerate(cur):
            s_ref[h] = states[h] * p["decay"] + mm(p["kdT_c"], v_new_c[h])
        for h, p in enumerate(cur):
            lanes = slice(h * D_V, (h + 1) * D_V)
            o = mm(p["qg_c"], states_c[h]) + mm(p["attn_c"], v_new_c[h])
            zz = z_ref[rows, lanes]
            o = o * lax.rsqrt(jnp.mean(o * o, -1, keepdims=True) + RMS_EPS) * ng_ref[...]
            o_ref[0, rows, lanes] = o * (zz * _sigmoid(zz))


def _gated_deltanet(proj, gatesT, conv_w, a_log, dt_bias, norm_g, bsz, seq):
    nt = seq // GDN_TILE
    width = D_HEADS * (2 * D_K + D_V)
    whole = lambda r, cc: pl.BlockSpec((r, cc), lambda b, t: (0, 0))
    return pl.pallas_call(
        _gdn_body,
        grid=(bsz, nt),
        in_specs=[pl.BlockSpec((GDN_TILE, width), lambda b, t: (b * nt + t, COL_D_QKV)),
                  pl.BlockSpec((GDN_TILE, COL), lambda b, t: (b * nt + t, COL_D_Z)),
                  pl.BlockSpec((GDN_TILE, NARROW), lambda b, t: (b * nt + t, COL_NARROW)),
                  pl.BlockSpec((1, 2 * D_HEADS, GDN_TILE), lambda b, t: (b, 0, t)),
                  whole(CONV_K, width), whole(1, D_HEADS), whole(1, D_HEADS), whole(D_HEADS, 1), whole(D_HEADS, 1),
                  whole(1, D_V)],
        out_specs=pl.BlockSpec((1, GDN_TILE, D_HEADS * D_V), lambda b, t: (b, t, 0)),
        out_shape=jax.ShapeDtypeStruct((bsz, seq, D_HEADS * D_V), jnp.float32),
        scratch_shapes=[pltpu.VMEM((SUBLANE, width), jnp.float32),
                        pltpu.VMEM((D_HEADS, D_K, D_V), jnp.float32)],
        compiler_params=_params("arbitrary", "arbitrary"),
        name="gated_deltanet",
    )(proj, proj, proj, gatesT, conv_w, a_log.reshape(1, D_HEADS), dt_bias.reshape(1, D_HEADS),
      a_log.reshape(D_HEADS, 1), dt_bias.reshape(D_HEADS, 1), norm_g.reshape(1, D_V))


def _ln_rows(t, g, b):
    mu = jnp.mean(t, -1, keepdims=True)
    d = t - mu
    var = jnp.mean(d * d, -1, keepdims=True)
    return d * lax.rsqrt(var + LN_EPS) * g + b


def _out_ln_router_body(ha_ref, hb_ref, hc_ref, hd_ref, w_ref, x_ref, g_ref, b_ref, wr_ref, br_ref,
                        h_ref, te_ref, tg_ref):
    f32 = jnp.float32
    acc = DEEPNORM_ALPHA * x_ref[...]
    row0 = 0
    for part in (ha_ref, hb_ref, hc_ref, hd_ref):
        width = part.shape[-1]
        acc = acc + jnp.dot(part[...].astype(MXU_DTYPE), w_ref[row0:row0 + width, :], preferred_element_type=f32)
        row0 += width
    h = _ln_rows(acc, g_ref[...], b_ref[...])
    h_ref[...] = h
    logits = jnp.dot(h.astype(MXU_DTYPE), wr_ref[...], preferred_element_type=f32) + br_ref[...]
    lane = lax.broadcasted_iota(jnp.int32, logits.shape, 1)
    vals, te = [], jnp.zeros(logits.shape, jnp.int32)
    for k in range(TOP_K):
        m = logits.max(-1, keepdims=True)
        e = jnp.min(jnp.where(logits == m, lane, LANE), -1, keepdims=True)
        vals.append(m)
        te = jnp.where(lane == k, e, te)
        logits = jnp.where(lane == e, -jnp.inf, logits)
    ex = [jnp.exp(v - vals[0]) for v in vals]
    den = ex[0] + ex[1] + ex[2] + ex[3]
    tg = jnp.zeros(logits.shape, f32)
    for k in range(TOP_K):
        tg = jnp.where(lane == k, ex[k] / den, tg)
    te_ref[...] = te
    tg_ref[...] = tg


def _out_ln_router(head_parts, w_out_c, x2d, g, b, w_router, b_router, tm=512):
    m = x2d.shape[0]
    k, n = w_out_c.shape
    wr = jnp.zeros((n, LANE), MXU_DTYPE).at[:, :N_EXPERTS].set(w_router.astype(MXU_DTYPE))
    br = jnp.full((1, LANE), -jnp.inf, jnp.float32).at[0, :N_EXPERTS].set(b_router)
    row_blk = lambda c: pl.BlockSpec((tm, c), lambda i: (i, 0))
    whole = lambda r, c: pl.BlockSpec((r, c), lambda i: (0, 0))
    return pl.pallas_call(
        _out_ln_router_body,
        grid=(m // tm,),
        in_specs=[row_blk(p.shape[1]) for p in head_parts]
        + [whole(k, n), row_blk(n), whole(1, n), whole(1, n), whole(n, LANE), whole(1, LANE)],
        out_specs=[row_blk(n), row_blk(LANE), row_blk(LANE)],
        out_shape=[jax.ShapeDtypeStruct((m, n), jnp.float32),
                   jax.ShapeDtypeStruct((m, LANE), jnp.int32),
                   jax.ShapeDtypeStruct((m, LANE), jnp.float32)],
        compiler_params=_params("arbitrary"),
        name="out_proj_ln_router",
    )(*head_parts, w_out_c, x2d, g.reshape(1, n), b.reshape(1, n), wr, br)


def _clamped_swiglu(hg):
    glu, lin = hg[..., :D_FF], hg[..., D_FF:]
    glu = jnp.minimum(glu, SWIGLU_LIMIT)
    lin = jnp.clip(lin, -SWIGLU_LIMIT, SWIGLU_LIMIT)
    return glu * _sigmoid(SWIGLU_ALPHA * glu) * (lin + 1.0)


def _moe_plan(top_e):
    n_tok = top_e.shape[0]
    n_blocks = -(-n_tok * TOP_K // MOE_BLOCK) + N_EXPERTS
    member = (top_e[:, :, None] == jnp.arange(N_EXPERTS, dtype=jnp.int32)).astype(jnp.int32).sum(1)
    rank = jnp.cumsum(member, axis=0) - member
    counts = member.sum(0)
    padded = (counts + MOE_BLOCK - 1) // MOE_BLOCK * MOE_BLOCK
    pad_end = jnp.cumsum(padded)
    dest = jnp.take_along_axis(rank + (pad_end - padded)[None, :], top_e, axis=1).astype(jnp.int32)
    tok = jnp.broadcast_to(jnp.arange(n_tok, dtype=jnp.int32)[:, None], dest.shape)
    row_tok = jnp.zeros((n_blocks * MOE_BLOCK,), jnp.int32).at[dest.reshape(-1)].set(tok.reshape(-1))
    block_e = jnp.minimum(jnp.searchsorted(pad_end, jnp.arange(n_blocks) * MOE_BLOCK, side='right'),
                          N_EXPERTS - 1).astype(jnp.int32)
    n_used = (pad_end[-1:] // MOE_BLOCK).astype(jnp.int32)
    return dest, row_tok, block_e, n_used


def _row_gather_start(src_hbm, rows_smem, n_rows, dst, sem):
    def body(r, carry):
        pltpu.make_async_copy(src_hbm.at[pl.ds(rows_smem[0, 0, r], 1)], dst.at[pl.ds(r, 1)], sem).start()
        return carry
    lax.fori_loop(0, n_rows, body, 0, unroll=8)


def _row_gather_wait(src_hbm, n_rows, dst, sem):
    pltpu.make_async_copy(src_hbm.at[pl.ds(0, n_rows)], dst, sem).wait()


def _moe_expert_body(be_ref, nused_ref, tok_ref, tok_next_ref, h_hbm, wgu_ref, bgu_ref, wdn_ref, bdn_ref,
                     y_ref, buf, sem, wgu_c, wdn_c):
    f32 = jnp.float32
    i = pl.program_id(0)
    n_used = nused_ref[0]
    slot = i % 2

    @pl.when(jnp.logical_and(i == 0, n_used > 0))
    def _():
        _row_gather_start(h_hbm, tok_ref, MOE_BLOCK, buf.at[0], sem.at[0])

    @pl.when(i + 1 < n_used)
    def _():
        _row_gather_start(h_hbm, tok_next_ref, MOE_BLOCK, buf.at[1 - slot], sem.at[1 - slot])

    @pl.when(jnp.logical_or(i == 0, be_ref[i] != be_ref[jnp.maximum(i - 1, 0)]))
    def _():
        wgu_c[...] = wgu_ref[0, 0].astype(MXU_DTYPE)
        wdn_c[...] = wdn_ref[0, 0].astype(MXU_DTYPE)

    @pl.when(i < n_used)
    def _():
        _row_gather_wait(h_hbm, MOE_BLOCK, buf.at[slot], sem.at[slot])
        xb = buf[slot].astype(MXU_DTYPE)
        hg = jnp.dot(xb, wgu_c[...], preferred_element_type=f32) + bgu_ref[0, 0]
        act = _clamped_swiglu(hg)
        y_ref[...] = jnp.dot(act.astype(MXU_DTYPE), wdn_c[...], preferred_element_type=f32) + bdn_ref[0, 0]

    @pl.when(i >= n_used)
    def _():
        y_ref[...] = jnp.zeros_like(y_ref)


def _moe_experts(h2d, row_tok, block_e, n_used, layer, w_gu, b_gu, w_dn, b_dn):
    n_tok, d = h2d.shape
    n_blocks = block_e.shape[0]
    n_layers = w_gu.shape[0]
    tok3 = row_tok.reshape(n_blocks, 1, MOE_BLOCK)
    smem_blk = lambda imap: pl.BlockSpec((1, 1, MOE_BLOCK), imap, memory_space=pltpu.SMEM)
    by_expert = lambda shape: pl.BlockSpec((1, 1) + shape, lambda i, be, nu: (layer, be[i], 0, 0))
    return pl.pallas_call(
        _moe_expert_body,
        grid_spec=pltpu.PrefetchScalarGridSpec(
            num_scalar_prefetch=2,
            grid=(n_blocks,),
            in_specs=[smem_blk(lambda i, be, nu: (i, 0, 0)),
                      smem_blk(lambda i, be, nu: (jnp.minimum(i + 1, n_blocks - 1), 0, 0)),
                      pl.BlockSpec(memory_space=pl.ANY),
                      by_expert((d, 2 * D_FF)), by_expert((1, 2 * D_FF)),
                      by_expert((D_FF, d)), by_expert((1, d))],
            out_specs=pl.BlockSpec((MOE_BLOCK, d), lambda i, be, nu: (i, 0)),
            scratch_shapes=[pltpu.VMEM((2, MOE_BLOCK, d), jnp.float32), pltpu.SemaphoreType.DMA((2,)),
                            pltpu.VMEM((d, 2 * D_FF), MXU_DTYPE), pltpu.VMEM((D_FF, d), MXU_DTYPE)]),
        out_shape=jax.ShapeDtypeStruct((n_blocks * MOE_BLOCK, d), jnp.float32),
        compiler_params=_params("arbitrary"),
        name="moe_experts",
    )(block_e, n_used, tok3, tok3, h2d, w_gu, b_gu.reshape(n_layers, N_EXPERTS, 1, 2 * D_FF),
      w_dn, b_dn.reshape(n_layers, N_EXPERTS, 1, d))


MOE_COMBINE_ROWS = 128


def _moe_combine_ln_body(dest_ref, dest_next_ref, h_ref, tg_ref, g_ref, b_ref, y_hbm, o_ref, buf, sem):
    tm = MOE_COMBINE_ROWS
    i = pl.program_id(0)
    n = pl.num_programs(0)
    slot = i % 2

    def start(rows_smem, s):
        for k in range(TOP_K):
            def body(r, carry):
                pltpu.make_async_copy(y_hbm.at[pl.ds(rows_smem[0, 0, k * tm + r], 1)],
                                      buf.at[s, k, pl.ds(r, 1)], sem.at[s]).start()
                return carry
            lax.fori_loop(0, tm, body, 0, unroll=8)

    @pl.when(i == 0)
    def _():
        start(dest_ref, 0)

    @pl.when(i + 1 < n)
    def _():
        start(dest_next_ref, 1 - slot)

    tg = tg_ref[...]
    y = jnp.zeros(h_ref.shape, jnp.float32)
    for k in range(TOP_K):
        _row_gather_wait(y_hbm, tm, buf.at[slot, k], sem.at[slot])
    for k in range(TOP_K):
        y = y + tg[:, k:k + 1] * buf[slot, k]
    o_ref[...] = _ln_rows(DEEPNORM_ALPHA * h_ref[...] + y, g_ref[...], b_ref[...])


def _moe_combine_ln(h2d, y_rows, dest, tg, g, b):
    n_tok, d = h2d.shape
    tm = MOE_COMBINE_ROWS
    nt = n_tok // tm
    dest3 = dest.reshape(nt, tm, TOP_K).transpose(0, 2, 1).reshape(nt, 1, TOP_K * tm)
    smem_blk = lambda imap: pl.BlockSpec((1, 1, TOP_K * tm), imap, memory_space=pltpu.SMEM)
    return pl.pallas_call(
        _moe_combine_ln_body,
        grid=(nt,),
        in_specs=[smem_blk(lambda i: (i, 0, 0)),
                  smem_blk(lambda i: (jnp.minimum(i + 1, nt - 1), 0, 0)),
                  pl.BlockSpec((tm, d), lambda i: (i, 0)),
                  pl.BlockSpec((tm, LANE), lambda i: (i, 0)),
                  pl.BlockSpec((1, d), lambda i: (0, 0)),
                  pl.BlockSpec((1, d), lambda i: (0, 0)),
                  pl.BlockSpec(memory_space=pl.ANY)],
        out_specs=pl.BlockSpec((tm, d), lambda i: (i, 0)),
        out_shape=jax.ShapeDtypeStruct((n_tok, d), jnp.float32),
        scratch_shapes=[pltpu.VMEM((2, TOP_K, tm, d), jnp.float32), pltpu.SemaphoreType.DMA((2,))],
        compiler_params=_params("arbitrary"),
        name="moe_combine_ln",
    )(dest3, dest3, h2d, tg, g.reshape(1, d), b.reshape(1, d), y_rows)


def _moe_layer(h2d, te, tg, layer, w_gu, b_gu, w_dn, b_dn, g, b):
    dest, row_tok, block_e, n_used = _moe_plan(te[:, :TOP_K])
    y_rows = _moe_experts(h2d, row_tok, block_e, n_used, layer, w_gu, b_gu, w_dn, b_dn)
    return _moe_combine_ln(h2d, y_rows, dest, tg, g, b)


def _mixers(proj, cos, sin, bsz, seq, lambda_init, a_kv_norm, a_w_uk, a_w_uv, c_lambda, c_subln,
            d_conv, d_a_log, d_dt_bias, d_norm):
    cdt = MXU_DTYPE
    (qT, iqT, iw, ikc, kvc, ckvT, qbd, kc, vT, rq, rkT, gatesT) = _mixer_prep(
        proj, cos, sin, a_kv_norm, a_w_uk.transpose(1, 0, 2).astype(cdt), bsz, seq)
    head_a = _dsa_attention(iqT, iw, qT, ikc, kvc, ckvT, a_w_uv.transpose(1, 0, 2).astype(cdt),
                            n_keep=min(INDEX_TOPK, seq // 4))
    head_b = _retention(rq, rkT, proj, bsz, seq)
    lam_full = (jnp.exp(jnp.sum(c_lambda[0] * c_lambda[1])) - jnp.exp(jnp.sum(c_lambda[2] * c_lambda[3]))
                + lambda_init)
    gain = (c_subln * (1.0 - lambda_init)).reshape(1, 2 * C_DH)
    head_c = _diff_attention(lam_full.reshape(1), qbd, kc, vT, gain)
    head_d = _gated_deltanet(proj, gatesT, d_conv, d_a_log, d_dt_bias, d_norm, bsz, seq)
    return [t.reshape(bsz * seq, t.shape[-1]) for t in (head_a, head_b, head_c, head_d)]


def kernel(x, positions, w_in, w_out, a_kv_norm, a_w_uk, a_w_uv, c_lambda, c_subln, d_conv, d_a_log, d_dt_bias, d_norm, ln1_g, ln1_b, w_router, b_router, w_gu, b_gu, w_dn, b_dn, ln2_g, ln2_b):
    bsz, seq, d = x.shape
    x2d = x.reshape(bsz * seq, d)
    cos, sin = _rope_tables(positions)
    for l in range(DEPTH):
        proj = _in_proj(x2d, _pack_w_in(w_in[l]))
        lambda_init = 0.8 - 0.6 * math.exp(-0.3 * l)
        heads = _mixers(proj, cos, sin, bsz, seq, lambda_init, a_kv_norm[l], a_w_uk[l], a_w_uv[l], c_lambda[l],
                        c_subln[l], d_conv[l], d_a_log[l], d_dt_bias[l], d_norm[l])
        h2d, te, tg = _out_ln_router(heads, w_out[l].astype(MXU_DTYPE), x2d, ln1_g[l], ln1_b[l],
                                     w_router[l], b_router[l])
        x2d = _moe_layer(h2d, te, tg, l, w_gu, b_gu, w_dn, b_dn, ln2_g[l], ln2_b[l])
    return x2d.reshape(bsz, seq, d)
```

```python
import functools
import math

import jax
import jax.numpy as jnp
import numpy as np
from jax import lax
from jax.experimental import pallas as pl
from jax.experimental.pallas import tpu as pltpu

D_MODEL = 2048
DEPTH = 4
A_HEADS, A_NOPE, A_ROPE, A_V, A_KV_RANK = 4, 128, 64, 128, 256
IDX_HEADS, IDX_DIM, INDEX_TOPK = 8, 64, 256
B_HEADS, B_QK, B_V, RET_CHUNK = 4, 64, 128, 128
C_HEADS, C_DH = 4, 64
D_HEADS, D_K, D_V, CONV_K, GDN_CHUNK = 4, 128, 128, 4, 64
N_EXPERTS, TOP_K, D_FF = 32, 4, 512
SWIGLU_ALPHA, SWIGLU_LIMIT, MOE_BLOCK = 1.702, 7.0, 256
ROPE_THETA = 10000.0
ROPE_DIM = 64
Q_BLOCK = 128
DEEPNORM_ALPHA = (2 * DEPTH) ** 0.25
LN_EPS = 1e-5
RMS_EPS = 1e-6
D_MIX = A_HEADS * A_V + B_HEADS * B_V + C_HEADS * 2 * C_DH + D_HEADS * D_V
A_LAT = A_KV_RANK + A_ROPE

IN_WIDTHS = (
    A_HEADS * (A_NOPE + A_ROPE), A_KV_RANK, A_ROPE, IDX_HEADS * IDX_DIM, IDX_DIM, IDX_HEADS,
    B_HEADS * B_QK, B_HEADS * B_QK, B_HEADS * B_V, B_HEADS * B_V,
    C_HEADS * 2 * C_DH, C_HEADS * 2 * C_DH, C_HEADS * 2 * C_DH,
    D_HEADS * (2 * D_K + D_V), D_HEADS * D_V, D_HEADS, D_HEADS,
)
_IN_OFFS = np.concatenate([[0], np.cumsum(IN_WIDTHS)]).tolist()

LANE = 128
SUBLANE = 8
VMEM_LIMIT = 56 * 1024 * 1024

MXU_DTYPE = jnp.bfloat16
_MASKED = -1e30

COL = 512
COL_D_QKV = 0
COL_D_Z = 3
COL_B_V = 4
COL_B_G = 5
COL_B_QK = 6
COL_C_Q = 7
COL_C_K = 8
COL_C_V = 9
COL_A_CKV_ROPE = 10
COL_A_NOPE = 11
COL_A_IQ = 12
NARROW = 256
COL_NARROW = 26
IN_PACKED = (COL_NARROW + 1) * NARROW
NARROW_IW = 0
NARROW_DB = IDX_HEADS
NARROW_DA = IDX_HEADS + D_HEADS


def _pack_w_in(w):
    g = lambda i: w[:, _IN_OFFS[i]:_IN_OFFS[i + 1]]
    a_q = g(0).reshape(w.shape[0], A_HEADS, A_NOPE + A_ROPE)
    nope = a_q[:, :, :A_NOPE].reshape(w.shape[0], A_HEADS * A_NOPE)
    rope = a_q[:, :, A_NOPE:].reshape(w.shape[0], A_HEADS * A_ROPE)
    cols = [g(13), g(14), g(8), g(9), g(6), g(7), g(10), g(11), g(12), g(1), rope, nope, g(3),
            g(2), g(4), g(5), g(15), g(16)]
    used = sum(c.shape[1] for c in cols)
    cols.append(jnp.zeros((w.shape[0], IN_PACKED - used), w.dtype))
    return jnp.concatenate(cols, axis=1).astype(MXU_DTYPE)


def _sigmoid(t):
    return 1.0 / (1.0 + jnp.exp(-t))


def _softplus(t):
    return jnp.maximum(t, 0.0) + jnp.log1p(jnp.exp(-jnp.abs(t)))


def _params(*sem):
    return pltpu.CompilerParams(dimension_semantics=sem, vmem_limit_bytes=VMEM_LIMIT)


def _in_proj_body(x_ref, w_ref, o_ref, xb_ref):
    @pl.when(pl.program_id(1) == 0)
    def _():
        xb_ref[...] = x_ref[...].astype(MXU_DTYPE)

    o_ref[...] = jnp.dot(xb_ref[...], w_ref[...], preferred_element_type=jnp.float32)


def _in_proj(x2d, w_packed, tm=1024, tn=768):
    m, k = x2d.shape
    n = w_packed.shape[1]
    return pl.pallas_call(
        _in_proj_body,
        grid=(m // tm, n // tn),
        in_specs=[pl.BlockSpec((tm, k), lambda i, j: (i, 0)),
                  pl.BlockSpec((k, tn), lambda i, j: (0, j))],
        out_specs=pl.BlockSpec((tm, tn), lambda i, j: (i, j)),
        out_shape=jax.ShapeDtypeStruct((m, n), jnp.float32),
        scratch_shapes=[pltpu.VMEM((tm, k), MXU_DTYPE)],
        compiler_params=_params("arbitrary", "arbitrary"),
        name="in_proj",
    )(x2d, w_packed)


PREP_TILE = 256
KCHUNK = 256


def _prep_body(bqk_ref, cq_ref, ck_ref, cv_ref, ckvrope_ref, nope_ref, iq_ref, nar_ref, cos_ref, sin_ref,
               kvn_ref, wukT_ref,
               qT_ref, iqT_ref, iw_ref, ik_ref, kv_ref, ckvT_ref, qbd_ref, kc_ref, vT_ref, rq_ref, rkT_ref, gT_ref):
    f32 = jnp.float32
    cdt = MXU_DTYPE
    cos, sin = cos_ref[0], sin_ref[0]
    lane = lax.broadcasted_iota(jnp.int32, (PREP_TILE, LANE), 1)
    first_half = (lane % ROPE_DIM) < ROPE_DIM // 2
    eye = (lax.broadcasted_iota(jnp.int32, (LANE, LANE), 0)
           == lax.broadcasted_iota(jnp.int32, (LANE, LANE), 1)).astype(cdt)
    top = lax.broadcasted_iota(jnp.int32, (LANE, LANE), 0) < LANE // 2
    top_wide = lax.broadcasted_iota(jnp.int32, (LANE, PREP_TILE), 0) < LANE // 2

    def rope(t):
        outs = []
        for j in range(t.shape[1] // LANE):
            ts = t[:, j * LANE:(j + 1) * LANE]
            partner = jnp.where(first_half, pltpu.roll(ts, LANE - ROPE_DIM // 2, 1), pltpu.roll(ts, ROPE_DIM // 2, 1))
            outs.append(ts * cos + partner * sin)
        return outs[0] if len(outs) == 1 else jnp.concatenate(outs, axis=1)

    def tr(t):
        return lax.dot_general(eye, t.astype(cdt), (((1,), (1,)), ((), ())), preferred_element_type=f32)

    ckv_rope = ckvrope_ref[...]
    a_ckv = ckv_rope[:, :A_KV_RANK]
    c_kv = a_ckv * lax.rsqrt(jnp.mean(a_ckv * a_ckv, -1, keepdims=True) + RMS_EPS) * kvn_ref[...]
    nar = nar_ref[...]
    kr_ik = rope(nar[:, :LANE])
    kv_ref[0, 0, :, :A_KV_RANK] = c_kv.astype(cdt)
    kv_ref[0, 0, :, A_KV_RANK:] = kr_ik[:, :A_ROPE].astype(cdt)
    ik_ref[0, 0] = kr_ik[:, A_ROPE:].astype(cdt)
    for rb in range(A_KV_RANK // LANE):
        ckvT_ref[0, 0, rb * LANE:(rb + 1) * LANE, :] = tr(c_kv[:, rb * LANE:(rb + 1) * LANE]).astype(cdt)
    q_rope = rope(ckv_rope[:, A_KV_RANK:])
    nope = nope_ref[...]
    iq = rope(iq_ref[...])
    narT = nar[:, LANE:].T
    gT_ref[0] = narT[NARROW_DB:NARROW_DB + 2 * D_HEADS]
    cq = rope(cq_ref[...]) * C_DH ** -0.5
    for j in range(PREP_TILE // Q_BLOCK):
        rows = slice(j * Q_BLOCK, (j + 1) * Q_BLOCK)
        for h in range(A_HEADS):
            q_latT = lax.dot_general(wukT_ref[h], nope[rows, h * A_NOPE:(h + 1) * A_NOPE].astype(cdt),
                                     (((1,), (1,)), ((), ())), preferred_element_type=f32)
            qT_ref[0, j, :A_KV_RANK, h * Q_BLOCK:(h + 1) * Q_BLOCK] = q_latT.astype(cdt)
        for p in range(A_HEADS * A_ROPE // LANE):
            t = tr(q_rope[rows, p * LANE:(p + 1) * LANE])
            qT_ref[0, j, A_KV_RANK:, (2 * p) * Q_BLOCK:(2 * p + 1) * Q_BLOCK] = t[:A_ROPE].astype(cdt)
            qT_ref[0, j, A_KV_RANK:, (2 * p + 1) * Q_BLOCK:(2 * p + 2) * Q_BLOCK] = t[A_ROPE:].astype(cdt)
        for p in range(IDX_HEADS * IDX_DIM // LANE):
            t = tr(iq[rows, p * LANE:(p + 1) * LANE])
            iqT_ref[0, j, :, (2 * p) * Q_BLOCK:(2 * p + 1) * Q_BLOCK] = t[:IDX_DIM].astype(cdt)
            iqT_ref[0, j, :, (2 * p + 1) * Q_BLOCK:(2 * p + 2) * Q_BLOCK] = t[IDX_DIM:].astype(cdt)
        iw_ref[0, j] = narT[NARROW_IW:NARROW_IW + IDX_HEADS, rows] * (IDX_HEADS * IDX_DIM) ** -0.5
        for h in range(C_HEADS):
            t = tr(cq[rows, h * LANE:(h + 1) * LANE])
            qbd_ref[0, j, h, :, :Q_BLOCK] = jnp.where(top, t, 0.0).astype(cdt)
            qbd_ref[0, j, h, :, Q_BLOCK:] = jnp.where(top, 0.0, t).astype(cdt)
    ck = rope(ck_ref[...])
    cv = cv_ref[...]
    for h in range(C_HEADS):
        kc_ref[0, 0, h] = ck[:, h * LANE:(h + 1) * LANE].astype(cdt)
        vT_ref[0, 0, h] = tr(cv[:, h * LANE:(h + 1) * LANE]).astype(cdt)
    bqk = rope(bqk_ref[...])
    rq_ref[0] = bqk[:, :B_HEADS * B_QK].astype(cdt)
    for p in range(B_HEADS * B_QK // LANE):
        t = tr(bqk[:, B_HEADS * B_QK + p * LANE:B_HEADS * B_QK + (p + 1) * LANE]) * B_QK ** -0.5
        rkT_ref[0, 2 * p] = jnp.where(top_wide, t, 0.0).astype(cdt)
        rkT_ref[0, 2 * p + 1] = jnp.where(top_wide, 0.0, t).astype(cdt)


def _mixer_prep(proj, cos, sin, kv_norm, wukT, bsz, seq):
    nt = seq // PREP_TILE
    nq = seq // Q_BLOCK
    cdt = MXU_DTYPE
    col = lambda width, c: pl.BlockSpec((PREP_TILE, width), lambda b, t: (b * nt + t, c))
    table = pl.BlockSpec((1, PREP_TILE, LANE), lambda b, t: (b, t, 0))
    per_q = lambda *tail: pl.BlockSpec((1, PREP_TILE // Q_BLOCK) + tail, lambda b, t: (b, t) + (0,) * len(tail))
    per_c = lambda *tail: pl.BlockSpec((1, 1) + tail, lambda b, t: (b, t) + (0,) * len(tail))
    sds = jax.ShapeDtypeStruct
    outs = [
        (sds((bsz, nq, A_LAT, A_HEADS * Q_BLOCK), cdt), per_q(A_LAT, A_HEADS * Q_BLOCK)),
        (sds((bsz, nq, IDX_DIM, IDX_HEADS * Q_BLOCK), cdt), per_q(IDX_DIM, IDX_HEADS * Q_BLOCK)),
        (sds((bsz, nq, IDX_HEADS, Q_BLOCK), jnp.float32), per_q(IDX_HEADS, Q_BLOCK)),
        (sds((bsz, nt, KCHUNK, IDX_DIM), cdt), per_c(KCHUNK, IDX_DIM)),
        (sds((bsz, nt, KCHUNK, A_LAT), cdt), per_c(KCHUNK, A_LAT)),
        (sds((bsz, nt, A_KV_RANK, KCHUNK), cdt), per_c(A_KV_RANK, KCHUNK)),
        (sds((bsz, nq, C_HEADS, 2 * C_DH, 2 * Q_BLOCK), cdt), per_q(C_HEADS, 2 * C_DH, 2 * Q_BLOCK)),
        (sds((bsz, nt, C_HEADS, KCHUNK, 2 * C_DH), cdt), per_c(C_HEADS, KCHUNK, 2 * C_DH)),
        (sds((bsz, nt, C_HEADS, 2 * C_DH, KCHUNK), cdt), per_c(C_HEADS, 2 * C_DH, KCHUNK)),
        (sds((bsz, seq, B_HEADS * B_QK), cdt), pl.BlockSpec((1, PREP_TILE, B_HEADS * B_QK), lambda b, t: (b, t, 0))),
        (sds((bsz, B_HEADS, LANE, seq), cdt), pl.BlockSpec((1, B_HEADS, LANE, PREP_TILE), lambda b, t: (b, 0, 0, t))),
        (sds((bsz, 2 * D_HEADS, seq), jnp.float32), pl.BlockSpec((1, 2 * D_HEADS, PREP_TILE), lambda b, t: (b, 0, t))),
    ]
    return pl.pallas_call(
        _prep_body,
        grid=(bsz, nt),
        in_specs=[col(COL, COL_B_QK), col(COL, COL_C_Q), col(COL, COL_C_K), col(COL, COL_C_V),
                  col(COL, COL_A_CKV_ROPE), col(COL, COL_A_NOPE), col(COL, COL_A_IQ), col(NARROW, COL_NARROW),
                  table, table,
                  pl.BlockSpec((1, A_KV_RANK), lambda b, t: (0, 0)),
                  pl.BlockSpec((A_HEADS, A_KV_RANK, A_NOPE), lambda b, t: (0, 0, 0))],
        out_specs=[o[1] for o in outs],
        out_shape=[o[0] for o in outs],
        compiler_params=_params("arbitrary", "arbitrary"),
        name="mixer_prep",
    )(proj, proj, proj, proj, proj, proj, proj, proj, cos, sin, kv_norm.reshape(1, A_KV_RANK), wukT)


def _rope_tables(positions):
    half = ROPE_DIM // 2
    inv_freq = ROPE_THETA ** (-jnp.arange(half, dtype=jnp.float32) / half)
    ang = positions.astype(jnp.float32)[:, :, None] * inv_freq
    cos, sin = jnp.cos(ang), jnp.sin(ang)
    reps = LANE // ROPE_DIM
    return (jnp.concatenate([cos, cos] * reps, -1), jnp.concatenate([-sin, sin] * reps, -1))


_SIGN_BIT = np.int32(-2 ** 31)
_LOW31 = np.int32(2 ** 31 - 1)
_KEY_NEG_INF = np.int32(np.array(-np.inf, np.float32).view(np.int32) ^ _LOW31)


def _dsa_body(iqT_ref, iw_ref, qT_ref, ik_ref, kv_ref, ckvT_ref, wuv_ref, o_ref,
              key_ref, acc_ref, thr_ref, need_ref, *, n_keep):
    f32 = jnp.float32
    qi = pl.program_id(1)
    nk = (qi * Q_BLOCK + Q_BLOCK + KCHUNK - 1) // KCHUNK
    iqT = iqT_ref[0, 0]
    iw8 = iw_ref[0, 0]
    iw = jnp.concatenate([iw8[h:h + 1] for h in range(IDX_HEADS)], axis=1)
    qpos = qi * Q_BLOCK + lax.broadcasted_iota(jnp.int32, (KCHUNK, Q_BLOCK), 1)
    krow = lax.broadcasted_iota(jnp.int32, (KCHUNK, Q_BLOCK), 0)

    def index_chunk(c, carry):
        lg = jnp.dot(ik_ref[0, c], iqT, preferred_element_type=f32)
        w = jnp.maximum(lg, 0.0) * iw
        idx = w[:, :Q_BLOCK]
        for h in range(1, IDX_HEADS):
            idx = idx + w[:, h * Q_BLOCK:(h + 1) * Q_BLOCK]
        idx = jnp.where(idx == 0.0, 0.0, idx)
        idx = jnp.where(krow + c * KCHUNK <= qpos, idx, -jnp.inf)
        bits = pltpu.bitcast(idx, jnp.int32)
        key_ref[c] = bits ^ ((bits >> 31) & _LOW31)
        return carry

    lax.fori_loop(0, nk, index_chunk, 0)

    def count(cmp, t):
        def body(c, cnt):
            hit = cmp(key_ref[c], t).astype(jnp.int32)
            return cnt + hit.reshape(KCHUNK // SUBLANE, SUBLANE, Q_BLOCK).sum(0)
        cnt = lax.fori_loop(0, nk, body, jnp.zeros((SUBLANE, Q_BLOCK), jnp.int32))
        return cnt.sum(0, keepdims=True)

    @pl.when(qi * Q_BLOCK + Q_BLOCK <= n_keep)
    def _():
        thr_ref[...] = jnp.full((1, Q_BLOCK), _KEY_NEG_INF, jnp.int32)
        need_ref[...] = jnp.zeros((1, Q_BLOCK), jnp.int32)

    @pl.when(qi * Q_BLOCK + Q_BLOCK > n_keep)
    def _():
        def bit_step(i, prefix):
            cand = prefix | jnp.left_shift(jnp.int32(1), 31 - i)
            cnt = count(lambda k, t: k >= t, cand ^ _SIGN_BIT)
            return jnp.where(cnt >= n_keep, cand, prefix)
        prefix = lax.fori_loop(0, 32, bit_step, jnp.zeros((1, Q_BLOCK), jnp.int32))
        thr = prefix ^ _SIGN_BIT
        thr_ref[...] = thr
        need_ref[...] = n_keep - count(lambda k, t: k > t, thr)

    thr = thr_ref[...]
    need = need_ref[...].astype(f32)
    qT = qT_ref[0, 0]
    scale = (A_NOPE + A_ROPE) ** -0.5
    r_i = lax.broadcasted_iota(jnp.int32, (KCHUNK, KCHUNK), 0)
    c_i = lax.broadcasted_iota(jnp.int32, (KCHUNK, KCHUNK), 1)
    before = (c_i < r_i).astype(jnp.bfloat16)
    acc_ref[...] = jnp.zeros_like(acc_ref)

    def attend_chunk(c, carry):
        m, l, ties_seen = carry
        s = jnp.dot(kv_ref[0, c], qT, preferred_element_type=f32) * scale
        key = key_ref[c]
        tie = key == thr
        tie_f = tie.astype(f32)
        rank = jnp.dot(before, tie_f.astype(jnp.bfloat16), preferred_element_type=f32) + ties_seen
        keep = (key > thr) | (tie & (rank < need))
        bias = jnp.where(keep, 0.0, _MASKED)
        s = s + jnp.concatenate([bias] * A_HEADS, axis=1)
        m_new = jnp.maximum(m, s.max(0, keepdims=True))
        alpha = jnp.exp(m - m_new)
        p = jnp.exp(s - m_new)
        l = alpha * l + p.sum(0, keepdims=True)
        pv = jnp.dot(ckvT_ref[0, c], p.astype(ckvT_ref.dtype), preferred_element_type=f32)
        acc_ref[...] = acc_ref[...] * alpha + pv
        return m_new, l, ties_seen + tie_f.sum(0, keepdims=True)

    lanes = A_HEADS * Q_BLOCK
    init = (jnp.full((1, lanes), _MASKED, f32), jnp.zeros((1, lanes), f32), jnp.zeros((1, Q_BLOCK), f32))
    _, l, _ = lax.fori_loop(0, nk, attend_chunk, init)
    o_latT = acc_ref[...] / l
    for h in range(A_HEADS):
        o_lat = o_latT[:, h * Q_BLOCK:(h + 1) * Q_BLOCK].T.astype(wuv_ref.dtype)
        o_ref[0, :, h * A_V:(h + 1) * A_V] = jnp.dot(o_lat, wuv_ref[h], preferred_element_type=f32)


def _dsa_attention(iqT, iw, qT, ikc, kvc, ckvT, wuv, *, n_keep):
    bsz, nq = qT.shape[:2]
    nc = ikc.shape[1]
    seq = nq * Q_BLOCK
    per_q = lambda b, q: (b, q, 0, 0)
    per_b = lambda b, q: (b, 0, 0, 0)
    return pl.pallas_call(
        functools.partial(_dsa_body, n_keep=n_keep),
        grid=(bsz, nq),
        in_specs=[pl.BlockSpec((1, 1) + iqT.shape[2:], per_q),
                  pl.BlockSpec((1, 1) + iw.shape[2:], per_q),
                  pl.BlockSpec((1, 1) + qT.shape[2:], per_q),
                  pl.BlockSpec((1,) + ikc.shape[1:], per_b),
                  pl.BlockSpec((1,) + kvc.shape[1:], per_b),
                  pl.BlockSpec((1,) + ckvT.shape[1:], per_b),
                  pl.BlockSpec(wuv.shape, lambda b, q: (0, 0, 0))],
        out_specs=pl.BlockSpec((1, Q_BLOCK, A_HEADS * A_V), lambda b, q: (b, q, 0)),
        out_shape=jax.ShapeDtypeStruct((bsz, seq, A_HEADS * A_V), jnp.float32),
        scratch_shapes=[pltpu.VMEM((nc, KCHUNK, Q_BLOCK), jnp.int32),
                        pltpu.VMEM((A_KV_RANK, A_HEADS * Q_BLOCK), jnp.float32),
                        pltpu.VMEM((1, Q_BLOCK), jnp.int32),
                        pltpu.VMEM((1, Q_BLOCK), jnp.int32)],
        compiler_params=_params("arbitrary", "arbitrary"),
        name="dsa_attention",
    )(iqT, iw, qT, ikc, kvc, ckvT, wuv)


RET_TILE = 256


def _retention_body(q_ref, kT_ref, v_ref, g_ref, o_ref, s_ref):
    f32 = jnp.float32
    cdt = MXU_DTYPE
    c = RET_CHUNK

    @pl.when(pl.program_id(1) == 0)
    def _():
        s_ref[...] = jnp.zeros_like(s_ref)

    rel = (lax.broadcasted_iota(jnp.int32, (c, c), 0) - lax.broadcasted_iota(jnp.int32, (c, c), 1)).astype(f32)
    pos_c = lax.broadcasted_iota(jnp.int32, (c, 1), 0).astype(f32)
    pos_r = lax.broadcasted_iota(jnp.int32, (1, c), 1).astype(f32)

    def mm(a, b):
        return jnp.dot(a.astype(cdt), b.astype(cdt), preferred_element_type=f32)

    for h in range(B_HEADS):
        log_gamma = float(np.log(np.float32(1.0) - np.float32(2.0) ** np.float32(-5 - h)))
        intra = jnp.where(rel >= 0, jnp.exp(log_gamma * jnp.maximum(rel, 0.0)), 0.0)
        to_end = jnp.exp(log_gamma * (c - 1 - pos_r))
        from_start = jnp.exp(log_gamma * (pos_c + 1.0))
        chunk_decay = float(np.exp(np.float32(log_gamma) * np.float32(c)))
        lanes = slice(h * B_V, (h + 1) * B_V)
        pair = slice((h // 2) * LANE, (h // 2 + 1) * LANE)
        for ci in range(RET_TILE // c):
            rows = slice(ci * c, (ci + 1) * c)
            q = q_ref[0, rows, pair].astype(f32)
            kT = kT_ref[0, h, :, rows].astype(f32)
            v = v_ref[rows, lanes]
            state = s_ref[h]
            o = mm(mm(q, kT) * intra, v) + mm(q * from_start, state)
            s_ref[h] = state * chunk_decay + mm(kT * to_end, v)
            gate = g_ref[rows, lanes]
            o = o * lax.rsqrt(jnp.mean(o * o, -1, keepdims=True) + RMS_EPS)
            o_ref[0, rows, lanes] = o * (gate * _sigmoid(gate))


def _retention(rq, rkT, proj, bsz, seq):
    nt = seq // RET_TILE
    width = B_HEADS * B_V
    col = lambda c: pl.BlockSpec((RET_TILE, COL), lambda b, t: (b * nt + t, c))
    return pl.pallas_call(
        _retention_body,
        grid=(bsz, nt),
        in_specs=[pl.BlockSpec((1, RET_TILE, B_HEADS * B_QK), lambda b, t: (b, t, 0)),
                  pl.BlockSpec((1, B_HEADS, LANE, RET_TILE), lambda b, t: (b, 0, 0, t)),
                  col(COL_B_V), col(COL_B_G)],
        out_specs=pl.BlockSpec((1, RET_TILE, width), lambda b, t: (b, t, 0)),
        out_shape=jax.ShapeDtypeStruct((bsz, seq, width), jnp.float32),
        scratch_shapes=[pltpu.VMEM((B_HEADS, LANE, B_V), jnp.float32)],
        compiler_params=_params("arbitrary", "arbitrary"),
        name="retention",
    )(rq, rkT, proj, proj)


def _diff_attention_body(lam_ref, qbd_ref, k_ref, vT_ref, gain_ref, o_ref, acc_ref):
    f32 = jnp.float32
    qi = pl.program_id(1)
    nk = (qi * Q_BLOCK + Q_BLOCK + KCHUNK - 1) // KCHUNK
    qpos = qi * Q_BLOCK + lax.broadcasted_iota(jnp.int32, (KCHUNK, Q_BLOCK), 1)
    krow = lax.broadcasted_iota(jnp.int32, (KCHUNK, Q_BLOCK), 0)
    acc_ref[...] = jnp.zeros_like(acc_ref)

    def chunk(c, carry):
        ms, ls = carry
        causal = jnp.where(krow + c * KCHUNK <= qpos, 0.0, _MASKED)
        bias = jnp.concatenate([causal, causal], axis=1)
        heads = range(C_HEADS)
        s = [jnp.dot(k_ref[0, c, h], qbd_ref[0, 0, h], preferred_element_type=f32) + bias for h in heads]
        new_ms = [jnp.maximum(ms[h], s[h].max(0, keepdims=True)) for h in heads]
        alpha = [jnp.exp(ms[h] - new_ms[h]) for h in heads]
        p = [jnp.exp(s[h] - new_ms[h]) for h in heads]
        new_ls = [alpha[h] * ls[h] + p[h].sum(0, keepdims=True) for h in heads]
        pv = [jnp.dot(vT_ref[0, c, h], p[h].astype(vT_ref.dtype), preferred_element_type=f32) for h in heads]
        for h in heads:
            acc_ref[h] = acc_ref[h] * alpha[h] + pv[h]
        return tuple(new_ms), tuple(new_ls)

    lanes = 2 * Q_BLOCK
    init = (tuple(jnp.full((1, lanes), _MASKED, f32) for _ in range(C_HEADS)),
            tuple(jnp.zeros((1, lanes), f32) for _ in range(C_HEADS)))
    _, ls = lax.fori_loop(0, nk, chunk, init)
    lam = lam_ref[0]
    for h in range(C_HEADS):
        o = acc_ref[h] / ls[h]
        o = o[:, :Q_BLOCK] - lam * o[:, Q_BLOCK:]
        o = o * lax.rsqrt(jnp.mean(o * o, 0, keepdims=True) + RMS_EPS)
        o_ref[0, :, h * 2 * C_DH:(h + 1) * 2 * C_DH] = o.T * gain_ref[...]


def _diff_attention(lam, qbd, kc, vT, gain):
    bsz, nq = qbd.shape[:2]
    seq = nq * Q_BLOCK
    width = C_HEADS * 2 * C_DH
    return pl.pallas_call(
        _diff_attention_body,
        grid=(bsz, nq),
        in_specs=[pl.BlockSpec(memory_space=pltpu.SMEM),
                  pl.BlockSpec((1, 1) + qbd.shape[2:], lambda b, q: (b, q, 0, 0, 0)),
                  pl.BlockSpec((1,) + kc.shape[1:], lambda b, q: (b, 0, 0, 0, 0)),
                  pl.BlockSpec((1,) + vT.shape[1:], lambda b, q: (b, 0, 0, 0, 0)),
                  pl.BlockSpec(gain.shape, lambda b, q: (0, 0))],
        out_specs=pl.BlockSpec((1, Q_BLOCK, width), lambda b, q: (b, q, 0)),
        out_shape=jax.ShapeDtypeStruct((bsz, seq, width), jnp.float32),
        scratch_shapes=[pltpu.VMEM((C_HEADS, 2 * C_DH, 2 * Q_BLOCK), jnp.float32)],
        compiler_params=_params("arbitrary", "arbitrary"),
        name="diff_attention",
    )(lam, qbd, kc, vT, gain)


GDN_TILE = 256
GDN_SUB = 16


def _gdn_body(qkv_ref, z_ref, nar_ref, gateT_ref, conv_ref, alog_r_ref, dtb_r_ref, alog_c_ref, dtb_c_ref,
              ng_ref, o_ref, halo_ref, s_ref):
    f32 = jnp.float32
    cdt = MXU_DTYPE
    c = GDN_CHUNK

    @pl.when(pl.program_id(1) == 0)
    def _():
        halo_ref[...] = jnp.zeros_like(halo_ref)
        s_ref[...] = jnp.zeros_like(s_ref)

    x = qkv_ref[...]
    xp = jnp.concatenate([halo_ref[...], x], 0)
    w = conv_ref[...]
    pre = xp[SUBLANE - CONV_K + 1:SUBLANE - CONV_K + 1 + GDN_TILE] * w[0:1]
    for j in range(1, CONV_K):
        off = SUBLANE - CONV_K + 1 + j
        pre = pre + xp[off:off + GDN_TILE] * w[j:j + 1]
    halo_ref[...] = x[GDN_TILE - SUBLANE:]
    qkv = pre * _sigmoid(pre)

    gates = nar_ref[:, LANE:]
    beta_c = _sigmoid(gates[:, NARROW_DB:NARROW_DB + D_HEADS])
    g_c = -jnp.exp(alog_r_ref[...]) * _softplus(gates[:, NARROW_DA:NARROW_DA + D_HEADS] + dtb_r_ref[...])
    g_r = -jnp.exp(alog_c_ref[...]) * _softplus(gateT_ref[0][D_HEADS:] + dtb_c_ref[...])

    row = lax.broadcasted_iota(jnp.int32, (c, c), 0)
    col = lax.broadcasted_iota(jnp.int32, (c, c), 1)
    tril = row >= col
    strict = row > col
    same_sub = (row // GDN_SUB) == (col // GDN_SUB)
    eye = (row == col).astype(f32)
    lower_ones = tril.astype(f32)
    upper_ones = (row <= col).astype(f32)

    def mm(a, b):
        return jnp.dot(a, b, preferred_element_type=f32)

    n_chunks = GDN_TILE // c
    pairs = [(ci, h) for ci in range(n_chunks) for h in range(D_HEADS)]
    rows_of = lambda ci: slice(ci * c, (ci + 1) * c)
    g_cum_c = [mm(lower_ones, g_c[rows_of(ci)]) for ci in range(n_chunks)]
    g_cum_r = [mm(g_r[:, rows_of(ci)], upper_ones) for ci in range(n_chunks)]
    st = []
    for ci, h in pairs:
        rows = rows_of(ci)
        q = qkv[rows, h * D_K:(h + 1) * D_K]
        k = qkv[rows, D_HEADS * D_K + h * D_K:D_HEADS * D_K + (h + 1) * D_K]
        v = qkv[rows, 2 * D_HEADS * D_K + h * D_V:2 * D_HEADS * D_K + (h + 1) * D_V]
        q = q * lax.rsqrt(jnp.sum(q * q, -1, keepdims=True) + RMS_EPS) * D_K ** -0.5
        k = k * lax.rsqrt(jnp.sum(k * k, -1, keepdims=True) + RMS_EPS)
        gc = g_cum_c[ci][:, h:h + 1]
        gr = g_cum_r[ci][h:h + 1, :]
        g_last = gc[c - 1:c]
        gam = jnp.where(tril, jnp.exp(jnp.where(tril, gc - gr, 0.0)), 0.0)
        e_g = jnp.exp(gc)
        beta = beta_c[rows, h:h + 1]
        kT = k.T
        kb = k * beta
        st.append(dict(gam=gam, kT_c=kT.astype(cdt), kb_c=kb.astype(cdt), q_c=q.astype(cdt),
                       qg_c=(q * e_g).astype(cdt), rhs=jnp.concatenate([v * beta, kb * e_g], 1),
                       kdT_c=(kT * jnp.exp(g_last - gr)).astype(cdt), decay=jnp.exp(g_last)))
    for p in st:
        p["low"] = jnp.where(strict, mm(p["kb_c"], p["kT_c"]) * p["gam"], 0.0)
        p["attn_c"] = jnp.where(tril, mm(p["q_c"], p["kT_c"]) * p["gam"], 0.0).astype(cdt)
        p["low_d"] = jnp.where(same_sub, p["low"], 0.0)
        p["xk"] = -p["low_d"]
        p["td"] = eye + p["xk"]
    for _ in range(3):
        for p in st:
            p["xk"] = mm(p["xk"], p["xk"])
        for p in st:
            p["td"] = p["td"] + mm(p["td"], p["xk"])
    for p in st:
        p["m1"] = mm(p["td"], p["low"] - p["low_d"])
        p["sol"] = mm(p["td"], p["rhs"])
    for p in st:
        p["m2"] = mm(p["m1"], p["m1"])
    for p in st:
        p["sol"] = p["sol"] + mm(p["m2"], p["sol"])
    for p in st:
        p["sol"] = p["sol"] - mm(p["m1"], p["sol"])
    for ci in range(n_chunks):
        rows = rows_of(ci)
        cur = [st[ci * D_HEADS + h] for h in range(D_HEADS)]
        states = [s_ref[h] for h in range(D_HEADS)]
        states_c = [s.astype(cdt) for s in states]
        v_new_c = [(p["sol"][:, :D_V] - mm(p["sol"][:, D_V:].astype(cdt), states_c[h])).astype(cdt)
                   for h, p in enumerate(cur)]
        for h, p in enumerate(cur):
            s_ref[h] = states[h] * p["decay"] + mm(p["kdT_c"], v_new_c[h])
        for h, p in enumerate(cur):
            lanes = slice(h * D_V, (h + 1) * D_V)
            o = mm(p["qg_c"], states_c[h]) + mm(p["attn_c"], v_new_c[h])
            zz = z_ref[rows, lanes]
            o = o * lax.rsqrt(jnp.mean(o * o, -1, keepdims=True) + RMS_EPS) * ng_ref[...]
            o_ref[0, rows, lanes] = o * (zz * _sigmoid(zz))


def _gated_deltanet(proj, gatesT, conv_w, a_log, dt_bias, norm_g, bsz, seq):
    nt = seq // GDN_TILE
    width = D_HEADS * (2 * D_K + D_V)
    whole = lambda r, cc: pl.BlockSpec((r, cc), lambda b, t: (0, 0))
    return pl.pallas_call(
        _gdn_body,
        grid=(bsz, nt),
        in_specs=[pl.BlockSpec((GDN_TILE, width), lambda b, t: (b * nt + t, COL_D_QKV)),
                  pl.BlockSpec((GDN_TILE, COL), lambda b, t: (b * nt + t, COL_D_Z)),
                  pl.BlockSpec((GDN_TILE, NARROW), lambda b, t: (b * nt + t, COL_NARROW)),
                  pl.BlockSpec((1, 2 * D_HEADS, GDN_TILE), lambda b, t: (b, 0, t)),
                  whole(CONV_K, width), whole(1, D_HEADS), whole(1, D_HEADS), whole(D_HEADS, 1), whole(D_HEADS, 1),
                  whole(1, D_V)],
        out_specs=pl.BlockSpec((1, GDN_TILE, D_HEADS * D_V), lambda b, t: (b, t, 0)),
        out_shape=jax.ShapeDtypeStruct((bsz, seq, D_HEADS * D_V), jnp.float32),
        scratch_shapes=[pltpu.VMEM((SUBLANE, width), jnp.float32),
                        pltpu.VMEM((D_HEADS, D_K, D_V), jnp.float32)],
        compiler_params=_params("arbitrary", "arbitrary"),
        name="gated_deltanet",
    )(proj, proj, proj, gatesT, conv_w, a_log.reshape(1, D_HEADS), dt_bias.reshape(1, D_HEADS),
      a_log.reshape(D_HEADS, 1), dt_bias.reshape(D_HEADS, 1), norm_g.reshape(1, D_V))


def _ln_rows(t, g, b):
    mu = jnp.mean(t, -1, keepdims=True)
    d = t - mu
    var = jnp.mean(d * d, -1, keepdims=True)
    return d * lax.rsqrt(var + LN_EPS) * g + b


def _out_ln_router_body(ha_ref, hb_ref, hc_ref, hd_ref, w_ref, x_ref, g_ref, b_ref, wr_ref, br_ref,
                        h_ref, te_ref, tg_ref):
    f32 = jnp.float32
    acc = DEEPNORM_ALPHA * x_ref[...]
    row0 = 0
    for part in (ha_ref, hb_ref, hc_ref, hd_ref):
        width = part.shape[-1]
        acc = acc + jnp.dot(part[...].astype(MXU_DTYPE), w_ref[row0:row0 + width, :], preferred_element_type=f32)
        row0 += width
    h = _ln_rows(acc, g_ref[...], b_ref[...])
    h_ref[...] = h
    logits = jnp.dot(h.astype(MXU_DTYPE), wr_ref[...], preferred_element_type=f32) + br_ref[...]
    lane = lax.broadcasted_iota(jnp.int32, logits.shape, 1)
    vals, te = [], jnp.zeros(logits.shape, jnp.int32)
    for k in range(TOP_K):
        m = logits.max(-1, keepdims=True)
        e = jnp.min(jnp.where(logits == m, lane, LANE), -1, keepdims=True)
        vals.append(m)
        te = jnp.where(lane == k, e, te)
        logits = jnp.where(lane == e, -jnp.inf, logits)
    ex = [jnp.exp(v - vals[0]) for v in vals]
    den = ex[0] + ex[1] + ex[2] + ex[3]
    tg = jnp.zeros(logits.shape, f32)
    for k in range(TOP_K):
        tg = jnp.where(lane == k, ex[k] / den, tg)
    te_ref[...] = te
    tg_ref[...] = tg


def _out_ln_router(head_parts, w_out_c, x2d, g, b, w_router, b_router, tm=512):
    m = x2d.shape[0]
    k, n = w_out_c.shape
    wr = jnp.zeros((n, LANE), MXU_DTYPE).at[:, :N_EXPERTS].set(w_router.astype(MXU_DTYPE))
    br = jnp.full((1, LANE), -jnp.inf, jnp.float32).at[0, :N_EXPERTS].set(b_router)
    row_blk = lambda c: pl.BlockSpec((tm, c), lambda i: (i, 0))
    whole = lambda r, c: pl.BlockSpec((r, c), lambda i: (0, 0))
    return pl.pallas_call(
        _out_ln_router_body,
        grid=(m // tm,),
        in_specs=[row_blk(p.shape[1]) for p in head_parts]
        + [whole(k, n), row_blk(n), whole(1, n), whole(1, n), whole(n, LANE), whole(1, LANE)],
        out_specs=[row_blk(n), row_blk(LANE), row_blk(LANE)],
        out_shape=[jax.ShapeDtypeStruct((m, n), jnp.float32),
                   jax.ShapeDtypeStruct((m, LANE), jnp.int32),
                   jax.ShapeDtypeStruct((m, LANE), jnp.float32)],
        compiler_params=_params("arbitrary"),
        name="out_proj_ln_router",
    )(*head_parts, w_out_c, x2d, g.reshape(1, n), b.reshape(1, n), wr, br)


def _clamped_swiglu(hg):
    glu, lin = hg[..., :D_FF], hg[..., D_FF:]
    glu = jnp.minimum(glu, SWIGLU_LIMIT)
    lin = jnp.clip(lin, -SWIGLU_LIMIT, SWIGLU_LIMIT)
    return glu * _sigmoid(SWIGLU_ALPHA * glu) * (lin + 1.0)


MOE_RANK_TILE = 512


def _moe_rank_body(te_ref, rank_ref, counts_ref, seen_ref):
    f32 = jnp.float32
    tm = MOE_RANK_TILE

    @pl.when(pl.program_id(0) == 0)
    def _():
        seen_ref[...] = jnp.zeros_like(seen_ref)

    te = te_ref[...]
    lane = lax.broadcasted_iota(jnp.int32, (tm, LANE), 1)
    picks = [lane == te[:, k:k + 1] for k in range(TOP_K)]
    member = picks[0].astype(f32)
    for k in range(1, TOP_K):
        member = member + picks[k].astype(f32)
    earlier = (lax.broadcasted_iota(jnp.int32, (tm, tm), 1) < lax.broadcasted_iota(jnp.int32, (tm, tm), 0))
    prefix = jnp.dot(earlier.astype(jnp.bfloat16), member.astype(jnp.bfloat16), preferred_element_type=f32)
    prefix = prefix + seen_ref[...]
    rank = jnp.zeros((tm, LANE), jnp.int32)
    for k in range(TOP_K):
        r_k = jnp.sum(jnp.where(picks[k], prefix, 0.0), -1, keepdims=True)
        rank = jnp.where(lane == k, r_k.astype(jnp.int32), rank)
    rank_ref[...] = rank
    seen = seen_ref[...] + member.sum(0, keepdims=True)
    seen_ref[...] = seen
    counts_ref[...] = seen.astype(jnp.int32)


def _moe_rank(te):
    n_tok = te.shape[0]
    tm = MOE_RANK_TILE
    return pl.pallas_call(
        _moe_rank_body,
        grid=(n_tok // tm,),
        in_specs=[pl.BlockSpec((tm, LANE), lambda i: (i, 0))],
        out_specs=[pl.BlockSpec((tm, LANE), lambda i: (i, 0)), pl.BlockSpec((1, LANE), lambda i: (0, 0))],
        out_shape=[jax.ShapeDtypeStruct((n_tok, LANE), jnp.int32), jax.ShapeDtypeStruct((1, LANE), jnp.int32)],
        scratch_shapes=[pltpu.VMEM((1, LANE), jnp.float32)],
        compiler_params=_params("arbitrary"),
        name="moe_rank",
    )(te)


def _moe_plan(te):
    top_e = te[:, :TOP_K]
    n_tok = top_e.shape[0]
    n_blocks = -(-n_tok * TOP_K // MOE_BLOCK) + N_EXPERTS
    rank, counts = _moe_rank(te)
    counts = counts[0, :N_EXPERTS]
    padded = (counts + MOE_BLOCK - 1) // MOE_BLOCK * MOE_BLOCK
    pad_end = jnp.cumsum(padded)
    one_hot = top_e[:, :, None] == jnp.arange(N_EXPERTS, dtype=jnp.int32)
    dest = rank[:, :TOP_K] + jnp.sum(jnp.where(one_hot, (pad_end - padded)[None, None, :], 0), -1)
    tok = jnp.broadcast_to(jnp.arange(n_tok, dtype=jnp.int32)[:, None], dest.shape)
    row_tok = jnp.zeros((n_blocks * MOE_BLOCK,), jnp.int32).at[dest.reshape(-1)].set(tok.reshape(-1))
    block_e = jnp.minimum(jnp.searchsorted(pad_end, jnp.arange(n_blocks) * MOE_BLOCK, side='right'),
                          N_EXPERTS - 1).astype(jnp.int32)
    n_used = (pad_end[-1:] // MOE_BLOCK).astype(jnp.int32)
    return dest, row_tok, block_e, n_used


def _row_gather_start(src_hbm, rows_smem, n_rows, dst, sem):
    def body(r, carry):
        pltpu.make_async_copy(src_hbm.at[pl.ds(rows_smem[0, 0, r], 1)], dst.at[pl.ds(r, 1)], sem).start()
        return carry
    lax.fori_loop(0, n_rows, body, 0, unroll=8)


def _row_gather_wait(src_hbm, n_rows, dst, sem):
    pltpu.make_async_copy(src_hbm.at[pl.ds(0, n_rows)], dst, sem).wait()


def _moe_expert_body(be_ref, nused_ref, tok_ref, tok_next_ref, h_hbm, wgu_ref, bgu_ref, wdn_ref, bdn_ref,
                     y_ref, buf_even, buf_odd, sem, wgu_c, wdn_c):
    f32 = jnp.float32
    i = pl.program_id(0)
    n_used = nused_ref[0]
    bufs = (buf_even, buf_odd)

    @pl.when(i == 0)
    def _():
        _row_gather_start(h_hbm, tok_ref, MOE_BLOCK, buf_even, sem.at[0])

    @pl.when(jnp.logical_or(i == 0, be_ref[i] != be_ref[jnp.maximum(i - 1, 0)]))
    def _():
        wgu_c[...] = wgu_ref[0, 0].astype(MXU_DTYPE)
        wdn_c[...] = wdn_ref[0, 0].astype(MXU_DTYPE)

    for parity in (0, 1):
        cur, nxt = bufs[parity], bufs[1 - parity]

        @pl.when(jnp.logical_and(i < n_used, i % 2 == parity))
        def _():
            _row_gather_wait(h_hbm, MOE_BLOCK, cur, sem.at[parity])
            for r in range(MOE_BLOCK):
                pltpu.make_async_copy(h_hbm.at[pl.ds(tok_next_ref[0, 0, r], 1)], nxt.at[pl.ds(r, 1)],
                                      sem.at[1 - parity]).start()
            xb = cur[...].astype(MXU_DTYPE)
            hg = jnp.dot(xb, wgu_c[...], preferred_element_type=f32) + bgu_ref[0, 0]
            act = _clamped_swiglu(hg)
            y_ref[...] = jnp.dot(act.astype(MXU_DTYPE), wdn_c[...], preferred_element_type=f32) + bdn_ref[0, 0]

        @pl.when(jnp.logical_and(i == n_used, i % 2 == parity))
        def _():
            _row_gather_wait(h_hbm, MOE_BLOCK, cur, sem.at[parity])

    @pl.when(i >= n_used)
    def _():
        y_ref[...] = jnp.zeros_like(y_ref)


def _moe_experts(h2d, row_tok, block_e, n_used, layer, w_gu, b_gu, w_dn, b_dn):
    n_tok, d = h2d.shape
    n_blocks = block_e.shape[0]
    n_layers = w_gu.shape[0]
    assert n_blocks * MOE_BLOCK > n_tok * TOP_K + N_EXPERTS * (MOE_BLOCK - 1)
    tok3 = row_tok.reshape(n_blocks, 1, MOE_BLOCK)
    smem_blk = lambda imap: pl.BlockSpec((1, 1, MOE_BLOCK), imap, memory_space=pltpu.SMEM)
    by_expert = lambda shape: pl.BlockSpec((1, 1) + shape, lambda i, be, nu: (layer, be[i], 0, 0))
    return pl.pallas_call(
        _moe_expert_body,
        grid_spec=pltpu.PrefetchScalarGridSpec(
            num_scalar_prefetch=2,
            grid=(n_blocks,),
            in_specs=[smem_blk(lambda i, be, nu: (i, 0, 0)),
                      smem_blk(lambda i, be, nu: (jnp.minimum(i + 1, n_blocks - 1), 0, 0)),
                      pl.BlockSpec(memory_space=pl.ANY),
                      by_expert((d, 2 * D_FF)), by_expert((1, 2 * D_FF)),
                      by_expert((D_FF, d)), by_expert((1, d))],
            out_specs=pl.BlockSpec((MOE_BLOCK, d), lambda i, be, nu: (i, 0)),
            scratch_shapes=[pltpu.VMEM((MOE_BLOCK, d), jnp.float32), pltpu.VMEM((MOE_BLOCK, d), jnp.float32),
                            pltpu.SemaphoreType.DMA((2,)),
                            pltpu.VMEM((d, 2 * D_FF), MXU_DTYPE), pltpu.VMEM((D_FF, d), MXU_DTYPE)]),
        out_shape=jax.ShapeDtypeStruct((n_blocks * MOE_BLOCK, d), jnp.float32),
        compiler_params=_params("arbitrary"),
        name="moe_experts",
    )(block_e, n_used, tok3, tok3, h2d, w_gu, b_gu.reshape(n_layers, N_EXPERTS, 1, 2 * D_FF),
      w_dn, b_dn.reshape(n_layers, N_EXPERTS, 1, d))


MOE_COMBINE_ROWS = 128


def _moe_combine_ln_body(dest_ref, dest_next_ref, h_ref, tg_ref, g_ref, b_ref, y_hbm, o_ref,
                         buf_even, buf_odd, sem):
    tm = MOE_COMBINE_ROWS
    i = pl.program_id(0)
    n = pl.num_programs(0)
    bufs = (buf_even, buf_odd)

    def start(rows_smem, dst, s):
        for k in range(TOP_K):
            for r in range(tm):
                pltpu.make_async_copy(y_hbm.at[pl.ds(rows_smem[0, 0, k * tm + r], 1)],
                                      dst.at[k, pl.ds(r, 1)], sem.at[s]).start()

    def drain(dst, s):
        for k in range(TOP_K):
            _row_gather_wait(y_hbm, tm, dst.at[k], sem.at[s])

    @pl.when(i == 0)
    def _():
        start(dest_ref, buf_even, 0)

    for parity in (0, 1):
        cur, nxt = bufs[parity], bufs[1 - parity]

        @pl.when(i % 2 == parity)
        def _():
            drain(cur, parity)
            start(dest_next_ref, nxt, 1 - parity)
            tg = tg_ref[...]
            y = tg[:, 0:1] * cur[0]
            for k in range(1, TOP_K):
                y = y + tg[:, k:k + 1] * cur[k]
            o_ref[...] = _ln_rows(DEEPNORM_ALPHA * h_ref[...] + y, g_ref[...], b_ref[...])

        @pl.when(jnp.logical_and(i == n - 1, i % 2 == parity))
        def _():
            drain(nxt, 1 - parity)


def _moe_combine_ln(h2d, y_rows, dest, tg, g, b):
    n_tok, d = h2d.shape
    tm = MOE_COMBINE_ROWS
    nt = n_tok // tm
    dest3 = dest.reshape(nt, tm, TOP_K).transpose(0, 2, 1).reshape(nt, 1, TOP_K * tm)
    smem_blk = lambda imap: pl.BlockSpec((1, 1, TOP_K * tm), imap, memory_space=pltpu.SMEM)
    return pl.pallas_call(
        _moe_combine_ln_body,
        grid=(nt,),
        in_specs=[smem_blk(lambda i: (i, 0, 0)),
                  smem_blk(lambda i: (jnp.minimum(i + 1, nt - 1), 0, 0)),
                  pl.BlockSpec((tm, d), lambda i: (i, 0)),
                  pl.BlockSpec((tm, LANE), lambda i: (i, 0)),
                  pl.BlockSpec((1, d), lambda i: (0, 0)),
                  pl.BlockSpec((1, d), lambda i: (0, 0)),
                  pl.BlockSpec(memory_space=pl.ANY)],
        out_specs=pl.BlockSpec((tm, d), lambda i: (i, 0)),
        out_shape=jax.ShapeDtypeStruct((n_tok, d), jnp.float32),
        scratch_shapes=[pltpu.VMEM((TOP_K, tm, d), jnp.float32), pltpu.VMEM((TOP_K, tm, d), jnp.float32),
                        pltpu.SemaphoreType.DMA((2,))],
        compiler_params=_params("arbitrary"),
        name="moe_combine_ln",
    )(dest3, dest3, h2d, tg, g.reshape(1, d), b.reshape(1, d), y_rows)


def _moe_layer(h2d, te, tg, layer, w_gu, b_gu, w_dn, b_dn, g, b):
    dest, row_tok, block_e, n_used = _moe_plan(te)
    y_rows = _moe_experts(h2d, row_tok, block_e, n_used, layer, w_gu, b_gu, w_dn, b_dn)
    return _moe_combine_ln(h2d, y_rows, dest, tg, g, b)


def _mixers(proj, cos, sin, bsz, seq, lambda_init, a_kv_norm, a_w_uk, a_w_uv, c_lambda, c_subln,
            d_conv, d_a_log, d_dt_bias, d_norm):
    cdt = MXU_DTYPE
    (qT, iqT, iw, ikc, kvc, ckvT, qbd, kc, vT, rq, rkT, gatesT) = _mixer_prep(
        proj, cos, sin, a_kv_norm, a_w_uk.transpose(1, 0, 2).astype(cdt), bsz, seq)
    head_a = _dsa_attention(iqT, iw, qT, ikc, kvc, ckvT, a_w_uv.transpose(1, 0, 2).astype(cdt),
                            n_keep=min(INDEX_TOPK, seq // 4))
    head_b = _retention(rq, rkT, proj, bsz, seq)
    lam_full = (jnp.exp(jnp.sum(c_lambda[0] * c_lambda[1])) - jnp.exp(jnp.sum(c_lambda[2] * c_lambda[3]))
                + lambda_init)
    gain = (c_subln * (1.0 - lambda_init)).reshape(1, 2 * C_DH)
    head_c = _diff_attention(lam_full.reshape(1), qbd, kc, vT, gain)
    head_d = _gated_deltanet(proj, gatesT, d_conv, d_a_log, d_dt_bias, d_norm, bsz, seq)
    return [t.reshape(bsz * seq, t.shape[-1]) for t in (head_a, head_b, head_c, head_d)]


def kernel(x, positions, w_in, w_out, a_kv_norm, a_w_uk, a_w_uv, c_lambda, c_subln, d_conv, d_a_log, d_dt_bias, d_norm, ln1_g, ln1_b, w_router, b_router, w_gu, b_gu, w_dn, b_dn, ln2_g, ln2_b):
    bsz, seq, d = x.shape
    x2d = x.reshape(bsz * seq, d)
    cos, sin = _rope_tables(positions)
    for l in range(DEPTH):
        proj = _in_proj(x2d, _pack_w_in(w_in[l]))
        lambda_init = 0.8 - 0.6 * math.exp(-0.3 * l)
        heads = _mixers(proj, cos, sin, bsz, seq, lambda_init, a_kv_norm[l], a_w_uk[l], a_w_uv[l], c_lambda[l],
                        c_subln[l], d_conv[l], d_a_log[l], d_dt_bias[l], d_norm[l])
        h2d, te, tg = _out_ln_router(heads, w_out[l].astype(MXU_DTYPE), x2d, ln1_g[l], ln1_b[l],
                                     w_router[l], b_router[l])
        x2d = _moe_layer(h2d, te, tg, l, w_gu, b_gu, w_dn, b_dn, ln2_g[l], ln2_b[l])
    return x2d.reshape(bsz, seq, d)
```

```python
import functools
import math

import jax
import jax.numpy as jnp
import numpy as np
from jax import lax
from jax.experimental import pallas as pl
from jax.experimental.pallas import tpu as pltpu

D_MODEL = 2048
DEPTH = 4
A_HEADS, A_NOPE, A_ROPE, A_V, A_KV_RANK = 4, 128, 64, 128, 256
IDX_HEADS, IDX_DIM, INDEX_TOPK = 8, 64, 256
B_HEADS, B_QK, B_V, RET_CHUNK = 4, 64, 128, 128
C_HEADS, C_DH = 4, 64
D_HEADS, D_K, D_V, CONV_K, GDN_CHUNK = 4, 128, 128, 4, 64
N_EXPERTS, TOP_K, D_FF = 32, 4, 512
SWIGLU_ALPHA, SWIGLU_LIMIT = 1.702, 7.0
MOE_BLOCK = 512
ROPE_THETA = 10000.0
ROPE_DIM = 64
Q_BLOCK = 128
DEEPNORM_ALPHA = (2 * DEPTH) ** 0.25
LN_EPS = 1e-5
RMS_EPS = 1e-6
D_MIX = A_HEADS * A_V + B_HEADS * B_V + C_HEADS * 2 * C_DH + D_HEADS * D_V
A_LAT = A_KV_RANK + A_ROPE

IN_WIDTHS = (
    A_HEADS * (A_NOPE + A_ROPE), A_KV_RANK, A_ROPE, IDX_HEADS * IDX_DIM, IDX_DIM, IDX_HEADS,
    B_HEADS * B_QK, B_HEADS * B_QK, B_HEADS * B_V, B_HEADS * B_V,
    C_HEADS * 2 * C_DH, C_HEADS * 2 * C_DH, C_HEADS * 2 * C_DH,
    D_HEADS * (2 * D_K + D_V), D_HEADS * D_V, D_HEADS, D_HEADS,
)
_IN_OFFS = np.concatenate([[0], np.cumsum(IN_WIDTHS)]).tolist()

LANE = 128
SUBLANE = 8
VMEM_LIMIT = 56 * 1024 * 1024

MXU_DTYPE = jnp.bfloat16
_MASKED = -1e30

COL = 512
COL_D_QKV = 0
COL_D_Z = 3
COL_B_V = 4
COL_B_G = 5
COL_B_QK = 6
COL_C_Q = 7
COL_C_K = 8
COL_C_V = 9
COL_A_CKV_ROPE = 10
COL_A_NOPE = 11
COL_A_IQ = 12
NARROW = 256
COL_NARROW = 26
IN_PACKED = (COL_NARROW + 1) * NARROW
NARROW_IW = 0
NARROW_DB = IDX_HEADS
NARROW_DA = IDX_HEADS + D_HEADS


def _pack_w_in(w):
    g = lambda i: w[:, _IN_OFFS[i]:_IN_OFFS[i + 1]]
    a_q = g(0).reshape(w.shape[0], A_HEADS, A_NOPE + A_ROPE)
    nope = a_q[:, :, :A_NOPE].reshape(w.shape[0], A_HEADS * A_NOPE)
    rope = a_q[:, :, A_NOPE:].reshape(w.shape[0], A_HEADS * A_ROPE)
    cols = [g(13), g(14), g(8), g(9), g(6), g(7), g(10), g(11), g(12), g(1), rope, nope, g(3),
            g(2), g(4), g(5), g(15), g(16)]
    used = sum(c.shape[1] for c in cols)
    cols.append(jnp.zeros((w.shape[0], IN_PACKED - used), w.dtype))
    return jnp.concatenate(cols, axis=1).astype(MXU_DTYPE)


def _sigmoid(t):
    return 1.0 / (1.0 + jnp.exp(-t))


def _softplus(t):
    return jnp.maximum(t, 0.0) + jnp.log1p(jnp.exp(-jnp.abs(t)))


def _params(*sem):
    return pltpu.CompilerParams(dimension_semantics=sem, vmem_limit_bytes=VMEM_LIMIT)


def _in_proj_body(x_ref, w_ref, o_ref, xb_ref):
    @pl.when(pl.program_id(1) == 0)
    def _():
        xb_ref[...] = x_ref[...].astype(MXU_DTYPE)

    o_ref[...] = jnp.dot(xb_ref[...], w_ref[...], preferred_element_type=jnp.float32)


def _in_proj(x2d, w_packed, tm=1024, tn=768):
    m, k = x2d.shape
    n = w_packed.shape[1]
    return pl.pallas_call(
        _in_proj_body,
        grid=(m // tm, n // tn),
        in_specs=[pl.BlockSpec((tm, k), lambda i, j: (i, 0)),
                  pl.BlockSpec((k, tn), lambda i, j: (0, j))],
        out_specs=pl.BlockSpec((tm, tn), lambda i, j: (i, j)),
        out_shape=jax.ShapeDtypeStruct((m, n), jnp.float32),
        scratch_shapes=[pltpu.VMEM((tm, k), MXU_DTYPE)],
        compiler_params=_params("arbitrary", "arbitrary"),
        name="in_proj",
    )(x2d, w_packed)


PREP_TILE = 256
KCHUNK = 256


def _prep_body(bqk_ref, cq_ref, ck_ref, cv_ref, ckvrope_ref, nope_ref, iq_ref, nar_ref, cos_ref, sin_ref,
               kvn_ref, wukT_ref,
               qT_ref, iqT_ref, iw_ref, ik_ref, kv_ref, ckvT_ref, qbd_ref, kc_ref, vT_ref, rq_ref, rkT_ref, gT_ref):
    f32 = jnp.float32
    cdt = MXU_DTYPE
    cos, sin = cos_ref[0], sin_ref[0]
    lane = lax.broadcasted_iota(jnp.int32, (PREP_TILE, LANE), 1)
    first_half = (lane % ROPE_DIM) < ROPE_DIM // 2
    eye = (lax.broadcasted_iota(jnp.int32, (LANE, LANE), 0)
           == lax.broadcasted_iota(jnp.int32, (LANE, LANE), 1)).astype(cdt)
    top = lax.broadcasted_iota(jnp.int32, (LANE, LANE), 0) < LANE // 2
    top_wide = lax.broadcasted_iota(jnp.int32, (LANE, PREP_TILE), 0) < LANE // 2

    def rope(t):
        outs = []
        for j in range(t.shape[1] // LANE):
            ts = t[:, j * LANE:(j + 1) * LANE]
            partner = jnp.where(first_half, pltpu.roll(ts, LANE - ROPE_DIM // 2, 1), pltpu.roll(ts, ROPE_DIM // 2, 1))
            outs.append(ts * cos + partner * sin)
        return outs[0] if len(outs) == 1 else jnp.concatenate(outs, axis=1)

    def tr(t):
        return lax.dot_general(eye, t.astype(cdt), (((1,), (1,)), ((), ())), preferred_element_type=f32)

    ckv_rope = ckvrope_ref[...]
    a_ckv = ckv_rope[:, :A_KV_RANK]
    c_kv = a_ckv * lax.rsqrt(jnp.mean(a_ckv * a_ckv, -1, keepdims=True) + RMS_EPS) * kvn_ref[...]
    nar = nar_ref[...]
    kr_ik = rope(nar[:, :LANE])
    kv_ref[0, 0, :, :A_KV_RANK] = c_kv.astype(cdt)
    kv_ref[0, 0, :, A_KV_RANK:] = kr_ik[:, :A_ROPE].astype(cdt)
    ik_ref[0, 0] = kr_ik[:, A_ROPE:].astype(cdt)
    for rb in range(A_KV_RANK // LANE):
        ckvT_ref[0, 0, rb * LANE:(rb + 1) * LANE, :] = tr(c_kv[:, rb * LANE:(rb + 1) * LANE]).astype(cdt)
    q_rope = rope(ckv_rope[:, A_KV_RANK:])
    nope = nope_ref[...]
    iq = rope(iq_ref[...])
    narT = nar[:, LANE:].T
    gT_ref[0] = narT[NARROW_DB:NARROW_DB + 2 * D_HEADS]
    cq = rope(cq_ref[...]) * C_DH ** -0.5
    for j in range(PREP_TILE // Q_BLOCK):
        rows = slice(j * Q_BLOCK, (j + 1) * Q_BLOCK)
        for h in range(A_HEADS):
            q_latT = lax.dot_general(wukT_ref[h], nope[rows, h * A_NOPE:(h + 1) * A_NOPE].astype(cdt),
                                     (((1,), (1,)), ((), ())), preferred_element_type=f32)
            qT_ref[0, j, :A_KV_RANK, h * Q_BLOCK:(h + 1) * Q_BLOCK] = q_latT.astype(cdt)
        for p in range(A_HEADS * A_ROPE // LANE):
            t = tr(q_rope[rows, p * LANE:(p + 1) * LANE])
            qT_ref[0, j, A_KV_RANK:, (2 * p) * Q_BLOCK:(2 * p + 1) * Q_BLOCK] = t[:A_ROPE].astype(cdt)
            qT_ref[0, j, A_KV_RANK:, (2 * p + 1) * Q_BLOCK:(2 * p + 2) * Q_BLOCK] = t[A_ROPE:].astype(cdt)
        for p in range(IDX_HEADS * IDX_DIM // LANE):
            t = tr(iq[rows, p * LANE:(p + 1) * LANE])
            iqT_ref[0, j, :, (2 * p) * Q_BLOCK:(2 * p + 1) * Q_BLOCK] = t[:IDX_DIM].astype(cdt)
            iqT_ref[0, j, :, (2 * p + 1) * Q_BLOCK:(2 * p + 2) * Q_BLOCK] = t[IDX_DIM:].astype(cdt)
        iw_ref[0, j] = narT[NARROW_IW:NARROW_IW + IDX_HEADS, rows] * (IDX_HEADS * IDX_DIM) ** -0.5
        for h in range(C_HEADS):
            t = tr(cq[rows, h * LANE:(h + 1) * LANE])
            qbd_ref[0, j, h, :, :Q_BLOCK] = jnp.where(top, t, 0.0).astype(cdt)
            qbd_ref[0, j, h, :, Q_BLOCK:] = jnp.where(top, 0.0, t).astype(cdt)
    ck = rope(ck_ref[...])
    cv = cv_ref[...]
    for h in range(C_HEADS):
        kc_ref[0, 0, h] = ck[:, h * LANE:(h + 1) * LANE].astype(cdt)
        vT_ref[0, 0, h] = tr(cv[:, h * LANE:(h + 1) * LANE]).astype(cdt)
    bqk = rope(bqk_ref[...])
    rq_ref[0] = bqk[:, :B_HEADS * B_QK].astype(cdt)
    for p in range(B_HEADS * B_QK // LANE):
        t = tr(bqk[:, B_HEADS * B_QK + p * LANE:B_HEADS * B_QK + (p + 1) * LANE]) * B_QK ** -0.5
        rkT_ref[0, 2 * p] = jnp.where(top_wide, t, 0.0).astype(cdt)
        rkT_ref[0, 2 * p + 1] = jnp.where(top_wide, 0.0, t).astype(cdt)


def _mixer_prep(proj, cos, sin, kv_norm, wukT, bsz, seq):
    nt = seq // PREP_TILE
    nq = seq // Q_BLOCK
    cdt = MXU_DTYPE
    col = lambda width, c: pl.BlockSpec((PREP_TILE, width), lambda b, t: (b * nt + t, c))
    table = pl.BlockSpec((1, PREP_TILE, LANE), lambda b, t: (b, t, 0))
    per_q = lambda *tail: pl.BlockSpec((1, PREP_TILE // Q_BLOCK) + tail, lambda b, t: (b, t) + (0,) * len(tail))
    per_c = lambda *tail: pl.BlockSpec((1, 1) + tail, lambda b, t: (b, t) + (0,) * len(tail))
    sds = jax.ShapeDtypeStruct
    outs = [
        (sds((bsz, nq, A_LAT, A_HEADS * Q_BLOCK), cdt), per_q(A_LAT, A_HEADS * Q_BLOCK)),
        (sds((bsz, nq, IDX_DIM, IDX_HEADS * Q_BLOCK), cdt), per_q(IDX_DIM, IDX_HEADS * Q_BLOCK)),
        (sds((bsz, nq, IDX_HEADS, Q_BLOCK), jnp.float32), per_q(IDX_HEADS, Q_BLOCK)),
        (sds((bsz, nt, KCHUNK, IDX_DIM), cdt), per_c(KCHUNK, IDX_DIM)),
        (sds((bsz, nt, KCHUNK, A_LAT), cdt), per_c(KCHUNK, A_LAT)),
        (sds((bsz, nt, A_KV_RANK, KCHUNK), cdt), per_c(A_KV_RANK, KCHUNK)),
        (sds((bsz, nq, C_HEADS, 2 * C_DH, 2 * Q_BLOCK), cdt), per_q(C_HEADS, 2 * C_DH, 2 * Q_BLOCK)),
        (sds((bsz, nt, C_HEADS, KCHUNK, 2 * C_DH), cdt), per_c(C_HEADS, KCHUNK, 2 * C_DH)),
        (sds((bsz, nt, C_HEADS, 2 * C_DH, KCHUNK), cdt), per_c(C_HEADS, 2 * C_DH, KCHUNK)),
        (sds((bsz, seq, B_HEADS * B_QK), cdt), pl.BlockSpec((1, PREP_TILE, B_HEADS * B_QK), lambda b, t: (b, t, 0))),
        (sds((bsz, B_HEADS, LANE, seq), cdt), pl.BlockSpec((1, B_HEADS, LANE, PREP_TILE), lambda b, t: (b, 0, 0, t))),
        (sds((bsz, 2 * D_HEADS, seq), jnp.float32), pl.BlockSpec((1, 2 * D_HEADS, PREP_TILE), lambda b, t: (b, 0, t))),
    ]
    return pl.pallas_call(
        _prep_body,
        grid=(bsz, nt),
        in_specs=[col(COL, COL_B_QK), col(COL, COL_C_Q), col(COL, COL_C_K), col(COL, COL_C_V),
                  col(COL, COL_A_CKV_ROPE), col(COL, COL_A_NOPE), col(COL, COL_A_IQ), col(NARROW, COL_NARROW),
                  table, table,
                  pl.BlockSpec((1, A_KV_RANK), lambda b, t: (0, 0)),
                  pl.BlockSpec((A_HEADS, A_KV_RANK, A_NOPE), lambda b, t: (0, 0, 0))],
        out_specs=[o[1] for o in outs],
        out_shape=[o[0] for o in outs],
        compiler_params=_params("arbitrary", "arbitrary"),
        name="mixer_prep",
    )(proj, proj, proj, proj, proj, proj, proj, proj, cos, sin, kv_norm.reshape(1, A_KV_RANK), wukT)


def _rope_tables(positions):
    half = ROPE_DIM // 2
    inv_freq = ROPE_THETA ** (-jnp.arange(half, dtype=jnp.float32) / half)
    ang = positions.astype(jnp.float32)[:, :, None] * inv_freq
    cos, sin = jnp.cos(ang), jnp.sin(ang)
    reps = LANE // ROPE_DIM
    return (jnp.concatenate([cos, cos] * reps, -1), jnp.concatenate([-sin, sin] * reps, -1))


_SIGN_BIT = np.int32(-2 ** 31)
_LOW31 = np.int32(2 ** 31 - 1)
_KEY_NEG_INF = np.int32(np.array(-np.inf, np.float32).view(np.int32) ^ _LOW31)


def _dsa_body(iqT_ref, iw_ref, qT_ref, ik_ref, kv_ref, ckvT_ref, wuv_ref, o_ref,
              key_ref, acc_ref, thr_ref, need_ref, *, n_keep):
    f32 = jnp.float32
    qi = pl.program_id(1)
    nk = (qi * Q_BLOCK + Q_BLOCK + KCHUNK - 1) // KCHUNK
    iqT = iqT_ref[0, 0]
    iw8 = iw_ref[0, 0]
    iw = jnp.concatenate([iw8[h:h + 1] for h in range(IDX_HEADS)], axis=1)
    qpos = qi * Q_BLOCK + lax.broadcasted_iota(jnp.int32, (KCHUNK, Q_BLOCK), 1)
    krow = lax.broadcasted_iota(jnp.int32, (KCHUNK, Q_BLOCK), 0)

    def index_chunk(c, carry):
        lg = jnp.dot(ik_ref[0, c], iqT, preferred_element_type=f32)
        w = jnp.maximum(lg, 0.0) * iw
        idx = w[:, :Q_BLOCK]
        for h in range(1, IDX_HEADS):
            idx = idx + w[:, h * Q_BLOCK:(h + 1) * Q_BLOCK]
        idx = jnp.where(idx == 0.0, 0.0, idx)
        idx = jnp.where(krow + c * KCHUNK <= qpos, idx, -jnp.inf)
        bits = pltpu.bitcast(idx, jnp.int32)
        key_ref[c] = bits ^ ((bits >> 31) & _LOW31)
        return carry

    lax.fori_loop(0, nk, index_chunk, 0)

    def count(cmp, t):
        def body(c, cnt):
            hit = cmp(key_ref[c], t).astype(jnp.int32)
            return cnt + hit.reshape(KCHUNK // SUBLANE, SUBLANE, Q_BLOCK).sum(0)
        cnt = lax.fori_loop(0, nk, body, jnp.zeros((SUBLANE, Q_BLOCK), jnp.int32))
        return cnt.sum(0, keepdims=True)

    @pl.when(qi * Q_BLOCK + Q_BLOCK <= n_keep)
    def _():
        thr_ref[...] = jnp.full((1, Q_BLOCK), _KEY_NEG_INF, jnp.int32)
        need_ref[...] = jnp.zeros((1, Q_BLOCK), jnp.int32)

    @pl.when(qi * Q_BLOCK + Q_BLOCK > n_keep)
    def _():
        def bit_step(i, prefix):
            cand = prefix | jnp.left_shift(jnp.int32(1), 31 - i)
            cnt = count(lambda k, t: k >= t, cand ^ _SIGN_BIT)
            return jnp.where(cnt >= n_keep, cand, prefix)
        prefix = lax.fori_loop(0, 32, bit_step, jnp.zeros((1, Q_BLOCK), jnp.int32))
        thr = prefix ^ _SIGN_BIT
        thr_ref[...] = thr
        need_ref[...] = n_keep - count(lambda k, t: k > t, thr)

    thr = thr_ref[...]
    need = need_ref[...].astype(f32)
    qT = qT_ref[0, 0]
    scale = (A_NOPE + A_ROPE) ** -0.5
    r_i = lax.broadcasted_iota(jnp.int32, (KCHUNK, KCHUNK), 0)
    c_i = lax.broadcasted_iota(jnp.int32, (KCHUNK, KCHUNK), 1)
    before = (c_i < r_i).astype(jnp.bfloat16)
    acc_ref[...] = jnp.zeros_like(acc_ref)

    def attend_chunk(c, carry):
        m, l, ties_seen = carry
        s = jnp.dot(kv_ref[0, c], qT, preferred_element_type=f32) * scale
        key = key_ref[c]
        tie = key == thr
        tie_f = tie.astype(f32)
        rank = jnp.dot(before, tie_f.astype(jnp.bfloat16), preferred_element_type=f32) + ties_seen
        keep = (key > thr) | (tie & (rank < need))
        bias = jnp.where(keep, 0.0, _MASKED)
        s = s + jnp.concatenate([bias] * A_HEADS, axis=1)
        m_new = jnp.maximum(m, s.max(0, keepdims=True))
        alpha = jnp.exp(m - m_new)
        p = jnp.exp(s - m_new)
        l = alpha * l + p.sum(0, keepdims=True)
        pv = jnp.dot(ckvT_ref[0, c], p.astype(ckvT_ref.dtype), preferred_element_type=f32)
        acc_ref[...] = acc_ref[...] * alpha + pv
        return m_new, l, ties_seen + tie_f.sum(0, keepdims=True)

    lanes = A_HEADS * Q_BLOCK
    init = (jnp.full((1, lanes), _MASKED, f32), jnp.zeros((1, lanes), f32), jnp.zeros((1, Q_BLOCK), f32))
    _, l, _ = lax.fori_loop(0, nk, attend_chunk, init)
    o_latT = acc_ref[...] / l
    for h in range(A_HEADS):
        o_lat = o_latT[:, h * Q_BLOCK:(h + 1) * Q_BLOCK].T.astype(wuv_ref.dtype)
        o_ref[0, :, h * A_V:(h + 1) * A_V] = jnp.dot(o_lat, wuv_ref[h], preferred_element_type=f32)


def _dsa_attention(iqT, iw, qT, ikc, kvc, ckvT, wuv, *, n_keep):
    bsz, nq = qT.shape[:2]
    nc = ikc.shape[1]
    seq = nq * Q_BLOCK
    per_q = lambda b, q: (b, q, 0, 0)
    per_b = lambda b, q: (b, 0, 0, 0)
    return pl.pallas_call(
        functools.partial(_dsa_body, n_keep=n_keep),
        grid=(bsz, nq),
        in_specs=[pl.BlockSpec((1, 1) + iqT.shape[2:], per_q),
                  pl.BlockSpec((1, 1) + iw.shape[2:], per_q),
                  pl.BlockSpec((1, 1) + qT.shape[2:], per_q),
                  pl.BlockSpec((1,) + ikc.shape[1:], per_b),
                  pl.BlockSpec((1,) + kvc.shape[1:], per_b),
                  pl.BlockSpec((1,) + ckvT.shape[1:], per_b),
                  pl.BlockSpec(wuv.shape, lambda b, q: (0, 0, 0))],
        out_specs=pl.BlockSpec((1, Q_BLOCK, A_HEADS * A_V), lambda b, q: (b, q, 0)),
        out_shape=jax.ShapeDtypeStruct((bsz, seq, A_HEADS * A_V), jnp.float32),
        scratch_shapes=[pltpu.VMEM((nc, KCHUNK, Q_BLOCK), jnp.int32),
                        pltpu.VMEM((A_KV_RANK, A_HEADS * Q_BLOCK), jnp.float32),
                        pltpu.VMEM((1, Q_BLOCK), jnp.int32),
                        pltpu.VMEM((1, Q_BLOCK), jnp.int32)],
        compiler_params=_params("arbitrary", "arbitrary"),
        name="dsa_attention",
    )(iqT, iw, qT, ikc, kvc, ckvT, wuv)


RET_TILE = 256


def _retention_body(q_ref, kT_ref, v_ref, g_ref, o_ref, s_ref):
    f32 = jnp.float32
    cdt = MXU_DTYPE
    c = RET_CHUNK

    @pl.when(pl.program_id(1) == 0)
    def _():
        s_ref[...] = jnp.zeros_like(s_ref)

    rel = (lax.broadcasted_iota(jnp.int32, (c, c), 0) - lax.broadcasted_iota(jnp.int32, (c, c), 1)).astype(f32)
    pos_c = lax.broadcasted_iota(jnp.int32, (c, 1), 0).astype(f32)
    pos_r = lax.broadcasted_iota(jnp.int32, (1, c), 1).astype(f32)

    def mm(a, b):
        return jnp.dot(a.astype(cdt), b.astype(cdt), preferred_element_type=f32)

    for h in range(B_HEADS):
        log_gamma = float(np.log(np.float32(1.0) - np.float32(2.0) ** np.float32(-5 - h)))
        intra = jnp.where(rel >= 0, jnp.exp(log_gamma * jnp.maximum(rel, 0.0)), 0.0)
        to_end = jnp.exp(log_gamma * (c - 1 - pos_r))
        from_start = jnp.exp(log_gamma * (pos_c + 1.0))
        chunk_decay = float(np.exp(np.float32(log_gamma) * np.float32(c)))
        lanes = slice(h * B_V, (h + 1) * B_V)
        pair = slice((h // 2) * LANE, (h // 2 + 1) * LANE)
        for ci in range(RET_TILE // c):
            rows = slice(ci * c, (ci + 1) * c)
            q = q_ref[0, rows, pair].astype(f32)
            kT = kT_ref[0, h, :, rows].astype(f32)
            v = v_ref[rows, lanes]
            state = s_ref[h]
            o = mm(mm(q, kT) * intra, v) + mm(q * from_start, state)
            s_ref[h] = state * chunk_decay + mm(kT * to_end, v)
            gate = g_ref[rows, lanes]
            o = o * lax.rsqrt(jnp.mean(o * o, -1, keepdims=True) + RMS_EPS)
            o_ref[0, rows, lanes] = o * (gate * _sigmoid(gate))


def _retention(rq, rkT, proj, bsz, seq):
    nt = seq // RET_TILE
    width = B_HEADS * B_V
    col = lambda c: pl.BlockSpec((RET_TILE, COL), lambda b, t: (b * nt + t, c))
    return pl.pallas_call(
        _retention_body,
        grid=(bsz, nt),
        in_specs=[pl.BlockSpec((1, RET_TILE, B_HEADS * B_QK), lambda b, t: (b, t, 0)),
                  pl.BlockSpec((1, B_HEADS, LANE, RET_TILE), lambda b, t: (b, 0, 0, t)),
                  col(COL_B_V), col(COL_B_G)],
        out_specs=pl.BlockSpec((1, RET_TILE, width), lambda b, t: (b, t, 0)),
        out_shape=jax.ShapeDtypeStruct((bsz, seq, width), jnp.float32),
        scratch_shapes=[pltpu.VMEM((B_HEADS, LANE, B_V), jnp.float32)],
        compiler_params=_params("arbitrary", "arbitrary"),
        name="retention",
    )(rq, rkT, proj, proj)


def _diff_attention_body(lam_ref, qbd_ref, k_ref, vT_ref, gain_ref, o_ref, acc_ref):
    f32 = jnp.float32
    qi = pl.program_id(1)
    nk = (qi * Q_BLOCK + Q_BLOCK + KCHUNK - 1) // KCHUNK
    qpos = qi * Q_BLOCK + lax.broadcasted_iota(jnp.int32, (KCHUNK, Q_BLOCK), 1)
    krow = lax.broadcasted_iota(jnp.int32, (KCHUNK, Q_BLOCK), 0)
    acc_ref[...] = jnp.zeros_like(acc_ref)

    heads = range(C_HEADS)

    def chunk(c, carry):
        ms, ls = carry
        causal = jnp.where(krow + c * KCHUNK <= qpos, 0.0, _MASKED)
        bias = jnp.concatenate([causal, causal], axis=1)
        s = [jnp.dot(k_ref[0, c, h], qbd_ref[0, 0, h], preferred_element_type=f32) + bias for h in heads]
        new_ms = [jnp.maximum(ms[h], s[h].max(0, keepdims=True)) for h in heads]
        alpha = [jnp.exp(ms[h] - new_ms[h]) for h in heads]
        p = [jnp.exp(s[h] - new_ms[h]) for h in heads]
        new_ls = [alpha[h] * ls[h] + p[h].sum(0, keepdims=True) for h in heads]
        pv = [jnp.dot(vT_ref[0, c, h], p[h].astype(vT_ref.dtype), preferred_element_type=f32) for h in heads]
        for h in heads:
            acc_ref[h] = acc_ref[h] * alpha[h] + pv[h]
        return tuple(new_ms), tuple(new_ls)

    lanes = 2 * Q_BLOCK
    init = (tuple(jnp.full((1, lanes), _MASKED, f32) for _ in range(C_HEADS)),
            tuple(jnp.zeros((1, lanes), f32) for _ in range(C_HEADS)))
    _, ls = lax.fori_loop(0, nk, chunk, init)
    lam = lam_ref[0]
    for h in range(C_HEADS):
        o = acc_ref[h] / ls[h]
        o = o[:, :Q_BLOCK] - lam * o[:, Q_BLOCK:]
        o = o * lax.rsqrt(jnp.mean(o * o, 0, keepdims=True) + RMS_EPS)
        o_ref[0, :, h * 2 * C_DH:(h + 1) * 2 * C_DH] = o.T * gain_ref[...]


def _diff_attention(lam, qbd, kc, vT, gain):
    bsz, nq = qbd.shape[:2]
    seq = nq * Q_BLOCK
    width = C_HEADS * 2 * C_DH
    return pl.pallas_call(
        _diff_attention_body,
        grid=(bsz, nq),
        in_specs=[pl.BlockSpec(memory_space=pltpu.SMEM),
                  pl.BlockSpec((1, 1) + qbd.shape[2:], lambda b, q: (b, q, 0, 0, 0)),
                  pl.BlockSpec((1,) + kc.shape[1:], lambda b, q: (b, 0, 0, 0, 0)),
                  pl.BlockSpec((1,) + vT.shape[1:], lambda b, q: (b, 0, 0, 0, 0)),
                  pl.BlockSpec(gain.shape, lambda b, q: (0, 0))],
        out_specs=pl.BlockSpec((1, Q_BLOCK, width), lambda b, q: (b, q, 0)),
        out_shape=jax.ShapeDtypeStruct((bsz, seq, width), jnp.float32),
        scratch_shapes=[pltpu.VMEM((C_HEADS, 2 * C_DH, 2 * Q_BLOCK), jnp.float32)],
        compiler_params=_params("arbitrary", "arbitrary"),
        name="diff_attention",
    )(lam, qbd, kc, vT, gain)


GDN_TILE = 256
GDN_SUB = 16


def _gdn_body(qkv_ref, z_ref, nar_ref, gateT_ref, conv_ref, alog_r_ref, dtb_r_ref, alog_c_ref, dtb_c_ref,
              ng_ref, o_ref, halo_ref, s_ref):
    f32 = jnp.float32
    cdt = MXU_DTYPE
    c = GDN_CHUNK

    @pl.when(pl.program_id(1) == 0)
    def _():
        halo_ref[...] = jnp.zeros_like(halo_ref)
        s_ref[...] = jnp.zeros_like(s_ref)

    x = qkv_ref[...]
    xp = jnp.concatenate([halo_ref[...], x], 0)
    w = conv_ref[...]
    pre = xp[SUBLANE - CONV_K + 1:SUBLANE - CONV_K + 1 + GDN_TILE] * w[0:1]
    for j in range(1, CONV_K):
        off = SUBLANE - CONV_K + 1 + j
        pre = pre + xp[off:off + GDN_TILE] * w[j:j + 1]
    halo_ref[...] = x[GDN_TILE - SUBLANE:]
    qkv = pre * _sigmoid(pre)

    gates = nar_ref[:, LANE:]
    beta_c = _sigmoid(gates[:, NARROW_DB:NARROW_DB + D_HEADS])
    g_c = -jnp.exp(alog_r_ref[...]) * _softplus(gates[:, NARROW_DA:NARROW_DA + D_HEADS] + dtb_r_ref[...])
    g_r = -jnp.exp(alog_c_ref[...]) * _softplus(gateT_ref[0][D_HEADS:] + dtb_c_ref[...])

    row = lax.broadcasted_iota(jnp.int32, (c, c), 0)
    col = lax.broadcasted_iota(jnp.int32, (c, c), 1)
    tril = row >= col
    strict = row > col
    same_sub = (row // GDN_SUB) == (col // GDN_SUB)
    eye = (row == col).astype(f32)
    lower_ones = tril.astype(f32)
    upper_ones = (row <= col).astype(f32)

    def mm(a, b):
        return jnp.dot(a, b, preferred_element_type=f32)

    n_chunks = GDN_TILE // c
    pairs = [(ci, h) for ci in range(n_chunks) for h in range(D_HEADS)]
    rows_of = lambda ci: slice(ci * c, (ci + 1) * c)
    g_cum_c = [mm(lower_ones, g_c[rows_of(ci)]) for ci in range(n_chunks)]
    g_cum_r = [mm(g_r[:, rows_of(ci)], upper_ones) for ci in range(n_chunks)]
    st = []
    for ci, h in pairs:
        rows = rows_of(ci)
        q = qkv[rows, h * D_K:(h + 1) * D_K]
        k = qkv[rows, D_HEADS * D_K + h * D_K:D_HEADS * D_K + (h + 1) * D_K]
        v = qkv[rows, 2 * D_HEADS * D_K + h * D_V:2 * D_HEADS * D_K + (h + 1) * D_V]
        q = q * lax.rsqrt(jnp.sum(q * q, -1, keepdims=True) + RMS_EPS) * D_K ** -0.5
        k = k * lax.rsqrt(jnp.sum(k * k, -1, keepdims=True) + RMS_EPS)
        gc = g_cum_c[ci][:, h:h + 1]
        gr = g_cum_r[ci][h:h + 1, :]
        g_last = gc[c - 1:c]
        gam = jnp.where(tril, jnp.exp(jnp.where(tril, gc - gr, 0.0)), 0.0)
        e_g = jnp.exp(gc)
        beta = beta_c[rows, h:h + 1]
        kT = k.T
        kb = k * beta
        st.append(dict(gam=gam, kT_c=kT.astype(cdt), kb_c=kb.astype(cdt), q_c=q.astype(cdt),
                       qg_c=(q * e_g).astype(cdt), rhs=jnp.concatenate([v * beta, kb * e_g], 1),
                       kdT_c=(kT * jnp.exp(g_last - gr)).astype(cdt), decay=jnp.exp(g_last)))
    for p in st:
        p["low"] = jnp.where(strict, mm(p["kb_c"], p["kT_c"]) * p["gam"], 0.0)
        p["attn_c"] = jnp.where(tril, mm(p["q_c"], p["kT_c"]) * p["gam"], 0.0).astype(cdt)
        p["low_d"] = jnp.where(same_sub, p["low"], 0.0)
        p["xk"] = -p["low_d"]
        p["td"] = eye + p["xk"]
    for _ in range(3):
        for p in st:
            p["xk"] = mm(p["xk"], p["xk"])
        for p in st:
            p["td"] = p["td"] + mm(p["td"], p["xk"])
    for p in st:
        p["m1"] = mm(p["td"], p["low"] - p["low_d"])
        p["sol"] = mm(p["td"], p["rhs"])
    for p in st:
        p["m2"] = mm(p["m1"], p["m1"])
    for p in st:
        p["sol"] = p["sol"] + mm(p["m2"], p["sol"])
    for p in st:
        p["sol"] = p["sol"] - mm(p["m1"], p["sol"])
    for ci in range(n_chunks):
        rows = rows_of(ci)
        cur = [st[ci * D_HEADS + h] for h in range(D_HEADS)]
        states = [s_ref[h] for h in range(D_HEADS)]
        states_c = [s.astype(cdt) for s in states]
        v_new_c = [(p["sol"][:, :D_V] - mm(p["sol"][:, D_V:].astype(cdt), states_c[h])).astype(cdt)
                   for h, p in enumerate(cur)]
        for h, p in enumerate(cur):
            s_ref[h] = states[h] * p["decay"] + mm(p["kdT_c"], v_new_c[h])
        for h, p in enumerate(cur):
            lanes = slice(h * D_V, (h + 1) * D_V)
            o = mm(p["qg_c"], states_c[h]) + mm(p["attn_c"], v_new_c[h])
            zz = z_ref[rows, lanes]
            o = o * lax.rsqrt(jnp.mean(o * o, -1, keepdims=True) + RMS_EPS) * ng_ref[...]
            o_ref[0, rows, lanes] = o * (zz * _sigmoid(zz))


def _gated_deltanet(proj, gatesT, conv_w, a_log, dt_bias, norm_g, bsz, seq):
    nt = seq // GDN_TILE
    width = D_HEADS * (2 * D_K + D_V)
    whole = lambda r, cc: pl.BlockSpec((r, cc), lambda b, t: (0, 0))
    return pl.pallas_call(
        _gdn_body,
        grid=(bsz, nt),
        in_specs=[pl.BlockSpec((GDN_TILE, width), lambda b, t: (b * nt + t, COL_D_QKV)),
                  pl.BlockSpec((GDN_TILE, COL), lambda b, t: (b * nt + t, COL_D_Z)),
                  pl.BlockSpec((GDN_TILE, NARROW), lambda b, t: (b * nt + t, COL_NARROW)),
                  pl.BlockSpec((1, 2 * D_HEADS, GDN_TILE), lambda b, t: (b, 0, t)),
                  whole(CONV_K, width), whole(1, D_HEADS), whole(1, D_HEADS), whole(D_HEADS, 1), whole(D_HEADS, 1),
                  whole(1, D_V)],
        out_specs=pl.BlockSpec((1, GDN_TILE, D_HEADS * D_V), lambda b, t: (b, t, 0)),
        out_shape=jax.ShapeDtypeStruct((bsz, seq, D_HEADS * D_V), jnp.float32),
        scratch_shapes=[pltpu.VMEM((SUBLANE, width), jnp.float32),
                        pltpu.VMEM((D_HEADS, D_K, D_V), jnp.float32)],
        compiler_params=_params("arbitrary", "arbitrary"),
        name="gated_deltanet",
    )(proj, proj, proj, gatesT, conv_w, a_log.reshape(1, D_HEADS), dt_bias.reshape(1, D_HEADS),
      a_log.reshape(D_HEADS, 1), dt_bias.reshape(D_HEADS, 1), norm_g.reshape(1, D_V))


def _ln_rows(t, g, b):
    mu = jnp.mean(t, -1, keepdims=True)
    d = t - mu
    var = jnp.mean(d * d, -1, keepdims=True)
    return d * lax.rsqrt(var + LN_EPS) * g + b


def _out_ln_router_body(ha_ref, hb_ref, hc_ref, hd_ref, w_ref, x_ref, g_ref, b_ref, wr_ref, br_ref,
                        h_ref, te_ref, tg_ref):
    f32 = jnp.float32
    acc = DEEPNORM_ALPHA * x_ref[...]
    row0 = 0
    for part in (ha_ref, hb_ref, hc_ref, hd_ref):
        width = part.shape[-1]
        acc = acc + jnp.dot(part[...].astype(MXU_DTYPE), w_ref[row0:row0 + width, :], preferred_element_type=f32)
        row0 += width
    h = _ln_rows(acc, g_ref[...], b_ref[...])
    h_ref[...] = h
    logits = jnp.dot(h.astype(MXU_DTYPE), wr_ref[...], preferred_element_type=f32) + br_ref[...]
    lane = lax.broadcasted_iota(jnp.int32, logits.shape, 1)
    vals, te = [], jnp.zeros(logits.shape, jnp.int32)
    for k in range(TOP_K):
        m = logits.max(-1, keepdims=True)
        e = jnp.min(jnp.where(logits == m, lane, LANE), -1, keepdims=True)
        vals.append(m)
        te = jnp.where(lane == k, e, te)
        logits = jnp.where(lane == e, -jnp.inf, logits)
    ex = [jnp.exp(v - vals[0]) for v in vals]
    den = ex[0] + ex[1] + ex[2] + ex[3]
    tg = jnp.zeros(logits.shape, f32)
    for k in range(TOP_K):
        tg = jnp.where(lane == k, ex[k] / den, tg)
    te_ref[...] = te
    tg_ref[...] = tg


def _out_ln_router(head_parts, w_out_c, x2d, g, b, w_router, b_router, tm=512):
    m = x2d.shape[0]
    k, n = w_out_c.shape
    wr = jnp.zeros((n, LANE), MXU_DTYPE).at[:, :N_EXPERTS].set(w_router.astype(MXU_DTYPE))
    br = jnp.full((1, LANE), -jnp.inf, jnp.float32).at[0, :N_EXPERTS].set(b_router)
    row_blk = lambda c: pl.BlockSpec((tm, c), lambda i: (i, 0))
    whole = lambda r, c: pl.BlockSpec((r, c), lambda i: (0, 0))
    return pl.pallas_call(
        _out_ln_router_body,
        grid=(m // tm,),
        in_specs=[row_blk(p.shape[1]) for p in head_parts]
        + [whole(k, n), row_blk(n), whole(1, n), whole(1, n), whole(n, LANE), whole(1, LANE)],
        out_specs=[row_blk(n), row_blk(LANE), row_blk(LANE)],
        out_shape=[jax.ShapeDtypeStruct((m, n), jnp.float32),
                   jax.ShapeDtypeStruct((m, LANE), jnp.int32),
                   jax.ShapeDtypeStruct((m, LANE), jnp.float32)],
        compiler_params=_params("arbitrary"),
        name="out_proj_ln_router",
    )(*head_parts, w_out_c, x2d, g.reshape(1, n), b.reshape(1, n), wr, br)


def _clamped_swiglu(hg):
    glu, lin = hg[..., :D_FF], hg[..., D_FF:]
    glu = jnp.minimum(glu, SWIGLU_LIMIT)
    lin = jnp.clip(lin, -SWIGLU_LIMIT, SWIGLU_LIMIT)
    return glu * _sigmoid(SWIGLU_ALPHA * glu) * (lin + 1.0)


MOE_RANK_TILE = 512


def _moe_rank_body(te_ref, rank_ref, counts_ref, seen_ref):
    f32 = jnp.float32
    tm = MOE_RANK_TILE

    @pl.when(pl.program_id(0) == 0)
    def _():
        seen_ref[...] = jnp.zeros_like(seen_ref)

    te = te_ref[...]
    lane = lax.broadcasted_iota(jnp.int32, (tm, LANE), 1)
    picks = [lane == te[:, k:k + 1] for k in range(TOP_K)]
    member = picks[0].astype(f32)
    for k in range(1, TOP_K):
        member = member + picks[k].astype(f32)
    earlier = (lax.broadcasted_iota(jnp.int32, (tm, tm), 1) < lax.broadcasted_iota(jnp.int32, (tm, tm), 0))
    prefix = jnp.dot(earlier.astype(jnp.bfloat16), member.astype(jnp.bfloat16), preferred_element_type=f32)
    prefix = prefix + seen_ref[...]
    rank = jnp.zeros((tm, LANE), jnp.int32)
    for k in range(TOP_K):
        r_k = jnp.sum(jnp.where(picks[k], prefix, 0.0), -1, keepdims=True)
        rank = jnp.where(lane == k, r_k.astype(jnp.int32), rank)
    rank_ref[...] = rank
    seen = seen_ref[...] + member.sum(0, keepdims=True)
    seen_ref[...] = seen
    counts_ref[...] = seen.astype(jnp.int32)


def _moe_rank(te):
    n_tok = te.shape[0]
    tm = MOE_RANK_TILE
    return pl.pallas_call(
        _moe_rank_body,
        grid=(n_tok // tm,),
        in_specs=[pl.BlockSpec((tm, LANE), lambda i: (i, 0))],
        out_specs=[pl.BlockSpec((tm, LANE), lambda i: (i, 0)), pl.BlockSpec((1, LANE), lambda i: (0, 0))],
        out_shape=[jax.ShapeDtypeStruct((n_tok, LANE), jnp.int32), jax.ShapeDtypeStruct((1, LANE), jnp.int32)],
        scratch_shapes=[pltpu.VMEM((1, LANE), jnp.float32)],
        compiler_params=_params("arbitrary"),
        name="moe_rank",
    )(te)


def _moe_plan(te):
    top_e = te[:, :TOP_K]
    n_tok = top_e.shape[0]
    n_blocks = -(-n_tok * TOP_K // MOE_BLOCK) + N_EXPERTS
    rank, counts = _moe_rank(te)
    counts = counts[0, :N_EXPERTS]
    padded = (counts + MOE_BLOCK - 1) // MOE_BLOCK * MOE_BLOCK
    pad_end = jnp.cumsum(padded)
    one_hot = top_e[:, :, None] == jnp.arange(N_EXPERTS, dtype=jnp.int32)
    dest = rank[:, :TOP_K] + jnp.sum(jnp.where(one_hot, (pad_end - padded)[None, None, :], 0), -1)
    tok = jnp.broadcast_to(jnp.arange(n_tok, dtype=jnp.int32)[:, None], dest.shape)
    row_tok = jnp.zeros((n_blocks * MOE_BLOCK,), jnp.int32).at[dest.reshape(-1)].set(tok.reshape(-1))
    block_start = jnp.arange(n_blocks, dtype=jnp.int32) * MOE_BLOCK
    block_e = jnp.minimum(jnp.sum(pad_end[None, :] <= block_start[:, None], axis=1), N_EXPERTS - 1).astype(jnp.int32)
    n_used = (pad_end[-1:] // MOE_BLOCK).astype(jnp.int32)
    return dest, row_tok, block_e, n_used


def _row_gather_start(src_hbm, rows_smem, n_rows, dst, sem):
    def body(r, carry):
        pltpu.make_async_copy(src_hbm.at[pl.ds(rows_smem[0, 0, r], 1)], dst.at[pl.ds(r, 1)], sem).start()
        return carry
    lax.fori_loop(0, n_rows, body, 0, unroll=8)


def _row_gather_wait(src_hbm, n_rows, dst, sem):
    pltpu.make_async_copy(src_hbm.at[pl.ds(0, n_rows)], dst, sem).wait()


def _moe_expert_body(be_ref, nused_ref, tok_ref, tok_next_ref, h_hbm, wgu_ref, bgu_ref, wdn_ref, bdn_ref,
                     y_ref, buf_even, buf_odd, sem, wgu_c, wdn_c):
    f32 = jnp.float32
    i = pl.program_id(0)
    n_used = nused_ref[0]
    bufs = (buf_even, buf_odd)

    @pl.when(i == 0)
    def _():
        _row_gather_start(h_hbm, tok_ref, MOE_BLOCK, buf_even, sem.at[0])

    @pl.when(jnp.logical_or(i == 0, be_ref[i] != be_ref[jnp.maximum(i - 1, 0)]))
    def _():
        wgu_c[...] = wgu_ref[0, 0].astype(MXU_DTYPE)
        wdn_c[...] = wdn_ref[0, 0].astype(MXU_DTYPE)

    for parity in (0, 1):
        cur, nxt = bufs[parity], bufs[1 - parity]

        @pl.when(jnp.logical_and(i < n_used, i % 2 == parity))
        def _():
            _row_gather_wait(h_hbm, MOE_BLOCK, cur, sem.at[parity])
            for r in range(MOE_BLOCK):
                pltpu.make_async_copy(h_hbm.at[pl.ds(tok_next_ref[0, 0, r], 1)], nxt.at[pl.ds(r, 1)],
                                      sem.at[1 - parity]).start()
            xb = cur[...].astype(MXU_DTYPE)
            hg = jnp.dot(xb, wgu_c[...], preferred_element_type=f32) + bgu_ref[0, 0]
            act = _clamped_swiglu(hg)
            y_ref[...] = jnp.dot(act.astype(MXU_DTYPE), wdn_c[...], preferred_element_type=f32) + bdn_ref[0, 0]

        @pl.when(jnp.logical_and(i == n_used, i % 2 == parity))
        def _():
            _row_gather_wait(h_hbm, MOE_BLOCK, cur, sem.at[parity])

    @pl.when(i >= n_used)
    def _():
        y_ref[...] = jnp.zeros_like(y_ref)


def _moe_experts(h2d, row_tok, block_e, n_used, layer, w_gu, b_gu, w_dn, b_dn):
    n_tok, d = h2d.shape
    n_blocks = block_e.shape[0]
    n_layers = w_gu.shape[0]
    assert n_blocks * MOE_BLOCK > n_tok * TOP_K + N_EXPERTS * (MOE_BLOCK - 1)
    tok3 = row_tok.reshape(n_blocks, 1, MOE_BLOCK)
    smem_blk = lambda imap: pl.BlockSpec((1, 1, MOE_BLOCK), imap, memory_space=pltpu.SMEM)
    by_expert = lambda shape: pl.BlockSpec((1, 1) + shape, lambda i, be, nu: (layer, be[i], 0, 0))
    return pl.pallas_call(
        _moe_expert_body,
        grid_spec=pltpu.PrefetchScalarGridSpec(
            num_scalar_prefetch=2,
            grid=(n_blocks,),
            in_specs=[smem_blk(lambda i, be, nu: (i, 0, 0)),
                      smem_blk(lambda i, be, nu: (jnp.minimum(i + 1, n_blocks - 1), 0, 0)),
                      pl.BlockSpec(memory_space=pl.ANY),
                      by_expert((d, 2 * D_FF)), by_expert((1, 2 * D_FF)),
                      by_expert((D_FF, d)), by_expert((1, d))],
            out_specs=pl.BlockSpec((MOE_BLOCK, d), lambda i, be, nu: (i, 0)),
            scratch_shapes=[pltpu.VMEM((MOE_BLOCK, d), jnp.float32), pltpu.VMEM((MOE_BLOCK, d), jnp.float32),
                            pltpu.SemaphoreType.DMA((2,)),
                            pltpu.VMEM((d, 2 * D_FF), MXU_DTYPE), pltpu.VMEM((D_FF, d), MXU_DTYPE)]),
        out_shape=jax.ShapeDtypeStruct((n_blocks * MOE_BLOCK, d), jnp.float32),
        compiler_params=_params("arbitrary"),
        name="moe_experts",
    )(block_e, n_used, tok3, tok3, h2d, w_gu, b_gu.reshape(n_layers, N_EXPERTS, 1, 2 * D_FF),
      w_dn, b_dn.reshape(n_layers, N_EXPERTS, 1, d))


MOE_COMBINE_ROWS = 128


def _moe_combine_ln_body(dest_ref, dest_next_ref, h_ref, tg_ref, g_ref, b_ref, y_hbm, o_ref,
                         buf_even, buf_odd, sem):
    tm = MOE_COMBINE_ROWS
    i = pl.program_id(0)
    n = pl.num_programs(0)
    bufs = (buf_even, buf_odd)

    def start(rows_smem, dst, s):
        for k in range(TOP_K):
            for r in range(tm):
                pltpu.make_async_copy(y_hbm.at[pl.ds(rows_smem[0, 0, k * tm + r], 1)],
                                      dst.at[k, pl.ds(r, 1)], sem.at[s]).start()

    def drain(dst, s):
        for k in range(TOP_K):
            _row_gather_wait(y_hbm, tm, dst.at[k], sem.at[s])

    @pl.when(i == 0)
    def _():
        start(dest_ref, buf_even, 0)

    for parity in (0, 1):
        cur, nxt = bufs[parity], bufs[1 - parity]

        @pl.when(i % 2 == parity)
        def _():
            drain(cur, parity)
            start(dest_next_ref, nxt, 1 - parity)
            tg = tg_ref[...]
            y = tg[:, 0:1] * cur[0]
            for k in range(1, TOP_K):
                y = y + tg[:, k:k + 1] * cur[k]
            o_ref[...] = _ln_rows(DEEPNORM_ALPHA * h_ref[...] + y, g_ref[...], b_ref[...])

        @pl.when(jnp.logical_and(i == n - 1, i % 2 == parity))
        def _():
            drain(nxt, 1 - parity)


def _moe_combine_ln(h2d, y_rows, dest, tg, g, b):
    n_tok, d = h2d.shape
    tm = MOE_COMBINE_ROWS
    nt = n_tok // tm
    dest3 = dest.reshape(nt, tm, TOP_K).transpose(0, 2, 1).reshape(nt, 1, TOP_K * tm)
    smem_blk = lambda imap: pl.BlockSpec((1, 1, TOP_K * tm), imap, memory_space=pltpu.SMEM)
    return pl.pallas_call(
        _moe_combine_ln_body,
        grid=(nt,),
        in_specs=[smem_blk(lambda i: (i, 0, 0)),
                  smem_blk(lambda i: (jnp.minimum(i + 1, nt - 1), 0, 0)),
                  pl.BlockSpec((tm, d), lambda i: (i, 0)),
                  pl.BlockSpec((tm, LANE), lambda i: (i, 0)),
                  pl.BlockSpec((1, d), lambda i: (0, 0)),
                  pl.BlockSpec((1, d), lambda i: (0, 0)),
                  pl.BlockSpec(memory_space=pl.ANY)],
        out_specs=pl.BlockSpec((tm, d), lambda i: (i, 0)),
        out_shape=jax.ShapeDtypeStruct((n_tok, d), jnp.float32),
        scratch_shapes=[pltpu.VMEM((TOP_K, tm, d), jnp.float32), pltpu.VMEM((TOP_K, tm, d), jnp.float32),
                        pltpu.SemaphoreType.DMA((2,))],
        compiler_params=_params("arbitrary"),
        name="moe_combine_ln",
    )(dest3, dest3, h2d, tg, g.reshape(1, d), b.reshape(1, d), y_rows)


def _moe_layer(h2d, te, tg, layer, w_gu, b_gu, w_dn, b_dn, g, b):
    dest, row_tok, block_e, n_used = _moe_plan(te)
    y_rows = _moe_experts(h2d, row_tok, block_e, n_used, layer, w_gu, b_gu, w_dn, b_dn)
    return _moe_combine_ln(h2d, y_rows, dest, tg, g, b)


def _mixers(proj, cos, sin, bsz, seq, lambda_init, a_kv_norm, a_w_uk, a_w_uv, c_lambda, c_subln,
            d_conv, d_a_log, d_dt_bias, d_norm):
    cdt = MXU_DTYPE
    (qT, iqT, iw, ikc, kvc, ckvT, qbd, kc, vT, rq, rkT, gatesT) = _mixer_prep(
        proj, cos, sin, a_kv_norm, a_w_uk.transpose(1, 0, 2).astype(cdt), bsz, seq)
    head_a = _dsa_attention(iqT, iw, qT, ikc, kvc, ckvT, a_w_uv.transpose(1, 0, 2).astype(cdt),
                            n_keep=min(INDEX_TOPK, seq // 4))
    head_b = _retention(rq, rkT, proj, bsz, seq)
    lam_full = (jnp.exp(jnp.sum(c_lambda[0] * c_lambda[1])) - jnp.exp(jnp.sum(c_lambda[2] * c_lambda[3]))
                + lambda_init)
    gain = (c_subln * (1.0 - lambda_init)).reshape(1, 2 * C_DH)
    head_c = _diff_attention(lam_full.reshape(1), qbd, kc, vT, gain)
    head_d = _gated_deltanet(proj, gatesT, d_conv, d_a_log, d_dt_bias, d_norm, bsz, seq)
    return [t.reshape(bsz * seq, t.shape[-1]) for t in (head_a, head_b, head_c, head_d)]


def kernel(x, positions, w_in, w_out, a_kv_norm, a_w_uk, a_w_uv, c_lambda, c_subln, d_conv, d_a_log, d_dt_bias, d_norm, ln1_g, ln1_b, w_router, b_router, w_gu, b_gu, w_dn, b_dn, ln2_g, ln2_b):
    bsz, seq, d = x.shape
    x2d = x.reshape(bsz * seq, d)
    cos, sin = _rope_tables(positions)
    for l in range(DEPTH):
        proj = _in_proj(x2d, _pack_w_in(w_in[l]))
        lambda_init = 0.8 - 0.6 * math.exp(-0.3 * l)
        heads = _mixers(proj, cos, sin, bsz, seq, lambda_init, a_kv_norm[l], a_w_uk[l], a_w_uv[l], c_lambda[l],
                        c_subln[l], d_conv[l], d_a_log[l], d_dt_bias[l], d_norm[l])
        h2d, te, tg = _out_ln_router(heads, w_out[l].astype(MXU_DTYPE), x2d, ln1_g[l], ln1_b[l],
                                     w_router[l], b_router[l])
        x2d = _moe_layer(h2d, te, tg, l, w_gu, b_gu, w_dn, b_dn, ln2_g[l], ln2_b[l])
    return x2d.reshape(bsz, seq, d)
```

```python
import functools
import math

import jax
import jax.numpy as jnp
import numpy as np
from jax import lax
from jax.experimental import pallas as pl
from jax.experimental.pallas import tpu as pltpu

D_MODEL = 2048
DEPTH = 4
A_HEADS, A_NOPE, A_ROPE, A_V, A_KV_RANK = 4, 128, 64, 128, 256
IDX_HEADS, IDX_DIM, INDEX_TOPK = 8, 64, 256
B_HEADS, B_QK, B_V, RET_CHUNK = 4, 64, 128, 128
C_HEADS, C_DH = 4, 64
D_HEADS, D_K, D_V, CONV_K, GDN_CHUNK = 4, 128, 128, 4, 64
N_EXPERTS, TOP_K, D_FF = 32, 4, 512
SWIGLU_ALPHA, SWIGLU_LIMIT = 1.702, 7.0
MOE_BLOCK = 256
ROPE_THETA = 10000.0
ROPE_DIM = 64
Q_BLOCK = 128
DEEPNORM_ALPHA = (2 * DEPTH) ** 0.25
LN_EPS = 1e-5
RMS_EPS = 1e-6
D_MIX = A_HEADS * A_V + B_HEADS * B_V + C_HEADS * 2 * C_DH + D_HEADS * D_V
A_LAT = A_KV_RANK + A_ROPE

IN_WIDTHS = (
    A_HEADS * (A_NOPE + A_ROPE), A_KV_RANK, A_ROPE, IDX_HEADS * IDX_DIM, IDX_DIM, IDX_HEADS,
    B_HEADS * B_QK, B_HEADS * B_QK, B_HEADS * B_V, B_HEADS * B_V,
    C_HEADS * 2 * C_DH, C_HEADS * 2 * C_DH, C_HEADS * 2 * C_DH,
    D_HEADS * (2 * D_K + D_V), D_HEADS * D_V, D_HEADS, D_HEADS,
)
_IN_OFFS = np.concatenate([[0], np.cumsum(IN_WIDTHS)]).tolist()

LANE = 128
SUBLANE = 8
VMEM_LIMIT = 56 * 1024 * 1024

MXU_DTYPE = jnp.bfloat16
_MASKED = -1e30

COL = 512
COL_D_QKV = 0
COL_D_Z = 3
COL_B_V = 4
COL_B_G = 5
COL_B_QK = 6
COL_C_Q = 7
COL_C_K = 8
COL_C_V = 9
COL_A_CKV_ROPE = 10
COL_A_NOPE = 11
COL_A_IQ = 12
NARROW = 256
COL_NARROW = 26
IN_PACKED = (COL_NARROW + 1) * NARROW
NARROW_IW = 0
NARROW_DB = IDX_HEADS
NARROW_DA = IDX_HEADS + D_HEADS


def _pack_w_in(w):
    g = lambda i: w[:, _IN_OFFS[i]:_IN_OFFS[i + 1]]
    a_q = g(0).reshape(w.shape[0], A_HEADS, A_NOPE + A_ROPE)
    nope = a_q[:, :, :A_NOPE].reshape(w.shape[0], A_HEADS * A_NOPE)
    rope = a_q[:, :, A_NOPE:].reshape(w.shape[0], A_HEADS * A_ROPE)
    cols = [g(13), g(14), g(8), g(9), g(6), g(7), g(10), g(11), g(12), g(1), rope, nope, g(3),
            g(2), g(4), g(5), g(15), g(16)]
    used = sum(c.shape[1] for c in cols)
    cols.append(jnp.zeros((w.shape[0], IN_PACKED - used), w.dtype))
    return jnp.concatenate(cols, axis=1).astype(MXU_DTYPE)


def _sigmoid(t):
    return 1.0 / (1.0 + jnp.exp(-t))


def _softplus(t):
    return jnp.maximum(t, 0.0) + jnp.log1p(jnp.exp(-jnp.abs(t)))


def _params(*sem):
    return pltpu.CompilerParams(dimension_semantics=sem, vmem_limit_bytes=VMEM_LIMIT)


def _in_proj_body(x_ref, w_ref, o_ref, xb_ref):
    @pl.when(pl.program_id(1) == 0)
    def _():
        xb_ref[...] = x_ref[...].astype(MXU_DTYPE)

    o_ref[...] = jnp.dot(xb_ref[...], w_ref[...], preferred_element_type=jnp.float32)


def _in_proj(x2d, w_packed, tm=1024, tn=768):
    m, k = x2d.shape
    n = w_packed.shape[1]
    return pl.pallas_call(
        _in_proj_body,
        grid=(m // tm, n // tn),
        in_specs=[pl.BlockSpec((tm, k), lambda i, j: (i, 0)),
                  pl.BlockSpec((k, tn), lambda i, j: (0, j))],
        out_specs=pl.BlockSpec((tm, tn), lambda i, j: (i, j)),
        out_shape=jax.ShapeDtypeStruct((m, n), jnp.float32),
        scratch_shapes=[pltpu.VMEM((tm, k), MXU_DTYPE)],
        compiler_params=_params("arbitrary", "arbitrary"),
        name="in_proj",
    )(x2d, w_packed)


PREP_TILE = 256
KCHUNK = 256


def _prep_body(bqk_ref, cq_ref, ck_ref, cv_ref, ckvrope_ref, nope_ref, iq_ref, nar_ref, cos_ref, sin_ref,
               kvn_ref, wukT_ref,
               qT_ref, iqT_ref, iw_ref, ik_ref, kv_ref, ckvT_ref, qbd_ref, kc_ref, vT_ref, rq_ref, rkT_ref, gT_ref):
    f32 = jnp.float32
    cdt = MXU_DTYPE
    cos, sin = cos_ref[0], sin_ref[0]
    lane = lax.broadcasted_iota(jnp.int32, (PREP_TILE, LANE), 1)
    first_half = (lane % ROPE_DIM) < ROPE_DIM // 2
    eye = (lax.broadcasted_iota(jnp.int32, (LANE, LANE), 0)
           == lax.broadcasted_iota(jnp.int32, (LANE, LANE), 1)).astype(cdt)
    top = lax.broadcasted_iota(jnp.int32, (LANE, LANE), 0) < LANE // 2
    top_wide = lax.broadcasted_iota(jnp.int32, (LANE, PREP_TILE), 0) < LANE // 2

    def rope(t):
        outs = []
        for j in range(t.shape[1] // LANE):
            ts = t[:, j * LANE:(j + 1) * LANE]
            partner = jnp.where(first_half, pltpu.roll(ts, LANE - ROPE_DIM // 2, 1), pltpu.roll(ts, ROPE_DIM // 2, 1))
            outs.append(ts * cos + partner * sin)
        return outs[0] if len(outs) == 1 else jnp.concatenate(outs, axis=1)

    def tr(t):
        return lax.dot_general(eye, t.astype(cdt), (((1,), (1,)), ((), ())), preferred_element_type=f32)

    ckv_rope = ckvrope_ref[...]
    a_ckv = ckv_rope[:, :A_KV_RANK]
    c_kv = a_ckv * lax.rsqrt(jnp.mean(a_ckv * a_ckv, -1, keepdims=True) + RMS_EPS) * kvn_ref[...]
    nar = nar_ref[...]
    kr_ik = rope(nar[:, :LANE])
    kv_ref[0, 0, :, :A_KV_RANK] = c_kv.astype(cdt)
    kv_ref[0, 0, :, A_KV_RANK:] = kr_ik[:, :A_ROPE].astype(cdt)
    ik_ref[0, 0] = kr_ik[:, A_ROPE:].astype(cdt)
    for rb in range(A_KV_RANK // LANE):
        ckvT_ref[0, 0, rb * LANE:(rb + 1) * LANE, :] = tr(c_kv[:, rb * LANE:(rb + 1) * LANE]).astype(cdt)
    q_rope = rope(ckv_rope[:, A_KV_RANK:])
    nope = nope_ref[...]
    iq = rope(iq_ref[...])
    narT = nar[:, LANE:].T
    gT_ref[0] = narT[NARROW_DB:NARROW_DB + 2 * D_HEADS]
    cq = rope(cq_ref[...]) * C_DH ** -0.5
    for j in range(PREP_TILE // Q_BLOCK):
        rows = slice(j * Q_BLOCK, (j + 1) * Q_BLOCK)
        for h in range(A_HEADS):
            q_latT = lax.dot_general(wukT_ref[h], nope[rows, h * A_NOPE:(h + 1) * A_NOPE].astype(cdt),
                                     (((1,), (1,)), ((), ())), preferred_element_type=f32)
            qT_ref[0, j, :A_KV_RANK, h * Q_BLOCK:(h + 1) * Q_BLOCK] = q_latT.astype(cdt)
        for p in range(A_HEADS * A_ROPE // LANE):
            t = tr(q_rope[rows, p * LANE:(p + 1) * LANE])
            qT_ref[0, j, A_KV_RANK:, (2 * p) * Q_BLOCK:(2 * p + 1) * Q_BLOCK] = t[:A_ROPE].astype(cdt)
            qT_ref[0, j, A_KV_RANK:, (2 * p + 1) * Q_BLOCK:(2 * p + 2) * Q_BLOCK] = t[A_ROPE:].astype(cdt)
        for p in range(IDX_HEADS * IDX_DIM // LANE):
            t = tr(iq[rows, p * LANE:(p + 1) * LANE])
            iqT_ref[0, j, :, (2 * p) * Q_BLOCK:(2 * p + 1) * Q_BLOCK] = t[:IDX_DIM].astype(cdt)
            iqT_ref[0, j, :, (2 * p + 1) * Q_BLOCK:(2 * p + 2) * Q_BLOCK] = t[IDX_DIM:].astype(cdt)
        iw_ref[0, j] = narT[NARROW_IW:NARROW_IW + IDX_HEADS, rows] * (IDX_HEADS * IDX_DIM) ** -0.5
        for h in range(C_HEADS):
            t = tr(cq[rows, h * LANE:(h + 1) * LANE])
            qbd_ref[0, j, h, :, :Q_BLOCK] = jnp.where(top, t, 0.0).astype(cdt)
            qbd_ref[0, j, h, :, Q_BLOCK:] = jnp.where(top, 0.0, t).astype(cdt)
    ck = rope(ck_ref[...])
    cv = cv_ref[...]
    for h in range(C_HEADS):
        kc_ref[0, 0, h] = ck[:, h * LANE:(h + 1) * LANE].astype(cdt)
        vT_ref[0, 0, h] = tr(cv[:, h * LANE:(h + 1) * LANE]).astype(cdt)
    bqk = rope(bqk_ref[...])
    rq_ref[0] = bqk[:, :B_HEADS * B_QK].astype(cdt)
    for p in range(B_HEADS * B_QK // LANE):
        t = tr(bqk[:, B_HEADS * B_QK + p * LANE:B_HEADS * B_QK + (p + 1) * LANE]) * B_QK ** -0.5
        rkT_ref[0, 2 * p] = jnp.where(top_wide, t, 0.0).astype(cdt)
        rkT_ref[0, 2 * p + 1] = jnp.where(top_wide, 0.0, t).astype(cdt)


def _mixer_prep(proj, cos, sin, kv_norm, wukT, bsz, seq):
    nt = seq // PREP_TILE
    nq = seq // Q_BLOCK
    cdt = MXU_DTYPE
    col = lambda width, c: pl.BlockSpec((PREP_TILE, width), lambda b, t: (b * nt + t, c))
    table = pl.BlockSpec((1, PREP_TILE, LANE), lambda b, t: (b, t, 0))
    per_q = lambda *tail: pl.BlockSpec((1, PREP_TILE // Q_BLOCK) + tail, lambda b, t: (b, t) + (0,) * len(tail))
    per_c = lambda *tail: pl.BlockSpec((1, 1) + tail, lambda b, t: (b, t) + (0,) * len(tail))
    sds = jax.ShapeDtypeStruct
    outs = [
        (sds((bsz, nq, A_LAT, A_HEADS * Q_BLOCK), cdt), per_q(A_LAT, A_HEADS * Q_BLOCK)),
        (sds((bsz, nq, IDX_DIM, IDX_HEADS * Q_BLOCK), cdt), per_q(IDX_DIM, IDX_HEADS * Q_BLOCK)),
        (sds((bsz, nq, IDX_HEADS, Q_BLOCK), jnp.float32), per_q(IDX_HEADS, Q_BLOCK)),
        (sds((bsz, nt, KCHUNK, IDX_DIM), cdt), per_c(KCHUNK, IDX_DIM)),
        (sds((bsz, nt, KCHUNK, A_LAT), cdt), per_c(KCHUNK, A_LAT)),
        (sds((bsz, nt, A_KV_RANK, KCHUNK), cdt), per_c(A_KV_RANK, KCHUNK)),
        (sds((bsz, nq, C_HEADS, 2 * C_DH, 2 * Q_BLOCK), cdt), per_q(C_HEADS, 2 * C_DH, 2 * Q_BLOCK)),
        (sds((bsz, nt, C_HEADS, KCHUNK, 2 * C_DH), cdt), per_c(C_HEADS, KCHUNK, 2 * C_DH)),
        (sds((bsz, nt, C_HEADS, 2 * C_DH, KCHUNK), cdt), per_c(C_HEADS, 2 * C_DH, KCHUNK)),
        (sds((bsz, seq, B_HEADS * B_QK), cdt), pl.BlockSpec((1, PREP_TILE, B_HEADS * B_QK), lambda b, t: (b, t, 0))),
        (sds((bsz, B_HEADS, LANE, seq), cdt), pl.BlockSpec((1, B_HEADS, LANE, PREP_TILE), lambda b, t: (b, 0, 0, t))),
        (sds((bsz, 2 * D_HEADS, seq), jnp.float32), pl.BlockSpec((1, 2 * D_HEADS, PREP_TILE), lambda b, t: (b, 0, t))),
    ]
    return pl.pallas_call(
        _prep_body,
        grid=(bsz, nt),
        in_specs=[col(COL, COL_B_QK), col(COL, COL_C_Q), col(COL, COL_C_K), col(COL, COL_C_V),
                  col(COL, COL_A_CKV_ROPE), col(COL, COL_A_NOPE), col(COL, COL_A_IQ), col(NARROW, COL_NARROW),
                  table, table,
                  pl.BlockSpec((1, A_KV_RANK), lambda b, t: (0, 0)),
                  pl.BlockSpec((A_HEADS, A_KV_RANK, A_NOPE), lambda b, t: (0, 0, 0))],
        out_specs=[o[1] for o in outs],
        out_shape=[o[0] for o in outs],
        compiler_params=_params("arbitrary", "arbitrary"),
        name="mixer_prep",
    )(proj, proj, proj, proj, proj, proj, proj, proj, cos, sin, kv_norm.reshape(1, A_KV_RANK), wukT)


def _rope_tables(positions):
    half = ROPE_DIM // 2
    inv_freq = ROPE_THETA ** (-jnp.arange(half, dtype=jnp.float32) / half)
    ang = positions.astype(jnp.float32)[:, :, None] * inv_freq
    cos, sin = jnp.cos(ang), jnp.sin(ang)
    reps = LANE // ROPE_DIM
    return (jnp.concatenate([cos, cos] * reps, -1), jnp.concatenate([-sin, sin] * reps, -1))


_SIGN_BIT = np.int32(-2 ** 31)
_LOW31 = np.int32(2 ** 31 - 1)
_KEY_NEG_INF = np.int32(np.array(-np.inf, np.float32).view(np.int32) ^ _LOW31)


def _dsa_body(iqT_ref, iw_ref, qT_ref, ik_ref, kv_ref, ckvT_ref, wuv_ref, o_ref,
              key_ref, acc_ref, thr_ref, need_ref, *, n_keep):
    f32 = jnp.float32
    qi = pl.program_id(1)
    nk = (qi * Q_BLOCK + Q_BLOCK + KCHUNK - 1) // KCHUNK
    iqT = iqT_ref[0, 0]
    iw8 = iw_ref[0, 0]
    iw = jnp.concatenate([iw8[h:h + 1] for h in range(IDX_HEADS)], axis=1)
    qpos = qi * Q_BLOCK + lax.broadcasted_iota(jnp.int32, (KCHUNK, Q_BLOCK), 1)
    krow = lax.broadcasted_iota(jnp.int32, (KCHUNK, Q_BLOCK), 0)

    def index_chunk(c, carry):
        lg = jnp.dot(ik_ref[0, c], iqT, preferred_element_type=f32)
        w = jnp.maximum(lg, 0.0) * iw
        idx = w[:, :Q_BLOCK]
        for h in range(1, IDX_HEADS):
            idx = idx + w[:, h * Q_BLOCK:(h + 1) * Q_BLOCK]
        idx = jnp.where(idx == 0.0, 0.0, idx)
        idx = jnp.where(krow + c * KCHUNK <= qpos, idx, -jnp.inf)
        bits = pltpu.bitcast(idx, jnp.int32)
        key_ref[c] = bits ^ ((bits >> 31) & _LOW31)
        return carry

    lax.fori_loop(0, nk, index_chunk, 0)

    def count(cmp, t):
        def body(c, cnt):
            hit = cmp(key_ref[c], t).astype(jnp.int32)
            return cnt + hit.reshape(KCHUNK // SUBLANE, SUBLANE, Q_BLOCK).sum(0)
        cnt = lax.fori_loop(0, nk, body, jnp.zeros((SUBLANE, Q_BLOCK), jnp.int32))
        return cnt.sum(0, keepdims=True)

    @pl.when(qi * Q_BLOCK + Q_BLOCK <= n_keep)
    def _():
        thr_ref[...] = jnp.full((1, Q_BLOCK), _KEY_NEG_INF, jnp.int32)
        need_ref[...] = jnp.zeros((1, Q_BLOCK), jnp.int32)

    @pl.when(qi * Q_BLOCK + Q_BLOCK > n_keep)
    def _():
        def bit_step(i, prefix):
            cand = prefix | jnp.left_shift(jnp.int32(1), 31 - i)
            cnt = count(lambda k, t: k >= t, cand ^ _SIGN_BIT)
            return jnp.where(cnt >= n_keep, cand, prefix)
        prefix = lax.fori_loop(0, 32, bit_step, jnp.zeros((1, Q_BLOCK), jnp.int32))
        thr = prefix ^ _SIGN_BIT
        thr_ref[...] = thr
        need_ref[...] = n_keep - count(lambda k, t: k > t, thr)

    thr = thr_ref[...]
    need = need_ref[...].astype(f32)
    qT = qT_ref[0, 0]
    scale = (A_NOPE + A_ROPE) ** -0.5
    r_i = lax.broadcasted_iota(jnp.int32, (KCHUNK, KCHUNK), 0)
    c_i = lax.broadcasted_iota(jnp.int32, (KCHUNK, KCHUNK), 1)
    before = (c_i < r_i).astype(jnp.bfloat16)
    acc_ref[...] = jnp.zeros_like(acc_ref)

    def attend_chunk(c, carry):
        m, l, ties_seen = carry
        s = jnp.dot(kv_ref[0, c], qT, preferred_element_type=f32) * scale
        key = key_ref[c]
        tie = key == thr
        tie_f = tie.astype(f32)
        rank = jnp.dot(before, tie_f.astype(jnp.bfloat16), preferred_element_type=f32) + ties_seen
        keep = (key > thr) | (tie & (rank < need))
        bias = jnp.where(keep, 0.0, _MASKED)
        s = s + jnp.concatenate([bias] * A_HEADS, axis=1)
        m_new = jnp.maximum(m, s.max(0, keepdims=True))
        alpha = jnp.exp(m - m_new)
        p = jnp.exp(s - m_new)
        l = alpha * l + p.sum(0, keepdims=True)
        pv = jnp.dot(ckvT_ref[0, c], p.astype(ckvT_ref.dtype), preferred_element_type=f32)
        acc_ref[...] = acc_ref[...] * alpha + pv
        return m_new, l, ties_seen + tie_f.sum(0, keepdims=True)

    lanes = A_HEADS * Q_BLOCK
    init = (jnp.full((1, lanes), _MASKED, f32), jnp.zeros((1, lanes), f32), jnp.zeros((1, Q_BLOCK), f32))
    _, l, _ = lax.fori_loop(0, nk, attend_chunk, init)
    o_latT = acc_ref[...] / l
    for h in range(A_HEADS):
        o_lat = o_latT[:, h * Q_BLOCK:(h + 1) * Q_BLOCK].T.astype(wuv_ref.dtype)
        o_ref[0, :, h * A_V:(h + 1) * A_V] = jnp.dot(o_lat, wuv_ref[h], preferred_element_type=f32)


def _dsa_attention(iqT, iw, qT, ikc, kvc, ckvT, wuv, *, n_keep):
    bsz, nq = qT.shape[:2]
    nc = ikc.shape[1]
    seq = nq * Q_BLOCK
    per_q = lambda b, q: (b, q, 0, 0)
    per_b = lambda b, q: (b, 0, 0, 0)
    return pl.pallas_call(
        functools.partial(_dsa_body, n_keep=n_keep),
        grid=(bsz, nq),
        in_specs=[pl.BlockSpec((1, 1) + iqT.shape[2:], per_q),
                  pl.BlockSpec((1, 1) + iw.shape[2:], per_q),
                  pl.BlockSpec((1, 1) + qT.shape[2:], per_q),
                  pl.BlockSpec((1,) + ikc.shape[1:], per_b),
                  pl.BlockSpec((1,) + kvc.shape[1:], per_b),
                  pl.BlockSpec((1,) + ckvT.shape[1:], per_b),
                  pl.BlockSpec(wuv.shape, lambda b, q: (0, 0, 0))],
        out_specs=pl.BlockSpec((1, Q_BLOCK, A_HEADS * A_V), lambda b, q: (b, q, 0)),
        out_shape=jax.ShapeDtypeStruct((bsz, seq, A_HEADS * A_V), jnp.float32),
        scratch_shapes=[pltpu.VMEM((nc, KCHUNK, Q_BLOCK), jnp.int32),
                        pltpu.VMEM((A_KV_RANK, A_HEADS * Q_BLOCK), jnp.float32),
                        pltpu.VMEM((1, Q_BLOCK), jnp.int32),
                        pltpu.VMEM((1, Q_BLOCK), jnp.int32)],
        compiler_params=_params("arbitrary", "arbitrary"),
        name="dsa_attention",
    )(iqT, iw, qT, ikc, kvc, ckvT, wuv)


RET_TILE = 256


def _retention_body(q_ref, kT_ref, v_ref, g_ref, o_ref, s_ref):
    f32 = jnp.float32
    cdt = MXU_DTYPE
    c = RET_CHUNK

    @pl.when(pl.program_id(1) == 0)
    def _():
        s_ref[...] = jnp.zeros_like(s_ref)

    rel = (lax.broadcasted_iota(jnp.int32, (c, c), 0) - lax.broadcasted_iota(jnp.int32, (c, c), 1)).astype(f32)
    pos_c = lax.broadcasted_iota(jnp.int32, (c, 1), 0).astype(f32)
    pos_r = lax.broadcasted_iota(jnp.int32, (1, c), 1).astype(f32)

    def mm(a, b):
        return jnp.dot(a.astype(cdt), b.astype(cdt), preferred_element_type=f32)

    for h in range(B_HEADS):
        log_gamma = float(np.log(np.float32(1.0) - np.float32(2.0) ** np.float32(-5 - h)))
        intra = jnp.where(rel >= 0, jnp.exp(log_gamma * jnp.maximum(rel, 0.0)), 0.0)
        to_end = jnp.exp(log_gamma * (c - 1 - pos_r))
        from_start = jnp.exp(log_gamma * (pos_c + 1.0))
        chunk_decay = float(np.exp(np.float32(log_gamma) * np.float32(c)))
        lanes = slice(h * B_V, (h + 1) * B_V)
        pair = slice((h // 2) * LANE, (h // 2 + 1) * LANE)
        for ci in range(RET_TILE // c):
            rows = slice(ci * c, (ci + 1) * c)
            q = q_ref[0, rows, pair].astype(f32)
            kT = kT_ref[0, h, :, rows].astype(f32)
            v = v_ref[rows, lanes]
            state = s_ref[h]
            o = mm(mm(q, kT) * intra, v) + mm(q * from_start, state)
            s_ref[h] = state * chunk_decay + mm(kT * to_end, v)
            gate = g_ref[rows, lanes]
            o = o * lax.rsqrt(jnp.mean(o * o, -1, keepdims=True) + RMS_EPS)
            o_ref[0, rows, lanes] = o * (gate * _sigmoid(gate))


def _retention(rq, rkT, proj, bsz, seq):
    nt = seq // RET_TILE
    width = B_HEADS * B_V
    col = lambda c: pl.BlockSpec((RET_TILE, COL), lambda b, t: (b * nt + t, c))
    return pl.pallas_call(
        _retention_body,
        grid=(bsz, nt),
        in_specs=[pl.BlockSpec((1, RET_TILE, B_HEADS * B_QK), lambda b, t: (b, t, 0)),
                  pl.BlockSpec((1, B_HEADS, LANE, RET_TILE), lambda b, t: (b, 0, 0, t)),
                  col(COL_B_V), col(COL_B_G)],
        out_specs=pl.BlockSpec((1, RET_TILE, width), lambda b, t: (b, t, 0)),
        out_shape=jax.ShapeDtypeStruct((bsz, seq, width), jnp.float32),
        scratch_shapes=[pltpu.VMEM((B_HEADS, LANE, B_V), jnp.float32)],
        compiler_params=_params("arbitrary", "arbitrary"),
        name="retention",
    )(rq, rkT, proj, proj)


def _diff_attention_body(lam_ref, qbd_ref, k_ref, vT_ref, gain_ref, o_ref, acc_ref):
    f32 = jnp.float32
    qi = pl.program_id(1)
    nk = (qi * Q_BLOCK + Q_BLOCK + KCHUNK - 1) // KCHUNK
    qpos = qi * Q_BLOCK + lax.broadcasted_iota(jnp.int32, (KCHUNK, Q_BLOCK), 1)
    krow = lax.broadcasted_iota(jnp.int32, (KCHUNK, Q_BLOCK), 0)
    acc_ref[...] = jnp.zeros_like(acc_ref)

    heads = range(C_HEADS)

    def chunk(c, carry):
        ms, ls = carry
        causal = jnp.where(krow + c * KCHUNK <= qpos, 0.0, _MASKED)
        bias = jnp.concatenate([causal, causal], axis=1)
        s = [jnp.dot(k_ref[0, c, h], qbd_ref[0, 0, h], preferred_element_type=f32) + bias for h in heads]
        new_ms = [jnp.maximum(ms[h], s[h].max(0, keepdims=True)) for h in heads]
        alpha = [jnp.exp(ms[h] - new_ms[h]) for h in heads]
        p = [jnp.exp(s[h] - new_ms[h]) for h in heads]
        new_ls = [alpha[h] * ls[h] + p[h].sum(0, keepdims=True) for h in heads]
        pv = [jnp.dot(vT_ref[0, c, h], p[h].astype(vT_ref.dtype), preferred_element_type=f32) for h in heads]
        for h in heads:
            acc_ref[h] = acc_ref[h] * alpha[h] + pv[h]
        return tuple(new_ms), tuple(new_ls)

    lanes = 2 * Q_BLOCK
    init = (tuple(jnp.full((1, lanes), _MASKED, f32) for _ in range(C_HEADS)),
            tuple(jnp.zeros((1, lanes), f32) for _ in range(C_HEADS)))
    _, ls = lax.fori_loop(0, nk, chunk, init)
    lam = lam_ref[0]
    for h in range(C_HEADS):
        o = acc_ref[h] / ls[h]
        o = o[:, :Q_BLOCK] - lam * o[:, Q_BLOCK:]
        o = o * lax.rsqrt(jnp.mean(o * o, 0, keepdims=True) + RMS_EPS)
        o_ref[0, :, h * 2 * C_DH:(h + 1) * 2 * C_DH] = o.T * gain_ref[...]


def _diff_attention(lam, qbd, kc, vT, gain):
    bsz, nq = qbd.shape[:2]
    seq = nq * Q_BLOCK
    width = C_HEADS * 2 * C_DH
    return pl.pallas_call(
        _diff_attention_body,
        grid=(bsz, nq),
        in_specs=[pl.BlockSpec(memory_space=pltpu.SMEM),
                  pl.BlockSpec((1, 1) + qbd.shape[2:], lambda b, q: (b, q, 0, 0, 0)),
                  pl.BlockSpec((1,) + kc.shape[1:], lambda b, q: (b, 0, 0, 0, 0)),
                  pl.BlockSpec((1,) + vT.shape[1:], lambda b, q: (b, 0, 0, 0, 0)),
                  pl.BlockSpec(gain.shape, lambda b, q: (0, 0))],
        out_specs=pl.BlockSpec((1, Q_BLOCK, width), lambda b, q: (b, q, 0)),
        out_shape=jax.ShapeDtypeStruct((bsz, seq, width), jnp.float32),
        scratch_shapes=[pltpu.VMEM((C_HEADS, 2 * C_DH, 2 * Q_BLOCK), jnp.float32)],
        compiler_params=_params("arbitrary", "arbitrary"),
        name="diff_attention",
    )(lam, qbd, kc, vT, gain)


GDN_TILE = 256
GDN_SUB = 16


def _gdn_body(qkv_ref, z_ref, nar_ref, gateT_ref, conv_ref, alog_r_ref, dtb_r_ref, alog_c_ref, dtb_c_ref,
              ng_ref, o_ref, halo_ref, s_ref):
    f32 = jnp.float32
    cdt = MXU_DTYPE
    c = GDN_CHUNK

    @pl.when(pl.program_id(1) == 0)
    def _():
        halo_ref[...] = jnp.zeros_like(halo_ref)
        s_ref[...] = jnp.zeros_like(s_ref)

    x = qkv_ref[...]
    xp = jnp.concatenate([halo_ref[...], x], 0)
    w = conv_ref[...]
    pre = xp[SUBLANE - CONV_K + 1:SUBLANE - CONV_K + 1 + GDN_TILE] * w[0:1]
    for j in range(1, CONV_K):
        off = SUBLANE - CONV_K + 1 + j
        pre = pre + xp[off:off + GDN_TILE] * w[j:j + 1]
    halo_ref[...] = x[GDN_TILE - SUBLANE:]
    qkv = pre * _sigmoid(pre)

    gates = nar_ref[:, LANE:]
    beta_c = _sigmoid(gates[:, NARROW_DB:NARROW_DB + D_HEADS])
    g_c = -jnp.exp(alog_r_ref[...]) * _softplus(gates[:, NARROW_DA:NARROW_DA + D_HEADS] + dtb_r_ref[...])
    g_r = -jnp.exp(alog_c_ref[...]) * _softplus(gateT_ref[0][D_HEADS:] + dtb_c_ref[...])

    row = lax.broadcasted_iota(jnp.int32, (c, c), 0)
    col = lax.broadcasted_iota(jnp.int32, (c, c), 1)
    tril = row >= col
    strict = row > col
    same_sub = (row // GDN_SUB) == (col // GDN_SUB)
    eye = (row == col).astype(f32)
    lower_ones = tril.astype(f32)
    upper_ones = (row <= col).astype(f32)

    def mm(a, b):
        return jnp.dot(a, b, preferred_element_type=f32)

    n_chunks = GDN_TILE // c
    pairs = [(ci, h) for ci in range(n_chunks) for h in range(D_HEADS)]
    rows_of = lambda ci: slice(ci * c, (ci + 1) * c)
    g_cum_c = [mm(lower_ones, g_c[rows_of(ci)]) for ci in range(n_chunks)]
    g_cum_r = [mm(g_r[:, rows_of(ci)], upper_ones) for ci in range(n_chunks)]
    st = []
    for ci, h in pairs:
        rows = rows_of(ci)
        q = qkv[rows, h * D_K:(h + 1) * D_K]
        k = qkv[rows, D_HEADS * D_K + h * D_K:D_HEADS * D_K + (h + 1) * D_K]
        v = qkv[rows, 2 * D_HEADS * D_K + h * D_V:2 * D_HEADS * D_K + (h + 1) * D_V]
        q = q * lax.rsqrt(jnp.sum(q * q, -1, keepdims=True) + RMS_EPS) * D_K ** -0.5
        k = k * lax.rsqrt(jnp.sum(k * k, -1, keepdims=True) + RMS_EPS)
        gc = g_cum_c[ci][:, h:h + 1]
        gr = g_cum_r[ci][h:h + 1, :]
        g_last = gc[c - 1:c]
        gam = jnp.where(tril, jnp.exp(jnp.where(tril, gc - gr, 0.0)), 0.0)
        e_g = jnp.exp(gc)
        beta = beta_c[rows, h:h + 1]
        kT = k.T
        kb = k * beta
        st.append(dict(gam=gam, kT_c=kT.astype(cdt), kb_c=kb.astype(cdt), q_c=q.astype(cdt),
                       qg_c=(q * e_g).astype(cdt), rhs=jnp.concatenate([v * beta, kb * e_g], 1),
                       kdT_c=(kT * jnp.exp(g_last - gr)).astype(cdt), decay=jnp.exp(g_last)))
    for p in st:
        p["low"] = jnp.where(strict, mm(p["kb_c"], p["kT_c"]) * p["gam"], 0.0)
        p["attn_c"] = jnp.where(tril, mm(p["q_c"], p["kT_c"]) * p["gam"], 0.0).astype(cdt)
        p["low_d"] = jnp.where(same_sub, p["low"], 0.0)
        p["xk"] = -p["low_d"]
        p["td"] = eye + p["xk"]
    for _ in range(3):
        for p in st:
            p["xk"] = mm(p["xk"], p["xk"])
        for p in st:
            p["td"] = p["td"] + mm(p["td"], p["xk"])
    for p in st:
        p["m1"] = mm(p["td"], p["low"] - p["low_d"])
        p["sol"] = mm(p["td"], p["rhs"])
    for p in st:
        p["m2"] = mm(p["m1"], p["m1"])
    for p in st:
        p["sol"] = p["sol"] + mm(p["m2"], p["sol"])
    for p in st:
        p["sol"] = p["sol"] - mm(p["m1"], p["sol"])
    for ci in range(n_chunks):
        rows = rows_of(ci)
        cur = [st[ci * D_HEADS + h] for h in range(D_HEADS)]
        states = [s_ref[h] for h in range(D_HEADS)]
        states_c = [s.astype(cdt) for s in states]
        v_new_c = [(p["sol"][:, :D_V] - mm(p["sol"][:, D_V:].astype(cdt), states_c[h])).astype(cdt)
                   for h, p in enumerate(cur)]
        for h, p in enumerate(cur):
            s_ref[h] = states[h] * p["decay"] + mm(p["kdT_c"], v_new_c[h])
        for h, p in enumerate(cur):
            lanes = slice(h * D_V, (h + 1) * D_V)
            o = mm(p["qg_c"], states_c[h]) + mm(p["attn_c"], v_new_c[h])
            zz = z_ref[rows, lanes]
            o = o * lax.rsqrt(jnp.mean(o * o, -1, keepdims=True) + RMS_EPS) * ng_ref[...]
            o_ref[0, rows, lanes] = o * (zz * _sigmoid(zz))


def _gated_deltanet(proj, gatesT, conv_w, a_log, dt_bias, norm_g, bsz, seq):
    nt = seq // GDN_TILE
    width = D_HEADS * (2 * D_K + D_V)
    whole = lambda r, cc: pl.BlockSpec((r, cc), lambda b, t: (0, 0))
    return pl.pallas_call(
        _gdn_body,
        grid=(bsz, nt),
        in_specs=[pl.BlockSpec((GDN_TILE, width), lambda b, t: (b * nt + t, COL_D_QKV)),
                  pl.BlockSpec((GDN_TILE, COL), lambda b, t: (b * nt + t, COL_D_Z)),
                  pl.BlockSpec((GDN_TILE, NARROW), lambda b, t: (b * nt + t, COL_NARROW)),
                  pl.BlockSpec((1, 2 * D_HEADS, GDN_TILE), lambda b, t: (b, 0, t)),
                  whole(CONV_K, width), whole(1, D_HEADS), whole(1, D_HEADS), whole(D_HEADS, 1), whole(D_HEADS, 1),
                  whole(1, D_V)],
        out_specs=pl.BlockSpec((1, GDN_TILE, D_HEADS * D_V), lambda b, t: (b, t, 0)),
        out_shape=jax.ShapeDtypeStruct((bsz, seq, D_HEADS * D_V), jnp.float32),
        scratch_shapes=[pltpu.VMEM((SUBLANE, width), jnp.float32),
                        pltpu.VMEM((D_HEADS, D_K, D_V), jnp.float32)],
        compiler_params=_params("arbitrary", "arbitrary"),
        name="gated_deltanet",
    )(proj, proj, proj, gatesT, conv_w, a_log.reshape(1, D_HEADS), dt_bias.reshape(1, D_HEADS),
      a_log.reshape(D_HEADS, 1), dt_bias.reshape(D_HEADS, 1), norm_g.reshape(1, D_V))


def _ln_rows(t, g, b):
    mu = jnp.mean(t, -1, keepdims=True)
    d = t - mu
    var = jnp.mean(d * d, -1, keepdims=True)
    return d * lax.rsqrt(var + LN_EPS) * g + b


_HIGH16 = np.int32(-(2 ** 16))


def _pack_bf16_pairs(t):
    n = t.shape[1] // 2
    lo = pltpu.bitcast(t[:, :n].astype(jnp.bfloat16).astype(jnp.float32), jnp.int32)
    hi = pltpu.bitcast(t[:, n:].astype(jnp.bfloat16).astype(jnp.float32), jnp.int32)
    return lax.shift_right_logical(lo, 16) | (hi & _HIGH16)


def _unpack_bf16_pairs(w):
    lo = pltpu.bitcast(w << 16, jnp.float32)
    hi = pltpu.bitcast(w & _HIGH16, jnp.float32)
    return jnp.concatenate([lo, hi], axis=1).astype(jnp.bfloat16)


def _out_ln_router_body(ha_ref, hb_ref, hc_ref, hd_ref, w_ref, x_ref, g_ref, b_ref, wr_ref, br_ref,
                        h_ref, hp_ref, te_ref, tg_ref):
    f32 = jnp.float32
    acc = DEEPNORM_ALPHA * x_ref[...]
    row0 = 0
    for part in (ha_ref, hb_ref, hc_ref, hd_ref):
        width = part.shape[-1]
        acc = acc + jnp.dot(part[...].astype(MXU_DTYPE), w_ref[row0:row0 + width, :], preferred_element_type=f32)
        row0 += width
    h = _ln_rows(acc, g_ref[...], b_ref[...])
    h_ref[...] = h
    hp_ref[...] = _pack_bf16_pairs(h)
    logits = jnp.dot(h.astype(MXU_DTYPE), wr_ref[...], preferred_element_type=f32) + br_ref[...]
    lane = lax.broadcasted_iota(jnp.int32, logits.shape, 1)
    vals, te = [], jnp.zeros(logits.shape, jnp.int32)
    for k in range(TOP_K):
        m = logits.max(-1, keepdims=True)
        e = jnp.min(jnp.where(logits == m, lane, LANE), -1, keepdims=True)
        vals.append(m)
        te = jnp.where(lane == k, e, te)
        logits = jnp.where(lane == e, -jnp.inf, logits)
    ex = [jnp.exp(v - vals[0]) for v in vals]
    den = ex[0] + ex[1] + ex[2] + ex[3]
    tg = jnp.zeros(logits.shape, f32)
    for k in range(TOP_K):
        tg = jnp.where(lane == k, ex[k] / den, tg)
    te_ref[...] = te
    tg_ref[...] = tg


def _out_ln_router(head_parts, w_out_c, x2d, g, b, w_router, b_router, tm=512):
    m = x2d.shape[0]
    k, n = w_out_c.shape
    wr = jnp.zeros((n, LANE), MXU_DTYPE).at[:, :N_EXPERTS].set(w_router.astype(MXU_DTYPE))
    br = jnp.full((1, LANE), -jnp.inf, jnp.float32).at[0, :N_EXPERTS].set(b_router)
    row_blk = lambda c: pl.BlockSpec((tm, c), lambda i: (i, 0))
    whole = lambda r, c: pl.BlockSpec((r, c), lambda i: (0, 0))
    return pl.pallas_call(
        _out_ln_router_body,
        grid=(m // tm,),
        in_specs=[row_blk(p.shape[1]) for p in head_parts]
        + [whole(k, n), row_blk(n), whole(1, n), whole(1, n), whole(n, LANE), whole(1, LANE)],
        out_specs=[row_blk(n), row_blk(n // 2), row_blk(LANE), row_blk(LANE)],
        out_shape=[jax.ShapeDtypeStruct((m, n), jnp.float32),
                   jax.ShapeDtypeStruct((m, n // 2), jnp.int32),
                   jax.ShapeDtypeStruct((m, LANE), jnp.int32),
                   jax.ShapeDtypeStruct((m, LANE), jnp.float32)],
        compiler_params=_params("arbitrary"),
        name="out_proj_ln_router",
    )(*head_parts, w_out_c, x2d, g.reshape(1, n), b.reshape(1, n), wr, br)


def _clamped_swiglu(hg):
    glu, lin = hg[..., :D_FF], hg[..., D_FF:]
    glu = jnp.minimum(glu, SWIGLU_LIMIT)
    lin = jnp.clip(lin, -SWIGLU_LIMIT, SWIGLU_LIMIT)
    return glu * _sigmoid(SWIGLU_ALPHA * glu) * (lin + 1.0)


MOE_RANK_TILE = 512


def _moe_rank_body(te_ref, rank_ref, counts_ref, seen_ref):
    f32 = jnp.float32
    tm = MOE_RANK_TILE

    @pl.when(pl.program_id(0) == 0)
    def _():
        seen_ref[...] = jnp.zeros_like(seen_ref)

    te = te_ref[...]
    lane = lax.broadcasted_iota(jnp.int32, (tm, LANE), 1)
    picks = [lane == te[:, k:k + 1] for k in range(TOP_K)]
    member = picks[0].astype(f32)
    for k in range(1, TOP_K):
        member = member + picks[k].astype(f32)
    earlier = (lax.broadcasted_iota(jnp.int32, (tm, tm), 1) < lax.broadcasted_iota(jnp.int32, (tm, tm), 0))
    prefix = jnp.dot(earlier.astype(jnp.bfloat16), member.astype(jnp.bfloat16), preferred_element_type=f32)
    prefix = prefix + seen_ref[...]
    rank = jnp.zeros((tm, LANE), jnp.int32)
    for k in range(TOP_K):
        r_k = jnp.sum(jnp.where(picks[k], prefix, 0.0), -1, keepdims=True)
        rank = jnp.where(lane == k, r_k.astype(jnp.int32), rank)
    rank_ref[...] = rank
    seen = seen_ref[...] + member.sum(0, keepdims=True)
    seen_ref[...] = seen
    counts_ref[...] = seen.astype(jnp.int32)


def _moe_rank(te):
    n_tok = te.shape[0]
    tm = MOE_RANK_TILE
    return pl.pallas_call(
        _moe_rank_body,
        grid=(n_tok // tm,),
        in_specs=[pl.BlockSpec((tm, LANE), lambda i: (i, 0))],
        out_specs=[pl.BlockSpec((tm, LANE), lambda i: (i, 0)), pl.BlockSpec((1, LANE), lambda i: (0, 0))],
        out_shape=[jax.ShapeDtypeStruct((n_tok, LANE), jnp.int32), jax.ShapeDtypeStruct((1, LANE), jnp.int32)],
        scratch_shapes=[pltpu.VMEM((1, LANE), jnp.float32)],
        compiler_params=_params("arbitrary"),
        name="moe_rank",
    )(te)


def _moe_plan(te):
    top_e = te[:, :TOP_K]
    n_tok = top_e.shape[0]
    n_blocks = -(-n_tok * TOP_K // MOE_BLOCK) + N_EXPERTS
    rank, counts = _moe_rank(te)
    counts = counts[0, :N_EXPERTS]
    padded = (counts + MOE_BLOCK - 1) // MOE_BLOCK * MOE_BLOCK
    pad_end = jnp.cumsum(padded)
    one_hot = top_e[:, :, None] == jnp.arange(N_EXPERTS, dtype=jnp.int32)
    dest = rank[:, :TOP_K] + jnp.sum(jnp.where(one_hot, (pad_end - padded)[None, None, :], 0), -1)
    tok = jnp.broadcast_to(jnp.arange(n_tok, dtype=jnp.int32)[:, None], dest.shape)
    row_tok = jnp.zeros((n_blocks * MOE_BLOCK,), jnp.int32).at[dest.reshape(-1)].set(tok.reshape(-1))
    block_start = jnp.arange(n_blocks, dtype=jnp.int32) * MOE_BLOCK
    block_e = jnp.minimum(jnp.sum(pad_end[None, :] <= block_start[:, None], axis=1), N_EXPERTS - 1).astype(jnp.int32)
    n_used = (pad_end[-1:] // MOE_BLOCK).astype(jnp.int32)
    return dest, row_tok, block_e, n_used


def _row_gather_start(src_hbm, rows_smem, n_rows, dst, sem):
    def body(r, carry):
        pltpu.make_async_copy(src_hbm.at[pl.ds(rows_smem[0, 0, r], 1)], dst.at[pl.ds(r, 1)], sem).start()
        return carry
    lax.fori_loop(0, n_rows, body, 0, unroll=8)


def _row_gather_wait(src_hbm, n_rows, dst, sem):
    pltpu.make_async_copy(src_hbm.at[pl.ds(0, n_rows)], dst, sem).wait()


def _moe_expert_body(be_ref, nused_ref, tok_ref, tok_next_ref, h_hbm, wgu_ref, bgu_ref, wdn_ref, bdn_ref,
                     y_ref, buf_even, buf_odd, sem, wgu_c, wdn_c):
    f32 = jnp.float32
    i = pl.program_id(0)
    n_used = nused_ref[0]
    bufs = (buf_even, buf_odd)

    @pl.when(i == 0)
    def _():
        _row_gather_start(h_hbm, tok_ref, MOE_BLOCK, buf_even, sem.at[0])

    @pl.when(jnp.logical_or(i == 0, be_ref[i] != be_ref[jnp.maximum(i - 1, 0)]))
    def _():
        wgu_c[...] = wgu_ref[0, 0].astype(MXU_DTYPE)
        wdn_c[...] = wdn_ref[0, 0].astype(MXU_DTYPE)

    for parity in (0, 1):
        cur, nxt = bufs[parity], bufs[1 - parity]

        @pl.when(jnp.logical_and(i < n_used, i % 2 == parity))
        def _():
            _row_gather_wait(h_hbm, MOE_BLOCK, cur, sem.at[parity])
            for r in range(MOE_BLOCK):
                pltpu.make_async_copy(h_hbm.at[pl.ds(tok_next_ref[0, 0, r], 1)], nxt.at[pl.ds(r, 1)],
                                      sem.at[1 - parity]).start()
            xb = _unpack_bf16_pairs(cur[...]).astype(MXU_DTYPE)
            hg = jnp.dot(xb, wgu_c[...], preferred_element_type=f32) + bgu_ref[0, 0]
            act = _clamped_swiglu(hg)
            y_ref[...] = jnp.dot(act.astype(MXU_DTYPE), wdn_c[...], preferred_element_type=f32) + bdn_ref[0, 0]

        @pl.when(jnp.logical_and(i == n_used, i % 2 == parity))
        def _():
            _row_gather_wait(h_hbm, MOE_BLOCK, cur, sem.at[parity])

    @pl.when(i >= n_used)
    def _():
        y_ref[...] = jnp.zeros_like(y_ref)


def _moe_experts(h_packed, row_tok, block_e, n_used, layer, w_gu, b_gu, w_dn, b_dn):
    n_tok = h_packed.shape[0]
    d = 2 * h_packed.shape[1]
    n_blocks = block_e.shape[0]
    n_layers = w_gu.shape[0]
    assert n_blocks * MOE_BLOCK > n_tok * TOP_K + N_EXPERTS * (MOE_BLOCK - 1)
    tok3 = row_tok.reshape(n_blocks, 1, MOE_BLOCK)
    smem_blk = lambda imap: pl.BlockSpec((1, 1, MOE_BLOCK), imap, memory_space=pltpu.SMEM)
    by_expert = lambda shape: pl.BlockSpec((1, 1) + shape, lambda i, be, nu: (layer, be[i], 0, 0))
    return pl.pallas_call(
        _moe_expert_body,
        grid_spec=pltpu.PrefetchScalarGridSpec(
            num_scalar_prefetch=2,
            grid=(n_blocks,),
            in_specs=[smem_blk(lambda i, be, nu: (i, 0, 0)),
                      smem_blk(lambda i, be, nu: (jnp.minimum(i + 1, n_blocks - 1), 0, 0)),
                      pl.BlockSpec(memory_space=pl.ANY),
                      by_expert((d, 2 * D_FF)), by_expert((1, 2 * D_FF)),
                      by_expert((D_FF, d)), by_expert((1, d))],
            out_specs=pl.BlockSpec((MOE_BLOCK, d), lambda i, be, nu: (i, 0)),
            scratch_shapes=[pltpu.VMEM((MOE_BLOCK, d // 2), jnp.int32), pltpu.VMEM((MOE_BLOCK, d // 2), jnp.int32),
                            pltpu.SemaphoreType.DMA((2,)),
                            pltpu.VMEM((d, 2 * D_FF), MXU_DTYPE), pltpu.VMEM((D_FF, d), MXU_DTYPE)]),
        out_shape=jax.ShapeDtypeStruct((n_blocks * MOE_BLOCK, d), jnp.float32),
        compiler_params=_params("arbitrary"),
        name="moe_experts",
    )(block_e, n_used, tok3, tok3, h_packed, w_gu, b_gu.reshape(n_layers, N_EXPERTS, 1, 2 * D_FF),
      w_dn, b_dn.reshape(n_layers, N_EXPERTS, 1, d))


MOE_COMBINE_ROWS = 128


def _moe_combine_ln_body(dest_ref, dest_next_ref, h_ref, tg_ref, g_ref, b_ref, y_hbm, o_ref,
                         buf_even, buf_odd, sem):
    tm = MOE_COMBINE_ROWS
    i = pl.program_id(0)
    n = pl.num_programs(0)
    bufs = (buf_even, buf_odd)

    def start(rows_smem, dst, s):
        for k in range(TOP_K):
            for r in range(tm):
                pltpu.make_async_copy(y_hbm.at[pl.ds(rows_smem[0, 0, k * tm + r], 1)],
                                      dst.at[k, pl.ds(r, 1)], sem.at[s]).start()

    def drain(dst, s):
        for k in range(TOP_K):
            _row_gather_wait(y_hbm, tm, dst.at[k], sem.at[s])

    @pl.when(i == 0)
    def _():
        start(dest_ref, buf_even, 0)

    for parity in (0, 1):
        cur, nxt = bufs[parity], bufs[1 - parity]

        @pl.when(i % 2 == parity)
        def _():
            drain(cur, parity)
            start(dest_next_ref, nxt, 1 - parity)
            tg = tg_ref[...]
            y = tg[:, 0:1] * cur[0]
            for k in range(1, TOP_K):
                y = y + tg[:, k:k + 1] * cur[k]
            o_ref[...] = _ln_rows(DEEPNORM_ALPHA * h_ref[...] + y, g_ref[...], b_ref[...])

        @pl.when(jnp.logical_and(i == n - 1, i % 2 == parity))
        def _():
            drain(nxt, 1 - parity)


def _moe_combine_ln(h2d, y_rows, dest, tg, g, b):
    n_tok, d = h2d.shape
    tm = MOE_COMBINE_ROWS
    nt = n_tok // tm
    dest3 = dest.reshape(nt, tm, TOP_K).transpose(0, 2, 1).reshape(nt, 1, TOP_K * tm)
    smem_blk = lambda imap: pl.BlockSpec((1, 1, TOP_K * tm), imap, memory_space=pltpu.SMEM)
    return pl.pallas_call(
        _moe_combine_ln_body,
        grid=(nt,),
        in_specs=[smem_blk(lambda i: (i, 0, 0)),
                  smem_blk(lambda i: (jnp.minimum(i + 1, nt - 1), 0, 0)),
                  pl.BlockSpec((tm, d), lambda i: (i, 0)),
                  pl.BlockSpec((tm, LANE), lambda i: (i, 0)),
                  pl.BlockSpec((1, d), lambda i: (0, 0)),
                  pl.BlockSpec((1, d), lambda i: (0, 0)),
                  pl.BlockSpec(memory_space=pl.ANY)],
        out_specs=pl.BlockSpec((tm, d), lambda i: (i, 0)),
        out_shape=jax.ShapeDtypeStruct((n_tok, d), jnp.float32),
        scratch_shapes=[pltpu.VMEM((TOP_K, tm, d), jnp.float32), pltpu.VMEM((TOP_K, tm, d), jnp.float32),
                        pltpu.SemaphoreType.DMA((2,))],
        compiler_params=_params("arbitrary"),
        name="moe_combine_ln",
    )(dest3, dest3, h2d, tg, g.reshape(1, d), b.reshape(1, d), y_rows)


def _moe_layer(h2d, h_packed, te, tg, layer, w_gu, b_gu, w_dn, b_dn, g, b):
    dest, row_tok, block_e, n_used = _moe_plan(te)
    y_rows = _moe_experts(h_packed, row_tok, block_e, n_used, layer, w_gu, b_gu, w_dn, b_dn)
    return _moe_combine_ln(h2d, y_rows, dest, tg, g, b)


def _mixers(proj, cos, sin, bsz, seq, lambda_init, a_kv_norm, a_w_uk, a_w_uv, c_lambda, c_subln,
            d_conv, d_a_log, d_dt_bias, d_norm):
    cdt = MXU_DTYPE
    (qT, iqT, iw, ikc, kvc, ckvT, qbd, kc, vT, rq, rkT, gatesT) = _mixer_prep(
        proj, cos, sin, a_kv_norm, a_w_uk.transpose(1, 0, 2).astype(cdt), bsz, seq)
    head_a = _dsa_attention(iqT, iw, qT, ikc, kvc, ckvT, a_w_uv.transpose(1, 0, 2).astype(cdt),
                            n_keep=min(INDEX_TOPK, seq // 4))
    head_b = _retention(rq, rkT, proj, bsz, seq)
    lam_full = (jnp.exp(jnp.sum(c_lambda[0] * c_lambda[1])) - jnp.exp(jnp.sum(c_lambda[2] * c_lambda[3]))
                + lambda_init)
    gain = (c_subln * (1.0 - lambda_init)).reshape(1, 2 * C_DH)
    head_c = _diff_attention(lam_full.reshape(1), qbd, kc, vT, gain)
    head_d = _gated_deltanet(proj, gatesT, d_conv, d_a_log, d_dt_bias, d_norm, bsz, seq)
    return [t.reshape(bsz * seq, t.shape[-1]) for t in (head_a, head_b, head_c, head_d)]


def kernel(x, positions, w_in, w_out, a_kv_norm, a_w_uk, a_w_uv, c_lambda, c_subln, d_conv, d_a_log, d_dt_bias, d_norm, ln1_g, ln1_b, w_router, b_router, w_gu, b_gu, w_dn, b_dn, ln2_g, ln2_b):
    bsz, seq, d = x.shape
    x2d = x.reshape(bsz * seq, d)
    cos, sin = _rope_tables(positions)
    for l in range(DEPTH):
        proj = _in_proj(x2d, _pack_w_in(w_in[l]))
        lambda_init = 0.8 - 0.6 * math.exp(-0.3 * l)
        heads = _mixers(proj, cos, sin, bsz, seq, lambda_init, a_kv_norm[l], a_w_uk[l], a_w_uv[l], c_lambda[l],
                        c_subln[l], d_conv[l], d_a_log[l], d_dt_bias[l], d_norm[l])
        h2d, h_packed, te, tg = _out_ln_router(heads, w_out[l].astype(MXU_DTYPE), x2d, ln1_g[l], ln1_b[l],
                                     w_router[l], b_router[l])
        x2d = _moe_layer(h2d, h_packed, te, tg, l, w_gu, b_gu, w_dn, b_dn, ln2_g[l], ln2_b[l])
    return x2d.reshape(bsz, seq, d)
```

```python
import functools
import math

import jax
import jax.numpy as jnp
import numpy as np
from jax import lax
from jax.experimental import pallas as pl
from jax.experimental.pallas import tpu as pltpu

D_MODEL = 2048
DEPTH = 4
A_HEADS, A_NOPE, A_ROPE, A_V, A_KV_RANK = 4, 128, 64, 128, 256
IDX_HEADS, IDX_DIM, INDEX_TOPK = 8, 64, 256
B_HEADS, B_QK, B_V, RET_CHUNK = 4, 64, 128, 128
C_HEADS, C_DH = 4, 64
D_HEADS, D_K, D_V, CONV_K, GDN_CHUNK = 4, 128, 128, 4, 64
N_EXPERTS, TOP_K, D_FF = 32, 4, 512
SWIGLU_ALPHA, SWIGLU_LIMIT = 1.702, 7.0
MOE_BLOCK = 256
ROPE_THETA = 10000.0
ROPE_DIM = 64
Q_BLOCK = 128
DEEPNORM_ALPHA = (2 * DEPTH) ** 0.25
LN_EPS = 1e-5
RMS_EPS = 1e-6
D_MIX = A_HEADS * A_V + B_HEADS * B_V + C_HEADS * 2 * C_DH + D_HEADS * D_V
A_LAT = A_KV_RANK + A_ROPE

IN_WIDTHS = (
    A_HEADS * (A_NOPE + A_ROPE), A_KV_RANK, A_ROPE, IDX_HEADS * IDX_DIM, IDX_DIM, IDX_HEADS,
    B_HEADS * B_QK, B_HEADS * B_QK, B_HEADS * B_V, B_HEADS * B_V,
    C_HEADS * 2 * C_DH, C_HEADS * 2 * C_DH, C_HEADS * 2 * C_DH,
    D_HEADS * (2 * D_K + D_V), D_HEADS * D_V, D_HEADS, D_HEADS,
)
_IN_OFFS = np.concatenate([[0], np.cumsum(IN_WIDTHS)]).tolist()

LANE = 128
SUBLANE = 8
VMEM_LIMIT = 56 * 1024 * 1024

MXU_DTYPE = jnp.bfloat16
_MASKED = -1e30

COL = 512
COL_D_QKV = 0
COL_D_Z = 3
COL_B_V = 4
COL_B_G = 5
COL_B_QK = 6
COL_C_Q = 7
COL_C_K = 8
COL_C_V = 9
COL_A_CKV_ROPE = 10
COL_A_NOPE = 11
COL_A_IQ = 12
NARROW = 256
COL_NARROW = 26
IN_PACKED = (COL_NARROW + 1) * NARROW
NARROW_IW = 0
NARROW_DB = IDX_HEADS
NARROW_DA = IDX_HEADS + D_HEADS


def _pack_w_in(w):
    g = lambda i: w[:, _IN_OFFS[i]:_IN_OFFS[i + 1]]
    a_q = g(0).reshape(w.shape[0], A_HEADS, A_NOPE + A_ROPE)
    nope = a_q[:, :, :A_NOPE].reshape(w.shape[0], A_HEADS * A_NOPE)
    rope = a_q[:, :, A_NOPE:].reshape(w.shape[0], A_HEADS * A_ROPE)
    cols = [g(13), g(14), g(8), g(9), g(6), g(7), g(10), g(11), g(12), g(1), rope, nope, g(3),
            g(2), g(4), g(5), g(15), g(16)]
    used = sum(c.shape[1] for c in cols)
    cols.append(jnp.zeros((w.shape[0], IN_PACKED - used), w.dtype))
    return jnp.concatenate(cols, axis=1).astype(MXU_DTYPE)


def _sigmoid(t):
    return 1.0 / (1.0 + jnp.exp(-t))


def _softplus(t):
    return jnp.maximum(t, 0.0) + jnp.log1p(jnp.exp(-jnp.abs(t)))


def _params(*sem):
    return pltpu.CompilerParams(dimension_semantics=sem, vmem_limit_bytes=VMEM_LIMIT)


def _in_proj_body(x_ref, w_ref, o_ref, xb_ref):
    @pl.when(pl.program_id(1) == 0)
    def _():
        xb_ref[...] = x_ref[...].astype(MXU_DTYPE)

    o_ref[...] = jnp.dot(xb_ref[...], w_ref[...], preferred_element_type=jnp.float32)


def _in_proj(x2d, w_packed, tm=1024, tn=768):
    m, k = x2d.shape
    n = w_packed.shape[1]
    return pl.pallas_call(
        _in_proj_body,
        grid=(m // tm, n // tn),
        in_specs=[pl.BlockSpec((tm, k), lambda i, j: (i, 0)),
                  pl.BlockSpec((k, tn), lambda i, j: (0, j))],
        out_specs=pl.BlockSpec((tm, tn), lambda i, j: (i, j)),
        out_shape=jax.ShapeDtypeStruct((m, n), jnp.float32),
        scratch_shapes=[pltpu.VMEM((tm, k), MXU_DTYPE)],
        compiler_params=_params("arbitrary", "arbitrary"),
        name="in_proj",
    )(x2d, w_packed)


PREP_TILE = 256
KCHUNK = 256


def _prep_body(bqk_ref, cq_ref, ck_ref, cv_ref, ckvrope_ref, nope_ref, iq_ref, nar_ref, cos_ref, sin_ref,
               kvn_ref, wukT_ref,
               qT_ref, iqT_ref, iw_ref, ik_ref, kv_ref, ckvT_ref, qbd_ref, kc_ref, vT_ref, rq_ref, rkT_ref, gT_ref):
    f32 = jnp.float32
    cdt = MXU_DTYPE
    cos, sin = cos_ref[0], sin_ref[0]
    lane = lax.broadcasted_iota(jnp.int32, (PREP_TILE, LANE), 1)
    first_half = (lane % ROPE_DIM) < ROPE_DIM // 2
    eye = (lax.broadcasted_iota(jnp.int32, (LANE, LANE), 0)
           == lax.broadcasted_iota(jnp.int32, (LANE, LANE), 1)).astype(cdt)
    top = lax.broadcasted_iota(jnp.int32, (LANE, LANE), 0) < LANE // 2
    top_wide = lax.broadcasted_iota(jnp.int32, (LANE, PREP_TILE), 0) < LANE // 2

    def rope(t):
        outs = []
        for j in range(t.shape[1] // LANE):
            ts = t[:, j * LANE:(j + 1) * LANE]
            partner = jnp.where(first_half, pltpu.roll(ts, LANE - ROPE_DIM // 2, 1), pltpu.roll(ts, ROPE_DIM // 2, 1))
            outs.append(ts * cos + partner * sin)
        return outs[0] if len(outs) == 1 else jnp.concatenate(outs, axis=1)

    def tr(t):
        return lax.dot_general(eye, t.astype(cdt), (((1,), (1,)), ((), ())), preferred_element_type=f32)

    ckv_rope = ckvrope_ref[...]
    a_ckv = ckv_rope[:, :A_KV_RANK]
    c_kv = a_ckv * lax.rsqrt(jnp.mean(a_ckv * a_ckv, -1, keepdims=True) + RMS_EPS) * kvn_ref[...]
    nar = nar_ref[...]
    kr_ik = rope(nar[:, :LANE])
    kv_ref[0, 0, :, :A_KV_RANK] = c_kv.astype(cdt)
    kv_ref[0, 0, :, A_KV_RANK:] = kr_ik[:, :A_ROPE].astype(cdt)
    ik_ref[0, 0] = kr_ik[:, A_ROPE:].astype(cdt)
    for rb in range(A_KV_RANK // LANE):
        ckvT_ref[0, 0, rb * LANE:(rb + 1) * LANE, :] = tr(c_kv[:, rb * LANE:(rb + 1) * LANE]).astype(cdt)
    q_rope = rope(ckv_rope[:, A_KV_RANK:])
    nope = nope_ref[...]
    iq = rope(iq_ref[...])
    narT = nar[:, LANE:].T
    gT_ref[0] = narT[NARROW_DB:NARROW_DB + 2 * D_HEADS]
    cq = rope(cq_ref[...]) * C_DH ** -0.5
    for j in range(PREP_TILE // Q_BLOCK):
        rows = slice(j * Q_BLOCK, (j + 1) * Q_BLOCK)
        for h in range(A_HEADS):
            q_latT = lax.dot_general(wukT_ref[h], nope[rows, h * A_NOPE:(h + 1) * A_NOPE].astype(cdt),
                                     (((1,), (1,)), ((), ())), preferred_element_type=f32)
            qT_ref[0, j, :A_KV_RANK, h * Q_BLOCK:(h + 1) * Q_BLOCK] = q_latT.astype(cdt)
        for p in range(A_HEADS * A_ROPE // LANE):
            t = tr(q_rope[rows, p * LANE:(p + 1) * LANE])
            qT_ref[0, j, A_KV_RANK:, (2 * p) * Q_BLOCK:(2 * p + 1) * Q_BLOCK] = t[:A_ROPE].astype(cdt)
            qT_ref[0, j, A_KV_RANK:, (2 * p + 1) * Q_BLOCK:(2 * p + 2) * Q_BLOCK] = t[A_ROPE:].astype(cdt)
        for p in range(IDX_HEADS * IDX_DIM // LANE):
            t = tr(iq[rows, p * LANE:(p + 1) * LANE])
            iqT_ref[0, j, :, (2 * p) * Q_BLOCK:(2 * p + 1) * Q_BLOCK] = t[:IDX_DIM].astype(cdt)
            iqT_ref[0, j, :, (2 * p + 1) * Q_BLOCK:(2 * p + 2) * Q_BLOCK] = t[IDX_DIM:].astype(cdt)
        iw_ref[0, j] = narT[NARROW_IW:NARROW_IW + IDX_HEADS, rows] * (IDX_HEADS * IDX_DIM) ** -0.5
        for h in range(C_HEADS):
            t = tr(cq[rows, h * LANE:(h + 1) * LANE])
            qbd_ref[0, j, h, :, :Q_BLOCK] = jnp.where(top, t, 0.0).astype(cdt)
            qbd_ref[0, j, h, :, Q_BLOCK:] = jnp.where(top, 0.0, t).astype(cdt)
    ck = rope(ck_ref[...])
    cv = cv_ref[...]
    for h in range(C_HEADS):
        kc_ref[0, 0, h] = ck[:, h * LANE:(h + 1) * LANE].astype(cdt)
        vT_ref[0, 0, h] = tr(cv[:, h * LANE:(h + 1) * LANE]).astype(cdt)
    bqk = rope(bqk_ref[...])
    rq_ref[0] = bqk[:, :B_HEADS * B_QK].astype(cdt)
    for p in range(B_HEADS * B_QK // LANE):
        t = tr(bqk[:, B_HEADS * B_QK + p * LANE:B_HEADS * B_QK + (p + 1) * LANE]) * B_QK ** -0.5
        rkT_ref[0, 2 * p] = jnp.where(top_wide, t, 0.0).astype(cdt)
        rkT_ref[0, 2 * p + 1] = jnp.where(top_wide, 0.0, t).astype(cdt)


def _mixer_prep(proj, cos, sin, kv_norm, wukT, bsz, seq):
    nt = seq // PREP_TILE
    nq = seq // Q_BLOCK
    cdt = MXU_DTYPE
    col = lambda width, c: pl.BlockSpec((PREP_TILE, width), lambda b, t: (b * nt + t, c))
    table = pl.BlockSpec((1, PREP_TILE, LANE), lambda b, t: (b, t, 0))
    per_q = lambda *tail: pl.BlockSpec((1, PREP_TILE // Q_BLOCK) + tail, lambda b, t: (b, t) + (0,) * len(tail))
    per_c = lambda *tail: pl.BlockSpec((1, 1) + tail, lambda b, t: (b, t) + (0,) * len(tail))
    sds = jax.ShapeDtypeStruct
    outs = [
        (sds((bsz, nq, A_LAT, A_HEADS * Q_BLOCK), cdt), per_q(A_LAT, A_HEADS * Q_BLOCK)),
        (sds((bsz, nq, IDX_DIM, IDX_HEADS * Q_BLOCK), cdt), per_q(IDX_DIM, IDX_HEADS * Q_BLOCK)),
        (sds((bsz, nq, IDX_HEADS, Q_BLOCK), jnp.float32), per_q(IDX_HEADS, Q_BLOCK)),
        (sds((bsz, nt, KCHUNK, IDX_DIM), cdt), per_c(KCHUNK, IDX_DIM)),
        (sds((bsz, nt, KCHUNK, A_LAT), cdt), per_c(KCHUNK, A_LAT)),
        (sds((bsz, nt, A_KV_RANK, KCHUNK), cdt), per_c(A_KV_RANK, KCHUNK)),
        (sds((bsz, nq, C_HEADS, 2 * C_DH, 2 * Q_BLOCK), cdt), per_q(C_HEADS, 2 * C_DH, 2 * Q_BLOCK)),
        (sds((bsz, nt, C_HEADS, KCHUNK, 2 * C_DH), cdt), per_c(C_HEADS, KCHUNK, 2 * C_DH)),
        (sds((bsz, nt, C_HEADS, 2 * C_DH, KCHUNK), cdt), per_c(C_HEADS, 2 * C_DH, KCHUNK)),
        (sds((bsz, seq, B_HEADS * B_QK), cdt), pl.BlockSpec((1, PREP_TILE, B_HEADS * B_QK), lambda b, t: (b, t, 0))),
        (sds((bsz, B_HEADS, LANE, seq), cdt), pl.BlockSpec((1, B_HEADS, LANE, PREP_TILE), lambda b, t: (b, 0, 0, t))),
        (sds((bsz, 2 * D_HEADS, seq), jnp.float32), pl.BlockSpec((1, 2 * D_HEADS, PREP_TILE), lambda b, t: (b, 0, t))),
    ]
    return pl.pallas_call(
        _prep_body,
        grid=(bsz, nt),
        in_specs=[col(COL, COL_B_QK), col(COL, COL_C_Q), col(COL, COL_C_K), col(COL, COL_C_V),
                  col(COL, COL_A_CKV_ROPE), col(COL, COL_A_NOPE), col(COL, COL_A_IQ), col(NARROW, COL_NARROW),
                  table, table,
                  pl.BlockSpec((1, A_KV_RANK), lambda b, t: (0, 0)),
                  pl.BlockSpec((A_HEADS, A_KV_RANK, A_NOPE), lambda b, t: (0, 0, 0))],
        out_specs=[o[1] for o in outs],
        out_shape=[o[0] for o in outs],
        compiler_params=_params("arbitrary", "arbitrary"),
        name="mixer_prep",
    )(proj, proj, proj, proj, proj, proj, proj, proj, cos, sin, kv_norm.reshape(1, A_KV_RANK), wukT)


def _rope_tables(positions):
    half = ROPE_DIM // 2
    inv_freq = ROPE_THETA ** (-jnp.arange(half, dtype=jnp.float32) / half)
    ang = positions.astype(jnp.float32)[:, :, None] * inv_freq
    cos, sin = jnp.cos(ang), jnp.sin(ang)
    reps = LANE // ROPE_DIM
    return (jnp.concatenate([cos, cos] * reps, -1), jnp.concatenate([-sin, sin] * reps, -1))


_SIGN_BIT = np.int32(-2 ** 31)
_LOW31 = np.int32(2 ** 31 - 1)
_KEY_NEG_INF = np.int32(np.array(-np.inf, np.float32).view(np.int32) ^ _LOW31)


def _dsa_body(iqT_ref, iw_ref, qT_ref, ik_ref, kv_ref, ckvT_ref, wuv_ref, o_ref,
              key_ref, acc_ref, thr_ref, need_ref, *, n_keep):
    f32 = jnp.float32
    qi = pl.program_id(1)
    nk = (qi * Q_BLOCK + Q_BLOCK + KCHUNK - 1) // KCHUNK
    iqT = iqT_ref[0, 0]
    iw8 = iw_ref[0, 0]
    iw = jnp.concatenate([iw8[h:h + 1] for h in range(IDX_HEADS)], axis=1)
    qpos = qi * Q_BLOCK + lax.broadcasted_iota(jnp.int32, (KCHUNK, Q_BLOCK), 1)
    krow = lax.broadcasted_iota(jnp.int32, (KCHUNK, Q_BLOCK), 0)

    def index_chunk(c, carry):
        lg = jnp.dot(ik_ref[0, c], iqT, preferred_element_type=f32)
        w = jnp.maximum(lg, 0.0) * iw
        idx = w[:, :Q_BLOCK]
        for h in range(1, IDX_HEADS):
            idx = idx + w[:, h * Q_BLOCK:(h + 1) * Q_BLOCK]
        idx = jnp.where(idx == 0.0, 0.0, idx)
        idx = jnp.where(krow + c * KCHUNK <= qpos, idx, -jnp.inf)
        bits = pltpu.bitcast(idx, jnp.int32)
        key_ref[c] = bits ^ ((bits >> 31) & _LOW31)
        return carry

    lax.fori_loop(0, nk, index_chunk, 0)

    def count(cmp, t):
        def body(c, cnt):
            hit = cmp(key_ref[c], t).astype(jnp.int32)
            return cnt + hit.reshape(KCHUNK // SUBLANE, SUBLANE, Q_BLOCK).sum(0)
        cnt = lax.fori_loop(0, nk, body, jnp.zeros((SUBLANE, Q_BLOCK), jnp.int32))
        return cnt.sum(0, keepdims=True)

    @pl.when(qi * Q_BLOCK + Q_BLOCK <= n_keep)
    def _():
        thr_ref[...] = jnp.full((1, Q_BLOCK), _KEY_NEG_INF, jnp.int32)
        need_ref[...] = jnp.zeros((1, Q_BLOCK), jnp.int32)

    @pl.when(qi * Q_BLOCK + Q_BLOCK > n_keep)
    def _():
        def bit_step(i, prefix):
            cand = prefix | jnp.left_shift(jnp.int32(1), 31 - i)
            cnt = count(lambda k, t: k >= t, cand ^ _SIGN_BIT)
            return jnp.where(cnt >= n_keep, cand, prefix)
        prefix = lax.fori_loop(0, 32, bit_step, jnp.zeros((1, Q_BLOCK), jnp.int32))
        thr = prefix ^ _SIGN_BIT
        thr_ref[...] = thr
        need_ref[...] = n_keep - count(lambda k, t: k > t, thr)

    thr = thr_ref[...]
    need = need_ref[...].astype(f32)
    qT = qT_ref[0, 0]
    scale = (A_NOPE + A_ROPE) ** -0.5
    r_i = lax.broadcasted_iota(jnp.int32, (KCHUNK, KCHUNK), 0)
    c_i = lax.broadcasted_iota(jnp.int32, (KCHUNK, KCHUNK), 1)
    before = (c_i < r_i).astype(jnp.bfloat16)
    acc_ref[...] = jnp.zeros_like(acc_ref)

    def attend_chunk(c, carry):
        m, l, ties_seen = carry
        s = jnp.dot(kv_ref[0, c], qT, preferred_element_type=f32) * scale
        key = key_ref[c]
        tie = key == thr
        tie_f = tie.astype(f32)
        rank = jnp.dot(before, tie_f.astype(jnp.bfloat16), preferred_element_type=f32) + ties_seen
        keep = (key > thr) | (tie & (rank < need))
        bias = jnp.where(keep, 0.0, _MASKED)
        s = s + jnp.concatenate([bias] * A_HEADS, axis=1)
        m_new = jnp.maximum(m, s.max(0, keepdims=True))
        alpha = jnp.exp(m - m_new)
        p = jnp.exp(s - m_new)
        l = alpha * l + p.sum(0, keepdims=True)
        pv = jnp.dot(ckvT_ref[0, c], p.astype(ckvT_ref.dtype), preferred_element_type=f32)
        acc_ref[...] = acc_ref[...] * alpha + pv
        return m_new, l, ties_seen + tie_f.sum(0, keepdims=True)

    lanes = A_HEADS * Q_BLOCK
    init = (jnp.full((1, lanes), _MASKED, f32), jnp.zeros((1, lanes), f32), jnp.zeros((1, Q_BLOCK), f32))
    _, l, _ = lax.fori_loop(0, nk, attend_chunk, init)
    o_latT = acc_ref[...] / l
    for h in range(A_HEADS):
        o_lat = o_latT[:, h * Q_BLOCK:(h + 1) * Q_BLOCK].T.astype(wuv_ref.dtype)
        o_ref[0, :, h * A_V:(h + 1) * A_V] = jnp.dot(o_lat, wuv_ref[h], preferred_element_type=f32)


def _dsa_attention(iqT, iw, qT, ikc, kvc, ckvT, wuv, *, n_keep):
    bsz, nq = qT.shape[:2]
    nc = ikc.shape[1]
    seq = nq * Q_BLOCK
    per_q = lambda b, q: (b, q, 0, 0)
    per_b = lambda b, q: (b, 0, 0, 0)
    return pl.pallas_call(
        functools.partial(_dsa_body, n_keep=n_keep),
        grid=(bsz, nq),
        in_specs=[pl.BlockSpec((1, 1) + iqT.shape[2:], per_q),
                  pl.BlockSpec((1, 1) + iw.shape[2:], per_q),
                  pl.BlockSpec((1, 1) + qT.shape[2:], per_q),
                  pl.BlockSpec((1,) + ikc.shape[1:], per_b),
                  pl.BlockSpec((1,) + kvc.shape[1:], per_b),
                  pl.BlockSpec((1,) + ckvT.shape[1:], per_b),
                  pl.BlockSpec(wuv.shape, lambda b, q: (0, 0, 0))],
        out_specs=pl.BlockSpec((1, Q_BLOCK, A_HEADS * A_V), lambda b, q: (b, q, 0)),
        out_shape=jax.ShapeDtypeStruct((bsz, seq, A_HEADS * A_V), jnp.float32),
        scratch_shapes=[pltpu.VMEM((nc, KCHUNK, Q_BLOCK), jnp.int32),
                        pltpu.VMEM((A_KV_RANK, A_HEADS * Q_BLOCK), jnp.float32),
                        pltpu.VMEM((1, Q_BLOCK), jnp.int32),
                        pltpu.VMEM((1, Q_BLOCK), jnp.int32)],
        compiler_params=_params("arbitrary", "arbitrary"),
        name="dsa_attention",
    )(iqT, iw, qT, ikc, kvc, ckvT, wuv)


RET_TILE = 256


def _retention_body(q_ref, kT_ref, v_ref, g_ref, o_ref, s_ref):
    f32 = jnp.float32
    cdt = MXU_DTYPE
    c = RET_CHUNK

    @pl.when(pl.program_id(1) == 0)
    def _():
        s_ref[...] = jnp.zeros_like(s_ref)

    rel = (lax.broadcasted_iota(jnp.int32, (c, c), 0) - lax.broadcasted_iota(jnp.int32, (c, c), 1)).astype(f32)
    pos_c = lax.broadcasted_iota(jnp.int32, (c, 1), 0).astype(f32)
    pos_r = lax.broadcasted_iota(jnp.int32, (1, c), 1).astype(f32)

    def mm(a, b):
        return jnp.dot(a.astype(cdt), b.astype(cdt), preferred_element_type=f32)

    for h in range(B_HEADS):
        log_gamma = float(np.log(np.float32(1.0) - np.float32(2.0) ** np.float32(-5 - h)))
        intra = jnp.where(rel >= 0, jnp.exp(log_gamma * jnp.maximum(rel, 0.0)), 0.0)
        to_end = jnp.exp(log_gamma * (c - 1 - pos_r))
        from_start = jnp.exp(log_gamma * (pos_c + 1.0))
        chunk_decay = float(np.exp(np.float32(log_gamma) * np.float32(c)))
        lanes = slice(h * B_V, (h + 1) * B_V)
        pair = slice((h // 2) * LANE, (h // 2 + 1) * LANE)
        for ci in range(RET_TILE // c):
            rows = slice(ci * c, (ci + 1) * c)
            q = q_ref[0, rows, pair].astype(f32)
            kT = kT_ref[0, h, :, rows].astype(f32)
            v = v_ref[rows, lanes]
            state = s_ref[h]
            o = mm(mm(q, kT) * intra, v) + mm(q * from_start, state)
            s_ref[h] = state * chunk_decay + mm(kT * to_end, v)
            gate = g_ref[rows, lanes]
            o = o * lax.rsqrt(jnp.mean(o * o, -1, keepdims=True) + RMS_EPS)
            o_ref[0, rows, lanes] = o * (gate * _sigmoid(gate))


def _retention(rq, rkT, proj, bsz, seq):
    nt = seq // RET_TILE
    width = B_HEADS * B_V
    col = lambda c: pl.BlockSpec((RET_TILE, COL), lambda b, t: (b * nt + t, c))
    return pl.pallas_call(
        _retention_body,
        grid=(bsz, nt),
        in_specs=[pl.BlockSpec((1, RET_TILE, B_HEADS * B_QK), lambda b, t: (b, t, 0)),
                  pl.BlockSpec((1, B_HEADS, LANE, RET_TILE), lambda b, t: (b, 0, 0, t)),
                  col(COL_B_V), col(COL_B_G)],
        out_specs=pl.BlockSpec((1, RET_TILE, width), lambda b, t: (b, t, 0)),
        out_shape=jax.ShapeDtypeStruct((bsz, seq, width), jnp.float32),
        scratch_shapes=[pltpu.VMEM((B_HEADS, LANE, B_V), jnp.float32)],
        compiler_params=_params("arbitrary", "arbitrary"),
        name="retention",
    )(rq, rkT, proj, proj)


def _diff_attention_body(lam_ref, qbd_ref, k_ref, vT_ref, gain_ref, o_ref, acc_ref):
    f32 = jnp.float32
    qi = pl.program_id(1)
    nk = (qi * Q_BLOCK + Q_BLOCK + KCHUNK - 1) // KCHUNK
    qpos = qi * Q_BLOCK + lax.broadcasted_iota(jnp.int32, (KCHUNK, Q_BLOCK), 1)
    krow = lax.broadcasted_iota(jnp.int32, (KCHUNK, Q_BLOCK), 0)
    acc_ref[...] = jnp.zeros_like(acc_ref)

    heads = range(C_HEADS)

    def chunk(c, carry):
        ms, ls = carry
        causal = jnp.where(krow + c * KCHUNK <= qpos, 0.0, _MASKED)
        bias = jnp.concatenate([causal, causal], axis=1)
        s = [jnp.dot(k_ref[0, c, h], qbd_ref[0, 0, h], preferred_element_type=f32) + bias for h in heads]
        new_ms = [jnp.maximum(ms[h], s[h].max(0, keepdims=True)) for h in heads]
        alpha = [jnp.exp(ms[h] - new_ms[h]) for h in heads]
        p = [jnp.exp(s[h] - new_ms[h]) for h in heads]
        new_ls = [alpha[h] * ls[h] + p[h].sum(0, keepdims=True) for h in heads]
        pv = [jnp.dot(vT_ref[0, c, h], p[h].astype(vT_ref.dtype), preferred_element_type=f32) for h in heads]
        for h in heads:
            acc_ref[h] = acc_ref[h] * alpha[h] + pv[h]
        return tuple(new_ms), tuple(new_ls)

    lanes = 2 * Q_BLOCK
    init = (tuple(jnp.full((1, lanes), _MASKED, f32) for _ in range(C_HEADS)),
            tuple(jnp.zeros((1, lanes), f32) for _ in range(C_HEADS)))
    _, ls = lax.fori_loop(0, nk, chunk, init)
    lam = lam_ref[0]
    for h in range(C_HEADS):
        o = acc_ref[h] / ls[h]
        o = o[:, :Q_BLOCK] - lam * o[:, Q_BLOCK:]
        o = o * lax.rsqrt(jnp.mean(o * o, 0, keepdims=True) + RMS_EPS)
        o_ref[0, :, h * 2 * C_DH:(h + 1) * 2 * C_DH] = o.T * gain_ref[...]


def _diff_attention(lam, qbd, kc, vT, gain):
    bsz, nq = qbd.shape[:2]
    seq = nq * Q_BLOCK
    width = C_HEADS * 2 * C_DH
    return pl.pallas_call(
        _diff_attention_body,
        grid=(bsz, nq),
        in_specs=[pl.BlockSpec(memory_space=pltpu.SMEM),
                  pl.BlockSpec((1, 1) + qbd.shape[2:], lambda b, q: (b, q, 0, 0, 0)),
                  pl.BlockSpec((1,) + kc.shape[1:], lambda b, q: (b, 0, 0, 0, 0)),
                  pl.BlockSpec((1,) + vT.shape[1:], lambda b, q: (b, 0, 0, 0, 0)),
                  pl.BlockSpec(gain.shape, lambda b, q: (0, 0))],
        out_specs=pl.BlockSpec((1, Q_BLOCK, width), lambda b, q: (b, q, 0)),
        out_shape=jax.ShapeDtypeStruct((bsz, seq, width), jnp.float32),
        scratch_shapes=[pltpu.VMEM((C_HEADS, 2 * C_DH, 2 * Q_BLOCK), jnp.float32)],
        compiler_params=_params("arbitrary", "arbitrary"),
        name="diff_attention",
    )(lam, qbd, kc, vT, gain)


GDN_TILE = 256
GDN_SUB = 16


def _gdn_body(qkv_ref, z_ref, nar_ref, gateT_ref, conv_ref, alog_r_ref, dtb_r_ref, alog_c_ref, dtb_c_ref,
              ng_ref, o_ref, halo_ref, s_ref):
    f32 = jnp.float32
    cdt = MXU_DTYPE
    c = GDN_CHUNK

    @pl.when(pl.program_id(1) == 0)
    def _():
        halo_ref[...] = jnp.zeros_like(halo_ref)
        s_ref[...] = jnp.zeros_like(s_ref)

    x = qkv_ref[...]
    xp = jnp.concatenate([halo_ref[...], x], 0)
    w = conv_ref[...]
    pre = xp[SUBLANE - CONV_K + 1:SUBLANE - CONV_K + 1 + GDN_TILE] * w[0:1]
    for j in range(1, CONV_K):
        off = SUBLANE - CONV_K + 1 + j
        pre = pre + xp[off:off + GDN_TILE] * w[j:j + 1]
    halo_ref[...] = x[GDN_TILE - SUBLANE:]
    qkv = pre * _sigmoid(pre)

    gates = nar_ref[:, LANE:]
    beta_c = _sigmoid(gates[:, NARROW_DB:NARROW_DB + D_HEADS])
    g_c = -jnp.exp(alog_r_ref[...]) * _softplus(gates[:, NARROW_DA:NARROW_DA + D_HEADS] + dtb_r_ref[...])
    g_r = -jnp.exp(alog_c_ref[...]) * _softplus(gateT_ref[0][D_HEADS:] + dtb_c_ref[...])

    row = lax.broadcasted_iota(jnp.int32, (c, c), 0)
    col = lax.broadcasted_iota(jnp.int32, (c, c), 1)
    tril = row >= col
    strict = row > col
    same_sub = (row // GDN_SUB) == (col // GDN_SUB)
    eye = (row == col).astype(f32)
    lower_ones = tril.astype(f32)
    upper_ones = (row <= col).astype(f32)

    def mm(a, b):
        return jnp.dot(a, b, preferred_element_type=f32)

    n_chunks = GDN_TILE // c
    pairs = [(ci, h) for ci in range(n_chunks) for h in range(D_HEADS)]
    rows_of = lambda ci: slice(ci * c, (ci + 1) * c)
    g_cum_c = [mm(lower_ones, g_c[rows_of(ci)]) for ci in range(n_chunks)]
    g_cum_r = [mm(g_r[:, rows_of(ci)], upper_ones) for ci in range(n_chunks)]
    st = []
    for ci, h in pairs:
        rows = rows_of(ci)
        q = qkv[rows, h * D_K:(h + 1) * D_K]
        k = qkv[rows, D_HEADS * D_K + h * D_K:D_HEADS * D_K + (h + 1) * D_K]
        v = qkv[rows, 2 * D_HEADS * D_K + h * D_V:2 * D_HEADS * D_K + (h + 1) * D_V]
        q = q * lax.rsqrt(jnp.sum(q * q, -1, keepdims=True) + RMS_EPS) * D_K ** -0.5
        k = k * lax.rsqrt(jnp.sum(k * k, -1, keepdims=True) + RMS_EPS)
        gc = g_cum_c[ci][:, h:h + 1]
        gr = g_cum_r[ci][h:h + 1, :]
        g_last = gc[c - 1:c]
        gam = jnp.where(tril, jnp.exp(jnp.where(tril, gc - gr, 0.0)), 0.0)
        e_g = jnp.exp(gc)
        beta = beta_c[rows, h:h + 1]
        kT = k.T
        kb = k * beta
        st.append(dict(gam=gam, kT_c=kT.astype(cdt), kb_c=kb.astype(cdt), q_c=q.astype(cdt),
                       qg_c=(q * e_g).astype(cdt), rhs=jnp.concatenate([v * beta, kb * e_g], 1),
                       kdT_c=(kT * jnp.exp(g_last - gr)).astype(cdt), decay=jnp.exp(g_last)))
    for p in st:
        p["low"] = jnp.where(strict, mm(p["kb_c"], p["kT_c"]) * p["gam"], 0.0)
        p["attn_c"] = jnp.where(tril, mm(p["q_c"], p["kT_c"]) * p["gam"], 0.0).astype(cdt)
        p["low_d"] = jnp.where(same_sub, p["low"], 0.0)
        p["xk"] = -p["low_d"]
        p["td"] = eye + p["xk"]
    for _ in range(3):
        for p in st:
            p["xk"] = mm(p["xk"], p["xk"])
        for p in st:
            p["td"] = p["td"] + mm(p["td"], p["xk"])
    for p in st:
        p["m1"] = mm(p["td"], p["low"] - p["low_d"])
        p["sol"] = mm(p["td"], p["rhs"])
    for p in st:
        p["m2"] = mm(p["m1"], p["m1"])
    for p in st:
        p["sol"] = p["sol"] + mm(p["m2"], p["sol"])
    for p in st:
        p["sol"] = p["sol"] - mm(p["m1"], p["sol"])
    for ci in range(n_chunks):
        rows = rows_of(ci)
        cur = [st[ci * D_HEADS + h] for h in range(D_HEADS)]
        states = [s_ref[h] for h in range(D_HEADS)]
        states_c = [s.astype(cdt) for s in states]
        v_new_c = [(p["sol"][:, :D_V] - mm(p["sol"][:, D_V:].astype(cdt), states_c[h])).astype(cdt)
                   for h, p in enumerate(cur)]
        for h, p in enumerate(cur):
            s_ref[h] = states[h] * p["decay"] + mm(p["kdT_c"], v_new_c[h])
        for h, p in enumerate(cur):
            lanes = slice(h * D_V, (h + 1) * D_V)
            o = mm(p["qg_c"], states_c[h]) + mm(p["attn_c"], v_new_c[h])
            zz = z_ref[rows, lanes]
            o = o * lax.rsqrt(jnp.mean(o * o, -1, keepdims=True) + RMS_EPS) * ng_ref[...]
            o_ref[0, rows, lanes] = o * (zz * _sigmoid(zz))


def _gated_deltanet(proj, gatesT, conv_w, a_log, dt_bias, norm_g, bsz, seq):
    nt = seq // GDN_TILE
    width = D_HEADS * (2 * D_K + D_V)
    whole = lambda r, cc: pl.BlockSpec((r, cc), lambda b, t: (0, 0))
    return pl.pallas_call(
        _gdn_body,
        grid=(bsz, nt),
        in_specs=[pl.BlockSpec((GDN_TILE, width), lambda b, t: (b * nt + t, COL_D_QKV)),
                  pl.BlockSpec((GDN_TILE, COL), lambda b, t: (b * nt + t, COL_D_Z)),
                  pl.BlockSpec((GDN_TILE, NARROW), lambda b, t: (b * nt + t, COL_NARROW)),
                  pl.BlockSpec((1, 2 * D_HEADS, GDN_TILE), lambda b, t: (b, 0, t)),
                  whole(CONV_K, width), whole(1, D_HEADS), whole(1, D_HEADS), whole(D_HEADS, 1), whole(D_HEADS, 1),
                  whole(1, D_V)],
        out_specs=pl.BlockSpec((1, GDN_TILE, D_HEADS * D_V), lambda b, t: (b, t, 0)),
        out_shape=jax.ShapeDtypeStruct((bsz, seq, D_HEADS * D_V), jnp.float32),
        scratch_shapes=[pltpu.VMEM((SUBLANE, width), jnp.float32),
                        pltpu.VMEM((D_HEADS, D_K, D_V), jnp.float32)],
        compiler_params=_params("arbitrary", "arbitrary"),
        name="gated_deltanet",
    )(proj, proj, proj, gatesT, conv_w, a_log.reshape(1, D_HEADS), dt_bias.reshape(1, D_HEADS),
      a_log.reshape(D_HEADS, 1), dt_bias.reshape(D_HEADS, 1), norm_g.reshape(1, D_V))


def _ln_rows(t, g, b):
    mu = jnp.mean(t, -1, keepdims=True)
    d = t - mu
    var = jnp.mean(d * d, -1, keepdims=True)
    return d * lax.rsqrt(var + LN_EPS) * g + b


_HIGH16 = np.int32(-(2 ** 16))


def _pack_bf16_pairs(t):
    n = t.shape[1] // 2
    lo = pltpu.bitcast(t[:, :n].astype(jnp.bfloat16).astype(jnp.float32), jnp.int32)
    hi = pltpu.bitcast(t[:, n:].astype(jnp.bfloat16).astype(jnp.float32), jnp.int32)
    return lax.shift_right_logical(lo, 16) | (hi & _HIGH16)


def _unpack_bf16_pairs(w):
    lo = pltpu.bitcast(w << 16, jnp.float32)
    hi = pltpu.bitcast(w & _HIGH16, jnp.float32)
    return jnp.concatenate([lo, hi], axis=1).astype(jnp.bfloat16)


def _out_ln_router_body(ha_ref, hb_ref, hc_ref, hd_ref, w_ref, x_ref, g_ref, b_ref, wr_ref, br_ref,
                        h_ref, hp_ref, te_ref, tg_ref):
    f32 = jnp.float32
    acc = DEEPNORM_ALPHA * x_ref[...]
    row0 = 0
    for part in (ha_ref, hb_ref, hc_ref, hd_ref):
        width = part.shape[-1]
        acc = acc + jnp.dot(part[...].astype(MXU_DTYPE), w_ref[row0:row0 + width, :], preferred_element_type=f32)
        row0 += width
    h = _ln_rows(acc, g_ref[...], b_ref[...])
    h_ref[...] = h
    hp_ref[...] = _pack_bf16_pairs(h)
    logits = jnp.dot(h.astype(MXU_DTYPE), wr_ref[...], preferred_element_type=f32) + br_ref[...]
    lane = lax.broadcasted_iota(jnp.int32, logits.shape, 1)
    vals, te = [], jnp.zeros(logits.shape, jnp.int32)
    for k in range(TOP_K):
        m = logits.max(-1, keepdims=True)
        e = jnp.min(jnp.where(logits == m, lane, LANE), -1, keepdims=True)
        vals.append(m)
        te = jnp.where(lane == k, e, te)
        logits = jnp.where(lane == e, -jnp.inf, logits)
    ex = [jnp.exp(v - vals[0]) for v in vals]
    den = ex[0] + ex[1] + ex[2] + ex[3]
    tg = jnp.zeros(logits.shape, f32)
    for k in range(TOP_K):
        tg = jnp.where(lane == k, ex[k] / den, tg)
    te_ref[...] = te
    tg_ref[...] = tg


def _out_ln_router(head_parts, w_out_c, x2d, g, b, w_router, b_router, tm=512):
    m = x2d.shape[0]
    k, n = w_out_c.shape
    wr = jnp.zeros((n, LANE), MXU_DTYPE).at[:, :N_EXPERTS].set(w_router.astype(MXU_DTYPE))
    br = jnp.full((1, LANE), -jnp.inf, jnp.float32).at[0, :N_EXPERTS].set(b_router)
    row_blk = lambda c: pl.BlockSpec((tm, c), lambda i: (i, 0))
    whole = lambda r, c: pl.BlockSpec((r, c), lambda i: (0, 0))
    return pl.pallas_call(
        _out_ln_router_body,
        grid=(m // tm,),
        in_specs=[row_blk(p.shape[1]) for p in head_parts]
        + [whole(k, n), row_blk(n), whole(1, n), whole(1, n), whole(n, LANE), whole(1, LANE)],
        out_specs=[row_blk(n), row_blk(n // 2), row_blk(LANE), row_blk(LANE)],
        out_shape=[jax.ShapeDtypeStruct((m, n), jnp.float32),
                   jax.ShapeDtypeStruct((m, n // 2), jnp.int32),
                   jax.ShapeDtypeStruct((m, LANE), jnp.int32),
                   jax.ShapeDtypeStruct((m, LANE), jnp.float32)],
        compiler_params=_params("arbitrary"),
        name="out_proj_ln_router",
    )(*head_parts, w_out_c, x2d, g.reshape(1, n), b.reshape(1, n), wr, br)


def _clamped_swiglu(hg):
    glu, lin = hg[..., :D_FF], hg[..., D_FF:]
    glu = jnp.minimum(glu, SWIGLU_LIMIT)
    lin = jnp.clip(lin, -SWIGLU_LIMIT, SWIGLU_LIMIT)
    return glu * _sigmoid(SWIGLU_ALPHA * glu) * (lin + 1.0)


MOE_RANK_TILE = 512


def _moe_rank_body(te_ref, rank_ref, counts_ref, seen_ref):
    f32 = jnp.float32
    tm = MOE_RANK_TILE

    @pl.when(pl.program_id(0) == 0)
    def _():
        seen_ref[...] = jnp.zeros_like(seen_ref)

    te = te_ref[...]
    lane = lax.broadcasted_iota(jnp.int32, (tm, LANE), 1)
    picks = [lane == te[:, k:k + 1] for k in range(TOP_K)]
    member = picks[0].astype(f32)
    for k in range(1, TOP_K):
        member = member + picks[k].astype(f32)
    earlier = (lax.broadcasted_iota(jnp.int32, (tm, tm), 1) < lax.broadcasted_iota(jnp.int32, (tm, tm), 0))
    prefix = jnp.dot(earlier.astype(jnp.bfloat16), member.astype(jnp.bfloat16), preferred_element_type=f32)
    prefix = prefix + seen_ref[...]
    rank = jnp.zeros((tm, LANE), jnp.int32)
    for k in range(TOP_K):
        r_k = jnp.sum(jnp.where(picks[k], prefix, 0.0), -1, keepdims=True)
        rank = jnp.where(lane == k, r_k.astype(jnp.int32), rank)
    rank_ref[...] = rank
    seen = seen_ref[...] + member.sum(0, keepdims=True)
    seen_ref[...] = seen
    counts_ref[...] = seen.astype(jnp.int32)


def _moe_rank(te):
    n_tok = te.shape[0]
    tm = MOE_RANK_TILE
    return pl.pallas_call(
        _moe_rank_body,
        grid=(n_tok // tm,),
        in_specs=[pl.BlockSpec((tm, LANE), lambda i: (i, 0))],
        out_specs=[pl.BlockSpec((tm, LANE), lambda i: (i, 0)), pl.BlockSpec((1, LANE), lambda i: (0, 0))],
        out_shape=[jax.ShapeDtypeStruct((n_tok, LANE), jnp.int32), jax.ShapeDtypeStruct((1, LANE), jnp.int32)],
        scratch_shapes=[pltpu.VMEM((1, LANE), jnp.float32)],
        compiler_params=_params("arbitrary"),
        name="moe_rank",
    )(te)


def _moe_plan(te):
    top_e = te[:, :TOP_K]
    n_tok = top_e.shape[0]
    n_blocks = -(-n_tok * TOP_K // MOE_BLOCK) + N_EXPERTS + MOE_GATHER_AHEAD - 1
    rank, counts = _moe_rank(te)
    counts = counts[0, :N_EXPERTS]
    padded = (counts + MOE_BLOCK - 1) // MOE_BLOCK * MOE_BLOCK
    pad_end = jnp.cumsum(padded)
    one_hot = top_e[:, :, None] == jnp.arange(N_EXPERTS, dtype=jnp.int32)
    dest = rank[:, :TOP_K] + jnp.sum(jnp.where(one_hot, (pad_end - padded)[None, None, :], 0), -1)
    tok = jnp.broadcast_to(jnp.arange(n_tok, dtype=jnp.int32)[:, None], dest.shape)
    row_tok = jnp.zeros((n_blocks * MOE_BLOCK,), jnp.int32).at[dest.reshape(-1)].set(tok.reshape(-1))
    block_start = jnp.arange(n_blocks, dtype=jnp.int32) * MOE_BLOCK
    block_e = jnp.minimum(jnp.sum(pad_end[None, :] <= block_start[:, None], axis=1), N_EXPERTS - 1).astype(jnp.int32)
    n_used = (pad_end[-1:] // MOE_BLOCK).astype(jnp.int32)
    return dest, row_tok, block_e, n_used


def _row_gather_start(src_hbm, rows_smem, n_rows, dst, sem):
    def body(r, carry):
        pltpu.make_async_copy(src_hbm.at[pl.ds(rows_smem[0, 0, r], 1)], dst.at[pl.ds(r, 1)], sem).start()
        return carry
    lax.fori_loop(0, n_rows, body, 0, unroll=8)


def _row_gather_wait(src_hbm, n_rows, dst, sem):
    pltpu.make_async_copy(src_hbm.at[pl.ds(0, n_rows)], dst, sem).wait()


MOE_GATHER_AHEAD = 2
MOE_GATHER_BUFS = MOE_GATHER_AHEAD + 1


def _moe_expert_body(be_ref, nused_ref, tok_ref, tok_1_ref, tok_ahead_ref, h_hbm, wgu_ref, bgu_ref, wdn_ref, bdn_ref,
                     y_ref, buf_0, buf_1, buf_2, sem, wgu_c, wdn_c):
    f32 = jnp.float32
    i = pl.program_id(0)
    n_used = nused_ref[0]
    bufs = (buf_0, buf_1, buf_2)

    @pl.when(i == 0)
    def _():
        _row_gather_start(h_hbm, tok_ref, MOE_BLOCK, buf_0, sem.at[0])
        _row_gather_start(h_hbm, tok_1_ref, MOE_BLOCK, buf_1, sem.at[1])

    @pl.when(jnp.logical_or(i == 0, be_ref[i] != be_ref[jnp.maximum(i - 1, 0)]))
    def _():
        wgu_c[...] = wgu_ref[0, 0].astype(MXU_DTYPE)
        wdn_c[...] = wdn_ref[0, 0].astype(MXU_DTYPE)

    for phase in range(MOE_GATHER_BUFS):
        cur = bufs[phase]
        ahead = (phase + MOE_GATHER_AHEAD) % MOE_GATHER_BUFS

        @pl.when(jnp.logical_and(i < n_used, i % MOE_GATHER_BUFS == phase))
        def _():
            _row_gather_wait(h_hbm, MOE_BLOCK, cur, sem.at[phase])
            for r in range(MOE_BLOCK):
                pltpu.make_async_copy(h_hbm.at[pl.ds(tok_ahead_ref[0, 0, r], 1)], bufs[ahead].at[pl.ds(r, 1)],
                                      sem.at[ahead]).start()
            xb = _unpack_bf16_pairs(cur[...]).astype(MXU_DTYPE)
            hg = jnp.dot(xb, wgu_c[...], preferred_element_type=f32) + bgu_ref[0, 0]
            act = _clamped_swiglu(hg)
            y_ref[...] = jnp.dot(act.astype(MXU_DTYPE), wdn_c[...], preferred_element_type=f32) + bdn_ref[0, 0]

        @pl.when(jnp.logical_and(jnp.logical_and(i >= n_used, i < n_used + MOE_GATHER_AHEAD),
                                 i % MOE_GATHER_BUFS == phase))
        def _():
            _row_gather_wait(h_hbm, MOE_BLOCK, cur, sem.at[phase])

    @pl.when(i >= n_used)
    def _():
        y_ref[...] = jnp.zeros_like(y_ref)


def _moe_experts(h_packed, row_tok, block_e, n_used, layer, w_gu, b_gu, w_dn, b_dn):
    n_tok = h_packed.shape[0]
    d = 2 * h_packed.shape[1]
    n_blocks = block_e.shape[0]
    n_layers = w_gu.shape[0]
    assert (n_tok * TOP_K + N_EXPERTS * (MOE_BLOCK - 1)) // MOE_BLOCK + MOE_GATHER_AHEAD <= n_blocks
    tok3 = row_tok.reshape(n_blocks, 1, MOE_BLOCK)
    smem_blk = lambda imap: pl.BlockSpec((1, 1, MOE_BLOCK), imap, memory_space=pltpu.SMEM)
    by_expert = lambda shape: pl.BlockSpec((1, 1) + shape, lambda i, be, nu: (layer, be[i], 0, 0))
    gather_buf = pltpu.VMEM((MOE_BLOCK, d // 2), jnp.int32)
    return pl.pallas_call(
        _moe_expert_body,
        grid_spec=pltpu.PrefetchScalarGridSpec(
            num_scalar_prefetch=2,
            grid=(n_blocks,),
            in_specs=[smem_blk(lambda i, be, nu: (i, 0, 0)),
                      smem_blk(lambda i, be, nu: (1, 0, 0)),
                      smem_blk(lambda i, be, nu: (jnp.minimum(i + MOE_GATHER_AHEAD, n_blocks - 1), 0, 0)),
                      pl.BlockSpec(memory_space=pl.ANY),
                      by_expert((d, 2 * D_FF)), by_expert((1, 2 * D_FF)),
                      by_expert((D_FF, d)), by_expert((1, d))],
            out_specs=pl.BlockSpec((MOE_BLOCK, d), lambda i, be, nu: (i, 0)),
            scratch_shapes=[gather_buf] * MOE_GATHER_BUFS
            + [pltpu.SemaphoreType.DMA((MOE_GATHER_BUFS,)),
               pltpu.VMEM((d, 2 * D_FF), MXU_DTYPE), pltpu.VMEM((D_FF, d), MXU_DTYPE)]),
        out_shape=jax.ShapeDtypeStruct((n_blocks * MOE_BLOCK, d), jnp.float32),
        compiler_params=_params("arbitrary"),
        name="moe_experts",
    )(block_e, n_used, tok3, tok3, tok3, h_packed, w_gu, b_gu.reshape(n_layers, N_EXPERTS, 1, 2 * D_FF),
      w_dn, b_dn.reshape(n_layers, N_EXPERTS, 1, d))


MOE_COMBINE_ROWS = 128


def _moe_combine_ln_body(dest_ref, dest_next_ref, h_ref, tg_ref, g_ref, b_ref, y_hbm, o_ref,
                         buf_even, buf_odd, sem):
    tm = MOE_COMBINE_ROWS
    i = pl.program_id(0)
    n = pl.num_programs(0)
    bufs = (buf_even, buf_odd)

    def start(rows_smem, dst, s):
        for k in range(TOP_K):
            for r in range(tm):
                pltpu.make_async_copy(y_hbm.at[pl.ds(rows_smem[0, 0, k * tm + r], 1)],
                                      dst.at[k, pl.ds(r, 1)], sem.at[s]).start()

    def drain(dst, s):
        for k in range(TOP_K):
            _row_gather_wait(y_hbm, tm, dst.at[k], sem.at[s])

    @pl.when(i == 0)
    def _():
        start(dest_ref, buf_even, 0)

    for parity in (0, 1):
        cur, nxt = bufs[parity], bufs[1 - parity]

        @pl.when(i % 2 == parity)
        def _():
            drain(cur, parity)
            start(dest_next_ref, nxt, 1 - parity)
            tg = tg_ref[...]
            y = tg[:, 0:1] * cur[0]
            for k in range(1, TOP_K):
                y = y + tg[:, k:k + 1] * cur[k]
            o_ref[...] = _ln_rows(DEEPNORM_ALPHA * h_ref[...] + y, g_ref[...], b_ref[...])

        @pl.when(jnp.logical_and(i == n - 1, i % 2 == parity))
        def _():
            drain(nxt, 1 - parity)


def _moe_combine_ln(h2d, y_rows, dest, tg, g, b):
    n_tok, d = h2d.shape
    tm = MOE_COMBINE_ROWS
    nt = n_tok // tm
    dest3 = dest.reshape(nt, tm, TOP_K).transpose(0, 2, 1).reshape(nt, 1, TOP_K * tm)
    smem_blk = lambda imap: pl.BlockSpec((1, 1, TOP_K * tm), imap, memory_space=pltpu.SMEM)
    return pl.pallas_call(
        _moe_combine_ln_body,
        grid=(nt,),
        in_specs=[smem_blk(lambda i: (i, 0, 0)),
                  smem_blk(lambda i: (jnp.minimum(i + 1, nt - 1), 0, 0)),
                  pl.BlockSpec((tm, d), lambda i: (i, 0)),
                  pl.BlockSpec((tm, LANE), lambda i: (i, 0)),
                  pl.BlockSpec((1, d), lambda i: (0, 0)),
                  pl.BlockSpec((1, d), lambda i: (0, 0)),
                  pl.BlockSpec(memory_space=pl.ANY)],
        out_specs=pl.BlockSpec((tm, d), lambda i: (i, 0)),
        out_shape=jax.ShapeDtypeStruct((n_tok, d), jnp.float32),
        scratch_shapes=[pltpu.VMEM((TOP_K, tm, d), jnp.float32), pltpu.VMEM((TOP_K, tm, d), jnp.float32),
                        pltpu.SemaphoreType.DMA((2,))],
        compiler_params=_params("arbitrary"),
        name="moe_combine_ln",
    )(dest3, dest3, h2d, tg, g.reshape(1, d), b.reshape(1, d), y_rows)


def _moe_layer(h2d, h_packed, te, tg, layer, w_gu, b_gu, w_dn, b_dn, g, b):
    dest, row_tok, block_e, n_used = _moe_plan(te)
    y_rows = _moe_experts(h_packed, row_tok, block_e, n_used, layer, w_gu, b_gu, w_dn, b_dn)
    return _moe_combine_ln(h2d, y_rows, dest, tg, g, b)


def _mixers(proj, cos, sin, bsz, seq, lambda_init, a_kv_norm, a_w_uk, a_w_uv, c_lambda, c_subln,
            d_conv, d_a_log, d_dt_bias, d_norm):
    cdt = MXU_DTYPE
    (qT, iqT, iw, ikc, kvc, ckvT, qbd, kc, vT, rq, rkT, gatesT) = _mixer_prep(
        proj, cos, sin, a_kv_norm, a_w_uk.transpose(1, 0, 2).astype(cdt), bsz, seq)
    head_a = _dsa_attention(iqT, iw, qT, ikc, kvc, ckvT, a_w_uv.transpose(1, 0, 2).astype(cdt),
                            n_keep=min(INDEX_TOPK, seq // 4))
    head_b = _retention(rq, rkT, proj, bsz, seq)
    lam_full = (jnp.exp(jnp.sum(c_lambda[0] * c_lambda[1])) - jnp.exp(jnp.sum(c_lambda[2] * c_lambda[3]))
                + lambda_init)
    gain = (c_subln * (1.0 - lambda_init)).reshape(1, 2 * C_DH)
    head_c = _diff_attention(lam_full.reshape(1), qbd, kc, vT, gain)
    head_d = _gated_deltanet(proj, gatesT, d_conv, d_a_log, d_dt_bias, d_norm, bsz, seq)
    return [t.reshape(bsz * seq, t.shape[-1]) for t in (head_a, head_b, head_c, head_d)]


def kernel(x, positions, w_in, w_out, a_kv_norm, a_w_uk, a_w_uv, c_lambda, c_subln, d_conv, d_a_log, d_dt_bias, d_norm, ln1_g, ln1_b, w_router, b_router, w_gu, b_gu, w_dn, b_dn, ln2_g, ln2_b):
    bsz, seq, d = x.shape
    x2d = x.reshape(bsz * seq, d)
    cos, sin = _rope_tables(positions)
    for l in range(DEPTH):
        proj = _in_proj(x2d, _pack_w_in(w_in[l]))
        lambda_init = 0.8 - 0.6 * math.exp(-0.3 * l)
        heads = _mixers(proj, cos, sin, bsz, seq, lambda_init, a_kv_norm[l], a_w_uk[l], a_w_uv[l], c_lambda[l],
                        c_subln[l], d_conv[l], d_a_log[l], d_dt_bias[l], d_norm[l])
        h2d, h_packed, te, tg = _out_ln_router(heads, w_out[l].astype(MXU_DTYPE), x2d, ln1_g[l], ln1_b[l],
                                     w_router[l], b_router[l])
        x2d = _moe_layer(h2d, h_packed, te, tg, l, w_gu, b_gu, w_dn, b_dn, ln2_g[l], ln2_b[l])
    return x2d.reshape(bsz, seq, d)
```

```python
import functools
import math

import jax
import jax.numpy as jnp
import numpy as np
from jax import lax
from jax.experimental import pallas as pl
from jax.experimental.pallas import tpu as pltpu

D_MODEL = 2048
DEPTH = 4
A_HEADS, A_NOPE, A_ROPE, A_V, A_KV_RANK = 4, 128, 64, 128, 256
IDX_HEADS, IDX_DIM, INDEX_TOPK = 8, 64, 256
B_HEADS, B_QK, B_V, RET_CHUNK = 4, 64, 128, 128
C_HEADS, C_DH = 4, 64
D_HEADS, D_K, D_V, CONV_K, GDN_CHUNK = 4, 128, 128, 4, 64
N_EXPERTS, TOP_K, D_FF = 32, 4, 512
SWIGLU_ALPHA, SWIGLU_LIMIT = 1.702, 7.0
MOE_BLOCK = 256
ROPE_THETA = 10000.0
ROPE_DIM = 64
Q_BLOCK = 128
DEEPNORM_ALPHA = (2 * DEPTH) ** 0.25
LN_EPS = 1e-5
RMS_EPS = 1e-6
D_MIX = A_HEADS * A_V + B_HEADS * B_V + C_HEADS * 2 * C_DH + D_HEADS * D_V
A_LAT = A_KV_RANK + A_ROPE

IN_WIDTHS = (
    A_HEADS * (A_NOPE + A_ROPE), A_KV_RANK, A_ROPE, IDX_HEADS * IDX_DIM, IDX_DIM, IDX_HEADS,
    B_HEADS * B_QK, B_HEADS * B_QK, B_HEADS * B_V, B_HEADS * B_V,
    C_HEADS * 2 * C_DH, C_HEADS * 2 * C_DH, C_HEADS * 2 * C_DH,
    D_HEADS * (2 * D_K + D_V), D_HEADS * D_V, D_HEADS, D_HEADS,
)
_IN_OFFS = np.concatenate([[0], np.cumsum(IN_WIDTHS)]).tolist()

LANE = 128
SUBLANE = 8
VMEM_LIMIT = 56 * 1024 * 1024

MXU_DTYPE = jnp.bfloat16
_MASKED = -1e30

COL = 512
COL_D_QKV = 0
COL_D_Z = 3
COL_B_V = 4
COL_B_G = 5
COL_B_QK = 6
COL_C_Q = 7
COL_C_K = 8
COL_C_V = 9
COL_A_CKV_ROPE = 10
COL_A_NOPE = 11
COL_A_IQ = 12
NARROW = 256
COL_NARROW = 26
IN_PACKED = (COL_NARROW + 1) * NARROW
NARROW_IW = 0
NARROW_DB = IDX_HEADS
NARROW_DA = IDX_HEADS + D_HEADS


def _pack_w_in(w):
    g = lambda i: w[:, _IN_OFFS[i]:_IN_OFFS[i + 1]]
    a_q = g(0).reshape(w.shape[0], A_HEADS, A_NOPE + A_ROPE)
    nope = a_q[:, :, :A_NOPE].reshape(w.shape[0], A_HEADS * A_NOPE)
    rope = a_q[:, :, A_NOPE:].reshape(w.shape[0], A_HEADS * A_ROPE)
    cols = [g(13), g(14), g(8), g(9), g(6), g(7), g(10), g(11), g(12), g(1), rope, nope, g(3),
            g(2), g(4), g(5), g(15), g(16)]
    used = sum(c.shape[1] for c in cols)
    cols.append(jnp.zeros((w.shape[0], IN_PACKED - used), w.dtype))
    return jnp.concatenate(cols, axis=1).astype(MXU_DTYPE)


def _sigmoid(t):
    return 1.0 / (1.0 + jnp.exp(-t))


def _softplus(t):
    return jnp.maximum(t, 0.0) + jnp.log1p(jnp.exp(-jnp.abs(t)))


def _params(*sem):
    return pltpu.CompilerParams(dimension_semantics=sem, vmem_limit_bytes=VMEM_LIMIT)


def _in_proj_body(x_ref, w_ref, o_ref, xb_ref):
    @pl.when(pl.program_id(1) == 0)
    def _():
        xb_ref[...] = x_ref[...].astype(MXU_DTYPE)

    o_ref[...] = jnp.dot(xb_ref[...], w_ref[...], preferred_element_type=jnp.float32)


def _in_proj(x2d, w_packed, tm=1024, tn=768):
    m, k = x2d.shape
    n = w_packed.shape[1]
    return pl.pallas_call(
        _in_proj_body,
        grid=(m // tm, n // tn),
        in_specs=[pl.BlockSpec((tm, k), lambda i, j: (i, 0)),
                  pl.BlockSpec((k, tn), lambda i, j: (0, j))],
        out_specs=pl.BlockSpec((tm, tn), lambda i, j: (i, j)),
        out_shape=jax.ShapeDtypeStruct((m, n), jnp.float32),
        scratch_shapes=[pltpu.VMEM((tm, k), MXU_DTYPE)],
        compiler_params=_params("arbitrary", "arbitrary"),
        name="in_proj",
    )(x2d, w_packed)


PREP_TILE = 256
KCHUNK = 256


def _prep_body(bqk_ref, cq_ref, ck_ref, cv_ref, ckvrope_ref, nope_ref, iq_ref, nar_ref, cos_ref, sin_ref,
               kvn_ref, wukT_ref,
               qT_ref, iqT_ref, iw_ref, ik_ref, kv_ref, ckvT_ref, qbd_ref, kc_ref, vT_ref, rq_ref, rkT_ref, gT_ref):
    f32 = jnp.float32
    cdt = MXU_DTYPE
    cos, sin = cos_ref[0], sin_ref[0]
    lane = lax.broadcasted_iota(jnp.int32, (PREP_TILE, LANE), 1)
    first_half = (lane % ROPE_DIM) < ROPE_DIM // 2
    eye = (lax.broadcasted_iota(jnp.int32, (LANE, LANE), 0)
           == lax.broadcasted_iota(jnp.int32, (LANE, LANE), 1)).astype(cdt)
    top = lax.broadcasted_iota(jnp.int32, (LANE, LANE), 0) < LANE // 2
    top_wide = lax.broadcasted_iota(jnp.int32, (LANE, PREP_TILE), 0) < LANE // 2

    def rope(t):
        outs = []
        for j in range(t.shape[1] // LANE):
            ts = t[:, j * LANE:(j + 1) * LANE]
            partner = jnp.where(first_half, pltpu.roll(ts, LANE - ROPE_DIM // 2, 1), pltpu.roll(ts, ROPE_DIM // 2, 1))
            outs.append(ts * cos + partner * sin)
        return outs[0] if len(outs) == 1 else jnp.concatenate(outs, axis=1)

    def tr(t):
        return lax.dot_general(eye, t.astype(cdt), (((1,), (1,)), ((), ())), preferred_element_type=f32)

    ckv_rope = ckvrope_ref[...]
    a_ckv = ckv_rope[:, :A_KV_RANK]
    c_kv = a_ckv * lax.rsqrt(jnp.mean(a_ckv * a_ckv, -1, keepdims=True) + RMS_EPS) * kvn_ref[...]
    nar = nar_ref[...]
    kr_ik = rope(nar[:, :LANE])
    kv_ref[0, 0, :, :A_KV_RANK] = c_kv.astype(cdt)
    kv_ref[0, 0, :, A_KV_RANK:] = kr_ik[:, :A_ROPE].astype(cdt)
    ik_ref[0, 0] = kr_ik[:, A_ROPE:].astype(cdt)
    for rb in range(A_KV_RANK // LANE):
        ckvT_ref[0, 0, rb * LANE:(rb + 1) * LANE, :] = tr(c_kv[:, rb * LANE:(rb + 1) * LANE]).astype(cdt)
    q_rope = rope(ckv_rope[:, A_KV_RANK:])
    nope = nope_ref[...]
    iq = rope(iq_ref[...])
    narT = nar[:, LANE:].T
    gT_ref[0] = narT[NARROW_DB:NARROW_DB + 2 * D_HEADS]
    cq = rope(cq_ref[...]) * C_DH ** -0.5
    for j in range(PREP_TILE // Q_BLOCK):
        rows = slice(j * Q_BLOCK, (j + 1) * Q_BLOCK)
        for h in range(A_HEADS):
            q_latT = lax.dot_general(wukT_ref[h], nope[rows, h * A_NOPE:(h + 1) * A_NOPE].astype(cdt),
                                     (((1,), (1,)), ((), ())), preferred_element_type=f32)
            qT_ref[0, j, :A_KV_RANK, h * Q_BLOCK:(h + 1) * Q_BLOCK] = q_latT.astype(cdt)
        for p in range(A_HEADS * A_ROPE // LANE):
            t = tr(q_rope[rows, p * LANE:(p + 1) * LANE])
            qT_ref[0, j, A_KV_RANK:, (2 * p) * Q_BLOCK:(2 * p + 1) * Q_BLOCK] = t[:A_ROPE].astype(cdt)
            qT_ref[0, j, A_KV_RANK:, (2 * p + 1) * Q_BLOCK:(2 * p + 2) * Q_BLOCK] = t[A_ROPE:].astype(cdt)
        for p in range(IDX_HEADS * IDX_DIM // LANE):
            t = tr(iq[rows, p * LANE:(p + 1) * LANE])
            iqT_ref[0, j, :, (2 * p) * Q_BLOCK:(2 * p + 1) * Q_BLOCK] = t[:IDX_DIM].astype(cdt)
            iqT_ref[0, j, :, (2 * p + 1) * Q_BLOCK:(2 * p + 2) * Q_BLOCK] = t[IDX_DIM:].astype(cdt)
        iw_ref[0, j] = narT[NARROW_IW:NARROW_IW + IDX_HEADS, rows] * (IDX_HEADS * IDX_DIM) ** -0.5
        for h in range(C_HEADS):
            t = tr(cq[rows, h * LANE:(h + 1) * LANE])
            qbd_ref[0, j, h, :, :Q_BLOCK] = jnp.where(top, t, 0.0).astype(cdt)
            qbd_ref[0, j, h, :, Q_BLOCK:] = jnp.where(top, 0.0, t).astype(cdt)
    ck = rope(ck_ref[...])
    cv = cv_ref[...]
    for h in range(C_HEADS):
        kc_ref[0, 0, h] = ck[:, h * LANE:(h + 1) * LANE].astype(cdt)
        vT_ref[0, 0, h] = tr(cv[:, h * LANE:(h + 1) * LANE]).astype(cdt)
    bqk = rope(bqk_ref[...])
    rq_ref[0] = bqk[:, :B_HEADS * B_QK].astype(cdt)
    for p in range(B_HEADS * B_QK // LANE):
        t = tr(bqk[:, B_HEADS * B_QK + p * LANE:B_HEADS * B_QK + (p + 1) * LANE]) * B_QK ** -0.5
        rkT_ref[0, 2 * p] = jnp.where(top_wide, t, 0.0).astype(cdt)
        rkT_ref[0, 2 * p + 1] = jnp.where(top_wide, 0.0, t).astype(cdt)


def _mixer_prep(proj, cos, sin, kv_norm, wukT, bsz, seq):
    nt = seq // PREP_TILE
    nq = seq // Q_BLOCK
    cdt = MXU_DTYPE
    col = lambda width, c: pl.BlockSpec((PREP_TILE, width), lambda b, t: (b * nt + t, c))
    table = pl.BlockSpec((1, PREP_TILE, LANE), lambda b, t: (b, t, 0))
    per_q = lambda *tail: pl.BlockSpec((1, PREP_TILE // Q_BLOCK) + tail, lambda b, t: (b, t) + (0,) * len(tail))
    per_c = lambda *tail: pl.BlockSpec((1, 1) + tail, lambda b, t: (b, t) + (0,) * len(tail))
    sds = jax.ShapeDtypeStruct
    outs = [
        (sds((bsz, nq, A_LAT, A_HEADS * Q_BLOCK), cdt), per_q(A_LAT, A_HEADS * Q_BLOCK)),
        (sds((bsz, nq, IDX_DIM, IDX_HEADS * Q_BLOCK), cdt), per_q(IDX_DIM, IDX_HEADS * Q_BLOCK)),
        (sds((bsz, nq, IDX_HEADS, Q_BLOCK), jnp.float32), per_q(IDX_HEADS, Q_BLOCK)),
        (sds((bsz, nt, KCHUNK, IDX_DIM), cdt), per_c(KCHUNK, IDX_DIM)),
        (sds((bsz, nt, KCHUNK, A_LAT), cdt), per_c(KCHUNK, A_LAT)),
        (sds((bsz, nt, A_KV_RANK, KCHUNK), cdt), per_c(A_KV_RANK, KCHUNK)),
        (sds((bsz, nq, C_HEADS, 2 * C_DH, 2 * Q_BLOCK), cdt), per_q(C_HEADS, 2 * C_DH, 2 * Q_BLOCK)),
        (sds((bsz, nt, C_HEADS, KCHUNK, 2 * C_DH), cdt), per_c(C_HEADS, KCHUNK, 2 * C_DH)),
        (sds((bsz, nt, C_HEADS, 2 * C_DH, KCHUNK), cdt), per_c(C_HEADS, 2 * C_DH, KCHUNK)),
        (sds((bsz, seq, B_HEADS * B_QK), cdt), pl.BlockSpec((1, PREP_TILE, B_HEADS * B_QK), lambda b, t: (b, t, 0))),
        (sds((bsz, B_HEADS, LANE, seq), cdt), pl.BlockSpec((1, B_HEADS, LANE, PREP_TILE), lambda b, t: (b, 0, 0, t))),
        (sds((bsz, 2 * D_HEADS, seq), jnp.float32), pl.BlockSpec((1, 2 * D_HEADS, PREP_TILE), lambda b, t: (b, 0, t))),
    ]
    return pl.pallas_call(
        _prep_body,
        grid=(bsz, nt),
        in_specs=[col(COL, COL_B_QK), col(COL, COL_C_Q), col(COL, COL_C_K), col(COL, COL_C_V),
                  col(COL, COL_A_CKV_ROPE), col(COL, COL_A_NOPE), col(COL, COL_A_IQ), col(NARROW, COL_NARROW),
                  table, table,
                  pl.BlockSpec((1, A_KV_RANK), lambda b, t: (0, 0)),
                  pl.BlockSpec((A_HEADS, A_KV_RANK, A_NOPE), lambda b, t: (0, 0, 0))],
        out_specs=[o[1] for o in outs],
        out_shape=[o[0] for o in outs],
        compiler_params=_params("arbitrary", "arbitrary"),
        name="mixer_prep",
    )(proj, proj, proj, proj, proj, proj, proj, proj, cos, sin, kv_norm.reshape(1, A_KV_RANK), wukT)


def _rope_tables(positions):
    half = ROPE_DIM // 2
    inv_freq = ROPE_THETA ** (-jnp.arange(half, dtype=jnp.float32) / half)
    ang = positions.astype(jnp.float32)[:, :, None] * inv_freq
    cos, sin = jnp.cos(ang), jnp.sin(ang)
    reps = LANE // ROPE_DIM
    return (jnp.concatenate([cos, cos] * reps, -1), jnp.concatenate([-sin, sin] * reps, -1))


_SIGN_BIT = np.int32(-2 ** 31)
_LOW31 = np.int32(2 ** 31 - 1)
_KEY_NEG_INF = np.int32(np.array(-np.inf, np.float32).view(np.int32) ^ _LOW31)
DSA_QB = 2


def _dsa_body(iqT_ref, iw_ref, qT_ref, ik_ref, kv_ref, ckvT_ref, wuv_ref, o_ref,
              key_ref, acc_ref, thr_ref, need_ref, *, n_keep):
    f32 = jnp.float32
    blocks = range(DSA_QB)
    q0 = pl.program_id(1) * DSA_QB
    nk = ((q0 + DSA_QB) * Q_BLOCK + KCHUNK - 1) // KCHUNK
    iqT = [iqT_ref[0, j] for j in blocks]
    iw = [jnp.concatenate([iw_ref[0, j][h:h + 1] for h in range(IDX_HEADS)], axis=1) for j in blocks]
    lane_q = lax.broadcasted_iota(jnp.int32, (KCHUNK, Q_BLOCK), 1)
    qpos = [(q0 + j) * Q_BLOCK + lane_q for j in blocks]
    krow = lax.broadcasted_iota(jnp.int32, (KCHUNK, Q_BLOCK), 0)

    def index_chunk(c, carry):
        ik = ik_ref[0, c]
        lg = [jnp.dot(ik, iqT[j], preferred_element_type=f32) for j in blocks]
        w = [jnp.maximum(lg[j], 0.0) * iw[j] for j in blocks]
        for j in blocks:
            idx = w[j][:, :Q_BLOCK]
            for h in range(1, IDX_HEADS):
                idx = idx + w[j][:, h * Q_BLOCK:(h + 1) * Q_BLOCK]
            idx = jnp.where(idx == 0.0, 0.0, idx)
            idx = jnp.where(krow + c * KCHUNK <= qpos[j], idx, -jnp.inf)
            bits = pltpu.bitcast(idx, jnp.int32)
            key_ref[j, c] = bits ^ ((bits >> 31) & _LOW31)
        return carry

    lax.fori_loop(0, nk, index_chunk, 0)

    def count(cmp, ts):
        def body(c, cnts):
            hits = [cmp(key_ref[j, c], ts[j]).astype(jnp.int32) for j in blocks]
            return tuple(cnts[j] + hits[j].reshape(KCHUNK // SUBLANE, SUBLANE, Q_BLOCK).sum(0) for j in blocks)
        cnts = lax.fori_loop(0, nk, body, tuple(jnp.zeros((SUBLANE, Q_BLOCK), jnp.int32) for _ in blocks))
        return [cnt.sum(0, keepdims=True) for cnt in cnts]

    keep_all = [(q0 + j) * Q_BLOCK + Q_BLOCK <= n_keep for j in blocks]

    @pl.when(keep_all[-1])
    def _():
        for j in blocks:
            thr_ref[j] = jnp.full((1, Q_BLOCK), _KEY_NEG_INF, jnp.int32)
            need_ref[j] = jnp.zeros((1, Q_BLOCK), jnp.int32)

    @pl.when(jnp.logical_not(keep_all[-1]))
    def _():
        def bit_step(i, prefixes):
            bit = jnp.left_shift(jnp.int32(1), 31 - i)
            cands = [prefixes[j] | bit for j in blocks]
            cnts = count(lambda k, t: k >= t, [cand ^ _SIGN_BIT for cand in cands])
            return tuple(jnp.where(cnts[j] >= n_keep, cands[j], prefixes[j]) for j in blocks)
        prefixes = lax.fori_loop(0, 32, bit_step, tuple(jnp.zeros((1, Q_BLOCK), jnp.int32) for _ in blocks))
        thrs = [prefixes[j] ^ _SIGN_BIT for j in blocks]
        above = count(lambda k, t: k > t, thrs)
        for j in blocks:
            thr_ref[j] = jnp.where(keep_all[j], _KEY_NEG_INF, thrs[j])
            need_ref[j] = jnp.where(keep_all[j], 0, n_keep - above[j])

    thr = [thr_ref[j] for j in blocks]
    need = [need_ref[j].astype(f32) for j in blocks]
    qT = [qT_ref[0, j] for j in blocks]
    scale = (A_NOPE + A_ROPE) ** -0.5
    r_i = lax.broadcasted_iota(jnp.int32, (KCHUNK, KCHUNK), 0)
    c_i = lax.broadcasted_iota(jnp.int32, (KCHUNK, KCHUNK), 1)
    before = (c_i < r_i).astype(jnp.bfloat16)
    acc_ref[...] = jnp.zeros_like(acc_ref)

    def attend_chunk(c, carry):
        ms, ls, seen = carry
        kv, ckvT = kv_ref[0, c], ckvT_ref[0, c]
        s = [jnp.dot(kv, qT[j], preferred_element_type=f32) * scale for j in blocks]
        key = [key_ref[j, c] for j in blocks]
        tie = [key[j] == thr[j] for j in blocks]
        tie_f = [tie[j].astype(f32) for j in blocks]
        rank = [jnp.dot(before, tie_f[j].astype(jnp.bfloat16), preferred_element_type=f32) + seen[j]
                for j in blocks]
        bias = [jnp.where((key[j] > thr[j]) | (tie[j] & (rank[j] < need[j])), 0.0, _MASKED) for j in blocks]
        s = [s[j] + jnp.concatenate([bias[j]] * A_HEADS, axis=1) for j in blocks]
        m_new = [jnp.maximum(ms[j], s[j].max(0, keepdims=True)) for j in blocks]
        alpha = [jnp.exp(ms[j] - m_new[j]) for j in blocks]
        p = [jnp.exp(s[j] - m_new[j]) for j in blocks]
        l_new = [alpha[j] * ls[j] + p[j].sum(0, keepdims=True) for j in blocks]
        pv = [jnp.dot(ckvT, p[j].astype(ckvT.dtype), preferred_element_type=f32) for j in blocks]
        for j in blocks:
            acc_ref[j] = acc_ref[j] * alpha[j] + pv[j]
        return (tuple(m_new), tuple(l_new), tuple(seen[j] + tie_f[j].sum(0, keepdims=True) for j in blocks))

    lanes = A_HEADS * Q_BLOCK
    init = (tuple(jnp.full((1, lanes), _MASKED, f32) for _ in blocks),
            tuple(jnp.zeros((1, lanes), f32) for _ in blocks),
            tuple(jnp.zeros((1, Q_BLOCK), f32) for _ in blocks))
    _, ls, _ = lax.fori_loop(0, nk, attend_chunk, init)
    for j in blocks:
        o_latT = acc_ref[j] / ls[j]
        for h in range(A_HEADS):
            o_lat = o_latT[:, h * Q_BLOCK:(h + 1) * Q_BLOCK].T.astype(wuv_ref.dtype)
            o_ref[0, j * Q_BLOCK:(j + 1) * Q_BLOCK, h * A_V:(h + 1) * A_V] = jnp.dot(
                o_lat, wuv_ref[h], preferred_element_type=f32)


def _dsa_attention(iqT, iw, qT, ikc, kvc, ckvT, wuv, *, n_keep):
    bsz, nq = qT.shape[:2]
    nc = ikc.shape[1]
    seq = nq * Q_BLOCK
    per_q = lambda b, q: (b, q, 0, 0)
    per_b = lambda b, q: (b, 0, 0, 0)
    return pl.pallas_call(
        functools.partial(_dsa_body, n_keep=n_keep),
        grid=(bsz, nq // DSA_QB),
        in_specs=[pl.BlockSpec((1, DSA_QB) + iqT.shape[2:], per_q),
                  pl.BlockSpec((1, DSA_QB) + iw.shape[2:], per_q),
                  pl.BlockSpec((1, DSA_QB) + qT.shape[2:], per_q),
                  pl.BlockSpec((1,) + ikc.shape[1:], per_b),
                  pl.BlockSpec((1,) + kvc.shape[1:], per_b),
                  pl.BlockSpec((1,) + ckvT.shape[1:], per_b),
                  pl.BlockSpec(wuv.shape, lambda b, q: (0, 0, 0))],
        out_specs=pl.BlockSpec((1, DSA_QB * Q_BLOCK, A_HEADS * A_V), lambda b, q: (b, q, 0)),
        out_shape=jax.ShapeDtypeStruct((bsz, seq, A_HEADS * A_V), jnp.float32),
        scratch_shapes=[pltpu.VMEM((DSA_QB, nc, KCHUNK, Q_BLOCK), jnp.int32),
                        pltpu.VMEM((DSA_QB, A_KV_RANK, A_HEADS * Q_BLOCK), jnp.float32),
                        pltpu.VMEM((DSA_QB, 1, Q_BLOCK), jnp.int32),
                        pltpu.VMEM((DSA_QB, 1, Q_BLOCK), jnp.int32)],
        compiler_params=_params("arbitrary", "arbitrary"),
        name="dsa_attention",
    )(iqT, iw, qT, ikc, kvc, ckvT, wuv)


RET_TILE = 256


def _retention_body(q_ref, kT_ref, v_ref, g_ref, o_ref, s_ref):
    f32 = jnp.float32
    cdt = MXU_DTYPE
    c = RET_CHUNK

    @pl.when(pl.program_id(1) == 0)
    def _():
        s_ref[...] = jnp.zeros_like(s_ref)

    rel = (lax.broadcasted_iota(jnp.int32, (c, c), 0) - lax.broadcasted_iota(jnp.int32, (c, c), 1)).astype(f32)
    pos_c = lax.broadcasted_iota(jnp.int32, (c, 1), 0).astype(f32)
    pos_r = lax.broadcasted_iota(jnp.int32, (1, c), 1).astype(f32)

    def mm(a, b):
        return jnp.dot(a.astype(cdt), b.astype(cdt), preferred_element_type=f32)

    for h in range(B_HEADS):
        log_gamma = float(np.log(np.float32(1.0) - np.float32(2.0) ** np.float32(-5 - h)))
        intra = jnp.where(rel >= 0, jnp.exp(log_gamma * jnp.maximum(rel, 0.0)), 0.0)
        to_end = jnp.exp(log_gamma * (c - 1 - pos_r))
        from_start = jnp.exp(log_gamma * (pos_c + 1.0))
        chunk_decay = float(np.exp(np.float32(log_gamma) * np.float32(c)))
        lanes = slice(h * B_V, (h + 1) * B_V)
        pair = slice((h // 2) * LANE, (h // 2 + 1) * LANE)
        for ci in range(RET_TILE // c):
            rows = slice(ci * c, (ci + 1) * c)
            q = q_ref[0, rows, pair].astype(f32)
            kT = kT_ref[0, h, :, rows].astype(f32)
            v = v_ref[rows, lanes]
            state = s_ref[h]
            o = mm(mm(q, kT) * intra, v) + mm(q * from_start, state)
            s_ref[h] = state * chunk_decay + mm(kT * to_end, v)
            gate = g_ref[rows, lanes]
            o = o * lax.rsqrt(jnp.mean(o * o, -1, keepdims=True) + RMS_EPS)
            o_ref[0, rows, lanes] = o * (gate * _sigmoid(gate))


def _retention(rq, rkT, proj, bsz, seq):
    nt = seq // RET_TILE
    width = B_HEADS * B_V
    col = lambda c: pl.BlockSpec((RET_TILE, COL), lambda b, t: (b * nt + t, c))
    return pl.pallas_call(
        _retention_body,
        grid=(bsz, nt),
        in_specs=[pl.BlockSpec((1, RET_TILE, B_HEADS * B_QK), lambda b, t: (b, t, 0)),
                  pl.BlockSpec((1, B_HEADS, LANE, RET_TILE), lambda b, t: (b, 0, 0, t)),
                  col(COL_B_V), col(COL_B_G)],
        out_specs=pl.BlockSpec((1, RET_TILE, width), lambda b, t: (b, t, 0)),
        out_shape=jax.ShapeDtypeStruct((bsz, seq, width), jnp.float32),
        scratch_shapes=[pltpu.VMEM((B_HEADS, LANE, B_V), jnp.float32)],
        compiler_params=_params("arbitrary", "arbitrary"),
        name="retention",
    )(rq, rkT, proj, proj)


def _diff_attention_body(lam_ref, qbd_ref, k_ref, vT_ref, gain_ref, o_ref, acc_ref):
    f32 = jnp.float32
    qi = pl.program_id(1)
    nk = (qi * Q_BLOCK + Q_BLOCK + KCHUNK - 1) // KCHUNK
    qpos = qi * Q_BLOCK + lax.broadcasted_iota(jnp.int32, (KCHUNK, Q_BLOCK), 1)
    krow = lax.broadcasted_iota(jnp.int32, (KCHUNK, Q_BLOCK), 0)
    acc_ref[...] = jnp.zeros_like(acc_ref)

    heads = range(C_HEADS)

    def chunk(c, carry):
        ms, ls = carry
        causal = jnp.where(krow + c * KCHUNK <= qpos, 0.0, _MASKED)
        bias = jnp.concatenate([causal, causal], axis=1)
        s = [jnp.dot(k_ref[0, c, h], qbd_ref[0, 0, h], preferred_element_type=f32) + bias for h in heads]
        new_ms = [jnp.maximum(ms[h], s[h].max(0, keepdims=True)) for h in heads]
        alpha = [jnp.exp(ms[h] - new_ms[h]) for h in heads]
        p = [jnp.exp(s[h] - new_ms[h]) for h in heads]
        new_ls = [alpha[h] * ls[h] + p[h].sum(0, keepdims=True) for h in heads]
        pv = [jnp.dot(vT_ref[0, c, h], p[h].astype(vT_ref.dtype), preferred_element_type=f32) for h in heads]
        for h in heads:
            acc_ref[h] = acc_ref[h] * alpha[h] + pv[h]
        return tuple(new_ms), tuple(new_ls)

    lanes = 2 * Q_BLOCK
    init = (tuple(jnp.full((1, lanes), _MASKED, f32) for _ in range(C_HEADS)),
            tuple(jnp.zeros((1, lanes), f32) for _ in range(C_HEADS)))
    _, ls = lax.fori_loop(0, nk, chunk, init)
    lam = lam_ref[0]
    for h in range(C_HEADS):
        o = acc_ref[h] / ls[h]
        o = o[:, :Q_BLOCK] - lam * o[:, Q_BLOCK:]
        o = o * lax.rsqrt(jnp.mean(o * o, 0, keepdims=True) + RMS_EPS)
        o_ref[0, :, h * 2 * C_DH:(h + 1) * 2 * C_DH] = o.T * gain_ref[...]


def _diff_attention(lam, qbd, kc, vT, gain):
    bsz, nq = qbd.shape[:2]
    seq = nq * Q_BLOCK
    width = C_HEADS * 2 * C_DH
    return pl.pallas_call(
        _diff_attention_body,
        grid=(bsz, nq),
        in_specs=[pl.BlockSpec(memory_space=pltpu.SMEM),
                  pl.BlockSpec((1, 1) + qbd.shape[2:], lambda b, q: (b, q, 0, 0, 0)),
                  pl.BlockSpec((1,) + kc.shape[1:], lambda b, q: (b, 0, 0, 0, 0)),
                  pl.BlockSpec((1,) + vT.shape[1:], lambda b, q: (b, 0, 0, 0, 0)),
                  pl.BlockSpec(gain.shape, lambda b, q: (0, 0))],
        out_specs=pl.BlockSpec((1, Q_BLOCK, width), lambda b, q: (b, q, 0)),
        out_shape=jax.ShapeDtypeStruct((bsz, seq, width), jnp.float32),
        scratch_shapes=[pltpu.VMEM((C_HEADS, 2 * C_DH, 2 * Q_BLOCK), jnp.float32)],
        compiler_params=_params("arbitrary", "arbitrary"),
        name="diff_attention",
    )(lam, qbd, kc, vT, gain)


GDN_TILE = 256
GDN_SUB = 16


def _gdn_body(qkv_ref, z_ref, nar_ref, gateT_ref, conv_ref, alog_r_ref, dtb_r_ref, alog_c_ref, dtb_c_ref,
              ng_ref, o_ref, halo_ref, s_ref):
    f32 = jnp.float32
    cdt = MXU_DTYPE
    c = GDN_CHUNK

    @pl.when(pl.program_id(1) == 0)
    def _():
        halo_ref[...] = jnp.zeros_like(halo_ref)
        s_ref[...] = jnp.zeros_like(s_ref)

    x = qkv_ref[...]
    xp = jnp.concatenate([halo_ref[...], x], 0)
    w = conv_ref[...]
    pre = xp[SUBLANE - CONV_K + 1:SUBLANE - CONV_K + 1 + GDN_TILE] * w[0:1]
    for j in range(1, CONV_K):
        off = SUBLANE - CONV_K + 1 + j
        pre = pre + xp[off:off + GDN_TILE] * w[j:j + 1]
    halo_ref[...] = x[GDN_TILE - SUBLANE:]
    qkv = pre * _sigmoid(pre)

    gates = nar_ref[:, LANE:]
    beta_c = _sigmoid(gates[:, NARROW_DB:NARROW_DB + D_HEADS])
    g_c = -jnp.exp(alog_r_ref[...]) * _softplus(gates[:, NARROW_DA:NARROW_DA + D_HEADS] + dtb_r_ref[...])
    g_r = -jnp.exp(alog_c_ref[...]) * _softplus(gateT_ref[0][D_HEADS:] + dtb_c_ref[...])

    row = lax.broadcasted_iota(jnp.int32, (c, c), 0)
    col = lax.broadcasted_iota(jnp.int32, (c, c), 1)
    tril = row >= col
    strict = row > col
    same_sub = (row // GDN_SUB) == (col // GDN_SUB)
    eye = (row == col).astype(f32)
    lower_ones = tril.astype(f32)
    upper_ones = (row <= col).astype(f32)

    def mm(a, b):
        return jnp.dot(a, b, preferred_element_type=f32)

    n_chunks = GDN_TILE // c
    pairs = [(ci, h) for ci in range(n_chunks) for h in range(D_HEADS)]
    rows_of = lambda ci: slice(ci * c, (ci + 1) * c)
    g_cum_c = [mm(lower_ones, g_c[rows_of(ci)]) for ci in range(n_chunks)]
    g_cum_r = [mm(g_r[:, rows_of(ci)], upper_ones) for ci in range(n_chunks)]
    st = []
    for ci, h in pairs:
        rows = rows_of(ci)
        q = qkv[rows, h * D_K:(h + 1) * D_K]
        k = qkv[rows, D_HEADS * D_K + h * D_K:D_HEADS * D_K + (h + 1) * D_K]
        v = qkv[rows, 2 * D_HEADS * D_K + h * D_V:2 * D_HEADS * D_K + (h + 1) * D_V]
        q = q * lax.rsqrt(jnp.sum(q * q, -1, keepdims=True) + RMS_EPS) * D_K ** -0.5
        k = k * lax.rsqrt(jnp.sum(k * k, -1, keepdims=True) + RMS_EPS)
        gc = g_cum_c[ci][:, h:h + 1]
        gr = g_cum_r[ci][h:h + 1, :]
        g_last = gc[c - 1:c]
        gam = jnp.where(tril, jnp.exp(jnp.where(tril, gc - gr, 0.0)), 0.0)
        e_g = jnp.exp(gc)
        beta = beta_c[rows, h:h + 1]
        kT = k.T
        kb = k * beta
        st.append(dict(gam=gam, kT_c=kT.astype(cdt), kb_c=kb.astype(cdt), q_c=q.astype(cdt),
                       qg_c=(q * e_g).astype(cdt), rhs=jnp.concatenate([v * beta, kb * e_g], 1),
                       kdT_c=(kT * jnp.exp(g_last - gr)).astype(cdt), decay=jnp.exp(g_last)))
    for p in st:
        p["low"] = jnp.where(strict, mm(p["kb_c"], p["kT_c"]) * p["gam"], 0.0)
        p["attn_c"] = jnp.where(tril, mm(p["q_c"], p["kT_c"]) * p["gam"], 0.0).astype(cdt)
        p["low_d"] = jnp.where(same_sub, p["low"], 0.0)
        p["xk"] = -p["low_d"]
        p["td"] = eye + p["xk"]
    for _ in range(3):
        for p in st:
            p["xk"] = mm(p["xk"], p["xk"])
        for p in st:
            p["td"] = p["td"] + mm(p["td"], p["xk"])
    for p in st:
        p["m1"] = mm(p["td"], p["low"] - p["low_d"])
        p["sol"] = mm(p["td"], p["rhs"])
    for p in st:
        p["m2"] = mm(p["m1"], p["m1"])
    for p in st:
        p["sol"] = p["sol"] + mm(p["m2"], p["sol"])
    for p in st:
        p["sol"] = p["sol"] - mm(p["m1"], p["sol"])
    for ci in range(n_chunks):
        rows = rows_of(ci)
        cur = [st[ci * D_HEADS + h] for h in range(D_HEADS)]
        states = [s_ref[h] for h in range(D_HEADS)]
        states_c = [s.astype(cdt) for s in states]
        v_new_c = [(p["sol"][:, :D_V] - mm(p["sol"][:, D_V:].astype(cdt), states_c[h])).astype(cdt)
                   for h, p in enumerate(cur)]
        for h, p in enumerate(cur):
            s_ref[h] = states[h] * p["decay"] + mm(p["kdT_c"], v_new_c[h])
        for h, p in enumerate(cur):
            lanes = slice(h * D_V, (h + 1) * D_V)
            o = mm(p["qg_c"], states_c[h]) + mm(p["attn_c"], v_new_c[h])
            zz = z_ref[rows, lanes]
            o = o * lax.rsqrt(jnp.mean(o * o, -1, keepdims=True) + RMS_EPS) * ng_ref[...]
            o_ref[0, rows, lanes] = o * (zz * _sigmoid(zz))


def _gated_deltanet(proj, gatesT, conv_w, a_log, dt_bias, norm_g, bsz, seq):
    nt = seq // GDN_TILE
    width = D_HEADS * (2 * D_K + D_V)
    whole = lambda r, cc: pl.BlockSpec((r, cc), lambda b, t: (0, 0))
    return pl.pallas_call(
        _gdn_body,
        grid=(bsz, nt),
        in_specs=[pl.BlockSpec((GDN_TILE, width), lambda b, t: (b * nt + t, COL_D_QKV)),
                  pl.BlockSpec((GDN_TILE, COL), lambda b, t: (b * nt + t, COL_D_Z)),
                  pl.BlockSpec((GDN_TILE, NARROW), lambda b, t: (b * nt + t, COL_NARROW)),
                  pl.BlockSpec((1, 2 * D_HEADS, GDN_TILE), lambda b, t: (b, 0, t)),
                  whole(CONV_K, width), whole(1, D_HEADS), whole(1, D_HEADS), whole(D_HEADS, 1), whole(D_HEADS, 1),
                  whole(1, D_V)],
        out_specs=pl.BlockSpec((1, GDN_TILE, D_HEADS * D_V), lambda b, t: (b, t, 0)),
        out_shape=jax.ShapeDtypeStruct((bsz, seq, D_HEADS * D_V), jnp.float32),
        scratch_shapes=[pltpu.VMEM((SUBLANE, width), jnp.float32),
                        pltpu.VMEM((D_HEADS, D_K, D_V), jnp.float32)],
        compiler_params=_params("arbitrary", "arbitrary"),
        name="gated_deltanet",
    )(proj, proj, proj, gatesT, conv_w, a_log.reshape(1, D_HEADS), dt_bias.reshape(1, D_HEADS),
      a_log.reshape(D_HEADS, 1), dt_bias.reshape(D_HEADS, 1), norm_g.reshape(1, D_V))


def _ln_rows(t, g, b):
    mu = jnp.mean(t, -1, keepdims=True)
    d = t - mu
    var = jnp.mean(d * d, -1, keepdims=True)
    return d * lax.rsqrt(var + LN_EPS) * g + b


_HIGH16 = np.int32(-(2 ** 16))


def _pack_bf16_pairs(t):
    n = t.shape[1] // 2
    lo = pltpu.bitcast(t[:, :n].astype(jnp.bfloat16).astype(jnp.float32), jnp.int32)
    hi = pltpu.bitcast(t[:, n:].astype(jnp.bfloat16).astype(jnp.float32), jnp.int32)
    return lax.shift_right_logical(lo, 16) | (hi & _HIGH16)


def _unpack_bf16_pairs(w):
    lo = pltpu.bitcast(w << 16, jnp.float32)
    hi = pltpu.bitcast(w & _HIGH16, jnp.float32)
    return jnp.concatenate([lo, hi], axis=1).astype(jnp.bfloat16)


def _out_ln_router_body(ha_ref, hb_ref, hc_ref, hd_ref, w_ref, x_ref, g_ref, b_ref, wr_ref, br_ref,
                        h_ref, hp_ref, te_ref, tg_ref):
    f32 = jnp.float32
    acc = DEEPNORM_ALPHA * x_ref[...]
    row0 = 0
    for part in (ha_ref, hb_ref, hc_ref, hd_ref):
        width = part.shape[-1]
        acc = acc + jnp.dot(part[...].astype(MXU_DTYPE), w_ref[row0:row0 + width, :], preferred_element_type=f32)
        row0 += width
    h = _ln_rows(acc, g_ref[...], b_ref[...])
    h_ref[...] = h
    hp_ref[...] = _pack_bf16_pairs(h)
    logits = jnp.dot(h.astype(MXU_DTYPE), wr_ref[...], preferred_element_type=f32) + br_ref[...]
    lane = lax.broadcasted_iota(jnp.int32, logits.shape, 1)
    vals, te = [], jnp.zeros(logits.shape, jnp.int32)
    for k in range(TOP_K):
        m = logits.max(-1, keepdims=True)
        e = jnp.min(jnp.where(logits == m, lane, LANE), -1, keepdims=True)
        vals.append(m)
        te = jnp.where(lane == k, e, te)
        logits = jnp.where(lane == e, -jnp.inf, logits)
    ex = [jnp.exp(v - vals[0]) for v in vals]
    den = ex[0] + ex[1] + ex[2] + ex[3]
    tg = jnp.zeros(logits.shape, f32)
    for k in range(TOP_K):
        tg = jnp.where(lane == k, ex[k] / den, tg)
    te_ref[...] = te
    tg_ref[...] = tg


def _out_ln_router(head_parts, w_out_c, x2d, g, b, w_router, b_router, tm=512):
    m = x2d.shape[0]
    k, n = w_out_c.shape
    wr = jnp.zeros((n, LANE), MXU_DTYPE).at[:, :N_EXPERTS].set(w_router.astype(MXU_DTYPE))
    br = jnp.full((1, LANE), -jnp.inf, jnp.float32).at[0, :N_EXPERTS].set(b_router)
    row_blk = lambda c: pl.BlockSpec((tm, c), lambda i: (i, 0))
    whole = lambda r, c: pl.BlockSpec((r, c), lambda i: (0, 0))
    return pl.pallas_call(
        _out_ln_router_body,
        grid=(m // tm,),
        in_specs=[row_blk(p.shape[1]) for p in head_parts]
        + [whole(k, n), row_blk(n), whole(1, n), whole(1, n), whole(n, LANE), whole(1, LANE)],
        out_specs=[row_blk(n), row_blk(n // 2), row_blk(LANE), row_blk(LANE)],
        out_shape=[jax.ShapeDtypeStruct((m, n), jnp.float32),
                   jax.ShapeDtypeStruct((m, n // 2), jnp.int32),
                   jax.ShapeDtypeStruct((m, LANE), jnp.int32),
                   jax.ShapeDtypeStruct((m, LANE), jnp.float32)],
        compiler_params=_params("arbitrary"),
        name="out_proj_ln_router",
    )(*head_parts, w_out_c, x2d, g.reshape(1, n), b.reshape(1, n), wr, br)


def _clamped_swiglu(hg):
    glu, lin = hg[..., :D_FF], hg[..., D_FF:]
    glu = jnp.minimum(glu, SWIGLU_LIMIT)
    lin = jnp.clip(lin, -SWIGLU_LIMIT, SWIGLU_LIMIT)
    return glu * _sigmoid(SWIGLU_ALPHA * glu) * (lin + 1.0)


MOE_RANK_TILE = 512


def _moe_rank_body(te_ref, rank_ref, counts_ref, seen_ref):
    f32 = jnp.float32
    tm = MOE_RANK_TILE

    @pl.when(pl.program_id(0) == 0)
    def _():
        seen_ref[...] = jnp.zeros_like(seen_ref)

    te = te_ref[...]
    lane = lax.broadcasted_iota(jnp.int32, (tm, LANE), 1)
    picks = [lane == te[:, k:k + 1] for k in range(TOP_K)]
    member = picks[0].astype(f32)
    for k in range(1, TOP_K):
        member = member + picks[k].astype(f32)
    earlier = (lax.broadcasted_iota(jnp.int32, (tm, tm), 1) < lax.broadcasted_iota(jnp.int32, (tm, tm), 0))
    prefix = jnp.dot(earlier.astype(jnp.bfloat16), member.astype(jnp.bfloat16), preferred_element_type=f32)
    prefix = prefix + seen_ref[...]
    rank = jnp.zeros((tm, LANE), jnp.int32)
    for k in range(TOP_K):
        r_k = jnp.sum(jnp.where(picks[k], prefix, 0.0), -1, keepdims=True)
        rank = jnp.where(lane == k, r_k.astype(jnp.int32), rank)
    rank_ref[...] = rank
    seen = seen_ref[...] + member.sum(0, keepdims=True)
    seen_ref[...] = seen
    counts_ref[...] = seen.astype(jnp.int32)


def _moe_rank(te):
    n_tok = te.shape[0]
    tm = MOE_RANK_TILE
    return pl.pallas_call(
        _moe_rank_body,
        grid=(n_tok // tm,),
        in_specs=[pl.BlockSpec((tm, LANE), lambda i: (i, 0))],
        out_specs=[pl.BlockSpec((tm, LANE), lambda i: (i, 0)), pl.BlockSpec((1, LANE), lambda i: (0, 0))],
        out_shape=[jax.ShapeDtypeStruct((n_tok, LANE), jnp.int32), jax.ShapeDtypeStruct((1, LANE), jnp.int32)],
        scratch_shapes=[pltpu.VMEM((1, LANE), jnp.float32)],
        compiler_params=_params("arbitrary"),
        name="moe_rank",
    )(te)


def _moe_plan(te):
    top_e = te[:, :TOP_K]
    n_tok = top_e.shape[0]
    n_blocks = -(-n_tok * TOP_K // MOE_BLOCK) + N_EXPERTS + MOE_GATHER_AHEAD - 1
    rank, counts = _moe_rank(te)
    counts = counts[0, :N_EXPERTS]
    padded = (counts + MOE_BLOCK - 1) // MOE_BLOCK * MOE_BLOCK
    pad_end = jnp.cumsum(padded)
    one_hot = top_e[:, :, None] == jnp.arange(N_EXPERTS, dtype=jnp.int32)
    dest = rank[:, :TOP_K] + jnp.sum(jnp.where(one_hot, (pad_end - padded)[None, None, :], 0), -1)
    tok = jnp.broadcast_to(jnp.arange(n_tok, dtype=jnp.int32)[:, None], dest.shape)
    row_tok = jnp.zeros((n_blocks * MOE_BLOCK,), jnp.int32).at[dest.reshape(-1)].set(tok.reshape(-1))
    block_start = jnp.arange(n_blocks, dtype=jnp.int32) * MOE_BLOCK
    block_e = jnp.minimum(jnp.sum(pad_end[None, :] <= block_start[:, None], axis=1), N_EXPERTS - 1).astype(jnp.int32)
    n_used = (pad_end[-1:] // MOE_BLOCK).astype(jnp.int32)
    return dest, row_tok, block_e, n_used


def _row_gather_start(src_hbm, rows_smem, n_rows, dst, sem):
    def body(r, carry):
        pltpu.make_async_copy(src_hbm.at[pl.ds(rows_smem[0, 0, r], 1)], dst.at[pl.ds(r, 1)], sem).start()
        return carry
    lax.fori_loop(0, n_rows, body, 0, unroll=8)


def _row_gather_wait(src_hbm, n_rows, dst, sem):
    pltpu.make_async_copy(src_hbm.at[pl.ds(0, n_rows)], dst, sem).wait()


MOE_GATHER_AHEAD = 2
MOE_GATHER_BUFS = MOE_GATHER_AHEAD + 1


def _moe_expert_body(be_ref, nused_ref, tok_ref, tok_1_ref, tok_ahead_ref, h_hbm, wgu_ref, bgu_ref, wdn_ref, bdn_ref,
                     y_ref, buf_0, buf_1, buf_2, sem, wgu_c, wdn_c):
    f32 = jnp.float32
    i = pl.program_id(0)
    n_used = nused_ref[0]
    bufs = (buf_0, buf_1, buf_2)

    @pl.when(i == 0)
    def _():
        _row_gather_start(h_hbm, tok_ref, MOE_BLOCK, buf_0, sem.at[0])
        _row_gather_start(h_hbm, tok_1_ref, MOE_BLOCK, buf_1, sem.at[1])

    @pl.when(jnp.logical_or(i == 0, be_ref[i] != be_ref[jnp.maximum(i - 1, 0)]))
    def _():
        wgu_c[...] = wgu_ref[0, 0].astype(MXU_DTYPE)
        wdn_c[...] = wdn_ref[0, 0].astype(MXU_DTYPE)

    for phase in range(MOE_GATHER_BUFS):
        cur = bufs[phase]
        ahead = (phase + MOE_GATHER_AHEAD) % MOE_GATHER_BUFS

        @pl.when(jnp.logical_and(i < n_used, i % MOE_GATHER_BUFS == phase))
        def _():
            _row_gather_wait(h_hbm, MOE_BLOCK, cur, sem.at[phase])
            for r in range(MOE_BLOCK):
                pltpu.make_async_copy(h_hbm.at[pl.ds(tok_ahead_ref[0, 0, r], 1)], bufs[ahead].at[pl.ds(r, 1)],
                                      sem.at[ahead]).start()
            xb = _unpack_bf16_pairs(cur[...]).astype(MXU_DTYPE)
            hg = jnp.dot(xb, wgu_c[...], preferred_element_type=f32) + bgu_ref[0, 0]
            act = _clamped_swiglu(hg)
            y_ref[...] = jnp.dot(act.astype(MXU_DTYPE), wdn_c[...], preferred_element_type=f32) + bdn_ref[0, 0]

        @pl.when(jnp.logical_and(jnp.logical_and(i >= n_used, i < n_used + MOE_GATHER_AHEAD),
                                 i % MOE_GATHER_BUFS == phase))
        def _():
            _row_gather_wait(h_hbm, MOE_BLOCK, cur, sem.at[phase])

    @pl.when(i >= n_used)
    def _():
        y_ref[...] = jnp.zeros_like(y_ref)


def _moe_experts(h_packed, row_tok, block_e, n_used, layer, w_gu, b_gu, w_dn, b_dn):
    n_tok = h_packed.shape[0]
    d = 2 * h_packed.shape[1]
    n_blocks = block_e.shape[0]
    n_layers = w_gu.shape[0]
    assert (n_tok * TOP_K + N_EXPERTS * (MOE_BLOCK - 1)) // MOE_BLOCK + MOE_GATHER_AHEAD <= n_blocks
    tok3 = row_tok.reshape(n_blocks, 1, MOE_BLOCK)
    smem_blk = lambda imap: pl.BlockSpec((1, 1, MOE_BLOCK), imap, memory_space=pltpu.SMEM)
    by_expert = lambda shape: pl.BlockSpec((1, 1) + shape, lambda i, be, nu: (layer, be[i], 0, 0))
    gather_buf = pltpu.VMEM((MOE_BLOCK, d // 2), jnp.int32)
    return pl.pallas_call(
        _moe_expert_body,
        grid_spec=pltpu.PrefetchScalarGridSpec(
            num_scalar_prefetch=2,
            grid=(n_blocks,),
            in_specs=[smem_blk(lambda i, be, nu: (i, 0, 0)),
                      smem_blk(lambda i, be, nu: (1, 0, 0)),
                      smem_blk(lambda i, be, nu: (jnp.minimum(i + MOE_GATHER_AHEAD, n_blocks - 1), 0, 0)),
                      pl.BlockSpec(memory_space=pl.ANY),
                      by_expert((d, 2 * D_FF)), by_expert((1, 2 * D_FF)),
                      by_expert((D_FF, d)), by_expert((1, d))],
            out_specs=pl.BlockSpec((MOE_BLOCK, d), lambda i, be, nu: (i, 0)),
            scratch_shapes=[gather_buf] * MOE_GATHER_BUFS
            + [pltpu.SemaphoreType.DMA((MOE_GATHER_BUFS,)),
               pltpu.VMEM((d, 2 * D_FF), MXU_DTYPE), pltpu.VMEM((D_FF, d), MXU_DTYPE)]),
        out_shape=jax.ShapeDtypeStruct((n_blocks * MOE_BLOCK, d), jnp.float32),
        compiler_params=_params("arbitrary"),
        name="moe_experts",
    )(block_e, n_used, tok3, tok3, tok3, h_packed, w_gu, b_gu.reshape(n_layers, N_EXPERTS, 1, 2 * D_FF),
      w_dn, b_dn.reshape(n_layers, N_EXPERTS, 1, d))


MOE_COMBINE_ROWS = 128


def _moe_combine_ln_body(dest_ref, dest_next_ref, h_ref, tg_ref, g_ref, b_ref, y_hbm, o_ref,
                         buf_even, buf_odd, sem):
    tm = MOE_COMBINE_ROWS
    i = pl.program_id(0)
    n = pl.num_programs(0)
    bufs = (buf_even, buf_odd)

    def start(rows_smem, dst, s):
        for k in range(TOP_K):
            for r in range(tm):
                pltpu.make_async_copy(y_hbm.at[pl.ds(rows_smem[0, 0, k * tm + r], 1)],
                                      dst.at[k, pl.ds(r, 1)], sem.at[s]).start()

    def drain(dst, s):
        for k in range(TOP_K):
            _row_gather_wait(y_hbm, tm, dst.at[k], sem.at[s])

    @pl.when(i == 0)
    def _():
        start(dest_ref, buf_even, 0)

    for parity in (0, 1):
        cur, nxt = bufs[parity], bufs[1 - parity]

        @pl.when(i % 2 == parity)
        def _():
            drain(cur, parity)
            start(dest_next_ref, nxt, 1 - parity)
            tg = tg_ref[...]
            y = tg[:, 0:1] * cur[0]
            for k in range(1, TOP_K):
                y = y + tg[:, k:k + 1] * cur[k]
            o_ref[...] = _ln_rows(DEEPNORM_ALPHA * h_ref[...] + y, g_ref[...], b_ref[...])

        @pl.when(jnp.logical_and(i == n - 1, i % 2 == parity))
        def _():
            drain(nxt, 1 - parity)


def _moe_combine_ln(h2d, y_rows, dest, tg, g, b):
    n_tok, d = h2d.shape
    tm = MOE_COMBINE_ROWS
    nt = n_tok // tm
    dest3 = dest.reshape(nt, tm, TOP_K).transpose(0, 2, 1).reshape(nt, 1, TOP_K * tm)
    smem_blk = lambda imap: pl.BlockSpec((1, 1, TOP_K * tm), imap, memory_space=pltpu.SMEM)
    return pl.pallas_call(
        _moe_combine_ln_body,
        grid=(nt,),
        in_specs=[smem_blk(lambda i: (i, 0, 0)),
                  smem_blk(lambda i: (jnp.minimum(i + 1, nt - 1), 0, 0)),
                  pl.BlockSpec((tm, d), lambda i: (i, 0)),
                  pl.BlockSpec((tm, LANE), lambda i: (i, 0)),
                  pl.BlockSpec((1, d), lambda i: (0, 0)),
                  pl.BlockSpec((1, d), lambda i: (0, 0)),
                  pl.BlockSpec(memory_space=pl.ANY)],
        out_specs=pl.BlockSpec((tm, d), lambda i: (i, 0)),
        out_shape=jax.ShapeDtypeStruct((n_tok, d), jnp.float32),
        scratch_shapes=[pltpu.VMEM((TOP_K, tm, d), jnp.float32), pltpu.VMEM((TOP_K, tm, d), jnp.float32),
                        pltpu.SemaphoreType.DMA((2,))],
        compiler_params=_params("arbitrary"),
        name="moe_combine_ln",
    )(dest3, dest3, h2d, tg, g.reshape(1, d), b.reshape(1, d), y_rows)


def _moe_layer(h2d, h_packed, te, tg, layer, w_gu, b_gu, w_dn, b_dn, g, b):
    dest, row_tok, block_e, n_used = _moe_plan(te)
    y_rows = _moe_experts(h_packed, row_tok, block_e, n_used, layer, w_gu, b_gu, w_dn, b_dn)
    return _moe_combine_ln(h2d, y_rows, dest, tg, g, b)


def _mixers(proj, cos, sin, bsz, seq, lambda_init, a_kv_norm, a_w_uk, a_w_uv, c_lambda, c_subln,
            d_conv, d_a_log, d_dt_bias, d_norm):
    cdt = MXU_DTYPE
    (qT, iqT, iw, ikc, kvc, ckvT, qbd, kc, vT, rq, rkT, gatesT) = _mixer_prep(
        proj, cos, sin, a_kv_norm, a_w_uk.transpose(1, 0, 2).astype(cdt), bsz, seq)
    head_a = _dsa_attention(iqT, iw, qT, ikc, kvc, ckvT, a_w_uv.transpose(1, 0, 2).astype(cdt),
                            n_keep=min(INDEX_TOPK, seq // 4))
    head_b = _retention(rq, rkT, proj, bsz, seq)
    lam_full = (jnp.exp(jnp.sum(c_lambda[0] * c_lambda[1])) - jnp.exp(jnp.sum(c_lambda[2] * c_lambda[3]))
                + lambda_init)
    gain = (c_subln * (1.0 - lambda_init)).reshape(1, 2 * C_DH)
    head_c = _diff_attention(lam_full.reshape(1), qbd, kc, vT, gain)
    head_d = _gated_deltanet(proj, gatesT, d_conv, d_a_log, d_dt_bias, d_norm, bsz, seq)
    return [t.reshape(bsz * seq, t.shape[-1]) for t in (head_a, head_b, head_c, head_d)]


def kernel(x, positions, w_in, w_out, a_kv_norm, a_w_uk, a_w_uv, c_lambda, c_subln, d_conv, d_a_log, d_dt_bias, d_norm, ln1_g, ln1_b, w_router, b_router, w_gu, b_gu, w_dn, b_dn, ln2_g, ln2_b):
    bsz, seq, d = x.shape
    x2d = x.reshape(bsz * seq, d)
    cos, sin = _rope_tables(positions)
    for l in range(DEPTH):
        proj = _in_proj(x2d, _pack_w_in(w_in[l]))
        lambda_init = 0.8 - 0.6 * math.exp(-0.3 * l)
        heads = _mixers(proj, cos, sin, bsz, seq, lambda_init, a_kv_norm[l], a_w_uk[l], a_w_uv[l], c_lambda[l],
                        c_subln[l], d_conv[l], d_a_log[l], d_dt_bias[l], d_norm[l])
        h2d, h_packed, te, tg = _out_ln_router(heads, w_out[l].astype(MXU_DTYPE), x2d, ln1_g[l], ln1_b[l],
                                     w_router[l], b_router[l])
        x2d = _moe_layer(h2d, h_packed, te, tg, l, w_gu, b_gu, w_dn, b_dn, ln2_g[l], ln2_b[l])
    return x2d.reshape(bsz, seq, d)
```

```python
import functools
import math

import jax
import jax.numpy as jnp
import numpy as np
from jax import lax
from jax.experimental import pallas as pl
from jax.experimental.pallas import tpu as pltpu

D_MODEL = 2048
DEPTH = 4
A_HEADS, A_NOPE, A_ROPE, A_V, A_KV_RANK = 4, 128, 64, 128, 256
IDX_HEADS, IDX_DIM, INDEX_TOPK = 8, 64, 256
B_HEADS, B_QK, B_V, RET_CHUNK = 4, 64, 128, 128
C_HEADS, C_DH = 4, 64
D_HEADS, D_K, D_V, CONV_K, GDN_CHUNK = 4, 128, 128, 4, 64
N_EXPERTS, TOP_K, D_FF = 32, 4, 512
SWIGLU_ALPHA, SWIGLU_LIMIT = 1.702, 7.0
MOE_BLOCK = 256
ROPE_THETA = 10000.0
ROPE_DIM = 64
Q_BLOCK = 128
DEEPNORM_ALPHA = (2 * DEPTH) ** 0.25
LN_EPS = 1e-5
RMS_EPS = 1e-6
D_MIX = A_HEADS * A_V + B_HEADS * B_V + C_HEADS * 2 * C_DH + D_HEADS * D_V
A_LAT = A_KV_RANK + A_ROPE

IN_WIDTHS = (
    A_HEADS * (A_NOPE + A_ROPE), A_KV_RANK, A_ROPE, IDX_HEADS * IDX_DIM, IDX_DIM, IDX_HEADS,
    B_HEADS * B_QK, B_HEADS * B_QK, B_HEADS * B_V, B_HEADS * B_V,
    C_HEADS * 2 * C_DH, C_HEADS * 2 * C_DH, C_HEADS * 2 * C_DH,
    D_HEADS * (2 * D_K + D_V), D_HEADS * D_V, D_HEADS, D_HEADS,
)
_IN_OFFS = np.concatenate([[0], np.cumsum(IN_WIDTHS)]).tolist()

LANE = 128
SUBLANE = 8
VMEM_LIMIT = 56 * 1024 * 1024

MXU_DTYPE = jnp.bfloat16
_MASKED = -1e30

COL = 512
COL_D_QKV = 0
COL_D_Z = 3
COL_B_V = 4
COL_B_G = 5
COL_B_QK = 6
COL_C_Q = 7
COL_C_K = 8
COL_C_V = 9
COL_A_CKV_ROPE = 10
COL_A_NOPE = 11
COL_A_IQ = 12
NARROW = 256
COL_NARROW = 26
IN_PACKED = (COL_NARROW + 1) * NARROW
NARROW_IW = 0
NARROW_DB = IDX_HEADS
NARROW_DA = IDX_HEADS + D_HEADS


def _pack_w_in(w):
    g = lambda i: w[:, _IN_OFFS[i]:_IN_OFFS[i + 1]]
    a_q = g(0).reshape(w.shape[0], A_HEADS, A_NOPE + A_ROPE)
    nope = a_q[:, :, :A_NOPE].reshape(w.shape[0], A_HEADS * A_NOPE)
    rope = a_q[:, :, A_NOPE:].reshape(w.shape[0], A_HEADS * A_ROPE)
    cols = [g(13), g(14), g(8), g(9), g(6), g(7), g(10), g(11), g(12), g(1), rope, nope, g(3),
            g(2), g(4), g(5), g(15), g(16)]
    used = sum(c.shape[1] for c in cols)
    cols.append(jnp.zeros((w.shape[0], IN_PACKED - used), w.dtype))
    return jnp.concatenate(cols, axis=1).astype(MXU_DTYPE)


def _sigmoid(t):
    return 1.0 / (1.0 + jnp.exp(-t))


def _softplus(t):
    return jnp.maximum(t, 0.0) + jnp.log1p(jnp.exp(-jnp.abs(t)))


def _params(*sem):
    return pltpu.CompilerParams(dimension_semantics=sem, vmem_limit_bytes=VMEM_LIMIT)


def _in_proj_body(x_ref, w_ref, o_ref, xb_ref):
    @pl.when(pl.program_id(1) == 0)
    def _():
        xb_ref[...] = x_ref[...].astype(MXU_DTYPE)

    o_ref[...] = jnp.dot(xb_ref[...], w_ref[...], preferred_element_type=jnp.float32)


def _in_proj(x2d, w_packed, tm=1024, tn=768):
    m, k = x2d.shape
    n = w_packed.shape[1]
    return pl.pallas_call(
        _in_proj_body,
        grid=(m // tm, n // tn),
        in_specs=[pl.BlockSpec((tm, k), lambda i, j: (i, 0)),
                  pl.BlockSpec((k, tn), lambda i, j: (0, j))],
        out_specs=pl.BlockSpec((tm, tn), lambda i, j: (i, j)),
        out_shape=jax.ShapeDtypeStruct((m, n), jnp.float32),
        scratch_shapes=[pltpu.VMEM((tm, k), MXU_DTYPE)],
        compiler_params=_params("arbitrary", "arbitrary"),
        name="in_proj",
    )(x2d, w_packed)


PREP_TILE = 256
KCHUNK = 256


def _prep_body(bqk_ref, cq_ref, ck_ref, cv_ref, ckvrope_ref, nope_ref, iq_ref, nar_ref, cos_ref, sin_ref,
               kvn_ref, wukT_ref,
               qT_ref, iqT_ref, iw_ref, ik_ref, kv_ref, ckvT_ref, qbd_ref, kc_ref, vT_ref, rq_ref, rkT_ref, gT_ref):
    f32 = jnp.float32
    cdt = MXU_DTYPE
    cos, sin = cos_ref[0], sin_ref[0]
    lane = lax.broadcasted_iota(jnp.int32, (PREP_TILE, LANE), 1)
    first_half = (lane % ROPE_DIM) < ROPE_DIM // 2
    eye = (lax.broadcasted_iota(jnp.int32, (LANE, LANE), 0)
           == lax.broadcasted_iota(jnp.int32, (LANE, LANE), 1)).astype(cdt)
    top = lax.broadcasted_iota(jnp.int32, (LANE, LANE), 0) < LANE // 2
    top_wide = lax.broadcasted_iota(jnp.int32, (LANE, PREP_TILE), 0) < LANE // 2

    def rope(t):
        outs = []
        for j in range(t.shape[1] // LANE):
            ts = t[:, j * LANE:(j + 1) * LANE]
            partner = jnp.where(first_half, pltpu.roll(ts, LANE - ROPE_DIM // 2, 1), pltpu.roll(ts, ROPE_DIM // 2, 1))
            outs.append(ts * cos + partner * sin)
        return outs[0] if len(outs) == 1 else jnp.concatenate(outs, axis=1)

    def tr(t):
        return lax.dot_general(eye, t.astype(cdt), (((1,), (1,)), ((), ())), preferred_element_type=f32)

    ckv_rope = ckvrope_ref[...]
    a_ckv = ckv_rope[:, :A_KV_RANK]
    c_kv = a_ckv * lax.rsqrt(jnp.mean(a_ckv * a_ckv, -1, keepdims=True) + RMS_EPS) * kvn_ref[...]
    nar = nar_ref[...]
    kr_ik = rope(nar[:, :LANE])
    kv_ref[0, 0, :, :A_KV_RANK] = c_kv.astype(cdt)
    kv_ref[0, 0, :, A_KV_RANK:] = kr_ik[:, :A_ROPE].astype(cdt)
    ik_ref[0, 0] = kr_ik[:, A_ROPE:].astype(cdt)
    for rb in range(A_KV_RANK // LANE):
        ckvT_ref[0, 0, rb * LANE:(rb + 1) * LANE, :] = tr(c_kv[:, rb * LANE:(rb + 1) * LANE]).astype(cdt)
    q_rope = rope(ckv_rope[:, A_KV_RANK:])
    nope = nope_ref[...]
    iq = rope(iq_ref[...])
    narT = nar[:, LANE:].T
    gT_ref[0] = narT[NARROW_DB:NARROW_DB + 2 * D_HEADS]
    cq = rope(cq_ref[...]) * C_DH ** -0.5
    for j in range(PREP_TILE // Q_BLOCK):
        rows = slice(j * Q_BLOCK, (j + 1) * Q_BLOCK)
        for h in range(A_HEADS):
            q_latT = lax.dot_general(wukT_ref[h], nope[rows, h * A_NOPE:(h + 1) * A_NOPE].astype(cdt),
                                     (((1,), (1,)), ((), ())), preferred_element_type=f32)
            qT_ref[0, j, :A_KV_RANK, h * Q_BLOCK:(h + 1) * Q_BLOCK] = q_latT.astype(cdt)
        for p in range(A_HEADS * A_ROPE // LANE):
            t = tr(q_rope[rows, p * LANE:(p + 1) * LANE])
            qT_ref[0, j, A_KV_RANK:, (2 * p) * Q_BLOCK:(2 * p + 1) * Q_BLOCK] = t[:A_ROPE].astype(cdt)
            qT_ref[0, j, A_KV_RANK:, (2 * p + 1) * Q_BLOCK:(2 * p + 2) * Q_BLOCK] = t[A_ROPE:].astype(cdt)
        for p in range(IDX_HEADS * IDX_DIM // LANE):
            t = tr(iq[rows, p * LANE:(p + 1) * LANE])
            iqT_ref[0, j, :, (2 * p) * Q_BLOCK:(2 * p + 1) * Q_BLOCK] = t[:IDX_DIM].astype(cdt)
            iqT_ref[0, j, :, (2 * p + 1) * Q_BLOCK:(2 * p + 2) * Q_BLOCK] = t[IDX_DIM:].astype(cdt)
        iw_ref[0, j] = narT[NARROW_IW:NARROW_IW + IDX_HEADS, rows] * (IDX_HEADS * IDX_DIM) ** -0.5
        for h in range(C_HEADS):
            t = tr(cq[rows, h * LANE:(h + 1) * LANE])
            qbd_ref[0, j, h, :, :Q_BLOCK] = jnp.where(top, t, 0.0).astype(cdt)
            qbd_ref[0, j, h, :, Q_BLOCK:] = jnp.where(top, 0.0, t).astype(cdt)
    ck = rope(ck_ref[...])
    cv = cv_ref[...]
    for h in range(C_HEADS):
        kc_ref[0, 0, h] = ck[:, h * LANE:(h + 1) * LANE].astype(cdt)
        vT_ref[0, 0, h] = tr(cv[:, h * LANE:(h + 1) * LANE]).astype(cdt)
    bqk = rope(bqk_ref[...])
    rq_ref[0] = bqk[:, :B_HEADS * B_QK].astype(cdt)
    for p in range(B_HEADS * B_QK // LANE):
        t = tr(bqk[:, B_HEADS * B_QK + p * LANE:B_HEADS * B_QK + (p + 1) * LANE]) * B_QK ** -0.5
        rkT_ref[0, 2 * p] = jnp.where(top_wide, t, 0.0).astype(cdt)
        rkT_ref[0, 2 * p + 1] = jnp.where(top_wide, 0.0, t).astype(cdt)


def _mixer_prep(proj, cos, sin, kv_norm, wukT, bsz, seq):
    nt = seq // PREP_TILE
    nq = seq // Q_BLOCK
    cdt = MXU_DTYPE
    col = lambda width, c: pl.BlockSpec((PREP_TILE, width), lambda b, t: (b * nt + t, c))
    table = pl.BlockSpec((1, PREP_TILE, LANE), lambda b, t: (b, t, 0))
    per_q = lambda *tail: pl.BlockSpec((1, PREP_TILE // Q_BLOCK) + tail, lambda b, t: (b, t) + (0,) * len(tail))
    per_c = lambda *tail: pl.BlockSpec((1, 1) + tail, lambda b, t: (b, t) + (0,) * len(tail))
    sds = jax.ShapeDtypeStruct
    outs = [
        (sds((bsz, nq, A_LAT, A_HEADS * Q_BLOCK), cdt), per_q(A_LAT, A_HEADS * Q_BLOCK)),
        (sds((bsz, nq, IDX_DIM, IDX_HEADS * Q_BLOCK), cdt), per_q(IDX_DIM, IDX_HEADS * Q_BLOCK)),
        (sds((bsz, nq, IDX_HEADS, Q_BLOCK), jnp.float32), per_q(IDX_HEADS, Q_BLOCK)),
        (sds((bsz, nt, KCHUNK, IDX_DIM), cdt), per_c(KCHUNK, IDX_DIM)),
        (sds((bsz, nt, KCHUNK, A_LAT), cdt), per_c(KCHUNK, A_LAT)),
        (sds((bsz, nt, A_KV_RANK, KCHUNK), cdt), per_c(A_KV_RANK, KCHUNK)),
        (sds((bsz, nq, C_HEADS, 2 * C_DH, 2 * Q_BLOCK), cdt), per_q(C_HEADS, 2 * C_DH, 2 * Q_BLOCK)),
        (sds((bsz, nt, C_HEADS, KCHUNK, 2 * C_DH), cdt), per_c(C_HEADS, KCHUNK, 2 * C_DH)),
        (sds((bsz, nt, C_HEADS, 2 * C_DH, KCHUNK), cdt), per_c(C_HEADS, 2 * C_DH, KCHUNK)),
        (sds((bsz, seq, B_HEADS * B_QK), cdt), pl.BlockSpec((1, PREP_TILE, B_HEADS * B_QK), lambda b, t: (b, t, 0))),
        (sds((bsz, B_HEADS, LANE, seq), cdt), pl.BlockSpec((1, B_HEADS, LANE, PREP_TILE), lambda b, t: (b, 0, 0, t))),
        (sds((bsz, 2 * D_HEADS, seq), jnp.float32), pl.BlockSpec((1, 2 * D_HEADS, PREP_TILE), lambda b, t: (b, 0, t))),
    ]
    return pl.pallas_call(
        _prep_body,
        grid=(bsz, nt),
        in_specs=[col(COL, COL_B_QK), col(COL, COL_C_Q), col(COL, COL_C_K), col(COL, COL_C_V),
                  col(COL, COL_A_CKV_ROPE), col(COL, COL_A_NOPE), col(COL, COL_A_IQ), col(NARROW, COL_NARROW),
                  table, table,
                  pl.BlockSpec((1, A_KV_RANK), lambda b, t: (0, 0)),
                  pl.BlockSpec((A_HEADS, A_KV_RANK, A_NOPE), lambda b, t: (0, 0, 0))],
        out_specs=[o[1] for o in outs],
        out_shape=[o[0] for o in outs],
        compiler_params=_params("arbitrary", "arbitrary"),
        name="mixer_prep",
    )(proj, proj, proj, proj, proj, proj, proj, proj, cos, sin, kv_norm.reshape(1, A_KV_RANK), wukT)


def _rope_tables(positions):
    half = ROPE_DIM // 2
    inv_freq = ROPE_THETA ** (-jnp.arange(half, dtype=jnp.float32) / half)
    ang = positions.astype(jnp.float32)[:, :, None] * inv_freq
    cos, sin = jnp.cos(ang), jnp.sin(ang)
    reps = LANE // ROPE_DIM
    return (jnp.concatenate([cos, cos] * reps, -1), jnp.concatenate([-sin, sin] * reps, -1))


_SIGN_BIT = np.int32(-2 ** 31)
_LOW31 = np.int32(2 ** 31 - 1)
_KEY_NEG_INF = np.int32(np.array(-np.inf, np.float32).view(np.int32) ^ _LOW31)
DSA_QB = 2


def _dsa_body(iqT_ref, iw_ref, qT_ref, ik_ref, kv_ref, ckvT_ref, wuv_ref, o_ref,
              key_ref, acc_ref, thr_ref, need_ref, *, n_keep):
    f32 = jnp.float32
    blocks = range(DSA_QB)
    q0 = pl.program_id(1) * DSA_QB
    nk = ((q0 + DSA_QB) * Q_BLOCK + KCHUNK - 1) // KCHUNK
    iqT = [iqT_ref[0, j] for j in blocks]
    iw = [jnp.concatenate([iw_ref[0, j][h:h + 1] for h in range(IDX_HEADS)], axis=1) for j in blocks]
    lane_q = lax.broadcasted_iota(jnp.int32, (KCHUNK, Q_BLOCK), 1)
    qpos = [(q0 + j) * Q_BLOCK + lane_q for j in blocks]
    krow = lax.broadcasted_iota(jnp.int32, (KCHUNK, Q_BLOCK), 0)

    def index_chunk(c, carry):
        ik = ik_ref[0, c]
        lg = [jnp.dot(ik, iqT[j], preferred_element_type=f32) for j in blocks]
        w = [jnp.maximum(lg[j], 0.0) * iw[j] for j in blocks]
        for j in blocks:
            idx = w[j][:, :Q_BLOCK]
            for h in range(1, IDX_HEADS):
                idx = idx + w[j][:, h * Q_BLOCK:(h + 1) * Q_BLOCK]
            idx = jnp.where(idx == 0.0, 0.0, idx)
            idx = jnp.where(krow + c * KCHUNK <= qpos[j], idx, -jnp.inf)
            bits = pltpu.bitcast(idx, jnp.int32)
            key_ref[j, c] = bits ^ ((bits >> 31) & _LOW31)
        return carry

    lax.fori_loop(0, nk, index_chunk, 0)

    def count(cmp, ts):
        def body(c, cnts):
            hits = [cmp(key_ref[j, c], ts[j]).astype(jnp.int32) for j in blocks]
            return tuple(cnts[j] + hits[j].reshape(KCHUNK // SUBLANE, SUBLANE, Q_BLOCK).sum(0) for j in blocks)
        cnts = lax.fori_loop(0, nk, body, tuple(jnp.zeros((SUBLANE, Q_BLOCK), jnp.int32) for _ in blocks))
        return [cnt.sum(0, keepdims=True) for cnt in cnts]

    keep_all = [(q0 + j) * Q_BLOCK + Q_BLOCK <= n_keep for j in blocks]

    @pl.when(keep_all[-1])
    def _():
        for j in blocks:
            thr_ref[j] = jnp.full((1, Q_BLOCK), _KEY_NEG_INF, jnp.int32)
            need_ref[j] = jnp.zeros((1, Q_BLOCK), jnp.int32)

    @pl.when(jnp.logical_not(keep_all[-1]))
    def _():
        def bit_step(i, prefixes):
            bit = jnp.left_shift(jnp.int32(1), 31 - i)
            cands = [prefixes[j] | bit for j in blocks]
            cnts = count(lambda k, t: k >= t, [cand ^ _SIGN_BIT for cand in cands])
            return tuple(jnp.where(cnts[j] >= n_keep, cands[j], prefixes[j]) for j in blocks)
        prefixes = lax.fori_loop(0, 32, bit_step, tuple(jnp.zeros((1, Q_BLOCK), jnp.int32) for _ in blocks))
        thrs = [prefixes[j] ^ _SIGN_BIT for j in blocks]
        above = count(lambda k, t: k > t, thrs)
        for j in blocks:
            thr_ref[j] = jnp.where(keep_all[j], _KEY_NEG_INF, thrs[j])
            need_ref[j] = jnp.where(keep_all[j], 0, n_keep - above[j])

    thr = [thr_ref[j] for j in blocks]
    need = [need_ref[j].astype(f32) for j in blocks]
    qT = [qT_ref[0, j] for j in blocks]
    scale = (A_NOPE + A_ROPE) ** -0.5
    r_i = lax.broadcasted_iota(jnp.int32, (KCHUNK, KCHUNK), 0)
    c_i = lax.broadcasted_iota(jnp.int32, (KCHUNK, KCHUNK), 1)
    before = (c_i < r_i).astype(jnp.bfloat16)
    acc_ref[...] = jnp.zeros_like(acc_ref)

    def attend_chunk(c, carry):
        ms, ls, seen = carry
        kv, ckvT = kv_ref[0, c], ckvT_ref[0, c]
        s = [jnp.dot(kv, qT[j], preferred_element_type=f32) * scale for j in blocks]
        key = [key_ref[j, c] for j in blocks]
        tie = [key[j] == thr[j] for j in blocks]
        tie_f = [tie[j].astype(f32) for j in blocks]
        rank = [jnp.dot(before, tie_f[j].astype(jnp.bfloat16), preferred_element_type=f32) + seen[j]
                for j in blocks]
        bias = [jnp.where((key[j] > thr[j]) | (tie[j] & (rank[j] < need[j])), 0.0, _MASKED) for j in blocks]
        s = [s[j] + jnp.concatenate([bias[j]] * A_HEADS, axis=1) for j in blocks]
        m_new = [jnp.maximum(ms[j], s[j].max(0, keepdims=True)) for j in blocks]
        alpha = [jnp.exp(ms[j] - m_new[j]) for j in blocks]
        p = [jnp.exp(s[j] - m_new[j]) for j in blocks]
        l_new = [alpha[j] * ls[j] + p[j].sum(0, keepdims=True) for j in blocks]
        pv = [jnp.dot(ckvT, p[j].astype(ckvT.dtype), preferred_element_type=f32) for j in blocks]
        for j in blocks:
            acc_ref[j] = acc_ref[j] * alpha[j] + pv[j]
        return (tuple(m_new), tuple(l_new), tuple(seen[j] + tie_f[j].sum(0, keepdims=True) for j in blocks))

    lanes = A_HEADS * Q_BLOCK
    init = (tuple(jnp.full((1, lanes), _MASKED, f32) for _ in blocks),
            tuple(jnp.zeros((1, lanes), f32) for _ in blocks),
            tuple(jnp.zeros((1, Q_BLOCK), f32) for _ in blocks))
    _, ls, _ = lax.fori_loop(0, nk, attend_chunk, init)
    for j in blocks:
        o_latT = acc_ref[j] / ls[j]
        for h in range(A_HEADS):
            o_lat = o_latT[:, h * Q_BLOCK:(h + 1) * Q_BLOCK].T.astype(wuv_ref.dtype)
            o_ref[0, j * Q_BLOCK:(j + 1) * Q_BLOCK, h * A_V:(h + 1) * A_V] = jnp.dot(
                o_lat, wuv_ref[h], preferred_element_type=f32)


def _dsa_attention(iqT, iw, qT, ikc, kvc, ckvT, wuv, *, n_keep):
    bsz, nq = qT.shape[:2]
    nc = ikc.shape[1]
    seq = nq * Q_BLOCK
    per_q = lambda b, q: (b, q, 0, 0)
    per_b = lambda b, q: (b, 0, 0, 0)
    return pl.pallas_call(
        functools.partial(_dsa_body, n_keep=n_keep),
        grid=(bsz, nq // DSA_QB),
        in_specs=[pl.BlockSpec((1, DSA_QB) + iqT.shape[2:], per_q),
                  pl.BlockSpec((1, DSA_QB) + iw.shape[2:], per_q),
                  pl.BlockSpec((1, DSA_QB) + qT.shape[2:], per_q),
                  pl.BlockSpec((1,) + ikc.shape[1:], per_b),
                  pl.BlockSpec((1,) + kvc.shape[1:], per_b),
                  pl.BlockSpec((1,) + ckvT.shape[1:], per_b),
                  pl.BlockSpec(wuv.shape, lambda b, q: (0, 0, 0))],
        out_specs=pl.BlockSpec((1, DSA_QB * Q_BLOCK, A_HEADS * A_V), lambda b, q: (b, q, 0)),
        out_shape=jax.ShapeDtypeStruct((bsz, seq, A_HEADS * A_V), jnp.float32),
        scratch_shapes=[pltpu.VMEM((DSA_QB, nc, KCHUNK, Q_BLOCK), jnp.int32),
                        pltpu.VMEM((DSA_QB, A_KV_RANK, A_HEADS * Q_BLOCK), jnp.float32),
                        pltpu.VMEM((DSA_QB, 1, Q_BLOCK), jnp.int32),
                        pltpu.VMEM((DSA_QB, 1, Q_BLOCK), jnp.int32)],
        compiler_params=_params("arbitrary", "arbitrary"),
        name="dsa_attention",
    )(iqT, iw, qT, ikc, kvc, ckvT, wuv)


RET_TILE = 256


def _retention_body(q_ref, kT_ref, v_ref, g_ref, o_ref, s_ref):
    f32 = jnp.float32
    cdt = MXU_DTYPE
    c = RET_CHUNK

    @pl.when(pl.program_id(1) == 0)
    def _():
        s_ref[...] = jnp.zeros_like(s_ref)

    rel = (lax.broadcasted_iota(jnp.int32, (c, c), 0) - lax.broadcasted_iota(jnp.int32, (c, c), 1)).astype(f32)
    pos_c = lax.broadcasted_iota(jnp.int32, (c, 1), 0).astype(f32)
    pos_r = lax.broadcasted_iota(jnp.int32, (1, c), 1).astype(f32)

    def mm(a, b):
        return jnp.dot(a.astype(cdt), b.astype(cdt), preferred_element_type=f32)

    for h in range(B_HEADS):
        log_gamma = float(np.log(np.float32(1.0) - np.float32(2.0) ** np.float32(-5 - h)))
        intra = jnp.where(rel >= 0, jnp.exp(log_gamma * jnp.maximum(rel, 0.0)), 0.0)
        to_end = jnp.exp(log_gamma * (c - 1 - pos_r))
        from_start = jnp.exp(log_gamma * (pos_c + 1.0))
        chunk_decay = float(np.exp(np.float32(log_gamma) * np.float32(c)))
        lanes = slice(h * B_V, (h + 1) * B_V)
        pair = slice((h // 2) * LANE, (h // 2 + 1) * LANE)
        for ci in range(RET_TILE // c):
            rows = slice(ci * c, (ci + 1) * c)
            q = q_ref[0, rows, pair].astype(f32)
            kT = kT_ref[0, h, :, rows].astype(f32)
            v = v_ref[rows, lanes]
            state = s_ref[h]
            o = mm(mm(q, kT) * intra, v) + mm(q * from_start, state)
            s_ref[h] = state * chunk_decay + mm(kT * to_end, v)
            gate = g_ref[rows, lanes]
            o = o * lax.rsqrt(jnp.mean(o * o, -1, keepdims=True) + RMS_EPS)
            o_ref[0, rows, lanes] = o * (gate * _sigmoid(gate))


def _retention(rq, rkT, proj, bsz, seq):
    nt = seq // RET_TILE
    width = B_HEADS * B_V
    col = lambda c: pl.BlockSpec((RET_TILE, COL), lambda b, t: (b * nt + t, c))
    return pl.pallas_call(
        _retention_body,
        grid=(bsz, nt),
        in_specs=[pl.BlockSpec((1, RET_TILE, B_HEADS * B_QK), lambda b, t: (b, t, 0)),
                  pl.BlockSpec((1, B_HEADS, LANE, RET_TILE), lambda b, t: (b, 0, 0, t)),
                  col(COL_B_V), col(COL_B_G)],
        out_specs=pl.BlockSpec((1, RET_TILE, width), lambda b, t: (b, t, 0)),
        out_shape=jax.ShapeDtypeStruct((bsz, seq, width), jnp.float32),
        scratch_shapes=[pltpu.VMEM((B_HEADS, LANE, B_V), jnp.float32)],
        compiler_params=_params("arbitrary", "arbitrary"),
        name="retention",
    )(rq, rkT, proj, proj)


DIFF_QB = 2


def _diff_attention_body(lam_ref, qbd_ref, k_ref, vT_ref, gain_ref, o_ref, acc_ref):
    f32 = jnp.float32
    blocks = range(DIFF_QB)
    heads = range(C_HEADS)
    q0 = pl.program_id(1) * DIFF_QB
    nk = ((q0 + DIFF_QB) * Q_BLOCK + KCHUNK - 1) // KCHUNK
    lane_q = lax.broadcasted_iota(jnp.int32, (KCHUNK, Q_BLOCK), 1)
    krow = lax.broadcasted_iota(jnp.int32, (KCHUNK, Q_BLOCK), 0)
    q_all = [jnp.concatenate([qbd_ref[0, j, h] for j in blocks], axis=1) for h in heads]
    acc_ref[...] = jnp.zeros_like(acc_ref)

    def chunk(c, carry):
        ms, ls = carry
        causal = [jnp.where(krow + c * KCHUNK <= (q0 + j) * Q_BLOCK + lane_q, 0.0, _MASKED) for j in blocks]
        bias = jnp.concatenate([causal[j] for j in blocks for _ in range(2)], axis=1)
        s = [jnp.dot(k_ref[0, c, h], q_all[h], preferred_element_type=f32) + bias for h in heads]
        new_ms = [jnp.maximum(ms[h], s[h].max(0, keepdims=True)) for h in heads]
        alpha = [jnp.exp(ms[h] - new_ms[h]) for h in heads]
        p = [jnp.exp(s[h] - new_ms[h]) for h in heads]
        new_ls = [alpha[h] * ls[h] + p[h].sum(0, keepdims=True) for h in heads]
        pv = [jnp.dot(vT_ref[0, c, h], p[h].astype(vT_ref.dtype), preferred_element_type=f32) for h in heads]
        for h in heads:
            acc_ref[h] = acc_ref[h] * alpha[h] + pv[h]
        return tuple(new_ms), tuple(new_ls)

    lanes = DIFF_QB * 2 * Q_BLOCK
    init = (tuple(jnp.full((1, lanes), _MASKED, f32) for _ in heads),
            tuple(jnp.zeros((1, lanes), f32) for _ in heads))
    _, ls = lax.fori_loop(0, nk, chunk, init)
    lam = lam_ref[0]
    for h in heads:
        o_all = acc_ref[h] / ls[h]
        for j in blocks:
            o1 = o_all[:, (2 * j) * Q_BLOCK:(2 * j + 1) * Q_BLOCK]
            o2 = o_all[:, (2 * j + 1) * Q_BLOCK:(2 * j + 2) * Q_BLOCK]
            o = o1 - lam * o2
            o = o * lax.rsqrt(jnp.mean(o * o, 0, keepdims=True) + RMS_EPS)
            o_ref[0, j * Q_BLOCK:(j + 1) * Q_BLOCK, h * 2 * C_DH:(h + 1) * 2 * C_DH] = o.T * gain_ref[...]


def _diff_attention(lam, qbd, kc, vT, gain):
    bsz, nq = qbd.shape[:2]
    seq = nq * Q_BLOCK
    width = C_HEADS * 2 * C_DH
    return pl.pallas_call(
        _diff_attention_body,
        grid=(bsz, nq // DIFF_QB),
        in_specs=[pl.BlockSpec(memory_space=pltpu.SMEM),
                  pl.BlockSpec((1, DIFF_QB) + qbd.shape[2:], lambda b, q: (b, q, 0, 0, 0)),
                  pl.BlockSpec((1,) + kc.shape[1:], lambda b, q: (b, 0, 0, 0, 0)),
                  pl.BlockSpec((1,) + vT.shape[1:], lambda b, q: (b, 0, 0, 0, 0)),
                  pl.BlockSpec(gain.shape, lambda b, q: (0, 0))],
        out_specs=pl.BlockSpec((1, DIFF_QB * Q_BLOCK, width), lambda b, q: (b, q, 0)),
        out_shape=jax.ShapeDtypeStruct((bsz, seq, width), jnp.float32),
        scratch_shapes=[pltpu.VMEM((C_HEADS, 2 * C_DH, DIFF_QB * 2 * Q_BLOCK), jnp.float32)],
        compiler_params=_params("arbitrary", "arbitrary"),
        name="diff_attention",
    )(lam, qbd, kc, vT, gain)


GDN_TILE = 256
GDN_SUB = 16


def _gdn_body(qkv_ref, z_ref, nar_ref, gateT_ref, conv_ref, alog_r_ref, dtb_r_ref, alog_c_ref, dtb_c_ref,
              ng_ref, o_ref, halo_ref, s_ref):
    f32 = jnp.float32
    cdt = MXU_DTYPE
    c = GDN_CHUNK

    @pl.when(pl.program_id(1) == 0)
    def _():
        halo_ref[...] = jnp.zeros_like(halo_ref)
        s_ref[...] = jnp.zeros_like(s_ref)

    x = qkv_ref[...]
    xp = jnp.concatenate([halo_ref[...], x], 0)
    w = conv_ref[...]
    pre = xp[SUBLANE - CONV_K + 1:SUBLANE - CONV_K + 1 + GDN_TILE] * w[0:1]
    for j in range(1, CONV_K):
        off = SUBLANE - CONV_K + 1 + j
        pre = pre + xp[off:off + GDN_TILE] * w[j:j + 1]
    halo_ref[...] = x[GDN_TILE - SUBLANE:]
    qkv = pre * _sigmoid(pre)

    gates = nar_ref[:, LANE:]
    beta_c = _sigmoid(gates[:, NARROW_DB:NARROW_DB + D_HEADS])
    g_c = -jnp.exp(alog_r_ref[...]) * _softplus(gates[:, NARROW_DA:NARROW_DA + D_HEADS] + dtb_r_ref[...])
    g_r = -jnp.exp(alog_c_ref[...]) * _softplus(gateT_ref[0][D_HEADS:] + dtb_c_ref[...])

    row = lax.broadcasted_iota(jnp.int32, (c, c), 0)
    col = lax.broadcasted_iota(jnp.int32, (c, c), 1)
    tril = row >= col
    strict = row > col
    same_sub = (row // GDN_SUB) == (col // GDN_SUB)
    eye = (row == col).astype(f32)
    lower_ones = tril.astype(f32)
    upper_ones = (row <= col).astype(f32)

    def mm(a, b):
        return jnp.dot(a, b, preferred_element_type=f32)

    n_chunks = GDN_TILE // c
    pairs = [(ci, h) for ci in range(n_chunks) for h in range(D_HEADS)]
    rows_of = lambda ci: slice(ci * c, (ci + 1) * c)
    g_cum_c = [mm(lower_ones, g_c[rows_of(ci)]) for ci in range(n_chunks)]
    g_cum_r = [mm(g_r[:, rows_of(ci)], upper_ones) for ci in range(n_chunks)]
    st = []
    for ci, h in pairs:
        rows = rows_of(ci)
        q = qkv[rows, h * D_K:(h + 1) * D_K]
        k = qkv[rows, D_HEADS * D_K + h * D_K:D_HEADS * D_K + (h + 1) * D_K]
        v = qkv[rows, 2 * D_HEADS * D_K + h * D_V:2 * D_HEADS * D_K + (h + 1) * D_V]
        q = q * lax.rsqrt(jnp.sum(q * q, -1, keepdims=True) + RMS_EPS) * D_K ** -0.5
        k = k * lax.rsqrt(jnp.sum(k * k, -1, keepdims=True) + RMS_EPS)
        gc = g_cum_c[ci][:, h:h + 1]
        gr = g_cum_r[ci][h:h + 1, :]
        g_last = gc[c - 1:c]
        gam = jnp.where(tril, jnp.exp(jnp.where(tril, gc - gr, 0.0)), 0.0)
        e_g = jnp.exp(gc)
        beta = beta_c[rows, h:h + 1]
        kT = k.T
        kb = k * beta
        st.append(dict(gam=gam, kT_c=kT.astype(cdt), kb_c=kb.astype(cdt), q_c=q.astype(cdt),
                       qg_c=(q * e_g).astype(cdt), rhs=jnp.concatenate([v * beta, kb * e_g], 1),
                       kdT_c=(kT * jnp.exp(g_last - gr)).astype(cdt), decay=jnp.exp(g_last)))
    for p in st:
        p["low"] = jnp.where(strict, mm(p["kb_c"], p["kT_c"]) * p["gam"], 0.0)
        p["attn_c"] = jnp.where(tril, mm(p["q_c"], p["kT_c"]) * p["gam"], 0.0).astype(cdt)
        p["low_d"] = jnp.where(same_sub, p["low"], 0.0)
        p["xk"] = -p["low_d"]
        p["td"] = eye + p["xk"]
    for _ in range(3):
        for p in st:
            p["xk"] = mm(p["xk"], p["xk"])
        for p in st:
            p["td"] = p["td"] + mm(p["td"], p["xk"])
    for p in st:
        p["m1"] = mm(p["td"], p["low"] - p["low_d"])
        p["sol"] = mm(p["td"], p["rhs"])
    for p in st:
        p["m2"] = mm(p["m1"], p["m1"])
    for p in st:
        p["sol"] = p["sol"] + mm(p["m2"], p["sol"])
    for p in st:
        p["sol"] = p["sol"] - mm(p["m1"], p["sol"])
    for ci in range(n_chunks):
        rows = rows_of(ci)
        cur = [st[ci * D_HEADS + h] for h in range(D_HEADS)]
        states = [s_ref[h] for h in range(D_HEADS)]
        states_c = [s.astype(cdt) for s in states]
        v_new_c = [(p["sol"][:, :D_V] - mm(p["sol"][:, D_V:].astype(cdt), states_c[h])).astype(cdt)
                   for h, p in enumerate(cur)]
        for h, p in enumerate(cur):
            s_ref[h] = states[h] * p["decay"] + mm(p["kdT_c"], v_new_c[h])
        for h, p in enumerate(cur):
            lanes = slice(h * D_V, (h + 1) * D_V)
            o = mm(p["qg_c"], states_c[h]) + mm(p["attn_c"], v_new_c[h])
            zz = z_ref[rows, lanes]
            o = o * lax.rsqrt(jnp.mean(o * o, -1, keepdims=True) + RMS_EPS) * ng_ref[...]
            o_ref[0, rows, lanes] = o * (zz * _sigmoid(zz))


def _gated_deltanet(proj, gatesT, conv_w, a_log, dt_bias, norm_g, bsz, seq):
    nt = seq // GDN_TILE
    width = D_HEADS * (2 * D_K + D_V)
    whole = lambda r, cc: pl.BlockSpec((r, cc), lambda b, t: (0, 0))
    return pl.pallas_call(
        _gdn_body,
        grid=(bsz, nt),
        in_specs=[pl.BlockSpec((GDN_TILE, width), lambda b, t: (b * nt + t, COL_D_QKV)),
                  pl.BlockSpec((GDN_TILE, COL), lambda b, t: (b * nt + t, COL_D_Z)),
                  pl.BlockSpec((GDN_TILE, NARROW), lambda b, t: (b * nt + t, COL_NARROW)),
                  pl.BlockSpec((1, 2 * D_HEADS, GDN_TILE), lambda b, t: (b, 0, t)),
                  whole(CONV_K, width), whole(1, D_HEADS), whole(1, D_HEADS), whole(D_HEADS, 1), whole(D_HEADS, 1),
                  whole(1, D_V)],
        out_specs=pl.BlockSpec((1, GDN_TILE, D_HEADS * D_V), lambda b, t: (b, t, 0)),
        out_shape=jax.ShapeDtypeStruct((bsz, seq, D_HEADS * D_V), jnp.float32),
        scratch_shapes=[pltpu.VMEM((SUBLANE, width), jnp.float32),
                        pltpu.VMEM((D_HEADS, D_K, D_V), jnp.float32)],
        compiler_params=_params("arbitrary", "arbitrary"),
        name="gated_deltanet",
    )(proj, proj, proj, gatesT, conv_w, a_log.reshape(1, D_HEADS), dt_bias.reshape(1, D_HEADS),
      a_log.reshape(D_HEADS, 1), dt_bias.reshape(D_HEADS, 1), norm_g.reshape(1, D_V))


def _ln_rows(t, g, b):
    mu = jnp.mean(t, -1, keepdims=True)
    d = t - mu
    var = jnp.mean(d * d, -1, keepdims=True)
    return d * lax.rsqrt(var + LN_EPS) * g + b


_HIGH16 = np.int32(-(2 ** 16))


def _pack_bf16_pairs(t):
    n = t.shape[1] // 2
    lo = pltpu.bitcast(t[:, :n].astype(jnp.bfloat16).astype(jnp.float32), jnp.int32)
    hi = pltpu.bitcast(t[:, n:].astype(jnp.bfloat16).astype(jnp.float32), jnp.int32)
    return lax.shift_right_logical(lo, 16) | (hi & _HIGH16)


def _unpack_bf16_pairs(w):
    lo = pltpu.bitcast(w << 16, jnp.float32)
    hi = pltpu.bitcast(w & _HIGH16, jnp.float32)
    return jnp.concatenate([lo, hi], axis=1).astype(jnp.bfloat16)


def _out_ln_router_body(ha_ref, hb_ref, hc_ref, hd_ref, w_ref, x_ref, g_ref, b_ref, wr_ref, br_ref,
                        h_ref, hp_ref, te_ref, tg_ref):
    f32 = jnp.float32
    acc = DEEPNORM_ALPHA * x_ref[...]
    row0 = 0
    for part in (ha_ref, hb_ref, hc_ref, hd_ref):
        width = part.shape[-1]
        acc = acc + jnp.dot(part[...].astype(MXU_DTYPE), w_ref[row0:row0 + width, :], preferred_element_type=f32)
        row0 += width
    h = _ln_rows(acc, g_ref[...], b_ref[...])
    h_ref[...] = h
    hp_ref[...] = _pack_bf16_pairs(h)
    logits = jnp.dot(h.astype(MXU_DTYPE), wr_ref[...], preferred_element_type=f32) + br_ref[...]
    lane = lax.broadcasted_iota(jnp.int32, logits.shape, 1)
    vals, te = [], jnp.zeros(logits.shape, jnp.int32)
    for k in range(TOP_K):
        m = logits.max(-1, keepdims=True)
        e = jnp.min(jnp.where(logits == m, lane, LANE), -1, keepdims=True)
        vals.append(m)
        te = jnp.where(lane == k, e, te)
        logits = jnp.where(lane == e, -jnp.inf, logits)
    ex = [jnp.exp(v - vals[0]) for v in vals]
    den = ex[0] + ex[1] + ex[2] + ex[3]
    tg = jnp.zeros(logits.shape, f32)
    for k in range(TOP_K):
        tg = jnp.where(lane == k, ex[k] / den, tg)
    te_ref[...] = te
    tg_ref[...] = tg


def _out_ln_router(head_parts, w_out_c, x2d, g, b, w_router, b_router, tm=512):
    m = x2d.shape[0]
    k, n = w_out_c.shape
    wr = jnp.zeros((n, LANE), MXU_DTYPE).at[:, :N_EXPERTS].set(w_router.astype(MXU_DTYPE))
    br = jnp.full((1, LANE), -jnp.inf, jnp.float32).at[0, :N_EXPERTS].set(b_router)
    row_blk = lambda c: pl.BlockSpec((tm, c), lambda i: (i, 0))
    whole = lambda r, c: pl.BlockSpec((r, c), lambda i: (0, 0))
    return pl.pallas_call(
        _out_ln_router_body,
        grid=(m // tm,),
        in_specs=[row_blk(p.shape[1]) for p in head_parts]
        + [whole(k, n), row_blk(n), whole(1, n), whole(1, n), whole(n, LANE), whole(1, LANE)],
        out_specs=[row_blk(n), row_blk(n // 2), row_blk(LANE), row_blk(LANE)],
        out_shape=[jax.ShapeDtypeStruct((m, n), jnp.float32),
                   jax.ShapeDtypeStruct((m, n // 2), jnp.int32),
                   jax.ShapeDtypeStruct((m, LANE), jnp.int32),
                   jax.ShapeDtypeStruct((m, LANE), jnp.float32)],
        compiler_params=_params("arbitrary"),
        name="out_proj_ln_router",
    )(*head_parts, w_out_c, x2d, g.reshape(1, n), b.reshape(1, n), wr, br)


def _clamped_swiglu(hg):
    glu, lin = hg[..., :D_FF], hg[..., D_FF:]
    glu = jnp.minimum(glu, SWIGLU_LIMIT)
    lin = jnp.clip(lin, -SWIGLU_LIMIT, SWIGLU_LIMIT)
    return glu * _sigmoid(SWIGLU_ALPHA * glu) * (lin + 1.0)


MOE_RANK_TILE = 512


def _moe_rank_body(te_ref, rank_ref, counts_ref, seen_ref):
    f32 = jnp.float32
    tm = MOE_RANK_TILE

    @pl.when(pl.program_id(0) == 0)
    def _():
        seen_ref[...] = jnp.zeros_like(seen_ref)

    te = te_ref[...]
    lane = lax.broadcasted_iota(jnp.int32, (tm, LANE), 1)
    picks = [lane == te[:, k:k + 1] for k in range(TOP_K)]
    member = picks[0].astype(f32)
    for k in range(1, TOP_K):
        member = member + picks[k].astype(f32)
    earlier = (lax.broadcasted_iota(jnp.int32, (tm, tm), 1) < lax.broadcasted_iota(jnp.int32, (tm, tm), 0))
    prefix = jnp.dot(earlier.astype(jnp.bfloat16), member.astype(jnp.bfloat16), preferred_element_type=f32)
    prefix = prefix + seen_ref[...]
    rank = jnp.zeros((tm, LANE), jnp.int32)
    for k in range(TOP_K):
        r_k = jnp.sum(jnp.where(picks[k], prefix, 0.0), -1, keepdims=True)
        rank = jnp.where(lane == k, r_k.astype(jnp.int32), rank)
    rank_ref[...] = rank
    seen = seen_ref[...] + member.sum(0, keepdims=True)
    seen_ref[...] = seen
    counts_ref[...] = seen.astype(jnp.int32)


def _moe_rank(te):
    n_tok = te.shape[0]
    tm = MOE_RANK_TILE
    return pl.pallas_call(
        _moe_rank_body,
        grid=(n_tok // tm,),
        in_specs=[pl.BlockSpec((tm, LANE), lambda i: (i, 0))],
        out_specs=[pl.BlockSpec((tm, LANE), lambda i: (i, 0)), pl.BlockSpec((1, LANE), lambda i: (0, 0))],
        out_shape=[jax.ShapeDtypeStruct((n_tok, LANE), jnp.int32), jax.ShapeDtypeStruct((1, LANE), jnp.int32)],
        scratch_shapes=[pltpu.VMEM((1, LANE), jnp.float32)],
        compiler_params=_params("arbitrary"),
        name="moe_rank",
    )(te)


def _moe_plan(te):
    top_e = te[:, :TOP_K]
    n_tok = top_e.shape[0]
    n_blocks = -(-n_tok * TOP_K // MOE_BLOCK) + N_EXPERTS + MOE_GATHER_AHEAD - 1
    rank, counts = _moe_rank(te)
    counts = counts[0, :N_EXPERTS]
    padded = (counts + MOE_BLOCK - 1) // MOE_BLOCK * MOE_BLOCK
    pad_end = jnp.cumsum(padded)
    one_hot = top_e[:, :, None] == jnp.arange(N_EXPERTS, dtype=jnp.int32)
    dest = rank[:, :TOP_K] + jnp.sum(jnp.where(one_hot, (pad_end - padded)[None, None, :], 0), -1)
    tok = jnp.broadcast_to(jnp.arange(n_tok, dtype=jnp.int32)[:, None], dest.shape)
    row_tok = jnp.zeros((n_blocks * MOE_BLOCK,), jnp.int32).at[dest.reshape(-1)].set(tok.reshape(-1))
    block_start = jnp.arange(n_blocks, dtype=jnp.int32) * MOE_BLOCK
    block_e = jnp.minimum(jnp.sum(pad_end[None, :] <= block_start[:, None], axis=1), N_EXPERTS - 1).astype(jnp.int32)
    n_used = (pad_end[-1:] // MOE_BLOCK).astype(jnp.int32)
    return dest, row_tok, block_e, n_used


def _row_gather_start(src_hbm, rows_smem, n_rows, dst, sem):
    def body(r, carry):
        pltpu.make_async_copy(src_hbm.at[pl.ds(rows_smem[0, 0, r], 1)], dst.at[pl.ds(r, 1)], sem).start()
        return carry
    lax.fori_loop(0, n_rows, body, 0, unroll=8)


def _row_gather_wait(src_hbm, n_rows, dst, sem):
    pltpu.make_async_copy(src_hbm.at[pl.ds(0, n_rows)], dst, sem).wait()


MOE_GATHER_AHEAD = 2
MOE_GATHER_BUFS = MOE_GATHER_AHEAD + 1


def _moe_expert_body(be_ref, nused_ref, tok_ref, tok_1_ref, tok_ahead_ref, h_hbm, wgu_ref, bgu_ref, wdn_ref, bdn_ref,
                     y_ref, buf_0, buf_1, buf_2, sem, wgu_c, wdn_c):
    f32 = jnp.float32
    i = pl.program_id(0)
    n_used = nused_ref[0]
    bufs = (buf_0, buf_1, buf_2)

    @pl.when(i == 0)
    def _():
        _row_gather_start(h_hbm, tok_ref, MOE_BLOCK, buf_0, sem.at[0])
        _row_gather_start(h_hbm, tok_1_ref, MOE_BLOCK, buf_1, sem.at[1])

    @pl.when(jnp.logical_or(i == 0, be_ref[i] != be_ref[jnp.maximum(i - 1, 0)]))
    def _():
        wgu_c[...] = wgu_ref[0, 0].astype(MXU_DTYPE)
        wdn_c[...] = wdn_ref[0, 0].astype(MXU_DTYPE)

    for phase in range(MOE_GATHER_BUFS):
        cur = bufs[phase]
        ahead = (phase + MOE_GATHER_AHEAD) % MOE_GATHER_BUFS

        @pl.when(jnp.logical_and(i < n_used, i % MOE_GATHER_BUFS == phase))
        def _():
            _row_gather_wait(h_hbm, MOE_BLOCK, cur, sem.at[phase])
            for r in range(MOE_BLOCK):
                pltpu.make_async_copy(h_hbm.at[pl.ds(tok_ahead_ref[0, 0, r], 1)], bufs[ahead].at[pl.ds(r, 1)],
                                      sem.at[ahead]).start()
            xb = _unpack_bf16_pairs(cur[...]).astype(MXU_DTYPE)
            hg = jnp.dot(xb, wgu_c[...], preferred_element_type=f32) + bgu_ref[0, 0]
            act = _clamped_swiglu(hg)
            y_ref[...] = jnp.dot(act.astype(MXU_DTYPE), wdn_c[...], preferred_element_type=f32) + bdn_ref[0, 0]

        @pl.when(jnp.logical_and(jnp.logical_and(i >= n_used, i < n_used + MOE_GATHER_AHEAD),
                                 i % MOE_GATHER_BUFS == phase))
        def _():
            _row_gather_wait(h_hbm, MOE_BLOCK, cur, sem.at[phase])

    @pl.when(i >= n_used)
    def _():
        y_ref[...] = jnp.zeros_like(y_ref)


def _moe_experts(h_packed, row_tok, block_e, n_used, layer, w_gu, b_gu, w_dn, b_dn):
    n_tok = h_packed.shape[0]
    d = 2 * h_packed.shape[1]
    n_blocks = block_e.shape[0]
    n_layers = w_gu.shape[0]
    assert (n_tok * TOP_K + N_EXPERTS * (MOE_BLOCK - 1)) // MOE_BLOCK + MOE_GATHER_AHEAD <= n_blocks
    tok3 = row_tok.reshape(n_blocks, 1, MOE_BLOCK)
    smem_blk = lambda imap: pl.BlockSpec((1, 1, MOE_BLOCK), imap, memory_space=pltpu.SMEM)
    by_expert = lambda shape: pl.BlockSpec((1, 1) + shape, lambda i, be, nu: (layer, be[i], 0, 0))
    gather_buf = pltpu.VMEM((MOE_BLOCK, d // 2), jnp.int32)
    return pl.pallas_call(
        _moe_expert_body,
        grid_spec=pltpu.PrefetchScalarGridSpec(
            num_scalar_prefetch=2,
            grid=(n_blocks,),
            in_specs=[smem_blk(lambda i, be, nu: (i, 0, 0)),
                      smem_blk(lambda i, be, nu: (1, 0, 0)),
                      smem_blk(lambda i, be, nu: (jnp.minimum(i + MOE_GATHER_AHEAD, n_blocks - 1), 0, 0)),
                      pl.BlockSpec(memory_space=pl.ANY),
                      by_expert((d, 2 * D_FF)), by_expert((1, 2 * D_FF)),
                      by_expert((D_FF, d)), by_expert((1, d))],
            out_specs=pl.BlockSpec((MOE_BLOCK, d), lambda i, be, nu: (i, 0)),
            scratch_shapes=[gather_buf] * MOE_GATHER_BUFS
            + [pltpu.SemaphoreType.DMA((MOE_GATHER_BUFS,)),
               pltpu.VMEM((d, 2 * D_FF), MXU_DTYPE), pltpu.VMEM((D_FF, d), MXU_DTYPE)]),
        out_shape=jax.ShapeDtypeStruct((n_blocks * MOE_BLOCK, d), jnp.float32),
        compiler_params=_params("arbitrary"),
        name="moe_experts",
    )(block_e, n_used, tok3, tok3, tok3, h_packed, w_gu, b_gu.reshape(n_layers, N_EXPERTS, 1, 2 * D_FF),
      w_dn, b_dn.reshape(n_layers, N_EXPERTS, 1, d))


MOE_COMBINE_ROWS = 128


def _moe_combine_ln_body(dest_ref, dest_next_ref, h_ref, tg_ref, g_ref, b_ref, y_hbm, o_ref,
                         buf_even, buf_odd, sem):
    tm = MOE_COMBINE_ROWS
    i = pl.program_id(0)
    n = pl.num_programs(0)
    bufs = (buf_even, buf_odd)

    def start(rows_smem, dst, s):
        for k in range(TOP_K):
            for r in range(tm):
                pltpu.make_async_copy(y_hbm.at[pl.ds(rows_smem[0, 0, k * tm + r], 1)],
                                      dst.at[k, pl.ds(r, 1)], sem.at[s]).start()

    def drain(dst, s):
        for k in range(TOP_K):
            _row_gather_wait(y_hbm, tm, dst.at[k], sem.at[s])

    @pl.when(i == 0)
    def _():
        start(dest_ref, buf_even, 0)

    for parity in (0, 1):
        cur, nxt = bufs[parity], bufs[1 - parity]

        @pl.when(i % 2 == parity)
        def _():
            drain(cur, parity)
            start(dest_next_ref, nxt, 1 - parity)
            tg = tg_ref[...]
            y = tg[:, 0:1] * cur[0]
            for k in range(1, TOP_K):
                y = y + tg[:, k:k + 1] * cur[k]
            o_ref[...] = _ln_rows(DEEPNORM_ALPHA * h_ref[...] + y, g_ref[...], b_ref[...])

        @pl.when(jnp.logical_and(i == n - 1, i % 2 == parity))
        def _():
            drain(nxt, 1 - parity)


def _moe_combine_ln(h2d, y_rows, dest, tg, g, b):
    n_tok, d = h2d.shape
    tm = MOE_COMBINE_ROWS
    nt = n_tok // tm
    dest3 = dest.reshape(nt, tm, TOP_K).transpose(0, 2, 1).reshape(nt, 1, TOP_K * tm)
    smem_blk = lambda imap: pl.BlockSpec((1, 1, TOP_K * tm), imap, memory_space=pltpu.SMEM)
    return pl.pallas_call(
        _moe_combine_ln_body,
        grid=(nt,),
        in_specs=[smem_blk(lambda i: (i, 0, 0)),
                  smem_blk(lambda i: (jnp.minimum(i + 1, nt - 1), 0, 0)),
                  pl.BlockSpec((tm, d), lambda i: (i, 0)),
                  pl.BlockSpec((tm, LANE), lambda i: (i, 0)),
                  pl.BlockSpec((1, d), lambda i: (0, 0)),
                  pl.BlockSpec((1, d), lambda i: (0, 0)),
                  pl.BlockSpec(memory_space=pl.ANY)],
        out_specs=pl.BlockSpec((tm, d), lambda i: (i, 0)),
        out_shape=jax.ShapeDtypeStruct((n_tok, d), jnp.float32),
        scratch_shapes=[pltpu.VMEM((TOP_K, tm, d), jnp.float32), pltpu.VMEM((TOP_K, tm, d), jnp.float32),
                        pltpu.SemaphoreType.DMA((2,))],
        compiler_params=_params("arbitrary"),
        name="moe_combine_ln",
    )(dest3, dest3, h2d, tg, g.reshape(1, d), b.reshape(1, d), y_rows)


def _moe_layer(h2d, h_packed, te, tg, layer, w_gu, b_gu, w_dn, b_dn, g, b):
    dest, row_tok, block_e, n_used = _moe_plan(te)
    y_rows = _moe_experts(h_packed, row_tok, block_e, n_used, layer, w_gu, b_gu, w_dn, b_dn)
    return _moe_combine_ln(h2d, y_rows, dest, tg, g, b)


def _mixers(proj, cos, sin, bsz, seq, lambda_init, a_kv_norm, a_w_uk, a_w_uv, c_lambda, c_subln,
            d_conv, d_a_log, d_dt_bias, d_norm):
    cdt = MXU_DTYPE
    (qT, iqT, iw, ikc, kvc, ckvT, qbd, kc, vT, rq, rkT, gatesT) = _mixer_prep(
        proj, cos, sin, a_kv_norm, a_w_uk.transpose(1, 0, 2).astype(cdt), bsz, seq)
    head_a = _dsa_attention(iqT, iw, qT, ikc, kvc, ckvT, a_w_uv.transpose(1, 0, 2).astype(cdt),
                            n_keep=min(INDEX_TOPK, seq // 4))
    head_b = _retention(rq, rkT, proj, bsz, seq)
    lam_full = (jnp.exp(jnp.sum(c_lambda[0] * c_lambda[1])) - jnp.exp(jnp.sum(c_lambda[2] * c_lambda[3]))
                + lambda_init)
    gain = (c_subln * (1.0 - lambda_init)).reshape(1, 2 * C_DH)
    head_c = _diff_attention(lam_full.reshape(1), qbd, kc, vT, gain)
    head_d = _gated_deltanet(proj, gatesT, d_conv, d_a_log, d_dt_bias, d_norm, bsz, seq)
    return [t.reshape(bsz * seq, t.shape[-1]) for t in (head_a, head_b, head_c, head_d)]


def kernel(x, positions, w_in, w_out, a_kv_norm, a_w_uk, a_w_uv, c_lambda, c_subln, d_conv, d_a_log, d_dt_bias, d_norm, ln1_g, ln1_b, w_router, b_router, w_gu, b_gu, w_dn, b_dn, ln2_g, ln2_b):
    bsz, seq, d = x.shape
    x2d = x.reshape(bsz * seq, d)
    cos, sin = _rope_tables(positions)
    for l in range(DEPTH):
        proj = _in_proj(x2d, _pack_w_in(w_in[l]))
        lambda_init = 0.8 - 0.6 * math.exp(-0.3 * l)
        heads = _mixers(proj, cos, sin, bsz, seq, lambda_init, a_kv_norm[l], a_w_uk[l], a_w_uv[l], c_lambda[l],
                        c_subln[l], d_conv[l], d_a_log[l], d_dt_bias[l], d_norm[l])
        h2d, h_packed, te, tg = _out_ln_router(heads, w_out[l].astype(MXU_DTYPE), x2d, ln1_g[l], ln1_b[l],
                                     w_router[l], b_router[l])
        x2d = _moe_layer(h2d, h_packed, te, tg, l, w_gu, b_gu, w_dn, b_dn, ln2_g[l], ln2_b[l])
    return x2d.reshape(bsz, seq, d)
```

```python
import functools
import math

import jax
import jax.numpy as jnp
import numpy as np
from jax import lax
from jax.experimental import pallas as pl
from jax.experimental.pallas import tpu as pltpu

D_MODEL = 2048
DEPTH = 4
A_HEADS, A_NOPE, A_ROPE, A_V, A_KV_RANK = 4, 128, 64, 128, 256
IDX_HEADS, IDX_DIM, INDEX_TOPK = 8, 64, 256
B_HEADS, B_QK, B_V, RET_CHUNK = 4, 64, 128, 128
C_HEADS, C_DH = 4, 64
D_HEADS, D_K, D_V, CONV_K, GDN_CHUNK = 4, 128, 128, 4, 64
N_EXPERTS, TOP_K, D_FF = 32, 4, 512
SWIGLU_ALPHA, SWIGLU_LIMIT = 1.702, 7.0
MOE_BLOCK = 256
ROPE_THETA = 10000.0
ROPE_DIM = 64
Q_BLOCK = 128
DEEPNORM_ALPHA = (2 * DEPTH) ** 0.25
LN_EPS = 1e-5
RMS_EPS = 1e-6
D_MIX = A_HEADS * A_V + B_HEADS * B_V + C_HEADS * 2 * C_DH + D_HEADS * D_V
A_LAT = A_KV_RANK + A_ROPE

IN_WIDTHS = (
    A_HEADS * (A_NOPE + A_ROPE), A_KV_RANK, A_ROPE, IDX_HEADS * IDX_DIM, IDX_DIM, IDX_HEADS,
    B_HEADS * B_QK, B_HEADS * B_QK, B_HEADS * B_V, B_HEADS * B_V,
    C_HEADS * 2 * C_DH, C_HEADS * 2 * C_DH, C_HEADS * 2 * C_DH,
    D_HEADS * (2 * D_K + D_V), D_HEADS * D_V, D_HEADS, D_HEADS,
)
_IN_OFFS = np.concatenate([[0], np.cumsum(IN_WIDTHS)]).tolist()

LANE = 128
SUBLANE = 8
VMEM_LIMIT = 56 * 1024 * 1024

MXU_DTYPE = jnp.bfloat16
_MASKED = -1e30

COL = 512
COL_D_QKV = 0
COL_D_Z = 3
COL_B_V = 4
COL_B_G = 5
COL_B_QK = 6
COL_C_Q = 7
COL_C_K = 8
COL_C_V = 9
COL_A_CKV_ROPE = 10
COL_A_NOPE = 11
COL_A_IQ = 12
NARROW = 256
COL_NARROW = 26
IN_PACKED = (COL_NARROW + 1) * NARROW
NARROW_IW = 0
NARROW_DB = IDX_HEADS
NARROW_DA = IDX_HEADS + D_HEADS


def _pack_w_in(w):
    g = lambda i: w[:, _IN_OFFS[i]:_IN_OFFS[i + 1]]
    a_q = g(0).reshape(w.shape[0], A_HEADS, A_NOPE + A_ROPE)
    nope = a_q[:, :, :A_NOPE].reshape(w.shape[0], A_HEADS * A_NOPE)
    rope = a_q[:, :, A_NOPE:].reshape(w.shape[0], A_HEADS * A_ROPE)
    cols = [g(13), g(14), g(8), g(9), g(6), g(7), g(10), g(11), g(12), g(1), rope, nope, g(3),
            g(2), g(4), g(5), g(15), g(16)]
    used = sum(c.shape[1] for c in cols)
    cols.append(jnp.zeros((w.shape[0], IN_PACKED - used), w.dtype))
    return jnp.concatenate(cols, axis=1).astype(MXU_DTYPE)


def _sigmoid(t):
    return 1.0 / (1.0 + jnp.exp(-t))


def _softplus(t):
    return jnp.maximum(t, 0.0) + jnp.log1p(jnp.exp(-jnp.abs(t)))


def _params(*sem):
    return pltpu.CompilerParams(dimension_semantics=sem, vmem_limit_bytes=VMEM_LIMIT)


def _in_proj_body(x_ref, w_ref, o_ref, xb_ref):
    @pl.when(pl.program_id(1) == 0)
    def _():
        xb_ref[...] = x_ref[...].astype(MXU_DTYPE)

    o_ref[...] = jnp.dot(xb_ref[...], w_ref[...], preferred_element_type=jnp.float32)


def _in_proj(x2d, w_packed, tm=1024, tn=768):
    m, k = x2d.shape
    n = w_packed.shape[1]
    return pl.pallas_call(
        _in_proj_body,
        grid=(m // tm, n // tn),
        in_specs=[pl.BlockSpec((tm, k), lambda i, j: (i, 0)),
                  pl.BlockSpec((k, tn), lambda i, j: (0, j))],
        out_specs=pl.BlockSpec((tm, tn), lambda i, j: (i, j)),
        out_shape=jax.ShapeDtypeStruct((m, n), jnp.float32),
        scratch_shapes=[pltpu.VMEM((tm, k), MXU_DTYPE)],
        compiler_params=_params("arbitrary", "arbitrary"),
        name="in_proj",
    )(x2d, w_packed)


PREP_TILE = 256
KCHUNK = 256


def _prep_body(bqk_ref, cq_ref, ck_ref, cv_ref, ckvrope_ref, nope_ref, iq_ref, nar_ref, cos_ref, sin_ref,
               kvn_ref, wukT_ref,
               qT_ref, iqT_ref, iw_ref, ik_ref, kv_ref, ckvT_ref, qbd_ref, kc_ref, vT_ref, rq_ref, rkT_ref, gT_ref):
    f32 = jnp.float32
    cdt = MXU_DTYPE
    cos, sin = cos_ref[0], sin_ref[0]
    lane = lax.broadcasted_iota(jnp.int32, (PREP_TILE, LANE), 1)
    first_half = (lane % ROPE_DIM) < ROPE_DIM // 2
    eye = (lax.broadcasted_iota(jnp.int32, (LANE, LANE), 0)
           == lax.broadcasted_iota(jnp.int32, (LANE, LANE), 1)).astype(cdt)
    top = lax.broadcasted_iota(jnp.int32, (LANE, LANE), 0) < LANE // 2
    top_wide = lax.broadcasted_iota(jnp.int32, (LANE, PREP_TILE), 0) < LANE // 2

    def rope(t):
        outs = []
        for j in range(t.shape[1] // LANE):
            ts = t[:, j * LANE:(j + 1) * LANE]
            partner = jnp.where(first_half, pltpu.roll(ts, LANE - ROPE_DIM // 2, 1), pltpu.roll(ts, ROPE_DIM // 2, 1))
            outs.append(ts * cos + partner * sin)
        return outs[0] if len(outs) == 1 else jnp.concatenate(outs, axis=1)

    def tr(t):
        return lax.dot_general(eye, t.astype(cdt), (((1,), (1,)), ((), ())), preferred_element_type=f32)

    ckv_rope = ckvrope_ref[...]
    a_ckv = ckv_rope[:, :A_KV_RANK]
    c_kv = a_ckv * lax.rsqrt(jnp.mean(a_ckv * a_ckv, -1, keepdims=True) + RMS_EPS) * kvn_ref[...]
    nar = nar_ref[...]
    kr_ik = rope(nar[:, :LANE])
    kv_ref[0, 0, :, :A_KV_RANK] = c_kv.astype(cdt)
    kv_ref[0, 0, :, A_KV_RANK:] = kr_ik[:, :A_ROPE].astype(cdt)
    ik_ref[0, 0] = kr_ik[:, A_ROPE:].astype(cdt)
    for rb in range(A_KV_RANK // LANE):
        ckvT_ref[0, 0, rb * LANE:(rb + 1) * LANE, :] = tr(c_kv[:, rb * LANE:(rb + 1) * LANE]).astype(cdt)
    q_rope = rope(ckv_rope[:, A_KV_RANK:])
    nope = nope_ref[...]
    iq = rope(iq_ref[...])
    narT = nar[:, LANE:].T
    gT_ref[0] = narT[NARROW_DB:NARROW_DB + 2 * D_HEADS]
    cq = rope(cq_ref[...]) * C_DH ** -0.5
    for j in range(PREP_TILE // Q_BLOCK):
        rows = slice(j * Q_BLOCK, (j + 1) * Q_BLOCK)
        for h in range(A_HEADS):
            q_latT = lax.dot_general(wukT_ref[h], nope[rows, h * A_NOPE:(h + 1) * A_NOPE].astype(cdt),
                                     (((1,), (1,)), ((), ())), preferred_element_type=f32)
            qT_ref[0, j, :A_KV_RANK, h * Q_BLOCK:(h + 1) * Q_BLOCK] = q_latT.astype(cdt)
        for p in range(A_HEADS * A_ROPE // LANE):
            t = tr(q_rope[rows, p * LANE:(p + 1) * LANE])
            qT_ref[0, j, A_KV_RANK:, (2 * p) * Q_BLOCK:(2 * p + 1) * Q_BLOCK] = t[:A_ROPE].astype(cdt)
            qT_ref[0, j, A_KV_RANK:, (2 * p + 1) * Q_BLOCK:(2 * p + 2) * Q_BLOCK] = t[A_ROPE:].astype(cdt)
        for p in range(IDX_HEADS * IDX_DIM // LANE):
            t = tr(iq[rows, p * LANE:(p + 1) * LANE])
            iqT_ref[0, j, :, (2 * p) * Q_BLOCK:(2 * p + 1) * Q_BLOCK] = t[:IDX_DIM].astype(cdt)
            iqT_ref[0, j, :, (2 * p + 1) * Q_BLOCK:(2 * p + 2) * Q_BLOCK] = t[IDX_DIM:].astype(cdt)
        iw_ref[0, j] = narT[NARROW_IW:NARROW_IW + IDX_HEADS, rows] * (IDX_HEADS * IDX_DIM) ** -0.5
        for h in range(C_HEADS):
            t = tr(cq[rows, h * LANE:(h + 1) * LANE])
            qbd_ref[0, j, h, :, :Q_BLOCK] = jnp.where(top, t, 0.0).astype(cdt)
            qbd_ref[0, j, h, :, Q_BLOCK:] = jnp.where(top, 0.0, t).astype(cdt)
    ck = rope(ck_ref[...])
    cv = cv_ref[...]
    for h in range(C_HEADS):
        kc_ref[0, 0, h] = ck[:, h * LANE:(h + 1) * LANE].astype(cdt)
        vT_ref[0, 0, h] = tr(cv[:, h * LANE:(h + 1) * LANE]).astype(cdt)
    bqk = rope(bqk_ref[...])
    rq_ref[0] = bqk[:, :B_HEADS * B_QK].astype(cdt)
    for p in range(B_HEADS * B_QK // LANE):
        t = tr(bqk[:, B_HEADS * B_QK + p * LANE:B_HEADS * B_QK + (p + 1) * LANE]) * B_QK ** -0.5
        rkT_ref[0, 2 * p] = jnp.where(top_wide, t, 0.0).astype(cdt)
        rkT_ref[0, 2 * p + 1] = jnp.where(top_wide, 0.0, t).astype(cdt)


def _mixer_prep(proj, cos, sin, kv_norm, wukT, bsz, seq):
    nt = seq // PREP_TILE
    nq = seq // Q_BLOCK
    cdt = MXU_DTYPE
    col = lambda width, c: pl.BlockSpec((PREP_TILE, width), lambda b, t: (b * nt + t, c))
    table = pl.BlockSpec((1, PREP_TILE, LANE), lambda b, t: (b, t, 0))
    per_q = lambda *tail: pl.BlockSpec((1, PREP_TILE // Q_BLOCK) + tail, lambda b, t: (b, t) + (0,) * len(tail))
    per_c = lambda *tail: pl.BlockSpec((1, 1) + tail, lambda b, t: (b, t) + (0,) * len(tail))
    sds = jax.ShapeDtypeStruct
    outs = [
        (sds((bsz, nq, A_LAT, A_HEADS * Q_BLOCK), cdt), per_q(A_LAT, A_HEADS * Q_BLOCK)),
        (sds((bsz, nq, IDX_DIM, IDX_HEADS * Q_BLOCK), cdt), per_q(IDX_DIM, IDX_HEADS * Q_BLOCK)),
        (sds((bsz, nq, IDX_HEADS, Q_BLOCK), jnp.float32), per_q(IDX_HEADS, Q_BLOCK)),
        (sds((bsz, nt, KCHUNK, IDX_DIM), cdt), per_c(KCHUNK, IDX_DIM)),
        (sds((bsz, nt, KCHUNK, A_LAT), cdt), per_c(KCHUNK, A_LAT)),
        (sds((bsz, nt, A_KV_RANK, KCHUNK), cdt), per_c(A_KV_RANK, KCHUNK)),
        (sds((bsz, nq, C_HEADS, 2 * C_DH, 2 * Q_BLOCK), cdt), per_q(C_HEADS, 2 * C_DH, 2 * Q_BLOCK)),
        (sds((bsz, nt, C_HEADS, KCHUNK, 2 * C_DH), cdt), per_c(C_HEADS, KCHUNK, 2 * C_DH)),
        (sds((bsz, nt, C_HEADS, 2 * C_DH, KCHUNK), cdt), per_c(C_HEADS, 2 * C_DH, KCHUNK)),
        (sds((bsz, seq, B_HEADS * B_QK), cdt), pl.BlockSpec((1, PREP_TILE, B_HEADS * B_QK), lambda b, t: (b, t, 0))),
        (sds((bsz, B_HEADS, LANE, seq), cdt), pl.BlockSpec((1, B_HEADS, LANE, PREP_TILE), lambda b, t: (b, 0, 0, t))),
        (sds((bsz, 2 * D_HEADS, seq), jnp.float32), pl.BlockSpec((1, 2 * D_HEADS, PREP_TILE), lambda b, t: (b, 0, t))),
    ]
    return pl.pallas_call(
        _prep_body,
        grid=(bsz, nt),
        in_specs=[col(COL, COL_B_QK), col(COL, COL_C_Q), col(COL, COL_C_K), col(COL, COL_C_V),
                  col(COL, COL_A_CKV_ROPE), col(COL, COL_A_NOPE), col(COL, COL_A_IQ), col(NARROW, COL_NARROW),
                  table, table,
                  pl.BlockSpec((1, A_KV_RANK), lambda b, t: (0, 0)),
                  pl.BlockSpec((A_HEADS, A_KV_RANK, A_NOPE), lambda b, t: (0, 0, 0))],
        out_specs=[o[1] for o in outs],
        out_shape=[o[0] for o in outs],
        compiler_params=_params("arbitrary", "arbitrary"),
        name="mixer_prep",
    )(proj, proj, proj, proj, proj, proj, proj, proj, cos, sin, kv_norm.reshape(1, A_KV_RANK), wukT)


def _rope_tables(positions):
    half = ROPE_DIM // 2
    inv_freq = ROPE_THETA ** (-jnp.arange(half, dtype=jnp.float32) / half)
    ang = positions.astype(jnp.float32)[:, :, None] * inv_freq
    cos, sin = jnp.cos(ang), jnp.sin(ang)
    reps = LANE // ROPE_DIM
    return (jnp.concatenate([cos, cos] * reps, -1), jnp.concatenate([-sin, sin] * reps, -1))


_SIGN_BIT = np.int32(-2 ** 31)
_LOW31 = np.int32(2 ** 31 - 1)
_KEY_NEG_INF = np.int32(np.array(-np.inf, np.float32).view(np.int32) ^ _LOW31)
DSA_QB = 2


def _dsa_body(iqT_ref, iw_ref, qT_ref, ik_ref, kv_ref, ckvT_ref, wuv_ref, o_ref,
              key_ref, acc_ref, thr_ref, need_ref, *, n_keep):
    f32 = jnp.float32
    blocks = range(DSA_QB)
    q0 = pl.program_id(1) * DSA_QB
    nk = ((q0 + DSA_QB) * Q_BLOCK + KCHUNK - 1) // KCHUNK
    iqT = [iqT_ref[0, j] for j in blocks]
    iw = [jnp.concatenate([iw_ref[0, j][h:h + 1] for h in range(IDX_HEADS)], axis=1) for j in blocks]
    lane_q = lax.broadcasted_iota(jnp.int32, (KCHUNK, Q_BLOCK), 1)
    qpos = [(q0 + j) * Q_BLOCK + lane_q for j in blocks]
    krow = lax.broadcasted_iota(jnp.int32, (KCHUNK, Q_BLOCK), 0)

    def index_chunk(c, carry):
        ik = ik_ref[0, c]
        lg = [jnp.dot(ik, iqT[j], preferred_element_type=f32) for j in blocks]
        w = [jnp.maximum(lg[j], 0.0) * iw[j] for j in blocks]
        for j in blocks:
            idx = w[j][:, :Q_BLOCK]
            for h in range(1, IDX_HEADS):
                idx = idx + w[j][:, h * Q_BLOCK:(h + 1) * Q_BLOCK]
            idx = jnp.where(idx == 0.0, 0.0, idx)
            idx = jnp.where(krow + c * KCHUNK <= qpos[j], idx, -jnp.inf)
            bits = pltpu.bitcast(idx, jnp.int32)
            key_ref[j, c] = bits ^ ((bits >> 31) & _LOW31)
        return carry

    lax.fori_loop(0, nk, index_chunk, 0)

    def count(cmp, ts):
        def body(c, cnts):
            hits = [cmp(key_ref[j, c], ts[j]).astype(jnp.int32) for j in blocks]
            return tuple(cnts[j] + hits[j].reshape(KCHUNK // SUBLANE, SUBLANE, Q_BLOCK).sum(0) for j in blocks)
        cnts = lax.fori_loop(0, nk, body, tuple(jnp.zeros((SUBLANE, Q_BLOCK), jnp.int32) for _ in blocks))
        return [cnt.sum(0, keepdims=True) for cnt in cnts]

    keep_all = [(q0 + j) * Q_BLOCK + Q_BLOCK <= n_keep for j in blocks]

    @pl.when(keep_all[-1])
    def _():
        for j in blocks:
            thr_ref[j] = jnp.full((1, Q_BLOCK), _KEY_NEG_INF, jnp.int32)
            need_ref[j] = jnp.zeros((1, Q_BLOCK), jnp.int32)

    @pl.when(jnp.logical_not(keep_all[-1]))
    def _():
        def bit_step(i, prefixes):
            bit = jnp.left_shift(jnp.int32(1), 31 - i)
            cands = [prefixes[j] | bit for j in blocks]
            cnts = count(lambda k, t: k >= t, [cand ^ _SIGN_BIT for cand in cands])
            return tuple(jnp.where(cnts[j] >= n_keep, cands[j], prefixes[j]) for j in blocks)
        prefixes = lax.fori_loop(0, 32, bit_step, tuple(jnp.zeros((1, Q_BLOCK), jnp.int32) for _ in blocks))
        thrs = [prefixes[j] ^ _SIGN_BIT for j in blocks]
        above = count(lambda k, t: k > t, thrs)
        for j in blocks:
            thr_ref[j] = jnp.where(keep_all[j], _KEY_NEG_INF, thrs[j])
            need_ref[j] = jnp.where(keep_all[j], 0, n_keep - above[j])

    thr = [thr_ref[j] for j in blocks]
    need = [need_ref[j].astype(f32) for j in blocks]
    qT = [qT_ref[0, j] for j in blocks]
    scale = (A_NOPE + A_ROPE) ** -0.5
    r_i = lax.broadcasted_iota(jnp.int32, (KCHUNK, KCHUNK), 0)
    c_i = lax.broadcasted_iota(jnp.int32, (KCHUNK, KCHUNK), 1)
    before = (c_i < r_i).astype(jnp.bfloat16)
    acc_ref[...] = jnp.zeros_like(acc_ref)

    def attend_chunk(c, carry):
        ms, ls, seen = carry
        kv, ckvT = kv_ref[0, c], ckvT_ref[0, c]
        s = [jnp.dot(kv, qT[j], preferred_element_type=f32) * scale for j in blocks]
        key = [key_ref[j, c] for j in blocks]
        tie = [key[j] == thr[j] for j in blocks]
        tie_f = [tie[j].astype(f32) for j in blocks]
        rank = [jnp.dot(before, tie_f[j].astype(jnp.bfloat16), preferred_element_type=f32) + seen[j]
                for j in blocks]
        bias = [jnp.where((key[j] > thr[j]) | (tie[j] & (rank[j] < need[j])), 0.0, _MASKED) for j in blocks]
        s = [s[j] + jnp.concatenate([bias[j]] * A_HEADS, axis=1) for j in blocks]
        m_new = [jnp.maximum(ms[j], s[j].max(0, keepdims=True)) for j in blocks]
        alpha = [jnp.exp(ms[j] - m_new[j]) for j in blocks]
        p = [jnp.exp(s[j] - m_new[j]) for j in blocks]
        l_new = [alpha[j] * ls[j] + p[j].sum(0, keepdims=True) for j in blocks]
        pv = [jnp.dot(ckvT, p[j].astype(ckvT.dtype), preferred_element_type=f32) for j in blocks]
        for j in blocks:
            acc_ref[j] = acc_ref[j] * alpha[j] + pv[j]
        return (tuple(m_new), tuple(l_new), tuple(seen[j] + tie_f[j].sum(0, keepdims=True) for j in blocks))

    lanes = A_HEADS * Q_BLOCK
    init = (tuple(jnp.full((1, lanes), _MASKED, f32) for _ in blocks),
            tuple(jnp.zeros((1, lanes), f32) for _ in blocks),
            tuple(jnp.zeros((1, Q_BLOCK), f32) for _ in blocks))
    _, ls, _ = lax.fori_loop(0, nk, attend_chunk, init)
    for j in blocks:
        o_latT = acc_ref[j] / ls[j]
        for h in range(A_HEADS):
            o_lat = o_latT[:, h * Q_BLOCK:(h + 1) * Q_BLOCK].T.astype(wuv_ref.dtype)
            o_ref[0, j * Q_BLOCK:(j + 1) * Q_BLOCK, h * A_V:(h + 1) * A_V] = jnp.dot(
                o_lat, wuv_ref[h], preferred_element_type=f32)


def _dsa_attention(iqT, iw, qT, ikc, kvc, ckvT, wuv, *, n_keep):
    bsz, nq = qT.shape[:2]
    nc = ikc.shape[1]
    seq = nq * Q_BLOCK
    per_q = lambda b, q: (b, q, 0, 0)
    per_b = lambda b, q: (b, 0, 0, 0)
    return pl.pallas_call(
        functools.partial(_dsa_body, n_keep=n_keep),
        grid=(bsz, nq // DSA_QB),
        in_specs=[pl.BlockSpec((1, DSA_QB) + iqT.shape[2:], per_q),
                  pl.BlockSpec((1, DSA_QB) + iw.shape[2:], per_q),
                  pl.BlockSpec((1, DSA_QB) + qT.shape[2:], per_q),
                  pl.BlockSpec((1,) + ikc.shape[1:], per_b),
                  pl.BlockSpec((1,) + kvc.shape[1:], per_b),
                  pl.BlockSpec((1,) + ckvT.shape[1:], per_b),
                  pl.BlockSpec(wuv.shape, lambda b, q: (0, 0, 0))],
        out_specs=pl.BlockSpec((1, DSA_QB * Q_BLOCK, A_HEADS * A_V), lambda b, q: (b, q, 0)),
        out_shape=jax.ShapeDtypeStruct((bsz, seq, A_HEADS * A_V), jnp.float32),
        scratch_shapes=[pltpu.VMEM((DSA_QB, nc, KCHUNK, Q_BLOCK), jnp.int32),
                        pltpu.VMEM((DSA_QB, A_KV_RANK, A_HEADS * Q_BLOCK), jnp.float32),
                        pltpu.VMEM((DSA_QB, 1, Q_BLOCK), jnp.int32),
                        pltpu.VMEM((DSA_QB, 1, Q_BLOCK), jnp.int32)],
        compiler_params=_params("arbitrary", "arbitrary"),
        name="dsa_attention",
    )(iqT, iw, qT, ikc, kvc, ckvT, wuv)


RET_TILE = 256


def _retention_body(q_ref, kT_ref, v_ref, g_ref, o_ref, s_ref):
    f32 = jnp.float32
    cdt = MXU_DTYPE
    c = RET_CHUNK

    @pl.when(pl.program_id(1) == 0)
    def _():
        s_ref[...] = jnp.zeros_like(s_ref)

    rel = (lax.broadcasted_iota(jnp.int32, (c, c), 0) - lax.broadcasted_iota(jnp.int32, (c, c), 1)).astype(f32)
    pos_c = lax.broadcasted_iota(jnp.int32, (c, 1), 0).astype(f32)
    pos_r = lax.broadcasted_iota(jnp.int32, (1, c), 1).astype(f32)

    def mm(a, b):
        return jnp.dot(a.astype(cdt), b.astype(cdt), preferred_element_type=f32)

    for h in range(B_HEADS):
        log_gamma = float(np.log(np.float32(1.0) - np.float32(2.0) ** np.float32(-5 - h)))
        intra = jnp.where(rel >= 0, jnp.exp(log_gamma * jnp.maximum(rel, 0.0)), 0.0)
        to_end = jnp.exp(log_gamma * (c - 1 - pos_r))
        from_start = jnp.exp(log_gamma * (pos_c + 1.0))
        chunk_decay = float(np.exp(np.float32(log_gamma) * np.float32(c)))
        lanes = slice(h * B_V, (h + 1) * B_V)
        pair = slice((h // 2) * LANE, (h // 2 + 1) * LANE)
        for ci in range(RET_TILE // c):
            rows = slice(ci * c, (ci + 1) * c)
            q = q_ref[0, rows, pair].astype(f32)
            kT = kT_ref[0, h, :, rows].astype(f32)
            v = v_ref[rows, lanes]
            state = s_ref[h]
            o = mm(mm(q, kT) * intra, v) + mm(q * from_start, state)
            s_ref[h] = state * chunk_decay + mm(kT * to_end, v)
            gate = g_ref[rows, lanes]
            o = o * lax.rsqrt(jnp.mean(o * o, -1, keepdims=True) + RMS_EPS)
            o_ref[0, rows, lanes] = o * (gate * _sigmoid(gate))


def _retention(rq, rkT, proj, bsz, seq):
    nt = seq // RET_TILE
    width = B_HEADS * B_V
    col = lambda c: pl.BlockSpec((RET_TILE, COL), lambda b, t: (b * nt + t, c))
    return pl.pallas_call(
        _retention_body,
        grid=(bsz, nt),
        in_specs=[pl.BlockSpec((1, RET_TILE, B_HEADS * B_QK), lambda b, t: (b, t, 0)),
                  pl.BlockSpec((1, B_HEADS, LANE, RET_TILE), lambda b, t: (b, 0, 0, t)),
                  col(COL_B_V), col(COL_B_G)],
        out_specs=pl.BlockSpec((1, RET_TILE, width), lambda b, t: (b, t, 0)),
        out_shape=jax.ShapeDtypeStruct((bsz, seq, width), jnp.float32),
        scratch_shapes=[pltpu.VMEM((B_HEADS, LANE, B_V), jnp.float32)],
        compiler_params=_params("arbitrary", "arbitrary"),
        name="retention",
    )(rq, rkT, proj, proj)


DIFF_QB = 2


def _diff_attention_body(lam_ref, qbd_ref, k_ref, vT_ref, gain_ref, o_ref, acc_ref):
    f32 = jnp.float32
    blocks = range(DIFF_QB)
    heads = range(C_HEADS)
    q0 = pl.program_id(1) * DIFF_QB
    nk = ((q0 + DIFF_QB) * Q_BLOCK + KCHUNK - 1) // KCHUNK
    lane_q = lax.broadcasted_iota(jnp.int32, (KCHUNK, Q_BLOCK), 1)
    krow = lax.broadcasted_iota(jnp.int32, (KCHUNK, Q_BLOCK), 0)
    q_all = [jnp.concatenate([qbd_ref[0, j, h] for j in blocks], axis=1) for h in heads]
    acc_ref[...] = jnp.zeros_like(acc_ref)

    def chunk(c, carry):
        ms, ls = carry
        causal = [jnp.where(krow + c * KCHUNK <= (q0 + j) * Q_BLOCK + lane_q, 0.0, _MASKED) for j in blocks]
        bias = jnp.concatenate([causal[j] for j in blocks for _ in range(2)], axis=1)
        s = [jnp.dot(k_ref[0, c, h], q_all[h], preferred_element_type=f32) + bias for h in heads]
        new_ms = [jnp.maximum(ms[h], s[h].max(0, keepdims=True)) for h in heads]
        alpha = [jnp.exp(ms[h] - new_ms[h]) for h in heads]
        p = [jnp.exp(s[h] - new_ms[h]) for h in heads]
        new_ls = [alpha[h] * ls[h] + p[h].sum(0, keepdims=True) for h in heads]
        pv = [jnp.dot(vT_ref[0, c, h], p[h].astype(vT_ref.dtype), preferred_element_type=f32) for h in heads]
        for h in heads:
            acc_ref[h] = acc_ref[h] * alpha[h] + pv[h]
        return tuple(new_ms), tuple(new_ls)

    lanes = DIFF_QB * 2 * Q_BLOCK
    init = (tuple(jnp.full((1, lanes), _MASKED, f32) for _ in heads),
            tuple(jnp.zeros((1, lanes), f32) for _ in heads))
    _, ls = lax.fori_loop(0, nk, chunk, init)
    lam = lam_ref[0]
    for h in heads:
        o_all = acc_ref[h] / ls[h]
        for j in blocks:
            o1 = o_all[:, (2 * j) * Q_BLOCK:(2 * j + 1) * Q_BLOCK]
            o2 = o_all[:, (2 * j + 1) * Q_BLOCK:(2 * j + 2) * Q_BLOCK]
            o = o1 - lam * o2
            o = o * lax.rsqrt(jnp.mean(o * o, 0, keepdims=True) + RMS_EPS)
            o_ref[0, j * Q_BLOCK:(j + 1) * Q_BLOCK, h * 2 * C_DH:(h + 1) * 2 * C_DH] = o.T * gain_ref[...]


def _diff_attention(lam, qbd, kc, vT, gain):
    bsz, nq = qbd.shape[:2]
    seq = nq * Q_BLOCK
    width = C_HEADS * 2 * C_DH
    return pl.pallas_call(
        _diff_attention_body,
        grid=(bsz, nq // DIFF_QB),
        in_specs=[pl.BlockSpec(memory_space=pltpu.SMEM),
                  pl.BlockSpec((1, DIFF_QB) + qbd.shape[2:], lambda b, q: (b, q, 0, 0, 0)),
                  pl.BlockSpec((1,) + kc.shape[1:], lambda b, q: (b, 0, 0, 0, 0)),
                  pl.BlockSpec((1,) + vT.shape[1:], lambda b, q: (b, 0, 0, 0, 0)),
                  pl.BlockSpec(gain.shape, lambda b, q: (0, 0))],
        out_specs=pl.BlockSpec((1, DIFF_QB * Q_BLOCK, width), lambda b, q: (b, q, 0)),
        out_shape=jax.ShapeDtypeStruct((bsz, seq, width), jnp.float32),
        scratch_shapes=[pltpu.VMEM((C_HEADS, 2 * C_DH, DIFF_QB * 2 * Q_BLOCK), jnp.float32)],
        compiler_params=_params("arbitrary", "arbitrary"),
        name="diff_attention",
    )(lam, qbd, kc, vT, gain)


GDN_TILE = 256
GDN_SUB = 16


def _gdn_body(qkv_ref, z_ref, nar_ref, gateT_ref, conv_ref, alog_r_ref, dtb_r_ref, alog_c_ref, dtb_c_ref,
              ng_ref, o_ref, halo_ref, s_ref):
    f32 = jnp.float32
    cdt = MXU_DTYPE
    c = GDN_CHUNK

    @pl.when(pl.program_id(1) == 0)
    def _():
        halo_ref[...] = jnp.zeros_like(halo_ref)
        s_ref[...] = jnp.zeros_like(s_ref)

    x = qkv_ref[...]
    xp = jnp.concatenate([halo_ref[...], x], 0)
    w = conv_ref[...]
    pre = xp[SUBLANE - CONV_K + 1:SUBLANE - CONV_K + 1 + GDN_TILE] * w[0:1]
    for j in range(1, CONV_K):
        off = SUBLANE - CONV_K + 1 + j
        pre = pre + xp[off:off + GDN_TILE] * w[j:j + 1]
    halo_ref[...] = x[GDN_TILE - SUBLANE:]
    qkv = pre * _sigmoid(pre)

    gates = nar_ref[:, LANE:]
    beta_c = _sigmoid(gates[:, NARROW_DB:NARROW_DB + D_HEADS])
    g_c = -jnp.exp(alog_r_ref[...]) * _softplus(gates[:, NARROW_DA:NARROW_DA + D_HEADS] + dtb_r_ref[...])
    g_r = -jnp.exp(alog_c_ref[...]) * _softplus(gateT_ref[0][D_HEADS:] + dtb_c_ref[...])

    row = lax.broadcasted_iota(jnp.int32, (c, c), 0)
    col = lax.broadcasted_iota(jnp.int32, (c, c), 1)
    tril = row >= col
    strict = row > col
    same_sub = (row // GDN_SUB) == (col // GDN_SUB)
    eye = (row == col).astype(f32)
    lower_ones = tril.astype(f32)
    upper_ones = (row <= col).astype(f32)

    def mm(a, b):
        return jnp.dot(a, b, preferred_element_type=f32)

    n_chunks = GDN_TILE // c
    pairs = [(ci, h) for ci in range(n_chunks) for h in range(D_HEADS)]
    rows_of = lambda ci: slice(ci * c, (ci + 1) * c)
    g_cum_c = [mm(lower_ones, g_c[rows_of(ci)]) for ci in range(n_chunks)]
    g_cum_r = [mm(g_r[:, rows_of(ci)], upper_ones) for ci in range(n_chunks)]
    st = []
    for ci, h in pairs:
        rows = rows_of(ci)
        q = qkv[rows, h * D_K:(h + 1) * D_K]
        k = qkv[rows, D_HEADS * D_K + h * D_K:D_HEADS * D_K + (h + 1) * D_K]
        v = qkv[rows, 2 * D_HEADS * D_K + h * D_V:2 * D_HEADS * D_K + (h + 1) * D_V]
        q = q * lax.rsqrt(jnp.sum(q * q, -1, keepdims=True) + RMS_EPS) * D_K ** -0.5
        k = k * lax.rsqrt(jnp.sum(k * k, -1, keepdims=True) + RMS_EPS)
        gc = g_cum_c[ci][:, h:h + 1]
        gr = g_cum_r[ci][h:h + 1, :]
        g_last = gc[c - 1:c]
        gam = jnp.where(tril, jnp.exp(jnp.where(tril, gc - gr, 0.0)), 0.0)
        e_g = jnp.exp(gc)
        beta = beta_c[rows, h:h + 1]
        kT = k.T
        kb = k * beta
        st.append(dict(gam=gam, kT_c=kT.astype(cdt), kb_c=kb.astype(cdt), q_c=q.astype(cdt),
                       qg_c=(q * e_g).astype(cdt), rhs=jnp.concatenate([v * beta, kb * e_g], 1),
                       kdT_c=(kT * jnp.exp(g_last - gr)).astype(cdt), decay=jnp.exp(g_last)))
    for p in st:
        p["low"] = jnp.where(strict, mm(p["kb_c"], p["kT_c"]) * p["gam"], 0.0)
        p["attn_c"] = jnp.where(tril, mm(p["q_c"], p["kT_c"]) * p["gam"], 0.0).astype(cdt)
        p["low_d"] = jnp.where(same_sub, p["low"], 0.0)
        p["xk"] = -p["low_d"]
        p["td"] = eye + p["xk"]
    for _ in range(3):
        for p in st:
            p["xk"] = mm(p["xk"], p["xk"])
        for p in st:
            p["td"] = p["td"] + mm(p["td"], p["xk"])
    for p in st:
        p["m1"] = mm(p["td"], p["low"] - p["low_d"])
        p["sol"] = mm(p["td"], p["rhs"])
    for p in st:
        p["m2"] = mm(p["m1"], p["m1"])
    for p in st:
        p["sol"] = p["sol"] + mm(p["m2"], p["sol"])
    for p in st:
        p["sol"] = p["sol"] - mm(p["m1"], p["sol"])
    for ci in range(n_chunks):
        rows = rows_of(ci)
        cur = [st[ci * D_HEADS + h] for h in range(D_HEADS)]
        states = [s_ref[h] for h in range(D_HEADS)]
        states_c = [s.astype(cdt) for s in states]
        v_new_c = [(p["sol"][:, :D_V] - mm(p["sol"][:, D_V:].astype(cdt), states_c[h])).astype(cdt)
                   for h, p in enumerate(cur)]
        for h, p in enumerate(cur):
            s_ref[h] = states[h] * p["decay"] + mm(p["kdT_c"], v_new_c[h])
        for h, p in enumerate(cur):
            lanes = slice(h * D_V, (h + 1) * D_V)
            o = mm(p["qg_c"], states_c[h]) + mm(p["attn_c"], v_new_c[h])
            zz = z_ref[rows, lanes]
            o = o * lax.rsqrt(jnp.mean(o * o, -1, keepdims=True) + RMS_EPS) * ng_ref[...]
            o_ref[0, rows, lanes] = o * (zz * _sigmoid(zz))


def _gated_deltanet(proj, gatesT, conv_w, a_log, dt_bias, norm_g, bsz, seq):
    nt = seq // GDN_TILE
    width = D_HEADS * (2 * D_K + D_V)
    whole = lambda r, cc: pl.BlockSpec((r, cc), lambda b, t: (0, 0))
    return pl.pallas_call(
        _gdn_body,
        grid=(bsz, nt),
        in_specs=[pl.BlockSpec((GDN_TILE, width), lambda b, t: (b * nt + t, COL_D_QKV)),
                  pl.BlockSpec((GDN_TILE, COL), lambda b, t: (b * nt + t, COL_D_Z)),
                  pl.BlockSpec((GDN_TILE, NARROW), lambda b, t: (b * nt + t, COL_NARROW)),
                  pl.BlockSpec((1, 2 * D_HEADS, GDN_TILE), lambda b, t: (b, 0, t)),
                  whole(CONV_K, width), whole(1, D_HEADS), whole(1, D_HEADS), whole(D_HEADS, 1), whole(D_HEADS, 1),
                  whole(1, D_V)],
        out_specs=pl.BlockSpec((1, GDN_TILE, D_HEADS * D_V), lambda b, t: (b, t, 0)),
        out_shape=jax.ShapeDtypeStruct((bsz, seq, D_HEADS * D_V), jnp.float32),
        scratch_shapes=[pltpu.VMEM((SUBLANE, width), jnp.float32),
                        pltpu.VMEM((D_HEADS, D_K, D_V), jnp.float32)],
        compiler_params=_params("arbitrary", "arbitrary"),
        name="gated_deltanet",
    )(proj, proj, proj, gatesT, conv_w, a_log.reshape(1, D_HEADS), dt_bias.reshape(1, D_HEADS),
      a_log.reshape(D_HEADS, 1), dt_bias.reshape(D_HEADS, 1), norm_g.reshape(1, D_V))


def _ln_rows(t, g, b):
    mu = jnp.mean(t, -1, keepdims=True)
    d = t - mu
    var = jnp.mean(d * d, -1, keepdims=True)
    return d * lax.rsqrt(var + LN_EPS) * g + b


_HIGH16 = np.int32(-(2 ** 16))


def _pack_bf16_pairs(t):
    n = t.shape[1] // 2
    lo = pltpu.bitcast(t[:, :n].astype(jnp.bfloat16).astype(jnp.float32), jnp.int32)
    hi = pltpu.bitcast(t[:, n:].astype(jnp.bfloat16).astype(jnp.float32), jnp.int32)
    return lax.shift_right_logical(lo, 16) | (hi & _HIGH16)


def _unpack_bf16_pairs(w):
    lo = pltpu.bitcast(w << 16, jnp.float32)
    hi = pltpu.bitcast(w & _HIGH16, jnp.float32)
    return jnp.concatenate([lo, hi], axis=1).astype(jnp.bfloat16)


def _out_ln_router_body(ha_ref, hb_ref, hc_ref, hd_ref, w_ref, x_ref, g_ref, b_ref, wr_ref, br_ref,
                        h_ref, hp_ref, te_ref, tg_ref):
    f32 = jnp.float32
    acc = DEEPNORM_ALPHA * x_ref[...]
    row0 = 0
    for part in (ha_ref, hb_ref, hc_ref, hd_ref):
        width = part.shape[-1]
        acc = acc + jnp.dot(part[...].astype(MXU_DTYPE), w_ref[row0:row0 + width, :], preferred_element_type=f32)
        row0 += width
    h = _ln_rows(acc, g_ref[...], b_ref[...])
    h_ref[...] = h
    hp_ref[...] = _pack_bf16_pairs(h)
    logits = jnp.dot(h.astype(MXU_DTYPE), wr_ref[...], preferred_element_type=f32) + br_ref[...]
    lane = lax.broadcasted_iota(jnp.int32, logits.shape, 1)
    vals, te = [], jnp.zeros(logits.shape, jnp.int32)
    for k in range(TOP_K):
        m = logits.max(-1, keepdims=True)
        e = jnp.min(jnp.where(logits == m, lane, LANE), -1, keepdims=True)
        vals.append(m)
        te = jnp.where(lane == k, e, te)
        logits = jnp.where(lane == e, -jnp.inf, logits)
    ex = [jnp.exp(v - vals[0]) for v in vals]
    den = ex[0] + ex[1] + ex[2] + ex[3]
    tg = jnp.zeros(logits.shape, f32)
    for k in range(TOP_K):
        tg = jnp.where(lane == k, ex[k] / den, tg)
    te_ref[...] = te
    tg_ref[...] = tg


def _out_ln_router(head_parts, w_out_c, x2d, g, b, w_router, b_router, tm=512):
    m = x2d.shape[0]
    k, n = w_out_c.shape
    wr = jnp.zeros((n, LANE), MXU_DTYPE).at[:, :N_EXPERTS].set(w_router.astype(MXU_DTYPE))
    br = jnp.full((1, LANE), -jnp.inf, jnp.float32).at[0, :N_EXPERTS].set(b_router)
    row_blk = lambda c: pl.BlockSpec((tm, c), lambda i: (i, 0))
    whole = lambda r, c: pl.BlockSpec((r, c), lambda i: (0, 0))
    return pl.pallas_call(
        _out_ln_router_body,
        grid=(m // tm,),
        in_specs=[row_blk(p.shape[1]) for p in head_parts]
        + [whole(k, n), row_blk(n), whole(1, n), whole(1, n), whole(n, LANE), whole(1, LANE)],
        out_specs=[row_blk(n), row_blk(n // 2), row_blk(LANE), row_blk(LANE)],
        out_shape=[jax.ShapeDtypeStruct((m, n), jnp.float32),
                   jax.ShapeDtypeStruct((m, n // 2), jnp.int32),
                   jax.ShapeDtypeStruct((m, LANE), jnp.int32),
                   jax.ShapeDtypeStruct((m, LANE), jnp.float32)],
        compiler_params=_params("arbitrary"),
        name="out_proj_ln_router",
    )(*head_parts, w_out_c, x2d, g.reshape(1, n), b.reshape(1, n), wr, br)


def _clamped_swiglu(hg):
    glu, lin = hg[..., :D_FF], hg[..., D_FF:]
    glu = jnp.minimum(glu, SWIGLU_LIMIT)
    lin = jnp.clip(lin, -SWIGLU_LIMIT, SWIGLU_LIMIT)
    return glu * _sigmoid(SWIGLU_ALPHA * glu) * (lin + 1.0)


MOE_RANK_TILE = 512


def _moe_rank_body(te_ref, rank_ref, counts_ref, seen_ref):
    f32 = jnp.float32
    tm = MOE_RANK_TILE

    @pl.when(pl.program_id(0) == 0)
    def _():
        seen_ref[...] = jnp.zeros_like(seen_ref)

    te = te_ref[...]
    lane = lax.broadcasted_iota(jnp.int32, (tm, LANE), 1)
    picks = [lane == te[:, k:k + 1] for k in range(TOP_K)]
    member = picks[0].astype(f32)
    for k in range(1, TOP_K):
        member = member + picks[k].astype(f32)
    earlier = (lax.broadcasted_iota(jnp.int32, (tm, tm), 1) < lax.broadcasted_iota(jnp.int32, (tm, tm), 0))
    prefix = jnp.dot(earlier.astype(jnp.bfloat16), member.astype(jnp.bfloat16), preferred_element_type=f32)
    prefix = prefix + seen_ref[...]
    rank = jnp.zeros((tm, LANE), jnp.int32)
    for k in range(TOP_K):
        r_k = jnp.sum(jnp.where(picks[k], prefix, 0.0), -1, keepdims=True)
        rank = jnp.where(lane == k, r_k.astype(jnp.int32), rank)
    rank_ref[...] = rank
    seen = seen_ref[...] + member.sum(0, keepdims=True)
    seen_ref[...] = seen
    counts_ref[...] = seen.astype(jnp.int32)


def _moe_rank(te):
    n_tok = te.shape[0]
    tm = MOE_RANK_TILE
    return pl.pallas_call(
        _moe_rank_body,
        grid=(n_tok // tm,),
        in_specs=[pl.BlockSpec((tm, LANE), lambda i: (i, 0))],
        out_specs=[pl.BlockSpec((tm, LANE), lambda i: (i, 0)), pl.BlockSpec((1, LANE), lambda i: (0, 0))],
        out_shape=[jax.ShapeDtypeStruct((n_tok, LANE), jnp.int32), jax.ShapeDtypeStruct((1, LANE), jnp.int32)],
        scratch_shapes=[pltpu.VMEM((1, LANE), jnp.float32)],
        compiler_params=_params("arbitrary"),
        name="moe_rank",
    )(te)


def _moe_plan(te):
    top_e = te[:, :TOP_K]
    n_tok = top_e.shape[0]
    n_blocks = -(-n_tok * TOP_K // MOE_BLOCK) + N_EXPERTS + MOE_GATHER_AHEAD - 1
    rank, counts = _moe_rank(te)
    counts = counts[0, :N_EXPERTS]
    padded = (counts + MOE_BLOCK - 1) // MOE_BLOCK * MOE_BLOCK
    pad_end = jnp.cumsum(padded)
    one_hot = top_e[:, :, None] == jnp.arange(N_EXPERTS, dtype=jnp.int32)
    dest = rank[:, :TOP_K] + jnp.sum(jnp.where(one_hot, (pad_end - padded)[None, None, :], 0), -1)
    tok = jnp.broadcast_to(jnp.arange(n_tok, dtype=jnp.int32)[:, None], dest.shape)
    row_tok = jnp.zeros((n_blocks * MOE_BLOCK,), jnp.int32).at[dest.reshape(-1)].set(
        tok.reshape(-1), unique_indices=True, mode='promise_in_bounds')
    block_start = jnp.arange(n_blocks, dtype=jnp.int32) * MOE_BLOCK
    block_e = jnp.minimum(jnp.sum(pad_end[None, :] <= block_start[:, None], axis=1), N_EXPERTS - 1).astype(jnp.int32)
    n_used = (pad_end[-1:] // MOE_BLOCK).astype(jnp.int32)
    return dest, row_tok, block_e, n_used


def _row_gather_start(src_hbm, rows_smem, n_rows, dst, sem):
    def body(r, carry):
        pltpu.make_async_copy(src_hbm.at[pl.ds(rows_smem[0, 0, r], 1)], dst.at[pl.ds(r, 1)], sem).start()
        return carry
    lax.fori_loop(0, n_rows, body, 0, unroll=8)


def _row_gather_wait(src_hbm, n_rows, dst, sem):
    pltpu.make_async_copy(src_hbm.at[pl.ds(0, n_rows)], dst, sem).wait()


MOE_GATHER_AHEAD = 2
MOE_GATHER_BUFS = MOE_GATHER_AHEAD + 1


def _moe_expert_body(be_ref, nused_ref, tok_ref, tok_1_ref, tok_ahead_ref, h_hbm, wgu_ref, bgu_ref, wdn_ref, bdn_ref,
                     y_ref, buf_0, buf_1, buf_2, sem, wgu_c, wdn_c):
    f32 = jnp.float32
    i = pl.program_id(0)
    n_used = nused_ref[0]
    bufs = (buf_0, buf_1, buf_2)

    @pl.when(i == 0)
    def _():
        _row_gather_start(h_hbm, tok_ref, MOE_BLOCK, buf_0, sem.at[0])
        _row_gather_start(h_hbm, tok_1_ref, MOE_BLOCK, buf_1, sem.at[1])

    @pl.when(jnp.logical_or(i == 0, be_ref[i] != be_ref[jnp.maximum(i - 1, 0)]))
    def _():
        wgu_c[...] = wgu_ref[0, 0].astype(MXU_DTYPE)
        wdn_c[...] = wdn_ref[0, 0].astype(MXU_DTYPE)

    for phase in range(MOE_GATHER_BUFS):
        cur = bufs[phase]
        ahead = (phase + MOE_GATHER_AHEAD) % MOE_GATHER_BUFS

        @pl.when(jnp.logical_and(i < n_used, i % MOE_GATHER_BUFS == phase))
        def _():
            _row_gather_wait(h_hbm, MOE_BLOCK, cur, sem.at[phase])
            for r in range(MOE_BLOCK):
                pltpu.make_async_copy(h_hbm.at[pl.ds(tok_ahead_ref[0, 0, r], 1)], bufs[ahead].at[pl.ds(r, 1)],
                                      sem.at[ahead]).start()
            xb = _unpack_bf16_pairs(cur[...]).astype(MXU_DTYPE)
            hg = jnp.dot(xb, wgu_c[...], preferred_element_type=f32) + bgu_ref[0, 0]
            act = _clamped_swiglu(hg)
            y_ref[...] = jnp.dot(act.astype(MXU_DTYPE), wdn_c[...], preferred_element_type=f32) + bdn_ref[0, 0]

        @pl.when(jnp.logical_and(jnp.logical_and(i >= n_used, i < n_used + MOE_GATHER_AHEAD),
                                 i % MOE_GATHER_BUFS == phase))
        def _():
            _row_gather_wait(h_hbm, MOE_BLOCK, cur, sem.at[phase])

    @pl.when(i >= n_used)
    def _():
        y_ref[...] = jnp.zeros_like(y_ref)


def _moe_experts(h_packed, row_tok, block_e, n_used, layer, w_gu, b_gu, w_dn, b_dn):
    n_tok = h_packed.shape[0]
    d = 2 * h_packed.shape[1]
    n_blocks = block_e.shape[0]
    n_layers = w_gu.shape[0]
    assert (n_tok * TOP_K + N_EXPERTS * (MOE_BLOCK - 1)) // MOE_BLOCK + MOE_GATHER_AHEAD <= n_blocks
    tok3 = row_tok.reshape(n_blocks, 1, MOE_BLOCK)
    smem_blk = lambda imap: pl.BlockSpec((1, 1, MOE_BLOCK), imap, memory_space=pltpu.SMEM)
    by_expert = lambda shape: pl.BlockSpec((1, 1) + shape, lambda i, be, nu: (layer, be[i], 0, 0))
    gather_buf = pltpu.VMEM((MOE_BLOCK, d // 2), jnp.int32)
    return pl.pallas_call(
        _moe_expert_body,
        grid_spec=pltpu.PrefetchScalarGridSpec(
            num_scalar_prefetch=2,
            grid=(n_blocks,),
            in_specs=[smem_blk(lambda i, be, nu: (i, 0, 0)),
                      smem_blk(lambda i, be, nu: (1, 0, 0)),
                      smem_blk(lambda i, be, nu: (jnp.minimum(i + MOE_GATHER_AHEAD, n_blocks - 1), 0, 0)),
                      pl.BlockSpec(memory_space=pl.ANY),
                      by_expert((d, 2 * D_FF)), by_expert((1, 2 * D_FF)),
                      by_expert((D_FF, d)), by_expert((1, d))],
            out_specs=pl.BlockSpec((MOE_BLOCK, d), lambda i, be, nu: (i, 0)),
            scratch_shapes=[gather_buf] * MOE_GATHER_BUFS
            + [pltpu.SemaphoreType.DMA((MOE_GATHER_BUFS,)),
               pltpu.VMEM((d, 2 * D_FF), MXU_DTYPE), pltpu.VMEM((D_FF, d), MXU_DTYPE)]),
        out_shape=jax.ShapeDtypeStruct((n_blocks * MOE_BLOCK, d), jnp.float32),
        compiler_params=_params("arbitrary"),
        name="moe_experts",
    )(block_e, n_used, tok3, tok3, tok3, h_packed, w_gu, b_gu.reshape(n_layers, N_EXPERTS, 1, 2 * D_FF),
      w_dn, b_dn.reshape(n_layers, N_EXPERTS, 1, d))


MOE_COMBINE_ROWS = 128


def _moe_combine_ln_body(dest_ref, dest_next_ref, h_ref, tg_ref, g_ref, b_ref, y_hbm, o_ref,
                         buf_even, buf_odd, sem):
    tm = MOE_COMBINE_ROWS
    i = pl.program_id(0)
    n = pl.num_programs(0)
    bufs = (buf_even, buf_odd)

    def start(rows_smem, dst, s):
        for k in range(TOP_K):
            for r in range(tm):
                pltpu.make_async_copy(y_hbm.at[pl.ds(rows_smem[0, 0, k * tm + r], 1)],
                                      dst.at[k, pl.ds(r, 1)], sem.at[s]).start()

    def drain(dst, s):
        for k in range(TOP_K):
            _row_gather_wait(y_hbm, tm, dst.at[k], sem.at[s])

    @pl.when(i == 0)
    def _():
        start(dest_ref, buf_even, 0)

    for parity in (0, 1):
        cur, nxt = bufs[parity], bufs[1 - parity]

        @pl.when(i % 2 == parity)
        def _():
            drain(cur, parity)
            start(dest_next_ref, nxt, 1 - parity)
            tg = tg_ref[...]
            y = tg[:, 0:1] * cur[0]
            for k in range(1, TOP_K):
                y = y + tg[:, k:k + 1] * cur[k]
            o_ref[...] = _ln_rows(DEEPNORM_ALPHA * h_ref[...] + y, g_ref[...], b_ref[...])

        @pl.when(jnp.logical_and(i == n - 1, i % 2 == parity))
        def _():
            drain(nxt, 1 - parity)


def _moe_combine_ln(h2d, y_rows, dest, tg, g, b):
    n_tok, d = h2d.shape
    tm = MOE_COMBINE_ROWS
    nt = n_tok // tm
    dest3 = dest.reshape(nt, tm, TOP_K).transpose(0, 2, 1).reshape(nt, 1, TOP_K * tm)
    smem_blk = lambda imap: pl.BlockSpec((1, 1, TOP_K * tm), imap, memory_space=pltpu.SMEM)
    return pl.pallas_call(
        _moe_combine_ln_body,
        grid=(nt,),
        in_specs=[smem_blk(lambda i: (i, 0, 0)),
                  smem_blk(lambda i: (jnp.minimum(i + 1, nt - 1), 0, 0)),
                  pl.BlockSpec((tm, d), lambda i: (i, 0)),
                  pl.BlockSpec((tm, LANE), lambda i: (i, 0)),
                  pl.BlockSpec((1, d), lambda i: (0, 0)),
                  pl.BlockSpec((1, d), lambda i: (0, 0)),
                  pl.BlockSpec(memory_space=pl.ANY)],
        out_specs=pl.BlockSpec((tm, d), lambda i: (i, 0)),
        out_shape=jax.ShapeDtypeStruct((n_tok, d), jnp.float32),
        scratch_shapes=[pltpu.VMEM((TOP_K, tm, d), jnp.float32), pltpu.VMEM((TOP_K, tm, d), jnp.float32),
                        pltpu.SemaphoreType.DMA((2,))],
        compiler_params=_params("arbitrary"),
        name="moe_combine_ln",
    )(dest3, dest3, h2d, tg, g.reshape(1, d), b.reshape(1, d), y_rows)


def _moe_layer(h2d, h_packed, te, tg, layer, w_gu, b_gu, w_dn, b_dn, g, b):
    dest, row_tok, block_e, n_used = _moe_plan(te)
    y_rows = _moe_experts(h_packed, row_tok, block_e, n_used, layer, w_gu, b_gu, w_dn, b_dn)
    return _moe_combine_ln(h2d, y_rows, dest, tg, g, b)


def _mixers(proj, cos, sin, bsz, seq, lambda_init, a_kv_norm, a_w_uk, a_w_uv, c_lambda, c_subln,
            d_conv, d_a_log, d_dt_bias, d_norm):
    cdt = MXU_DTYPE
    (qT, iqT, iw, ikc, kvc, ckvT, qbd, kc, vT, rq, rkT, gatesT) = _mixer_prep(
        proj, cos, sin, a_kv_norm, a_w_uk.transpose(1, 0, 2).astype(cdt), bsz, seq)
    head_a = _dsa_attention(iqT, iw, qT, ikc, kvc, ckvT, a_w_uv.transpose(1, 0, 2).astype(cdt),
                            n_keep=min(INDEX_TOPK, seq // 4))
    head_b = _retention(rq, rkT, proj, bsz, seq)
    lam_full = (jnp.exp(jnp.sum(c_lambda[0] * c_lambda[1])) - jnp.exp(jnp.sum(c_lambda[2] * c_lambda[3]))
                + lambda_init)
    gain = (c_subln * (1.0 - lambda_init)).reshape(1, 2 * C_DH)
    head_c = _diff_attention(lam_full.reshape(1), qbd, kc, vT, gain)
    head_d = _gated_deltanet(proj, gatesT, d_conv, d_a_log, d_dt_bias, d_norm, bsz, seq)
    return [t.reshape(bsz * seq, t.shape[-1]) for t in (head_a, head_b, head_c, head_d)]


def kernel(x, positions, w_in, w_out, a_kv_norm, a_w_uk, a_w_uv, c_lambda, c_subln, d_conv, d_a_log, d_dt_bias, d_norm, ln1_g, ln1_b, w_router, b_router, w_gu, b_gu, w_dn, b_dn, ln2_g, ln2_b):
    bsz, seq, d = x.shape
    x2d = x.reshape(bsz * seq, d)
    cos, sin = _rope_tables(positions)
    for l in range(DEPTH):
        proj = _in_proj(x2d, _pack_w_in(w_in[l]))
        lambda_init = 0.8 - 0.6 * math.exp(-0.3 * l)
        heads = _mixers(proj, cos, sin, bsz, seq, lambda_init, a_kv_norm[l], a_w_uk[l], a_w_uv[l], c_lambda[l],
                        c_subln[l], d_conv[l], d_a_log[l], d_dt_bias[l], d_norm[l])
        h2d, h_packed, te, tg = _out_ln_router(heads, w_out[l].astype(MXU_DTYPE), x2d, ln1_g[l], ln1_b[l],
                                     w_router[l], b_router[l])
        x2d = _moe_layer(h2d, h_packed, te, tg, l, w_gu, b_gu, w_dn, b_dn, ln2_g[l], ln2_b[l])
    return x2d.reshape(bsz, seq, d)
```

```python
import functools
import math

import jax
import jax.numpy as jnp
import numpy as np
from jax import lax
from jax.experimental import pallas as pl
from jax.experimental.pallas import tpu as pltpu

D_MODEL = 2048
DEPTH = 4
A_HEADS, A_NOPE, A_ROPE, A_V, A_KV_RANK = 4, 128, 64, 128, 256
IDX_HEADS, IDX_DIM, INDEX_TOPK = 8, 64, 256
B_HEADS, B_QK, B_V, RET_CHUNK = 4, 64, 128, 128
C_HEADS, C_DH = 4, 64
D_HEADS, D_K, D_V, CONV_K, GDN_CHUNK = 4, 128, 128, 4, 64
N_EXPERTS, TOP_K, D_FF = 32, 4, 512
SWIGLU_ALPHA, SWIGLU_LIMIT = 1.702, 7.0
MOE_BLOCK = 256
ROPE_THETA = 10000.0
ROPE_DIM = 64
Q_BLOCK = 128
DEEPNORM_ALPHA = (2 * DEPTH) ** 0.25
LN_EPS = 1e-5
RMS_EPS = 1e-6
D_MIX = A_HEADS * A_V + B_HEADS * B_V + C_HEADS * 2 * C_DH + D_HEADS * D_V
A_LAT = A_KV_RANK + A_ROPE

IN_WIDTHS = (
    A_HEADS * (A_NOPE + A_ROPE), A_KV_RANK, A_ROPE, IDX_HEADS * IDX_DIM, IDX_DIM, IDX_HEADS,
    B_HEADS * B_QK, B_HEADS * B_QK, B_HEADS * B_V, B_HEADS * B_V,
    C_HEADS * 2 * C_DH, C_HEADS * 2 * C_DH, C_HEADS * 2 * C_DH,
    D_HEADS * (2 * D_K + D_V), D_HEADS * D_V, D_HEADS, D_HEADS,
)
_IN_OFFS = np.concatenate([[0], np.cumsum(IN_WIDTHS)]).tolist()

LANE = 128
SUBLANE = 8
VMEM_LIMIT = 56 * 1024 * 1024

MXU_DTYPE = jnp.bfloat16
_MASKED = -1e30

COL = 512
COL_D_QKV = 0
COL_D_Z = 3
COL_B_V = 4
COL_B_G = 5
COL_B_QK = 6
COL_C_Q = 7
COL_C_K = 8
COL_C_V = 9
COL_A_CKV_ROPE = 10
COL_A_NOPE = 11
COL_A_IQ = 12
NARROW = 256
COL_NARROW = 26
IN_PACKED = (COL_NARROW + 1) * NARROW
NARROW_IW = 0
NARROW_DB = IDX_HEADS
NARROW_DA = IDX_HEADS + D_HEADS


def _pack_w_in(w):
    g = lambda i: w[:, _IN_OFFS[i]:_IN_OFFS[i + 1]]
    a_q = g(0).reshape(w.shape[0], A_HEADS, A_NOPE + A_ROPE)
    nope = a_q[:, :, :A_NOPE].reshape(w.shape[0], A_HEADS * A_NOPE)
    rope = a_q[:, :, A_NOPE:].reshape(w.shape[0], A_HEADS * A_ROPE)
    cols = [g(13), g(14), g(8), g(9), g(6), g(7), g(10), g(11), g(12), g(1), rope, nope, g(3),
            g(2), g(4), g(5), g(15), g(16)]
    used = sum(c.shape[1] for c in cols)
    cols.append(jnp.zeros((w.shape[0], IN_PACKED - used), w.dtype))
    return jnp.concatenate(cols, axis=1).astype(MXU_DTYPE)


def _sigmoid(t):
    return 1.0 / (1.0 + jnp.exp(-t))


def _softplus(t):
    return jnp.maximum(t, 0.0) + jnp.log1p(jnp.exp(-jnp.abs(t)))


def _params(*sem):
    return pltpu.CompilerParams(dimension_semantics=sem, vmem_limit_bytes=VMEM_LIMIT)


def _in_proj_body(x_ref, w_ref, o_ref, xb_ref):
    @pl.when(pl.program_id(1) == 0)
    def _():
        xb_ref[...] = x_ref[...].astype(MXU_DTYPE)

    o_ref[...] = jnp.dot(xb_ref[...], w_ref[...], preferred_element_type=jnp.float32)


def _in_proj(x2d, w_packed, tm=1024, tn=768):
    m, k = x2d.shape
    n = w_packed.shape[1]
    return pl.pallas_call(
        _in_proj_body,
        grid=(m // tm, n // tn),
        in_specs=[pl.BlockSpec((tm, k), lambda i, j: (i, 0)),
                  pl.BlockSpec((k, tn), lambda i, j: (0, j))],
        out_specs=pl.BlockSpec((tm, tn), lambda i, j: (i, j)),
        out_shape=jax.ShapeDtypeStruct((m, n), jnp.float32),
        scratch_shapes=[pltpu.VMEM((tm, k), MXU_DTYPE)],
        compiler_params=_params("arbitrary", "arbitrary"),
        name="in_proj",
    )(x2d, w_packed)


PREP_TILE = 256
KCHUNK = 256


def _prep_body(bqk_ref, cq_ref, ck_ref, cv_ref, ckvrope_ref, nope_ref, iq_ref, nar_ref, cos_ref, sin_ref,
               kvn_ref, wukT_ref,
               qT_ref, iqT_ref, iw_ref, ik_ref, kv_ref, ckvT_ref, qbd_ref, kc_ref, vT_ref, rq_ref, rkT_ref, gT_ref):
    f32 = jnp.float32
    cdt = MXU_DTYPE
    cos, sin = cos_ref[0], sin_ref[0]
    lane = lax.broadcasted_iota(jnp.int32, (PREP_TILE, LANE), 1)
    first_half = (lane % ROPE_DIM) < ROPE_DIM // 2
    eye = (lax.broadcasted_iota(jnp.int32, (LANE, LANE), 0)
           == lax.broadcasted_iota(jnp.int32, (LANE, LANE), 1)).astype(cdt)
    top = lax.broadcasted_iota(jnp.int32, (LANE, LANE), 0) < LANE // 2
    top_wide = lax.broadcasted_iota(jnp.int32, (LANE, PREP_TILE), 0) < LANE // 2

    def rope(t):
        outs = []
        for j in range(t.shape[1] // LANE):
            ts = t[:, j * LANE:(j + 1) * LANE]
            partner = jnp.where(first_half, pltpu.roll(ts, LANE - ROPE_DIM // 2, 1), pltpu.roll(ts, ROPE_DIM // 2, 1))
            outs.append(ts * cos + partner * sin)
        return outs[0] if len(outs) == 1 else jnp.concatenate(outs, axis=1)

    def tr(t):
        return lax.dot_general(eye, t.astype(cdt), (((1,), (1,)), ((), ())), preferred_element_type=f32)

    ckv_rope = ckvrope_ref[...]
    a_ckv = ckv_rope[:, :A_KV_RANK]
    c_kv = a_ckv * lax.rsqrt(jnp.mean(a_ckv * a_ckv, -1, keepdims=True) + RMS_EPS) * kvn_ref[...]
    nar = nar_ref[...]
    kr_ik = rope(nar[:, :LANE])
    kv_ref[0, 0, :, :A_KV_RANK] = c_kv.astype(cdt)
    kv_ref[0, 0, :, A_KV_RANK:] = kr_ik[:, :A_ROPE].astype(cdt)
    ik_ref[0, 0] = kr_ik[:, A_ROPE:].astype(cdt)
    for rb in range(A_KV_RANK // LANE):
        ckvT_ref[0, 0, rb * LANE:(rb + 1) * LANE, :] = tr(c_kv[:, rb * LANE:(rb + 1) * LANE]).astype(cdt)
    q_rope = rope(ckv_rope[:, A_KV_RANK:])
    nope = nope_ref[...]
    iq = rope(iq_ref[...])
    narT = nar[:, LANE:].T
    gT_ref[0] = narT[NARROW_DB:NARROW_DB + 2 * D_HEADS]
    cq = rope(cq_ref[...]) * C_DH ** -0.5
    for j in range(PREP_TILE // Q_BLOCK):
        rows = slice(j * Q_BLOCK, (j + 1) * Q_BLOCK)
        for h in range(A_HEADS):
            q_latT = lax.dot_general(wukT_ref[h], nope[rows, h * A_NOPE:(h + 1) * A_NOPE].astype(cdt),
                                     (((1,), (1,)), ((), ())), preferred_element_type=f32)
            qT_ref[0, j, :A_KV_RANK, h * Q_BLOCK:(h + 1) * Q_BLOCK] = q_latT.astype(cdt)
        for p in range(A_HEADS * A_ROPE // LANE):
            t = tr(q_rope[rows, p * LANE:(p + 1) * LANE])
            qT_ref[0, j, A_KV_RANK:, (2 * p) * Q_BLOCK:(2 * p + 1) * Q_BLOCK] = t[:A_ROPE].astype(cdt)
            qT_ref[0, j, A_KV_RANK:, (2 * p + 1) * Q_BLOCK:(2 * p + 2) * Q_BLOCK] = t[A_ROPE:].astype(cdt)
        for p in range(IDX_HEADS * IDX_DIM // LANE):
            t = tr(iq[rows, p * LANE:(p + 1) * LANE])
            iqT_ref[0, j, :, (2 * p) * Q_BLOCK:(2 * p + 1) * Q_BLOCK] = t[:IDX_DIM].astype(cdt)
            iqT_ref[0, j, :, (2 * p + 1) * Q_BLOCK:(2 * p + 2) * Q_BLOCK] = t[IDX_DIM:].astype(cdt)
        iw_ref[0, j] = narT[NARROW_IW:NARROW_IW + IDX_HEADS, rows] * (IDX_HEADS * IDX_DIM) ** -0.5
        for h in range(C_HEADS):
            t = tr(cq[rows, h * LANE:(h + 1) * LANE])
            qbd_ref[0, j, h, :, :Q_BLOCK] = jnp.where(top, t, 0.0).astype(cdt)
            qbd_ref[0, j, h, :, Q_BLOCK:] = jnp.where(top, 0.0, t).astype(cdt)
    ck = rope(ck_ref[...])
    cv = cv_ref[...]
    for h in range(C_HEADS):
        kc_ref[0, 0, h] = ck[:, h * LANE:(h + 1) * LANE].astype(cdt)
        vT_ref[0, 0, h] = tr(cv[:, h * LANE:(h + 1) * LANE]).astype(cdt)
    bqk = rope(bqk_ref[...])
    rq_ref[0] = bqk[:, :B_HEADS * B_QK].astype(cdt)
    for p in range(B_HEADS * B_QK // LANE):
        t = tr(bqk[:, B_HEADS * B_QK + p * LANE:B_HEADS * B_QK + (p + 1) * LANE]) * B_QK ** -0.5
        rkT_ref[0, 2 * p] = jnp.where(top_wide, t, 0.0).astype(cdt)
        rkT_ref[0, 2 * p + 1] = jnp.where(top_wide, 0.0, t).astype(cdt)


def _mixer_prep(proj, cos, sin, kv_norm, wukT, bsz, seq):
    nt = seq // PREP_TILE
    nq = seq // Q_BLOCK
    cdt = MXU_DTYPE
    col = lambda width, c: pl.BlockSpec((PREP_TILE, width), lambda b, t: (b * nt + t, c))
    table = pl.BlockSpec((1, PREP_TILE, LANE), lambda b, t: (b, t, 0))
    per_q = lambda *tail: pl.BlockSpec((1, PREP_TILE // Q_BLOCK) + tail, lambda b, t: (b, t) + (0,) * len(tail))
    per_c = lambda *tail: pl.BlockSpec((1, 1) + tail, lambda b, t: (b, t) + (0,) * len(tail))
    sds = jax.ShapeDtypeStruct
    outs = [
        (sds((bsz, nq, A_LAT, A_HEADS * Q_BLOCK), cdt), per_q(A_LAT, A_HEADS * Q_BLOCK)),
        (sds((bsz, nq, IDX_DIM, IDX_HEADS * Q_BLOCK), cdt), per_q(IDX_DIM, IDX_HEADS * Q_BLOCK)),
        (sds((bsz, nq, IDX_HEADS, Q_BLOCK), jnp.float32), per_q(IDX_HEADS, Q_BLOCK)),
        (sds((bsz, nt, KCHUNK, IDX_DIM), cdt), per_c(KCHUNK, IDX_DIM)),
        (sds((bsz, nt, KCHUNK, A_LAT), cdt), per_c(KCHUNK, A_LAT)),
        (sds((bsz, nt, A_KV_RANK, KCHUNK), cdt), per_c(A_KV_RANK, KCHUNK)),
        (sds((bsz, nq, C_HEADS, 2 * C_DH, 2 * Q_BLOCK), cdt), per_q(C_HEADS, 2 * C_DH, 2 * Q_BLOCK)),
        (sds((bsz, nt, C_HEADS, KCHUNK, 2 * C_DH), cdt), per_c(C_HEADS, KCHUNK, 2 * C_DH)),
        (sds((bsz, nt, C_HEADS, 2 * C_DH, KCHUNK), cdt), per_c(C_HEADS, 2 * C_DH, KCHUNK)),
        (sds((bsz, seq, B_HEADS * B_QK), cdt), pl.BlockSpec((1, PREP_TILE, B_HEADS * B_QK), lambda b, t: (b, t, 0))),
        (sds((bsz, B_HEADS, LANE, seq), cdt), pl.BlockSpec((1, B_HEADS, LANE, PREP_TILE), lambda b, t: (b, 0, 0, t))),
        (sds((bsz, 2 * D_HEADS, seq), jnp.float32), pl.BlockSpec((1, 2 * D_HEADS, PREP_TILE), lambda b, t: (b, 0, t))),
    ]
    return pl.pallas_call(
        _prep_body,
        grid=(bsz, nt),
        in_specs=[col(COL, COL_B_QK), col(COL, COL_C_Q), col(COL, COL_C_K), col(COL, COL_C_V),
                  col(COL, COL_A_CKV_ROPE), col(COL, COL_A_NOPE), col(COL, COL_A_IQ), col(NARROW, COL_NARROW),
                  table, table,
                  pl.BlockSpec((1, A_KV_RANK), lambda b, t: (0, 0)),
                  pl.BlockSpec((A_HEADS, A_KV_RANK, A_NOPE), lambda b, t: (0, 0, 0))],
        out_specs=[o[1] for o in outs],
        out_shape=[o[0] for o in outs],
        compiler_params=_params("arbitrary", "arbitrary"),
        name="mixer_prep",
    )(proj, proj, proj, proj, proj, proj, proj, proj, cos, sin, kv_norm.reshape(1, A_KV_RANK), wukT)


def _rope_tables(positions):
    half = ROPE_DIM // 2
    inv_freq = ROPE_THETA ** (-jnp.arange(half, dtype=jnp.float32) / half)
    ang = positions.astype(jnp.float32)[:, :, None] * inv_freq
    cos, sin = jnp.cos(ang), jnp.sin(ang)
    reps = LANE // ROPE_DIM
    return (jnp.concatenate([cos, cos] * reps, -1), jnp.concatenate([-sin, sin] * reps, -1))


_SIGN_BIT = np.int32(-2 ** 31)
_LOW31 = np.int32(2 ** 31 - 1)
_KEY_NEG_INF = np.int32(np.array(-np.inf, np.float32).view(np.int32) ^ _LOW31)
DSA_QB = 4


def _dsa_body(iqT_ref, iw_ref, qT_ref, ik_ref, kv_ref, ckvT_ref, wuv_ref, o_ref,
              key_ref, acc_ref, thr_ref, need_ref, *, n_keep):
    f32 = jnp.float32
    blocks = range(DSA_QB)
    q0 = pl.program_id(1) * DSA_QB
    nk = ((q0 + DSA_QB) * Q_BLOCK + KCHUNK - 1) // KCHUNK
    iqT = [iqT_ref[0, j] for j in blocks]
    iw = [jnp.concatenate([iw_ref[0, j][h:h + 1] for h in range(IDX_HEADS)], axis=1) for j in blocks]
    lane_q = lax.broadcasted_iota(jnp.int32, (KCHUNK, Q_BLOCK), 1)
    qpos = [(q0 + j) * Q_BLOCK + lane_q for j in blocks]
    krow = lax.broadcasted_iota(jnp.int32, (KCHUNK, Q_BLOCK), 0)

    def index_chunk(c, carry):
        ik = ik_ref[0, c]
        lg = [jnp.dot(ik, iqT[j], preferred_element_type=f32) for j in blocks]
        w = [jnp.maximum(lg[j], 0.0) * iw[j] for j in blocks]
        for j in blocks:
            idx = w[j][:, :Q_BLOCK]
            for h in range(1, IDX_HEADS):
                idx = idx + w[j][:, h * Q_BLOCK:(h + 1) * Q_BLOCK]
            idx = jnp.where(idx == 0.0, 0.0, idx)
            idx = jnp.where(krow + c * KCHUNK <= qpos[j], idx, -jnp.inf)
            bits = pltpu.bitcast(idx, jnp.int32)
            key_ref[j, c] = bits ^ ((bits >> 31) & _LOW31)
        return carry

    lax.fori_loop(0, nk, index_chunk, 0)

    def count(cmp, ts):
        def body(c, cnts):
            hits = [cmp(key_ref[j, c], ts[j]).astype(jnp.int32) for j in blocks]
            return tuple(cnts[j] + hits[j].reshape(KCHUNK // SUBLANE, SUBLANE, Q_BLOCK).sum(0) for j in blocks)
        cnts = lax.fori_loop(0, nk, body, tuple(jnp.zeros((SUBLANE, Q_BLOCK), jnp.int32) for _ in blocks))
        return [cnt.sum(0, keepdims=True) for cnt in cnts]

    keep_all = [(q0 + j) * Q_BLOCK + Q_BLOCK <= n_keep for j in blocks]

    @pl.when(keep_all[-1])
    def _():
        for j in blocks:
            thr_ref[j] = jnp.full((1, Q_BLOCK), _KEY_NEG_INF, jnp.int32)
            need_ref[j] = jnp.zeros((1, Q_BLOCK), jnp.int32)

    @pl.when(jnp.logical_not(keep_all[-1]))
    def _():
        def bit_step(i, prefixes):
            bit = jnp.left_shift(jnp.int32(1), 31 - i)
            cands = [prefixes[j] | bit for j in blocks]
            cnts = count(lambda k, t: k >= t, [cand ^ _SIGN_BIT for cand in cands])
            return tuple(jnp.where(cnts[j] >= n_keep, cands[j], prefixes[j]) for j in blocks)
        prefixes = lax.fori_loop(0, 32, bit_step, tuple(jnp.zeros((1, Q_BLOCK), jnp.int32) for _ in blocks))
        thrs = [prefixes[j] ^ _SIGN_BIT for j in blocks]
        above = count(lambda k, t: k > t, thrs)
        for j in blocks:
            thr_ref[j] = jnp.where(keep_all[j], _KEY_NEG_INF, thrs[j])
            need_ref[j] = jnp.where(keep_all[j], 0, n_keep - above[j])

    thr = [thr_ref[j] for j in blocks]
    need = [need_ref[j].astype(f32) for j in blocks]
    qT = [qT_ref[0, j] for j in blocks]
    scale = (A_NOPE + A_ROPE) ** -0.5
    r_i = lax.broadcasted_iota(jnp.int32, (KCHUNK, KCHUNK), 0)
    c_i = lax.broadcasted_iota(jnp.int32, (KCHUNK, KCHUNK), 1)
    before = (c_i < r_i).astype(jnp.bfloat16)
    acc_ref[...] = jnp.zeros_like(acc_ref)

    def attend_chunk(c, carry):
        ms, ls, seen = carry
        kv, ckvT = kv_ref[0, c], ckvT_ref[0, c]
        s = [jnp.dot(kv, qT[j], preferred_element_type=f32) * scale for j in blocks]
        key = [key_ref[j, c] for j in blocks]
        tie = [key[j] == thr[j] for j in blocks]
        tie_f = [tie[j].astype(f32) for j in blocks]
        rank = [jnp.dot(before, tie_f[j].astype(jnp.bfloat16), preferred_element_type=f32) + seen[j]
                for j in blocks]
        bias = [jnp.where((key[j] > thr[j]) | (tie[j] & (rank[j] < need[j])), 0.0, _MASKED) for j in blocks]
        s = [s[j] + jnp.concatenate([bias[j]] * A_HEADS, axis=1) for j in blocks]
        m_new = [jnp.maximum(ms[j], s[j].max(0, keepdims=True)) for j in blocks]
        alpha = [jnp.exp(ms[j] - m_new[j]) for j in blocks]
        p = [jnp.exp(s[j] - m_new[j]) for j in blocks]
        l_new = [alpha[j] * ls[j] + p[j].sum(0, keepdims=True) for j in blocks]
        pv = [jnp.dot(ckvT, p[j].astype(ckvT.dtype), preferred_element_type=f32) for j in blocks]
        for j in blocks:
            acc_ref[j] = acc_ref[j] * alpha[j] + pv[j]
        return (tuple(m_new), tuple(l_new), tuple(seen[j] + tie_f[j].sum(0, keepdims=True) for j in blocks))

    lanes = A_HEADS * Q_BLOCK
    init = (tuple(jnp.full((1, lanes), _MASKED, f32) for _ in blocks),
            tuple(jnp.zeros((1, lanes), f32) for _ in blocks),
            tuple(jnp.zeros((1, Q_BLOCK), f32) for _ in blocks))
    _, ls, _ = lax.fori_loop(0, nk, attend_chunk, init)
    for j in blocks:
        o_latT = acc_ref[j] / ls[j]
        for h in range(A_HEADS):
            o_lat = o_latT[:, h * Q_BLOCK:(h + 1) * Q_BLOCK].T.astype(wuv_ref.dtype)
            o_ref[0, j * Q_BLOCK:(j + 1) * Q_BLOCK, h * A_V:(h + 1) * A_V] = jnp.dot(
                o_lat, wuv_ref[h], preferred_element_type=f32)


def _dsa_attention(iqT, iw, qT, ikc, kvc, ckvT, wuv, *, n_keep):
    bsz, nq = qT.shape[:2]
    nc = ikc.shape[1]
    seq = nq * Q_BLOCK
    per_q = lambda b, q: (b, q, 0, 0)
    per_b = lambda b, q: (b, 0, 0, 0)
    return pl.pallas_call(
        functools.partial(_dsa_body, n_keep=n_keep),
        grid=(bsz, nq // DSA_QB),
        in_specs=[pl.BlockSpec((1, DSA_QB) + iqT.shape[2:], per_q),
                  pl.BlockSpec((1, DSA_QB) + iw.shape[2:], per_q),
                  pl.BlockSpec((1, DSA_QB) + qT.shape[2:], per_q),
                  pl.BlockSpec((1,) + ikc.shape[1:], per_b),
                  pl.BlockSpec((1,) + kvc.shape[1:], per_b),
                  pl.BlockSpec((1,) + ckvT.shape[1:], per_b),
                  pl.BlockSpec(wuv.shape, lambda b, q: (0, 0, 0))],
        out_specs=pl.BlockSpec((1, DSA_QB * Q_BLOCK, A_HEADS * A_V), lambda b, q: (b, q, 0)),
        out_shape=jax.ShapeDtypeStruct((bsz, seq, A_HEADS * A_V), jnp.float32),
        scratch_shapes=[pltpu.VMEM((DSA_QB, nc, KCHUNK, Q_BLOCK), jnp.int32),
                        pltpu.VMEM((DSA_QB, A_KV_RANK, A_HEADS * Q_BLOCK), jnp.float32),
                        pltpu.VMEM((DSA_QB, 1, Q_BLOCK), jnp.int32),
                        pltpu.VMEM((DSA_QB, 1, Q_BLOCK), jnp.int32)],
        compiler_params=_params("arbitrary", "arbitrary"),
        name="dsa_attention",
    )(iqT, iw, qT, ikc, kvc, ckvT, wuv)


RET_TILE = 256


def _retention_body(q_ref, kT_ref, v_ref, g_ref, o_ref, s_ref):
    f32 = jnp.float32
    cdt = MXU_DTYPE
    c = RET_CHUNK

    @pl.when(pl.program_id(1) == 0)
    def _():
        s_ref[...] = jnp.zeros_like(s_ref)

    rel = (lax.broadcasted_iota(jnp.int32, (c, c), 0) - lax.broadcasted_iota(jnp.int32, (c, c), 1)).astype(f32)
    pos_c = lax.broadcasted_iota(jnp.int32, (c, 1), 0).astype(f32)
    pos_r = lax.broadcasted_iota(jnp.int32, (1, c), 1).astype(f32)

    def mm(a, b):
        return jnp.dot(a.astype(cdt), b.astype(cdt), preferred_element_type=f32)

    for h in range(B_HEADS):
        log_gamma = float(np.log(np.float32(1.0) - np.float32(2.0) ** np.float32(-5 - h)))
        intra = jnp.where(rel >= 0, jnp.exp(log_gamma * jnp.maximum(rel, 0.0)), 0.0)
        to_end = jnp.exp(log_gamma * (c - 1 - pos_r))
        from_start = jnp.exp(log_gamma * (pos_c + 1.0))
        chunk_decay = float(np.exp(np.float32(log_gamma) * np.float32(c)))
        lanes = slice(h * B_V, (h + 1) * B_V)
        pair = slice((h // 2) * LANE, (h // 2 + 1) * LANE)
        for ci in range(RET_TILE // c):
            rows = slice(ci * c, (ci + 1) * c)
            q = q_ref[0, rows, pair].astype(f32)
            kT = kT_ref[0, h, :, rows].astype(f32)
            v = v_ref[rows, lanes]
            state = s_ref[h]
            o = mm(mm(q, kT) * intra, v) + mm(q * from_start, state)
            s_ref[h] = state * chunk_decay + mm(kT * to_end, v)
            gate = g_ref[rows, lanes]
            o = o * lax.rsqrt(jnp.mean(o * o, -1, keepdims=True) + RMS_EPS)
            o_ref[0, rows, lanes] = o * (gate * _sigmoid(gate))


def _retention(rq, rkT, proj, bsz, seq):
    nt = seq // RET_TILE
    width = B_HEADS * B_V
    col = lambda c: pl.BlockSpec((RET_TILE, COL), lambda b, t: (b * nt + t, c))
    return pl.pallas_call(
        _retention_body,
        grid=(bsz, nt),
        in_specs=[pl.BlockSpec((1, RET_TILE, B_HEADS * B_QK), lambda b, t: (b, t, 0)),
                  pl.BlockSpec((1, B_HEADS, LANE, RET_TILE), lambda b, t: (b, 0, 0, t)),
                  col(COL_B_V), col(COL_B_G)],
        out_specs=pl.BlockSpec((1, RET_TILE, width), lambda b, t: (b, t, 0)),
        out_shape=jax.ShapeDtypeStruct((bsz, seq, width), jnp.float32),
        scratch_shapes=[pltpu.VMEM((B_HEADS, LANE, B_V), jnp.float32)],
        compiler_params=_params("arbitrary", "arbitrary"),
        name="retention",
    )(rq, rkT, proj, proj)


DIFF_QB = 2


def _diff_attention_body(lam_ref, qbd_ref, k_ref, vT_ref, gain_ref, o_ref, acc_ref):
    f32 = jnp.float32
    blocks = range(DIFF_QB)
    heads = range(C_HEADS)
    q0 = pl.program_id(1) * DIFF_QB
    nk = ((q0 + DIFF_QB) * Q_BLOCK + KCHUNK - 1) // KCHUNK
    lane_q = lax.broadcasted_iota(jnp.int32, (KCHUNK, Q_BLOCK), 1)
    krow = lax.broadcasted_iota(jnp.int32, (KCHUNK, Q_BLOCK), 0)
    q_all = [jnp.concatenate([qbd_ref[0, j, h] for j in blocks], axis=1) for h in heads]
    acc_ref[...] = jnp.zeros_like(acc_ref)

    def chunk(c, carry):
        ms, ls = carry
        causal = [jnp.where(krow + c * KCHUNK <= (q0 + j) * Q_BLOCK + lane_q, 0.0, _MASKED) for j in blocks]
        bias = jnp.concatenate([causal[j] for j in blocks for _ in range(2)], axis=1)
        s = [jnp.dot(k_ref[0, c, h], q_all[h], preferred_element_type=f32) + bias for h in heads]
        new_ms = [jnp.maximum(ms[h], s[h].max(0, keepdims=True)) for h in heads]
        alpha = [jnp.exp(ms[h] - new_ms[h]) for h in heads]
        p = [jnp.exp(s[h] - new_ms[h]) for h in heads]
        new_ls = [alpha[h] * ls[h] + p[h].sum(0, keepdims=True) for h in heads]
        pv = [jnp.dot(vT_ref[0, c, h], p[h].astype(vT_ref.dtype), preferred_element_type=f32) for h in heads]
        for h in heads:
            acc_ref[h] = acc_ref[h] * alpha[h] + pv[h]
        return tuple(new_ms), tuple(new_ls)

    lanes = DIFF_QB * 2 * Q_BLOCK
    init = (tuple(jnp.full((1, lanes), _MASKED, f32) for _ in heads),
            tuple(jnp.zeros((1, lanes), f32) for _ in heads))
    _, ls = lax.fori_loop(0, nk, chunk, init)
    lam = lam_ref[0]
    for h in heads:
        o_all = acc_ref[h] / ls[h]
        for j in blocks:
            o1 = o_all[:, (2 * j) * Q_BLOCK:(2 * j + 1) * Q_BLOCK]
            o2 = o_all[:, (2 * j + 1) * Q_BLOCK:(2 * j + 2) * Q_BLOCK]
            o = o1 - lam * o2
            o = o * lax.rsqrt(jnp.mean(o * o, 0, keepdims=True) + RMS_EPS)
            o_ref[0, j * Q_BLOCK:(j + 1) * Q_BLOCK, h * 2 * C_DH:(h + 1) * 2 * C_DH] = o.T * gain_ref[...]


def _diff_attention(lam, qbd, kc, vT, gain):
    bsz, nq = qbd.shape[:2]
    seq = nq * Q_BLOCK
    width = C_HEADS * 2 * C_DH
    return pl.pallas_call(
        _diff_attention_body,
        grid=(bsz, nq // DIFF_QB),
        in_specs=[pl.BlockSpec(memory_space=pltpu.SMEM),
                  pl.BlockSpec((1, DIFF_QB) + qbd.shape[2:], lambda b, q: (b, q, 0, 0, 0)),
                  pl.BlockSpec((1,) + kc.shape[1:], lambda b, q: (b, 0, 0, 0, 0)),
                  pl.BlockSpec((1,) + vT.shape[1:], lambda b, q: (b, 0, 0, 0, 0)),
                  pl.BlockSpec(gain.shape, lambda b, q: (0, 0))],
        out_specs=pl.BlockSpec((1, DIFF_QB * Q_BLOCK, width), lambda b, q: (b, q, 0)),
        out_shape=jax.ShapeDtypeStruct((bsz, seq, width), jnp.float32),
        scratch_shapes=[pltpu.VMEM((C_HEADS, 2 * C_DH, DIFF_QB * 2 * Q_BLOCK), jnp.float32)],
        compiler_params=_params("arbitrary", "arbitrary"),
        name="diff_attention",
    )(lam, qbd, kc, vT, gain)


GDN_TILE = 256
GDN_SUB = 16


def _gdn_body(qkv_ref, z_ref, nar_ref, gateT_ref, conv_ref, alog_r_ref, dtb_r_ref, alog_c_ref, dtb_c_ref,
              ng_ref, o_ref, halo_ref, s_ref):
    f32 = jnp.float32
    cdt = MXU_DTYPE
    c = GDN_CHUNK

    @pl.when(pl.program_id(1) == 0)
    def _():
        halo_ref[...] = jnp.zeros_like(halo_ref)
        s_ref[...] = jnp.zeros_like(s_ref)

    x = qkv_ref[...]
    xp = jnp.concatenate([halo_ref[...], x], 0)
    w = conv_ref[...]
    pre = xp[SUBLANE - CONV_K + 1:SUBLANE - CONV_K + 1 + GDN_TILE] * w[0:1]
    for j in range(1, CONV_K):
        off = SUBLANE - CONV_K + 1 + j
        pre = pre + xp[off:off + GDN_TILE] * w[j:j + 1]
    halo_ref[...] = x[GDN_TILE - SUBLANE:]
    qkv = pre * _sigmoid(pre)

    gates = nar_ref[:, LANE:]
    beta_c = _sigmoid(gates[:, NARROW_DB:NARROW_DB + D_HEADS])
    g_c = -jnp.exp(alog_r_ref[...]) * _softplus(gates[:, NARROW_DA:NARROW_DA + D_HEADS] + dtb_r_ref[...])
    g_r = -jnp.exp(alog_c_ref[...]) * _softplus(gateT_ref[0][D_HEADS:] + dtb_c_ref[...])

    row = lax.broadcasted_iota(jnp.int32, (c, c), 0)
    col = lax.broadcasted_iota(jnp.int32, (c, c), 1)
    tril = row >= col
    strict = row > col
    same_sub = (row // GDN_SUB) == (col // GDN_SUB)
    eye = (row == col).astype(f32)
    lower_ones = tril.astype(f32)
    upper_ones = (row <= col).astype(f32)

    def mm(a, b):
        return jnp.dot(a, b, preferred_element_type=f32)

    n_chunks = GDN_TILE // c
    pairs = [(ci, h) for ci in range(n_chunks) for h in range(D_HEADS)]
    rows_of = lambda ci: slice(ci * c, (ci + 1) * c)
    g_cum_c = [mm(lower_ones, g_c[rows_of(ci)]) for ci in range(n_chunks)]
    g_cum_r = [mm(g_r[:, rows_of(ci)], upper_ones) for ci in range(n_chunks)]
    st = []
    for ci, h in pairs:
        rows = rows_of(ci)
        q = qkv[rows, h * D_K:(h + 1) * D_K]
        k = qkv[rows, D_HEADS * D_K + h * D_K:D_HEADS * D_K + (h + 1) * D_K]
        v = qkv[rows, 2 * D_HEADS * D_K + h * D_V:2 * D_HEADS * D_K + (h + 1) * D_V]
        q = q * lax.rsqrt(jnp.sum(q * q, -1, keepdims=True) + RMS_EPS) * D_K ** -0.5
        k = k * lax.rsqrt(jnp.sum(k * k, -1, keepdims=True) + RMS_EPS)
        gc = g_cum_c[ci][:, h:h + 1]
        gr = g_cum_r[ci][h:h + 1, :]
        g_last = gc[c - 1:c]
        gam = jnp.where(tril, jnp.exp(jnp.where(tril, gc - gr, 0.0)), 0.0)
        e_g = jnp.exp(gc)
        beta = beta_c[rows, h:h + 1]
        kT = k.T
        kb = k * beta
        st.append(dict(gam=gam, kT_c=kT.astype(cdt), kb_c=kb.astype(cdt), q_c=q.astype(cdt),
                       qg_c=(q * e_g).astype(cdt), rhs=jnp.concatenate([v * beta, kb * e_g], 1),
                       kdT_c=(kT * jnp.exp(g_last - gr)).astype(cdt), decay=jnp.exp(g_last)))
    for p in st:
        p["low"] = jnp.where(strict, mm(p["kb_c"], p["kT_c"]) * p["gam"], 0.0)
        p["attn_c"] = jnp.where(tril, mm(p["q_c"], p["kT_c"]) * p["gam"], 0.0).astype(cdt)
        p["low_d"] = jnp.where(same_sub, p["low"], 0.0)
        p["xk"] = -p["low_d"]
        p["td"] = eye + p["xk"]
    for _ in range(3):
        for p in st:
            p["xk"] = mm(p["xk"], p["xk"])
        for p in st:
            p["td"] = p["td"] + mm(p["td"], p["xk"])
    for p in st:
        p["m1"] = mm(p["td"], p["low"] - p["low_d"])
        p["sol"] = mm(p["td"], p["rhs"])
    for p in st:
        p["m2"] = mm(p["m1"], p["m1"])
    for p in st:
        p["sol"] = p["sol"] + mm(p["m2"], p["sol"])
    for p in st:
        p["sol"] = p["sol"] - mm(p["m1"], p["sol"])
    for ci in range(n_chunks):
        rows = rows_of(ci)
        cur = [st[ci * D_HEADS + h] for h in range(D_HEADS)]
        states = [s_ref[h] for h in range(D_HEADS)]
        states_c = [s.astype(cdt) for s in states]
        v_new_c = [(p["sol"][:, :D_V] - mm(p["sol"][:, D_V:].astype(cdt), states_c[h])).astype(cdt)
                   for h, p in enumerate(cur)]
        for h, p in enumerate(cur):
            s_ref[h] = states[h] * p["decay"] + mm(p["kdT_c"], v_new_c[h])
        for h, p in enumerate(cur):
            lanes = slice(h * D_V, (h + 1) * D_V)
            o = mm(p["qg_c"], states_c[h]) + mm(p["attn_c"], v_new_c[h])
            zz = z_ref[rows, lanes]
            o = o * lax.rsqrt(jnp.mean(o * o, -1, keepdims=True) + RMS_EPS) * ng_ref[...]
            o_ref[0, rows, lanes] = o * (zz * _sigmoid(zz))


def _gated_deltanet(proj, gatesT, conv_w, a_log, dt_bias, norm_g, bsz, seq):
    nt = seq // GDN_TILE
    width = D_HEADS * (2 * D_K + D_V)
    whole = lambda r, cc: pl.BlockSpec((r, cc), lambda b, t: (0, 0))
    return pl.pallas_call(
        _gdn_body,
        grid=(bsz, nt),
        in_specs=[pl.BlockSpec((GDN_TILE, width), lambda b, t: (b * nt + t, COL_D_QKV)),
                  pl.BlockSpec((GDN_TILE, COL), lambda b, t: (b * nt + t, COL_D_Z)),
                  pl.BlockSpec((GDN_TILE, NARROW), lambda b, t: (b * nt + t, COL_NARROW)),
                  pl.BlockSpec((1, 2 * D_HEADS, GDN_TILE), lambda b, t: (b, 0, t)),
                  whole(CONV_K, width), whole(1, D_HEADS), whole(1, D_HEADS), whole(D_HEADS, 1), whole(D_HEADS, 1),
                  whole(1, D_V)],
        out_specs=pl.BlockSpec((1, GDN_TILE, D_HEADS * D_V), lambda b, t: (b, t, 0)),
        out_shape=jax.ShapeDtypeStruct((bsz, seq, D_HEADS * D_V), jnp.float32),
        scratch_shapes=[pltpu.VMEM((SUBLANE, width), jnp.float32),
                        pltpu.VMEM((D_HEADS, D_K, D_V), jnp.float32)],
        compiler_params=_params("arbitrary", "arbitrary"),
        name="gated_deltanet",
    )(proj, proj, proj, gatesT, conv_w, a_log.reshape(1, D_HEADS), dt_bias.reshape(1, D_HEADS),
      a_log.reshape(D_HEADS, 1), dt_bias.reshape(D_HEADS, 1), norm_g.reshape(1, D_V))


def _ln_rows(t, g, b):
    mu = jnp.mean(t, -1, keepdims=True)
    d = t - mu
    var = jnp.mean(d * d, -1, keepdims=True)
    return d * lax.rsqrt(var + LN_EPS) * g + b


_HIGH16 = np.int32(-(2 ** 16))


def _pack_bf16_pairs(t):
    n = t.shape[1] // 2
    lo = pltpu.bitcast(t[:, :n].astype(jnp.bfloat16).astype(jnp.float32), jnp.int32)
    hi = pltpu.bitcast(t[:, n:].astype(jnp.bfloat16).astype(jnp.float32), jnp.int32)
    return lax.shift_right_logical(lo, 16) | (hi & _HIGH16)


def _unpack_bf16_pairs(w):
    lo = pltpu.bitcast(w << 16, jnp.float32)
    hi = pltpu.bitcast(w & _HIGH16, jnp.float32)
    return jnp.concatenate([lo, hi], axis=1).astype(jnp.bfloat16)


def _out_ln_router_body(ha_ref, hb_ref, hc_ref, hd_ref, w_ref, x_ref, g_ref, b_ref, wr_ref, br_ref,
                        h_ref, hp_ref, te_ref, tg_ref):
    f32 = jnp.float32
    acc = DEEPNORM_ALPHA * x_ref[...]
    row0 = 0
    for part in (ha_ref, hb_ref, hc_ref, hd_ref):
        width = part.shape[-1]
        acc = acc + jnp.dot(part[...].astype(MXU_DTYPE), w_ref[row0:row0 + width, :], preferred_element_type=f32)
        row0 += width
    h = _ln_rows(acc, g_ref[...], b_ref[...])
    h_ref[...] = h
    hp_ref[...] = _pack_bf16_pairs(h)
    logits = jnp.dot(h.astype(MXU_DTYPE), wr_ref[...], preferred_element_type=f32) + br_ref[...]
    lane = lax.broadcasted_iota(jnp.int32, logits.shape, 1)
    vals, te = [], jnp.zeros(logits.shape, jnp.int32)
    for k in range(TOP_K):
        m = logits.max(-1, keepdims=True)
        e = jnp.min(jnp.where(logits == m, lane, LANE), -1, keepdims=True)
        vals.append(m)
        te = jnp.where(lane == k, e, te)
        logits = jnp.where(lane == e, -jnp.inf, logits)
    ex = [jnp.exp(v - vals[0]) for v in vals]
    den = ex[0] + ex[1] + ex[2] + ex[3]
    tg = jnp.zeros(logits.shape, f32)
    for k in range(TOP_K):
        tg = jnp.where(lane == k, ex[k] / den, tg)
    te_ref[...] = te
    tg_ref[...] = tg


def _out_ln_router(head_parts, w_out_c, x2d, g, b, w_router, b_router, tm=512):
    m = x2d.shape[0]
    k, n = w_out_c.shape
    wr = jnp.zeros((n, LANE), MXU_DTYPE).at[:, :N_EXPERTS].set(w_router.astype(MXU_DTYPE))
    br = jnp.full((1, LANE), -jnp.inf, jnp.float32).at[0, :N_EXPERTS].set(b_router)
    row_blk = lambda c: pl.BlockSpec((tm, c), lambda i: (i, 0))
    whole = lambda r, c: pl.BlockSpec((r, c), lambda i: (0, 0))
    return pl.pallas_call(
        _out_ln_router_body,
        grid=(m // tm,),
        in_specs=[row_blk(p.shape[1]) for p in head_parts]
        + [whole(k, n), row_blk(n), whole(1, n), whole(1, n), whole(n, LANE), whole(1, LANE)],
        out_specs=[row_blk(n), row_blk(n // 2), row_blk(LANE), row_blk(LANE)],
        out_shape=[jax.ShapeDtypeStruct((m, n), jnp.float32),
                   jax.ShapeDtypeStruct((m, n // 2), jnp.int32),
                   jax.ShapeDtypeStruct((m, LANE), jnp.int32),
                   jax.ShapeDtypeStruct((m, LANE), jnp.float32)],
        compiler_params=_params("arbitrary"),
        name="out_proj_ln_router",
    )(*head_parts, w_out_c, x2d, g.reshape(1, n), b.reshape(1, n), wr, br)


def _clamped_swiglu(hg):
    glu, lin = hg[..., :D_FF], hg[..., D_FF:]
    glu = jnp.minimum(glu, SWIGLU_LIMIT)
    lin = jnp.clip(lin, -SWIGLU_LIMIT, SWIGLU_LIMIT)
    return glu * _sigmoid(SWIGLU_ALPHA * glu) * (lin + 1.0)


MOE_RANK_TILE = 512


def _moe_rank_body(te_ref, rank_ref, counts_ref, seen_ref):
    f32 = jnp.float32
    tm = MOE_RANK_TILE

    @pl.when(pl.program_id(0) == 0)
    def _():
        seen_ref[...] = jnp.zeros_like(seen_ref)

    te = te_ref[...]
    lane = lax.broadcasted_iota(jnp.int32, (tm, LANE), 1)
    picks = [lane == te[:, k:k + 1] for k in range(TOP_K)]
    member = picks[0].astype(f32)
    for k in range(1, TOP_K):
        member = member + picks[k].astype(f32)
    earlier = (lax.broadcasted_iota(jnp.int32, (tm, tm), 1) < lax.broadcasted_iota(jnp.int32, (tm, tm), 0))
    prefix = jnp.dot(earlier.astype(jnp.bfloat16), member.astype(jnp.bfloat16), preferred_element_type=f32)
    prefix = prefix + seen_ref[...]
    rank = jnp.zeros((tm, LANE), jnp.int32)
    for k in range(TOP_K):
        r_k = jnp.sum(jnp.where(picks[k], prefix, 0.0), -1, keepdims=True)
        rank = jnp.where(lane == k, r_k.astype(jnp.int32), rank)
    rank_ref[...] = rank
    seen = seen_ref[...] + member.sum(0, keepdims=True)
    seen_ref[...] = seen
    counts_ref[...] = seen.astype(jnp.int32)


def _moe_rank(te):
    n_tok = te.shape[0]
    tm = MOE_RANK_TILE
    return pl.pallas_call(
        _moe_rank_body,
        grid=(n_tok // tm,),
        in_specs=[pl.BlockSpec((tm, LANE), lambda i: (i, 0))],
        out_specs=[pl.BlockSpec((tm, LANE), lambda i: (i, 0)), pl.BlockSpec((1, LANE), lambda i: (0, 0))],
        out_shape=[jax.ShapeDtypeStruct((n_tok, LANE), jnp.int32), jax.ShapeDtypeStruct((1, LANE), jnp.int32)],
        scratch_shapes=[pltpu.VMEM((1, LANE), jnp.float32)],
        compiler_params=_params("arbitrary"),
        name="moe_rank",
    )(te)


def _moe_plan(te):
    top_e = te[:, :TOP_K]
    n_tok = top_e.shape[0]
    n_blocks = -(-n_tok * TOP_K // MOE_BLOCK) + N_EXPERTS + MOE_GATHER_AHEAD - 1
    rank, counts = _moe_rank(te)
    counts = counts[0, :N_EXPERTS]
    padded = (counts + MOE_BLOCK - 1) // MOE_BLOCK * MOE_BLOCK
    pad_end = jnp.cumsum(padded)
    one_hot = top_e[:, :, None] == jnp.arange(N_EXPERTS, dtype=jnp.int32)
    dest = rank[:, :TOP_K] + jnp.sum(jnp.where(one_hot, (pad_end - padded)[None, None, :], 0), -1)
    tok = jnp.broadcast_to(jnp.arange(n_tok, dtype=jnp.int32)[:, None], dest.shape)
    row_tok = jnp.zeros((n_blocks * MOE_BLOCK,), jnp.int32).at[dest.reshape(-1)].set(tok.reshape(-1))
    block_start = jnp.arange(n_blocks, dtype=jnp.int32) * MOE_BLOCK
    block_e = jnp.minimum(jnp.sum(pad_end[None, :] <= block_start[:, None], axis=1), N_EXPERTS - 1).astype(jnp.int32)
    n_used = (pad_end[-1:] // MOE_BLOCK).astype(jnp.int32)
    return dest, row_tok, block_e, n_used


def _row_gather_start(src_hbm, rows_smem, n_rows, dst, sem):
    def body(r, carry):
        pltpu.make_async_copy(src_hbm.at[pl.ds(rows_smem[0, 0, r], 1)], dst.at[pl.ds(r, 1)], sem).start()
        return carry
    lax.fori_loop(0, n_rows, body, 0, unroll=8)


def _row_gather_wait(src_hbm, n_rows, dst, sem):
    pltpu.make_async_copy(src_hbm.at[pl.ds(0, n_rows)], dst, sem).wait()


MOE_GATHER_AHEAD = 2
MOE_GATHER_BUFS = MOE_GATHER_AHEAD + 1


def _moe_expert_body(be_ref, nused_ref, tok_ref, tok_1_ref, tok_ahead_ref, h_hbm, wgu_ref, bgu_ref, wdn_ref, bdn_ref,
                     y_ref, buf_0, buf_1, buf_2, sem, wgu_c, wdn_c):
    f32 = jnp.float32
    i = pl.program_id(0)
    n_used = nused_ref[0]
    bufs = (buf_0, buf_1, buf_2)

    @pl.when(i == 0)
    def _():
        _row_gather_start(h_hbm, tok_ref, MOE_BLOCK, buf_0, sem.at[0])
        _row_gather_start(h_hbm, tok_1_ref, MOE_BLOCK, buf_1, sem.at[1])

    @pl.when(jnp.logical_or(i == 0, be_ref[i] != be_ref[jnp.maximum(i - 1, 0)]))
    def _():
        wgu_c[...] = wgu_ref[0, 0].astype(MXU_DTYPE)
        wdn_c[...] = wdn_ref[0, 0].astype(MXU_DTYPE)

    for phase in range(MOE_GATHER_BUFS):
        cur = bufs[phase]
        ahead = (phase + MOE_GATHER_AHEAD) % MOE_GATHER_BUFS

        @pl.when(jnp.logical_and(i < n_used, i % MOE_GATHER_BUFS == phase))
        def _():
            _row_gather_wait(h_hbm, MOE_BLOCK, cur, sem.at[phase])
            for r in range(MOE_BLOCK):
                pltpu.make_async_copy(h_hbm.at[pl.ds(tok_ahead_ref[0, 0, r], 1)], bufs[ahead].at[pl.ds(r, 1)],
                                      sem.at[ahead]).start()
            xb = _unpack_bf16_pairs(cur[...]).astype(MXU_DTYPE)
            hg = jnp.dot(xb, wgu_c[...], preferred_element_type=f32) + bgu_ref[0, 0]
            act = _clamped_swiglu(hg)
            y_ref[...] = jnp.dot(act.astype(MXU_DTYPE), wdn_c[...], preferred_element_type=f32) + bdn_ref[0, 0]

        @pl.when(jnp.logical_and(jnp.logical_and(i >= n_used, i < n_used + MOE_GATHER_AHEAD),
                                 i % MOE_GATHER_BUFS == phase))
        def _():
            _row_gather_wait(h_hbm, MOE_BLOCK, cur, sem.at[phase])

    @pl.when(i >= n_used)
    def _():
        y_ref[...] = jnp.zeros_like(y_ref)


def _moe_experts(h_packed, row_tok, block_e, n_used, layer, w_gu, b_gu, w_dn, b_dn):
    n_tok = h_packed.shape[0]
    d = 2 * h_packed.shape[1]
    n_blocks = block_e.shape[0]
    n_layers = w_gu.shape[0]
    assert (n_tok * TOP_K + N_EXPERTS * (MOE_BLOCK - 1)) // MOE_BLOCK + MOE_GATHER_AHEAD <= n_blocks
    tok3 = row_tok.reshape(n_blocks, 1, MOE_BLOCK)
    smem_blk = lambda imap: pl.BlockSpec((1, 1, MOE_BLOCK), imap, memory_space=pltpu.SMEM)
    by_expert = lambda shape: pl.BlockSpec((1, 1) + shape, lambda i, be, nu: (layer, be[i], 0, 0))
    gather_buf = pltpu.VMEM((MOE_BLOCK, d // 2), jnp.int32)
    return pl.pallas_call(
        _moe_expert_body,
        grid_spec=pltpu.PrefetchScalarGridSpec(
            num_scalar_prefetch=2,
            grid=(n_blocks,),
            in_specs=[smem_blk(lambda i, be, nu: (i, 0, 0)),
                      smem_blk(lambda i, be, nu: (1, 0, 0)),
                      smem_blk(lambda i, be, nu: (jnp.minimum(i + MOE_GATHER_AHEAD, n_blocks - 1), 0, 0)),
                      pl.BlockSpec(memory_space=pl.ANY),
                      by_expert((d, 2 * D_FF)), by_expert((1, 2 * D_FF)),
                      by_expert((D_FF, d)), by_expert((1, d))],
            out_specs=pl.BlockSpec((MOE_BLOCK, d), lambda i, be, nu: (i, 0)),
            scratch_shapes=[gather_buf] * MOE_GATHER_BUFS
            + [pltpu.SemaphoreType.DMA((MOE_GATHER_BUFS,)),
               pltpu.VMEM((d, 2 * D_FF), MXU_DTYPE), pltpu.VMEM((D_FF, d), MXU_DTYPE)]),
        out_shape=jax.ShapeDtypeStruct((n_blocks * MOE_BLOCK, d), jnp.float32),
        compiler_params=_params("arbitrary"),
        name="moe_experts",
    )(block_e, n_used, tok3, tok3, tok3, h_packed, w_gu, b_gu.reshape(n_layers, N_EXPERTS, 1, 2 * D_FF),
      w_dn, b_dn.reshape(n_layers, N_EXPERTS, 1, d))


MOE_COMBINE_ROWS = 128


def _moe_combine_ln_body(dest_ref, dest_next_ref, h_ref, tg_ref, g_ref, b_ref, y_hbm, o_ref,
                         buf_even, buf_odd, sem):
    tm = MOE_COMBINE_ROWS
    i = pl.program_id(0)
    n = pl.num_programs(0)
    bufs = (buf_even, buf_odd)

    def start(rows_smem, dst, s):
        for k in range(TOP_K):
            for r in range(tm):
                pltpu.make_async_copy(y_hbm.at[pl.ds(rows_smem[0, 0, k * tm + r], 1)],
                                      dst.at[k, pl.ds(r, 1)], sem.at[s]).start()

    def drain(dst, s):
        for k in range(TOP_K):
            _row_gather_wait(y_hbm, tm, dst.at[k], sem.at[s])

    @pl.when(i == 0)
    def _():
        start(dest_ref, buf_even, 0)

    for parity in (0, 1):
        cur, nxt = bufs[parity], bufs[1 - parity]

        @pl.when(i % 2 == parity)
        def _():
            drain(cur, parity)
            start(dest_next_ref, nxt, 1 - parity)
            tg = tg_ref[...]
            y = tg[:, 0:1] * cur[0]
            for k in range(1, TOP_K):
                y = y + tg[:, k:k + 1] * cur[k]
            o_ref[...] = _ln_rows(DEEPNORM_ALPHA * h_ref[...] + y, g_ref[...], b_ref[...])

        @pl.when(jnp.logical_and(i == n - 1, i % 2 == parity))
        def _():
            drain(nxt, 1 - parity)


def _moe_combine_ln(h2d, y_rows, dest, tg, g, b):
    n_tok, d = h2d.shape
    tm = MOE_COMBINE_ROWS
    nt = n_tok // tm
    dest3 = dest.reshape(nt, tm, TOP_K).transpose(0, 2, 1).reshape(nt, 1, TOP_K * tm)
    smem_blk = lambda imap: pl.BlockSpec((1, 1, TOP_K * tm), imap, memory_space=pltpu.SMEM)
    return pl.pallas_call(
        _moe_combine_ln_body,
        grid=(nt,),
        in_specs=[smem_blk(lambda i: (i, 0, 0)),
                  smem_blk(lambda i: (jnp.minimum(i + 1, nt - 1), 0, 0)),
                  pl.BlockSpec((tm, d), lambda i: (i, 0)),
                  pl.BlockSpec((tm, LANE), lambda i: (i, 0)),
                  pl.BlockSpec((1, d), lambda i: (0, 0)),
                  pl.BlockSpec((1, d), lambda i: (0, 0)),
                  pl.BlockSpec(memory_space=pl.ANY)],
        out_specs=pl.BlockSpec((tm, d), lambda i: (i, 0)),
        out_shape=jax.ShapeDtypeStruct((n_tok, d), jnp.float32),
        scratch_shapes=[pltpu.VMEM((TOP_K, tm, d), jnp.float32), pltpu.VMEM((TOP_K, tm, d), jnp.float32),
                        pltpu.SemaphoreType.DMA((2,))],
        compiler_params=_params("arbitrary"),
        name="moe_combine_ln",
    )(dest3, dest3, h2d, tg, g.reshape(1, d), b.reshape(1, d), y_rows)


def _moe_layer(h2d, h_packed, te, tg, layer, w_gu, b_gu, w_dn, b_dn, g, b):
    dest, row_tok, block_e, n_used = _moe_plan(te)
    y_rows = _moe_experts(h_packed, row_tok, block_e, n_used, layer, w_gu, b_gu, w_dn, b_dn)
    return _moe_combine_ln(h2d, y_rows, dest, tg, g, b)


def _mixers(proj, cos, sin, bsz, seq, lambda_init, a_kv_norm, a_w_uk, a_w_uv, c_lambda, c_subln,
            d_conv, d_a_log, d_dt_bias, d_norm):
    cdt = MXU_DTYPE
    (qT, iqT, iw, ikc, kvc, ckvT, qbd, kc, vT, rq, rkT, gatesT) = _mixer_prep(
        proj, cos, sin, a_kv_norm, a_w_uk.transpose(1, 0, 2).astype(cdt), bsz, seq)
    head_a = _dsa_attention(iqT, iw, qT, ikc, kvc, ckvT, a_w_uv.transpose(1, 0, 2).astype(cdt),
                            n_keep=min(INDEX_TOPK, seq // 4))
    head_b = _retention(rq, rkT, proj, bsz, seq)
    lam_full = (jnp.exp(jnp.sum(c_lambda[0] * c_lambda[1])) - jnp.exp(jnp.sum(c_lambda[2] * c_lambda[3]))
                + lambda_init)
    gain = (c_subln * (1.0 - lambda_init)).reshape(1, 2 * C_DH)
    head_c = _diff_attention(lam_full.reshape(1), qbd, kc, vT, gain)
    head_d = _gated_deltanet(proj, gatesT, d_conv, d_a_log, d_dt_bias, d_norm, bsz, seq)
    return [t.reshape(bsz * seq, t.shape[-1]) for t in (head_a, head_b, head_c, head_d)]


def kernel(x, positions, w_in, w_out, a_kv_norm, a_w_uk, a_w_uv, c_lambda, c_subln, d_conv, d_a_log, d_dt_bias, d_norm, ln1_g, ln1_b, w_router, b_router, w_gu, b_gu, w_dn, b_dn, ln2_g, ln2_b):
    bsz, seq, d = x.shape
    x2d = x.reshape(bsz * seq, d)
    cos, sin = _rope_tables(positions)
    for l in range(DEPTH):
        proj = _in_proj(x2d, _pack_w_in(w_in[l]))
        lambda_init = 0.8 - 0.6 * math.exp(-0.3 * l)
        heads = _mixers(proj, cos, sin, bsz, seq, lambda_init, a_kv_norm[l], a_w_uk[l], a_w_uv[l], c_lambda[l],
                        c_subln[l], d_conv[l], d_a_log[l], d_dt_bias[l], d_norm[l])
        h2d, h_packed, te, tg = _out_ln_router(heads, w_out[l].astype(MXU_DTYPE), x2d, ln1_g[l], ln1_b[l],
                                     w_router[l], b_router[l])
        x2d = _moe_layer(h2d, h_packed, te, tg, l, w_gu, b_gu, w_dn, b_dn, ln2_g[l], ln2_b[l])
    return x2d.reshape(bsz, seq, d)
```

```python
import functools
import math

import jax
import jax.numpy as jnp
import numpy as np
from jax import lax
from jax.experimental import pallas as pl
from jax.experimental.pallas import tpu as pltpu

D_MODEL = 2048
DEPTH = 4
A_HEADS, A_NOPE, A_ROPE, A_V, A_KV_RANK = 4, 128, 64, 128, 256
IDX_HEADS, IDX_DIM, INDEX_TOPK = 8, 64, 256
B_HEADS, B_QK, B_V, RET_CHUNK = 4, 64, 128, 128
C_HEADS, C_DH = 4, 64
D_HEADS, D_K, D_V, CONV_K, GDN_CHUNK = 4, 128, 128, 4, 64
N_EXPERTS, TOP_K, D_FF = 32, 4, 512
SWIGLU_ALPHA, SWIGLU_LIMIT = 1.702, 7.0
MOE_BLOCK = 256
ROPE_THETA = 10000.0
ROPE_DIM = 64
Q_BLOCK = 128
DEEPNORM_ALPHA = (2 * DEPTH) ** 0.25
LN_EPS = 1e-5
RMS_EPS = 1e-6
D_MIX = A_HEADS * A_V + B_HEADS * B_V + C_HEADS * 2 * C_DH + D_HEADS * D_V
A_LAT = A_KV_RANK + A_ROPE

IN_WIDTHS = (
    A_HEADS * (A_NOPE + A_ROPE), A_KV_RANK, A_ROPE, IDX_HEADS * IDX_DIM, IDX_DIM, IDX_HEADS,
    B_HEADS * B_QK, B_HEADS * B_QK, B_HEADS * B_V, B_HEADS * B_V,
    C_HEADS * 2 * C_DH, C_HEADS * 2 * C_DH, C_HEADS * 2 * C_DH,
    D_HEADS * (2 * D_K + D_V), D_HEADS * D_V, D_HEADS, D_HEADS,
)
_IN_OFFS = np.concatenate([[0], np.cumsum(IN_WIDTHS)]).tolist()

LANE = 128
SUBLANE = 8
VMEM_LIMIT = 56 * 1024 * 1024

MXU_DTYPE = jnp.bfloat16
_MASKED = -1e30

COL = 512
COL_D_QKV = 0
COL_D_Z = 3
COL_B_V = 4
COL_B_G = 5
COL_B_QK = 6
COL_C_Q = 7
COL_C_K = 8
COL_C_V = 9
COL_A_CKV_ROPE = 10
COL_A_NOPE = 11
COL_A_IQ = 12
NARROW = 256
COL_NARROW = 26
IN_PACKED = (COL_NARROW + 1) * NARROW
NARROW_IW = 0
NARROW_DB = IDX_HEADS
NARROW_DA = IDX_HEADS + D_HEADS


def _pack_w_in(w):
    g = lambda i: w[:, _IN_OFFS[i]:_IN_OFFS[i + 1]]
    a_q = g(0).reshape(w.shape[0], A_HEADS, A_NOPE + A_ROPE)
    nope = a_q[:, :, :A_NOPE].reshape(w.shape[0], A_HEADS * A_NOPE)
    rope = a_q[:, :, A_NOPE:].reshape(w.shape[0], A_HEADS * A_ROPE)
    cols = [g(13), g(14), g(8), g(9), g(6), g(7), g(10), g(11), g(12), g(1), rope, nope, g(3),
            g(2), g(4), g(5), g(15), g(16)]
    used = sum(c.shape[1] for c in cols)
    cols.append(jnp.zeros((w.shape[0], IN_PACKED - used), w.dtype))
    return jnp.concatenate(cols, axis=1).astype(MXU_DTYPE)


def _sigmoid(t):
    return 1.0 / (1.0 + jnp.exp(-t))


def _softplus(t):
    return jnp.maximum(t, 0.0) + jnp.log1p(jnp.exp(-jnp.abs(t)))


def _params(*sem):
    return pltpu.CompilerParams(dimension_semantics=sem, vmem_limit_bytes=VMEM_LIMIT)


def _in_proj_body(x_ref, w_ref, o_ref, xb_ref):
    @pl.when(pl.program_id(1) == 0)
    def _():
        xb_ref[...] = x_ref[...].astype(MXU_DTYPE)

    o_ref[...] = jnp.dot(xb_ref[...], w_ref[...], preferred_element_type=jnp.float32)


def _in_proj(x2d, w_packed, tm=1024, tn=768):
    m, k = x2d.shape
    n = w_packed.shape[1]
    return pl.pallas_call(
        _in_proj_body,
        grid=(m // tm, n // tn),
        in_specs=[pl.BlockSpec((tm, k), lambda i, j: (i, 0)),
                  pl.BlockSpec((k, tn), lambda i, j: (0, j))],
        out_specs=pl.BlockSpec((tm, tn), lambda i, j: (i, j)),
        out_shape=jax.ShapeDtypeStruct((m, n), jnp.float32),
        scratch_shapes=[pltpu.VMEM((tm, k), MXU_DTYPE)],
        compiler_params=_params("arbitrary", "arbitrary"),
        name="in_proj",
    )(x2d, w_packed)


PREP_TILE = 256
KCHUNK = 256


def _prep_body(bqk_ref, cq_ref, ck_ref, cv_ref, ckvrope_ref, nope_ref, iq_ref, nar_ref, cos_ref, sin_ref,
               kvn_ref, wukT_ref,
               qT_ref, iqT_ref, iw_ref, ik_ref, kv_ref, ckvT_ref, qbd_ref, kc_ref, vT_ref, rq_ref, rkT_ref, gT_ref):
    f32 = jnp.float32
    cdt = MXU_DTYPE
    cos, sin = cos_ref[0], sin_ref[0]
    lane = lax.broadcasted_iota(jnp.int32, (PREP_TILE, LANE), 1)
    first_half = (lane % ROPE_DIM) < ROPE_DIM // 2
    eye = (lax.broadcasted_iota(jnp.int32, (LANE, LANE), 0)
           == lax.broadcasted_iota(jnp.int32, (LANE, LANE), 1)).astype(cdt)
    top = lax.broadcasted_iota(jnp.int32, (LANE, LANE), 0) < LANE // 2
    top_wide = lax.broadcasted_iota(jnp.int32, (LANE, PREP_TILE), 0) < LANE // 2

    def rope(t):
        outs = []
        for j in range(t.shape[1] // LANE):
            ts = t[:, j * LANE:(j + 1) * LANE]
            partner = jnp.where(first_half, pltpu.roll(ts, LANE - ROPE_DIM // 2, 1), pltpu.roll(ts, ROPE_DIM // 2, 1))
            outs.append(ts * cos + partner * sin)
        return outs[0] if len(outs) == 1 else jnp.concatenate(outs, axis=1)

    def tr(t):
        return lax.dot_general(eye, t.astype(cdt), (((1,), (1,)), ((), ())), preferred_element_type=f32)

    ckv_rope = ckvrope_ref[...]
    a_ckv = ckv_rope[:, :A_KV_RANK]
    c_kv = a_ckv * lax.rsqrt(jnp.mean(a_ckv * a_ckv, -1, keepdims=True) + RMS_EPS) * kvn_ref[...]
    nar = nar_ref[...]
    kr_ik = rope(nar[:, :LANE])
    kv_ref[0, 0, :, :A_KV_RANK] = c_kv.astype(cdt)
    kv_ref[0, 0, :, A_KV_RANK:] = kr_ik[:, :A_ROPE].astype(cdt)
    ik_ref[0, 0] = kr_ik[:, A_ROPE:].astype(cdt)
    for rb in range(A_KV_RANK // LANE):
        ckvT_ref[0, 0, rb * LANE:(rb + 1) * LANE, :] = tr(c_kv[:, rb * LANE:(rb + 1) * LANE]).astype(cdt)
    q_rope = rope(ckv_rope[:, A_KV_RANK:])
    nope = nope_ref[...]
    iq = rope(iq_ref[...])
    narT = nar[:, LANE:].T
    gT_ref[0] = narT[NARROW_DB:NARROW_DB + 2 * D_HEADS]
    cq = rope(cq_ref[...]) * C_DH ** -0.5
    for j in range(PREP_TILE // Q_BLOCK):
        rows = slice(j * Q_BLOCK, (j + 1) * Q_BLOCK)
        for h in range(A_HEADS):
            q_latT = lax.dot_general(wukT_ref[h], nope[rows, h * A_NOPE:(h + 1) * A_NOPE].astype(cdt),
                                     (((1,), (1,)), ((), ())), preferred_element_type=f32)
            qT_ref[0, j, :A_KV_RANK, h * Q_BLOCK:(h + 1) * Q_BLOCK] = q_latT.astype(cdt)
        for p in range(A_HEADS * A_ROPE // LANE):
            t = tr(q_rope[rows, p * LANE:(p + 1) * LANE])
            qT_ref[0, j, A_KV_RANK:, (2 * p) * Q_BLOCK:(2 * p + 1) * Q_BLOCK] = t[:A_ROPE].astype(cdt)
            qT_ref[0, j, A_KV_RANK:, (2 * p + 1) * Q_BLOCK:(2 * p + 2) * Q_BLOCK] = t[A_ROPE:].astype(cdt)
        for p in range(IDX_HEADS * IDX_DIM // LANE):
            t = tr(iq[rows, p * LANE:(p + 1) * LANE])
            iqT_ref[0, j, :, (2 * p) * Q_BLOCK:(2 * p + 1) * Q_BLOCK] = t[:IDX_DIM].astype(cdt)
            iqT_ref[0, j, :, (2 * p + 1) * Q_BLOCK:(2 * p + 2) * Q_BLOCK] = t[IDX_DIM:].astype(cdt)
        iw_ref[0, j] = narT[NARROW_IW:NARROW_IW + IDX_HEADS, rows] * (IDX_HEADS * IDX_DIM) ** -0.5
        for h in range(C_HEADS):
            t = tr(cq[rows, h * LANE:(h + 1) * LANE])
            qbd_ref[0, j, h, :, :Q_BLOCK] = jnp.where(top, t, 0.0).astype(cdt)
            qbd_ref[0, j, h, :, Q_BLOCK:] = jnp.where(top, 0.0, t).astype(cdt)
    ck = rope(ck_ref[...])
    cv = cv_ref[...]
    for h in range(C_HEADS):
        kc_ref[0, 0, h] = ck[:, h * LANE:(h + 1) * LANE].astype(cdt)
        vT_ref[0, 0, h] = tr(cv[:, h * LANE:(h + 1) * LANE]).astype(cdt)
    bqk = rope(bqk_ref[...])
    rq_ref[0] = bqk[:, :B_HEADS * B_QK].astype(cdt)
    for p in range(B_HEADS * B_QK // LANE):
        t = tr(bqk[:, B_HEADS * B_QK + p * LANE:B_HEADS * B_QK + (p + 1) * LANE]) * B_QK ** -0.5
        rkT_ref[0, 2 * p] = jnp.where(top_wide, t, 0.0).astype(cdt)
        rkT_ref[0, 2 * p + 1] = jnp.where(top_wide, 0.0, t).astype(cdt)


def _mixer_prep(proj, cos, sin, kv_norm, wukT, bsz, seq):
    nt = seq // PREP_TILE
    nq = seq // Q_BLOCK
    cdt = MXU_DTYPE
    col = lambda width, c: pl.BlockSpec((PREP_TILE, width), lambda b, t: (b * nt + t, c))
    table = pl.BlockSpec((1, PREP_TILE, LANE), lambda b, t: (b, t, 0))
    per_q = lambda *tail: pl.BlockSpec((1, PREP_TILE // Q_BLOCK) + tail, lambda b, t: (b, t) + (0,) * len(tail))
    per_c = lambda *tail: pl.BlockSpec((1, 1) + tail, lambda b, t: (b, t) + (0,) * len(tail))
    sds = jax.ShapeDtypeStruct
    outs = [
        (sds((bsz, nq, A_LAT, A_HEADS * Q_BLOCK), cdt), per_q(A_LAT, A_HEADS * Q_BLOCK)),
        (sds((bsz, nq, IDX_DIM, IDX_HEADS * Q_BLOCK), cdt), per_q(IDX_DIM, IDX_HEADS * Q_BLOCK)),
        (sds((bsz, nq, IDX_HEADS, Q_BLOCK), jnp.float32), per_q(IDX_HEADS, Q_BLOCK)),
        (sds((bsz, nt, KCHUNK, IDX_DIM), cdt), per_c(KCHUNK, IDX_DIM)),
        (sds((bsz, nt, KCHUNK, A_LAT), cdt), per_c(KCHUNK, A_LAT)),
        (sds((bsz, nt, A_KV_RANK, KCHUNK), cdt), per_c(A_KV_RANK, KCHUNK)),
        (sds((bsz, nq, C_HEADS, 2 * C_DH, 2 * Q_BLOCK), cdt), per_q(C_HEADS, 2 * C_DH, 2 * Q_BLOCK)),
        (sds((bsz, nt, C_HEADS, KCHUNK, 2 * C_DH), cdt), per_c(C_HEADS, KCHUNK, 2 * C_DH)),
        (sds((bsz, nt, C_HEADS, 2 * C_DH, KCHUNK), cdt), per_c(C_HEADS, 2 * C_DH, KCHUNK)),
        (sds((bsz, seq, B_HEADS * B_QK), cdt), pl.BlockSpec((1, PREP_TILE, B_HEADS * B_QK), lambda b, t: (b, t, 0))),
        (sds((bsz, B_HEADS, LANE, seq), cdt), pl.BlockSpec((1, B_HEADS, LANE, PREP_TILE), lambda b, t: (b, 0, 0, t))),
        (sds((bsz, 2 * D_HEADS, seq), jnp.float32), pl.BlockSpec((1, 2 * D_HEADS, PREP_TILE), lambda b, t: (b, 0, t))),
    ]
    return pl.pallas_call(
        _prep_body,
        grid=(bsz, nt),
        in_specs=[col(COL, COL_B_QK), col(COL, COL_C_Q), col(COL, COL_C_K), col(COL, COL_C_V),
                  col(COL, COL_A_CKV_ROPE), col(COL, COL_A_NOPE), col(COL, COL_A_IQ), col(NARROW, COL_NARROW),
                  table, table,
                  pl.BlockSpec((1, A_KV_RANK), lambda b, t: (0, 0)),
                  pl.BlockSpec((A_HEADS, A_KV_RANK, A_NOPE), lambda b, t: (0, 0, 0))],
        out_specs=[o[1] for o in outs],
        out_shape=[o[0] for o in outs],
        compiler_params=_params("arbitrary", "arbitrary"),
        name="mixer_prep",
    )(proj, proj, proj, proj, proj, proj, proj, proj, cos, sin, kv_norm.reshape(1, A_KV_RANK), wukT)


def _rope_tables(positions):
    half = ROPE_DIM // 2
    inv_freq = ROPE_THETA ** (-jnp.arange(half, dtype=jnp.float32) / half)
    ang = positions.astype(jnp.float32)[:, :, None] * inv_freq
    cos, sin = jnp.cos(ang), jnp.sin(ang)
    reps = LANE // ROPE_DIM
    return (jnp.concatenate([cos, cos] * reps, -1), jnp.concatenate([-sin, sin] * reps, -1))


_SIGN_BIT = np.int32(-2 ** 31)
_LOW31 = np.int32(2 ** 31 - 1)
_KEY_NEG_INF = np.int32(np.array(-np.inf, np.float32).view(np.int32) ^ _LOW31)
DSA_QB = 4


def _dsa_body(iqT_ref, iw_ref, qT_ref, ik_ref, kv_ref, ckvT_ref, wuv_ref, o_ref,
              key_ref, acc_ref, thr_ref, need_ref, *, n_keep):
    f32 = jnp.float32
    blocks = range(DSA_QB)
    q0 = pl.program_id(1) * DSA_QB
    nk = ((q0 + DSA_QB) * Q_BLOCK + KCHUNK - 1) // KCHUNK
    iqT = [iqT_ref[0, j] for j in blocks]
    iw = [jnp.concatenate([iw_ref[0, j][h:h + 1] for h in range(IDX_HEADS)], axis=1) for j in blocks]
    lane_q = lax.broadcasted_iota(jnp.int32, (KCHUNK, Q_BLOCK), 1)
    qpos = [(q0 + j) * Q_BLOCK + lane_q for j in blocks]
    krow = lax.broadcasted_iota(jnp.int32, (KCHUNK, Q_BLOCK), 0)

    def index_chunk(c, carry):
        ik = ik_ref[0, c]
        lg = [jnp.dot(ik, iqT[j], preferred_element_type=f32) for j in blocks]
        w = [jnp.maximum(lg[j], 0.0) * iw[j] for j in blocks]
        for j in blocks:
            idx = w[j][:, :Q_BLOCK]
            for h in range(1, IDX_HEADS):
                idx = idx + w[j][:, h * Q_BLOCK:(h + 1) * Q_BLOCK]
            idx = jnp.where(idx == 0.0, 0.0, idx)
            idx = jnp.where(krow + c * KCHUNK <= qpos[j], idx, -jnp.inf)
            bits = pltpu.bitcast(idx, jnp.int32)
            key_ref[j, c] = bits ^ ((bits >> 31) & _LOW31)
        return carry

    lax.fori_loop(0, nk, index_chunk, 0)

    def count(cmp, ts):
        def body(c, cnts):
            hits = [cmp(key_ref[j, c], ts[j]).astype(jnp.int32) for j in blocks]
            return tuple(cnts[j] + hits[j].reshape(KCHUNK // SUBLANE, SUBLANE, Q_BLOCK).sum(0) for j in blocks)
        cnts = lax.fori_loop(0, nk, body, tuple(jnp.zeros((SUBLANE, Q_BLOCK), jnp.int32) for _ in blocks))
        return [cnt.sum(0, keepdims=True) for cnt in cnts]

    keep_all = [(q0 + j) * Q_BLOCK + Q_BLOCK <= n_keep for j in blocks]

    @pl.when(keep_all[-1])
    def _():
        for j in blocks:
            thr_ref[j] = jnp.full((1, Q_BLOCK), _KEY_NEG_INF, jnp.int32)
            need_ref[j] = jnp.zeros((1, Q_BLOCK), jnp.int32)

    @pl.when(jnp.logical_not(keep_all[-1]))
    def _():
        def bit_step(i, prefixes):
            bit = jnp.left_shift(jnp.int32(1), 31 - i)
            cands = [prefixes[j] | bit for j in blocks]
            cnts = count(lambda k, t: k >= t, [cand ^ _SIGN_BIT for cand in cands])
            return tuple(jnp.where(cnts[j] >= n_keep, cands[j], prefixes[j]) for j in blocks)
        prefixes = lax.fori_loop(0, 32, bit_step, tuple(jnp.zeros((1, Q_BLOCK), jnp.int32) for _ in blocks))
        thrs = [prefixes[j] ^ _SIGN_BIT for j in blocks]
        above = count(lambda k, t: k > t, thrs)
        for j in blocks:
            thr_ref[j] = jnp.where(keep_all[j], _KEY_NEG_INF, thrs[j])
            need_ref[j] = jnp.where(keep_all[j], 0, n_keep - above[j])

    thr = [thr_ref[j] for j in blocks]
    need = [need_ref[j].astype(f32) for j in blocks]
    qT = [qT_ref[0, j] for j in blocks]
    scale = (A_NOPE + A_ROPE) ** -0.5
    r_i = lax.broadcasted_iota(jnp.int32, (KCHUNK, KCHUNK), 0)
    c_i = lax.broadcasted_iota(jnp.int32, (KCHUNK, KCHUNK), 1)
    before = (c_i < r_i).astype(jnp.bfloat16)
    acc_ref[...] = jnp.zeros_like(acc_ref)

    def attend_chunk(c, carry):
        ms, ls, seen = carry
        kv, ckvT = kv_ref[0, c], ckvT_ref[0, c]
        s = [jnp.dot(kv, qT[j], preferred_element_type=f32) * scale for j in blocks]
        key = [key_ref[j, c] for j in blocks]
        tie = [key[j] == thr[j] for j in blocks]
        tie_f = [tie[j].astype(f32) for j in blocks]
        rank = [jnp.dot(before, tie_f[j].astype(jnp.bfloat16), preferred_element_type=f32) + seen[j]
                for j in blocks]
        bias = [jnp.where((key[j] > thr[j]) | (tie[j] & (rank[j] < need[j])), 0.0, _MASKED) for j in blocks]
        s = [s[j] + jnp.concatenate([bias[j]] * A_HEADS, axis=1) for j in blocks]
        m_new = [jnp.maximum(ms[j], s[j].max(0, keepdims=True)) for j in blocks]
        alpha = [jnp.exp(ms[j] - m_new[j]) for j in blocks]
        p = [jnp.exp(s[j] - m_new[j]) for j in blocks]
        l_new = [alpha[j] * ls[j] + p[j].sum(0, keepdims=True) for j in blocks]
        pv = [jnp.dot(ckvT, p[j].astype(ckvT.dtype), preferred_element_type=f32) for j in blocks]
        for j in blocks:
            acc_ref[j] = acc_ref[j] * alpha[j] + pv[j]
        return (tuple(m_new), tuple(l_new), tuple(seen[j] + tie_f[j].sum(0, keepdims=True) for j in blocks))

    lanes = A_HEADS * Q_BLOCK
    init = (tuple(jnp.full((1, lanes), _MASKED, f32) for _ in blocks),
            tuple(jnp.zeros((1, lanes), f32) for _ in blocks),
            tuple(jnp.zeros((1, Q_BLOCK), f32) for _ in blocks))
    _, ls, _ = lax.fori_loop(0, nk, attend_chunk, init)
    for j in blocks:
        o_latT = acc_ref[j] / ls[j]
        for h in range(A_HEADS):
            o_lat = o_latT[:, h * Q_BLOCK:(h + 1) * Q_BLOCK].T.astype(wuv_ref.dtype)
            o_ref[0, j * Q_BLOCK:(j + 1) * Q_BLOCK, h * A_V:(h + 1) * A_V] = jnp.dot(
                o_lat, wuv_ref[h], preferred_element_type=f32)


def _dsa_attention(iqT, iw, qT, ikc, kvc, ckvT, wuv, *, n_keep):
    bsz, nq = qT.shape[:2]
    nc = ikc.shape[1]
    seq = nq * Q_BLOCK
    per_q = lambda b, q: (b, q, 0, 0)
    per_b = lambda b, q: (b, 0, 0, 0)
    return pl.pallas_call(
        functools.partial(_dsa_body, n_keep=n_keep),
        grid=(bsz, nq // DSA_QB),
        in_specs=[pl.BlockSpec((1, DSA_QB) + iqT.shape[2:], per_q),
                  pl.BlockSpec((1, DSA_QB) + iw.shape[2:], per_q),
                  pl.BlockSpec((1, DSA_QB) + qT.shape[2:], per_q),
                  pl.BlockSpec((1,) + ikc.shape[1:], per_b),
                  pl.BlockSpec((1,) + kvc.shape[1:], per_b),
                  pl.BlockSpec((1,) + ckvT.shape[1:], per_b),
                  pl.BlockSpec(wuv.shape, lambda b, q: (0, 0, 0))],
        out_specs=pl.BlockSpec((1, DSA_QB * Q_BLOCK, A_HEADS * A_V), lambda b, q: (b, q, 0)),
        out_shape=jax.ShapeDtypeStruct((bsz, seq, A_HEADS * A_V), jnp.float32),
        scratch_shapes=[pltpu.VMEM((DSA_QB, nc, KCHUNK, Q_BLOCK), jnp.int32),
                        pltpu.VMEM((DSA_QB, A_KV_RANK, A_HEADS * Q_BLOCK), jnp.float32),
                        pltpu.VMEM((DSA_QB, 1, Q_BLOCK), jnp.int32),
                        pltpu.VMEM((DSA_QB, 1, Q_BLOCK), jnp.int32)],
        compiler_params=_params("arbitrary", "arbitrary"),
        name="dsa_attention",
    )(iqT, iw, qT, ikc, kvc, ckvT, wuv)


RET_TILE = 256


def _retention_body(q_ref, kT_ref, v_ref, g_ref, o_ref, s_ref):
    f32 = jnp.float32
    cdt = MXU_DTYPE
    c = RET_CHUNK

    @pl.when(pl.program_id(1) == 0)
    def _():
        s_ref[...] = jnp.zeros_like(s_ref)

    rel = (lax.broadcasted_iota(jnp.int32, (c, c), 0) - lax.broadcasted_iota(jnp.int32, (c, c), 1)).astype(f32)
    pos_c = lax.broadcasted_iota(jnp.int32, (c, 1), 0).astype(f32)
    pos_r = lax.broadcasted_iota(jnp.int32, (1, c), 1).astype(f32)

    def mm(a, b):
        return jnp.dot(a.astype(cdt), b.astype(cdt), preferred_element_type=f32)

    for h in range(B_HEADS):
        log_gamma = float(np.log(np.float32(1.0) - np.float32(2.0) ** np.float32(-5 - h)))
        intra = jnp.where(rel >= 0, jnp.exp(log_gamma * jnp.maximum(rel, 0.0)), 0.0)
        to_end = jnp.exp(log_gamma * (c - 1 - pos_r))
        from_start = jnp.exp(log_gamma * (pos_c + 1.0))
        chunk_decay = float(np.exp(np.float32(log_gamma) * np.float32(c)))
        lanes = slice(h * B_V, (h + 1) * B_V)
        pair = slice((h // 2) * LANE, (h // 2 + 1) * LANE)
        for ci in range(RET_TILE // c):
            rows = slice(ci * c, (ci + 1) * c)
            q = q_ref[0, rows, pair].astype(f32)
            kT = kT_ref[0, h, :, rows].astype(f32)
            v = v_ref[rows, lanes]
            state = s_ref[h]
            o = mm(mm(q, kT) * intra, v) + mm(q * from_start, state)
            s_ref[h] = state * chunk_decay + mm(kT * to_end, v)
            gate = g_ref[rows, lanes]
            o = o * lax.rsqrt(jnp.mean(o * o, -1, keepdims=True) + RMS_EPS)
            o_ref[0, rows, lanes] = o * (gate * _sigmoid(gate))


def _retention(rq, rkT, proj, bsz, seq):
    nt = seq // RET_TILE
    width = B_HEADS * B_V
    col = lambda c: pl.BlockSpec((RET_TILE, COL), lambda b, t: (b * nt + t, c))
    return pl.pallas_call(
        _retention_body,
        grid=(bsz, nt),
        in_specs=[pl.BlockSpec((1, RET_TILE, B_HEADS * B_QK), lambda b, t: (b, t, 0)),
                  pl.BlockSpec((1, B_HEADS, LANE, RET_TILE), lambda b, t: (b, 0, 0, t)),
                  col(COL_B_V), col(COL_B_G)],
        out_specs=pl.BlockSpec((1, RET_TILE, width), lambda b, t: (b, t, 0)),
        out_shape=jax.ShapeDtypeStruct((bsz, seq, width), jnp.float32),
        scratch_shapes=[pltpu.VMEM((B_HEADS, LANE, B_V), jnp.float32)],
        compiler_params=_params("arbitrary", "arbitrary"),
        name="retention",
    )(rq, rkT, proj, proj)


DIFF_QB = 2


def _diff_attention_body(lam_ref, qbd_ref, k_ref, vT_ref, gain_ref, o_ref, acc_ref):
    f32 = jnp.float32
    blocks = range(DIFF_QB)
    heads = range(C_HEADS)
    q0 = pl.program_id(1) * DIFF_QB
    nk = ((q0 + DIFF_QB) * Q_BLOCK + KCHUNK - 1) // KCHUNK
    lane_q = lax.broadcasted_iota(jnp.int32, (KCHUNK, Q_BLOCK), 1)
    krow = lax.broadcasted_iota(jnp.int32, (KCHUNK, Q_BLOCK), 0)
    q_all = [jnp.concatenate([qbd_ref[0, j, h] for j in blocks], axis=1) for h in heads]
    acc_ref[...] = jnp.zeros_like(acc_ref)

    def chunk(c, carry):
        ms, ls = carry
        causal = [jnp.where(krow + c * KCHUNK <= (q0 + j) * Q_BLOCK + lane_q, 0.0, _MASKED) for j in blocks]
        bias = jnp.concatenate([causal[j] for j in blocks for _ in range(2)], axis=1)
        s = [jnp.dot(k_ref[0, c, h], q_all[h], preferred_element_type=f32) + bias for h in heads]
        new_ms = [jnp.maximum(ms[h], s[h].max(0, keepdims=True)) for h in heads]
        alpha = [jnp.exp(ms[h] - new_ms[h]) for h in heads]
        p = [jnp.exp(s[h] - new_ms[h]) for h in heads]
        new_ls = [alpha[h] * ls[h] + p[h].sum(0, keepdims=True) for h in heads]
        pv = [jnp.dot(vT_ref[0, c, h], p[h].astype(vT_ref.dtype), preferred_element_type=f32) for h in heads]
        for h in heads:
            acc_ref[h] = acc_ref[h] * alpha[h] + pv[h]
        return tuple(new_ms), tuple(new_ls)

    lanes = DIFF_QB * 2 * Q_BLOCK
    init = (tuple(jnp.full((1, lanes), _MASKED, f32) for _ in heads),
            tuple(jnp.zeros((1, lanes), f32) for _ in heads))
    _, ls = lax.fori_loop(0, nk, chunk, init)
    lam = lam_ref[0]
    for h in heads:
        o_all = acc_ref[h] / ls[h]
        for j in blocks:
            o1 = o_all[:, (2 * j) * Q_BLOCK:(2 * j + 1) * Q_BLOCK]
            o2 = o_all[:, (2 * j + 1) * Q_BLOCK:(2 * j + 2) * Q_BLOCK]
            o = o1 - lam * o2
            o = o * lax.rsqrt(jnp.mean(o * o, 0, keepdims=True) + RMS_EPS)
            o_ref[0, j * Q_BLOCK:(j + 1) * Q_BLOCK, h * 2 * C_DH:(h + 1) * 2 * C_DH] = o.T * gain_ref[...]


def _diff_attention(lam, qbd, kc, vT, gain):
    bsz, nq = qbd.shape[:2]
    seq = nq * Q_BLOCK
    width = C_HEADS * 2 * C_DH
    return pl.pallas_call(
        _diff_attention_body,
        grid=(bsz, nq // DIFF_QB),
        in_specs=[pl.BlockSpec(memory_space=pltpu.SMEM),
                  pl.BlockSpec((1, DIFF_QB) + qbd.shape[2:], lambda b, q: (b, q, 0, 0, 0)),
                  pl.BlockSpec((1,) + kc.shape[1:], lambda b, q: (b, 0, 0, 0, 0)),
                  pl.BlockSpec((1,) + vT.shape[1:], lambda b, q: (b, 0, 0, 0, 0)),
                  pl.BlockSpec(gain.shape, lambda b, q: (0, 0))],
        out_specs=pl.BlockSpec((1, DIFF_QB * Q_BLOCK, width), lambda b, q: (b, q, 0)),
        out_shape=jax.ShapeDtypeStruct((bsz, seq, width), jnp.float32),
        scratch_shapes=[pltpu.VMEM((C_HEADS, 2 * C_DH, DIFF_QB * 2 * Q_BLOCK), jnp.float32)],
        compiler_params=_params("arbitrary", "arbitrary"),
        name="diff_attention",
    )(lam, qbd, kc, vT, gain)


GDN_TILE = 256
GDN_SUB = 16


def _gdn_body(qkv_ref, z_ref, nar_ref, gateT_ref, conv_ref, alog_r_ref, dtb_r_ref, alog_c_ref, dtb_c_ref,
              ng_ref, o_ref, halo_ref, s_ref):
    f32 = jnp.float32
    cdt = MXU_DTYPE
    c = GDN_CHUNK

    @pl.when(pl.program_id(1) == 0)
    def _():
        halo_ref[...] = jnp.zeros_like(halo_ref)
        s_ref[...] = jnp.zeros_like(s_ref)

    x = qkv_ref[...]
    xp = jnp.concatenate([halo_ref[...], x], 0)
    w = conv_ref[...]
    pre = xp[SUBLANE - CONV_K + 1:SUBLANE - CONV_K + 1 + GDN_TILE] * w[0:1]
    for j in range(1, CONV_K):
        off = SUBLANE - CONV_K + 1 + j
        pre = pre + xp[off:off + GDN_TILE] * w[j:j + 1]
    halo_ref[...] = x[GDN_TILE - SUBLANE:]
    qkv = pre * _sigmoid(pre)

    gates = nar_ref[:, LANE:]
    beta_c = _sigmoid(gates[:, NARROW_DB:NARROW_DB + D_HEADS])
    g_c = -jnp.exp(alog_r_ref[...]) * _softplus(gates[:, NARROW_DA:NARROW_DA + D_HEADS] + dtb_r_ref[...])
    g_r = -jnp.exp(alog_c_ref[...]) * _softplus(gateT_ref[0][D_HEADS:] + dtb_c_ref[...])

    row = lax.broadcasted_iota(jnp.int32, (c, c), 0)
    col = lax.broadcasted_iota(jnp.int32, (c, c), 1)
    tril = row >= col
    strict = row > col
    same_sub = (row // GDN_SUB) == (col // GDN_SUB)
    eye = (row == col).astype(f32)
    lower_ones = tril.astype(f32)
    upper_ones = (row <= col).astype(f32)

    def mm(a, b):
        return jnp.dot(a, b, preferred_element_type=f32)

    n_chunks = GDN_TILE // c
    pairs = [(ci, h) for ci in range(n_chunks) for h in range(D_HEADS)]
    rows_of = lambda ci: slice(ci * c, (ci + 1) * c)
    g_cum_c = [mm(lower_ones, g_c[rows_of(ci)]) for ci in range(n_chunks)]
    g_cum_r = [mm(g_r[:, rows_of(ci)], upper_ones) for ci in range(n_chunks)]
    st = []
    for ci, h in pairs:
        rows = rows_of(ci)
        q = qkv[rows, h * D_K:(h + 1) * D_K]
        k = qkv[rows, D_HEADS * D_K + h * D_K:D_HEADS * D_K + (h + 1) * D_K]
        v = qkv[rows, 2 * D_HEADS * D_K + h * D_V:2 * D_HEADS * D_K + (h + 1) * D_V]
        q = q * lax.rsqrt(jnp.sum(q * q, -1, keepdims=True) + RMS_EPS) * D_K ** -0.5
        k = k * lax.rsqrt(jnp.sum(k * k, -1, keepdims=True) + RMS_EPS)
        gc = g_cum_c[ci][:, h:h + 1]
        gr = g_cum_r[ci][h:h + 1, :]
        g_last = gc[c - 1:c]
        gam = jnp.where(tril, jnp.exp(jnp.where(tril, gc - gr, 0.0)), 0.0)
        e_g = jnp.exp(gc)
        beta = beta_c[rows, h:h + 1]
        kT = k.T
        kb = k * beta
        st.append(dict(gam=gam, kT_c=kT.astype(cdt), kb_c=kb.astype(cdt), q_c=q.astype(cdt),
                       qg_c=(q * e_g).astype(cdt), rhs=jnp.concatenate([v * beta, kb * e_g], 1),
                       kdT_c=(kT * jnp.exp(g_last - gr)).astype(cdt), decay=jnp.exp(g_last)))
    for p in st:
        p["low"] = jnp.where(strict, mm(p["kb_c"], p["kT_c"]) * p["gam"], 0.0)
        p["attn_c"] = jnp.where(tril, mm(p["q_c"], p["kT_c"]) * p["gam"], 0.0).astype(cdt)
        p["low_d"] = jnp.where(same_sub, p["low"], 0.0)
        p["xk"] = -p["low_d"]
        p["td"] = eye + p["xk"]
    for _ in range(3):
        for p in st:
            p["xk"] = mm(p["xk"], p["xk"])
        for p in st:
            p["td"] = p["td"] + mm(p["td"], p["xk"])
    for p in st:
        p["m1"] = mm(p["td"], p["low"] - p["low_d"])
        p["sol"] = mm(p["td"], p["rhs"])
    for p in st:
        p["m2"] = mm(p["m1"], p["m1"])
    for p in st:
        p["sol"] = p["sol"] + mm(p["m2"], p["sol"])
    for p in st:
        p["sol"] = p["sol"] - mm(p["m1"], p["sol"])
    for ci in range(n_chunks):
        rows = rows_of(ci)
        cur = [st[ci * D_HEADS + h] for h in range(D_HEADS)]
        states = [s_ref[h] for h in range(D_HEADS)]
        states_c = [s.astype(cdt) for s in states]
        v_new_c = [(p["sol"][:, :D_V] - mm(p["sol"][:, D_V:].astype(cdt), states_c[h])).astype(cdt)
                   for h, p in enumerate(cur)]
        for h, p in enumerate(cur):
            s_ref[h] = states[h] * p["decay"] + mm(p["kdT_c"], v_new_c[h])
        for h, p in enumerate(cur):
            lanes = slice(h * D_V, (h + 1) * D_V)
            o = mm(p["qg_c"], states_c[h]) + mm(p["attn_c"], v_new_c[h])
            zz = z_ref[rows, lanes]
            o = o * lax.rsqrt(jnp.mean(o * o, -1, keepdims=True) + RMS_EPS) * ng_ref[...]
            o_ref[0, rows, lanes] = o * (zz * _sigmoid(zz))


def _gated_deltanet(proj, gatesT, conv_w, a_log, dt_bias, norm_g, bsz, seq):
    nt = seq // GDN_TILE
    width = D_HEADS * (2 * D_K + D_V)
    whole = lambda r, cc: pl.BlockSpec((r, cc), lambda b, t: (0, 0))
    return pl.pallas_call(
        _gdn_body,
        grid=(bsz, nt),
        in_specs=[pl.BlockSpec((GDN_TILE, width), lambda b, t: (b * nt + t, COL_D_QKV)),
                  pl.BlockSpec((GDN_TILE, COL), lambda b, t: (b * nt + t, COL_D_Z)),
                  pl.BlockSpec((GDN_TILE, NARROW), lambda b, t: (b * nt + t, COL_NARROW)),
                  pl.BlockSpec((1, 2 * D_HEADS, GDN_TILE), lambda b, t: (b, 0, t)),
                  whole(CONV_K, width), whole(1, D_HEADS), whole(1, D_HEADS), whole(D_HEADS, 1), whole(D_HEADS, 1),
                  whole(1, D_V)],
        out_specs=pl.BlockSpec((1, GDN_TILE, D_HEADS * D_V), lambda b, t: (b, t, 0)),
        out_shape=jax.ShapeDtypeStruct((bsz, seq, D_HEADS * D_V), jnp.float32),
        scratch_shapes=[pltpu.VMEM((SUBLANE, width), jnp.float32),
                        pltpu.VMEM((D_HEADS, D_K, D_V), jnp.float32)],
        compiler_params=_params("arbitrary", "arbitrary"),
        name="gated_deltanet",
    )(proj, proj, proj, gatesT, conv_w, a_log.reshape(1, D_HEADS), dt_bias.reshape(1, D_HEADS),
      a_log.reshape(D_HEADS, 1), dt_bias.reshape(D_HEADS, 1), norm_g.reshape(1, D_V))


def _ln_rows(t, g, b):
    mu = jnp.mean(t, -1, keepdims=True)
    d = t - mu
    var = jnp.mean(d * d, -1, keepdims=True)
    return d * lax.rsqrt(var + LN_EPS) * g + b


_HIGH16 = np.int32(-(2 ** 16))


def _pack_bf16_pairs(t):
    n = t.shape[1] // 2
    lo = pltpu.bitcast(t[:, :n].astype(jnp.bfloat16).astype(jnp.float32), jnp.int32)
    hi = pltpu.bitcast(t[:, n:].astype(jnp.bfloat16).astype(jnp.float32), jnp.int32)
    return lax.shift_right_logical(lo, 16) | (hi & _HIGH16)


def _unpack_bf16_pairs(w):
    lo = pltpu.bitcast(w << 16, jnp.float32)
    hi = pltpu.bitcast(w & _HIGH16, jnp.float32)
    return jnp.concatenate([lo, hi], axis=1).astype(jnp.bfloat16)


def _out_ln_router_body(ha_ref, hb_ref, hc_ref, hd_ref, w_ref, x_ref, g_ref, b_ref, wr_ref, br_ref,
                        h_ref, hp_ref, te_ref, tg_ref):
    f32 = jnp.float32
    acc = DEEPNORM_ALPHA * x_ref[...]
    row0 = 0
    for part in (ha_ref, hb_ref, hc_ref, hd_ref):
        width = part.shape[-1]
        acc = acc + jnp.dot(part[...].astype(MXU_DTYPE), w_ref[row0:row0 + width, :], preferred_element_type=f32)
        row0 += width
    h = _ln_rows(acc, g_ref[...], b_ref[...])
    h_ref[...] = h
    hp_ref[...] = _pack_bf16_pairs(h)
    logits = jnp.dot(h.astype(MXU_DTYPE), wr_ref[...], preferred_element_type=f32) + br_ref[...]
    lane = lax.broadcasted_iota(jnp.int32, logits.shape, 1)
    vals, te = [], jnp.zeros(logits.shape, jnp.int32)
    for k in range(TOP_K):
        m = logits.max(-1, keepdims=True)
        e = jnp.min(jnp.where(logits == m, lane, LANE), -1, keepdims=True)
        vals.append(m)
        te = jnp.where(lane == k, e, te)
        logits = jnp.where(lane == e, -jnp.inf, logits)
    ex = [jnp.exp(v - vals[0]) for v in vals]
    den = ex[0] + ex[1] + ex[2] + ex[3]
    tg = jnp.zeros(logits.shape, f32)
    for k in range(TOP_K):
        tg = jnp.where(lane == k, ex[k] / den, tg)
    te_ref[...] = te
    tg_ref[...] = tg


def _out_ln_router(head_parts, w_out_c, x2d, g, b, w_router, b_router, tm=512):
    m = x2d.shape[0]
    k, n = w_out_c.shape
    wr = jnp.zeros((n, LANE), MXU_DTYPE).at[:, :N_EXPERTS].set(w_router.astype(MXU_DTYPE))
    br = jnp.full((1, LANE), -jnp.inf, jnp.float32).at[0, :N_EXPERTS].set(b_router)
    row_blk = lambda c: pl.BlockSpec((tm, c), lambda i: (i, 0))
    whole = lambda r, c: pl.BlockSpec((r, c), lambda i: (0, 0))
    return pl.pallas_call(
        _out_ln_router_body,
        grid=(m // tm,),
        in_specs=[row_blk(p.shape[1]) for p in head_parts]
        + [whole(k, n), row_blk(n), whole(1, n), whole(1, n), whole(n, LANE), whole(1, LANE)],
        out_specs=[row_blk(n), row_blk(n // 2), row_blk(LANE), row_blk(LANE)],
        out_shape=[jax.ShapeDtypeStruct((m, n), jnp.float32),
                   jax.ShapeDtypeStruct((m, n // 2), jnp.int32),
                   jax.ShapeDtypeStruct((m, LANE), jnp.int32),
                   jax.ShapeDtypeStruct((m, LANE), jnp.float32)],
        compiler_params=_params("arbitrary"),
        name="out_proj_ln_router",
    )(*head_parts, w_out_c, x2d, g.reshape(1, n), b.reshape(1, n), wr, br)


def _clamped_swiglu(hg):
    glu, lin = hg[..., :D_FF], hg[..., D_FF:]
    glu = jnp.minimum(glu, SWIGLU_LIMIT)
    lin = jnp.clip(lin, -SWIGLU_LIMIT, SWIGLU_LIMIT)
    return glu * _sigmoid(SWIGLU_ALPHA * glu) * (lin + 1.0)


MOE_RANK_TILE = 512


def _moe_rank_body(te_ref, rank_ref, counts_ref, seen_ref):
    f32 = jnp.float32
    tm = MOE_RANK_TILE

    @pl.when(pl.program_id(0) == 0)
    def _():
        seen_ref[...] = jnp.zeros_like(seen_ref)

    te = te_ref[...]
    lane = lax.broadcasted_iota(jnp.int32, (tm, LANE), 1)
    picks = [lane == te[:, k:k + 1] for k in range(TOP_K)]
    member = picks[0].astype(f32)
    for k in range(1, TOP_K):
        member = member + picks[k].astype(f32)
    earlier = (lax.broadcasted_iota(jnp.int32, (tm, tm), 1) < lax.broadcasted_iota(jnp.int32, (tm, tm), 0))
    prefix = jnp.dot(earlier.astype(jnp.bfloat16), member.astype(jnp.bfloat16), preferred_element_type=f32)
    prefix = prefix + seen_ref[...]
    rank = jnp.zeros((tm, LANE), jnp.int32)
    for k in range(TOP_K):
        r_k = jnp.sum(jnp.where(picks[k], prefix, 0.0), -1, keepdims=True)
        rank = jnp.where(lane == k, r_k.astype(jnp.int32), rank)
    rank_ref[...] = rank
    seen = seen_ref[...] + member.sum(0, keepdims=True)
    seen_ref[...] = seen
    counts_ref[...] = seen.astype(jnp.int32)


def _moe_rank(te):
    n_tok = te.shape[0]
    tm = MOE_RANK_TILE
    return pl.pallas_call(
        _moe_rank_body,
        grid=(n_tok // tm,),
        in_specs=[pl.BlockSpec((tm, LANE), lambda i: (i, 0))],
        out_specs=[pl.BlockSpec((tm, LANE), lambda i: (i, 0)), pl.BlockSpec((1, LANE), lambda i: (0, 0))],
        out_shape=[jax.ShapeDtypeStruct((n_tok, LANE), jnp.int32), jax.ShapeDtypeStruct((1, LANE), jnp.int32)],
        scratch_shapes=[pltpu.VMEM((1, LANE), jnp.float32)],
        compiler_params=_params("arbitrary"),
        name="moe_rank",
    )(te)


def _moe_plan(te):
    top_e = te[:, :TOP_K]
    n_tok = top_e.shape[0]
    n_blocks = -(-n_tok * TOP_K // MOE_BLOCK) + N_EXPERTS + MOE_GATHER_AHEAD - 1
    rank, counts = _moe_rank(te)
    counts = counts[0, :N_EXPERTS]
    padded = (counts + MOE_BLOCK - 1) // MOE_BLOCK * MOE_BLOCK
    pad_end = jnp.cumsum(padded)
    one_hot = top_e[:, :, None] == jnp.arange(N_EXPERTS, dtype=jnp.int32)
    dest = rank[:, :TOP_K] + jnp.sum(jnp.where(one_hot, (pad_end - padded)[None, None, :], 0), -1)
    tok = jnp.broadcast_to(jnp.arange(n_tok, dtype=jnp.int32)[:, None], dest.shape)
    row_tok = jnp.zeros((n_blocks * MOE_BLOCK,), jnp.int32).at[dest.reshape(-1)].set(tok.reshape(-1))
    block_start = jnp.arange(n_blocks, dtype=jnp.int32) * MOE_BLOCK
    block_e = jnp.minimum(jnp.sum(pad_end[None, :] <= block_start[:, None], axis=1), N_EXPERTS - 1).astype(jnp.int32)
    n_used = (pad_end[-1:] // MOE_BLOCK).astype(jnp.int32)
    return dest, row_tok, block_e, n_used


def _row_gather_start(src_hbm, rows_smem, n_rows, dst, sem):
    def body(r, carry):
        pltpu.make_async_copy(src_hbm.at[pl.ds(rows_smem[0, 0, r], 1)], dst.at[pl.ds(r, 1)], sem).start()
        return carry
    lax.fori_loop(0, n_rows, body, 0, unroll=8)


def _row_gather_wait(src_hbm, n_rows, dst, sem):
    pltpu.make_async_copy(src_hbm.at[pl.ds(0, n_rows)], dst, sem).wait()


MOE_GATHER_AHEAD = 2
MOE_GATHER_BUFS = MOE_GATHER_AHEAD + 1


def _moe_expert_body(be_ref, nused_ref, tok_ref, tok_1_ref, tok_ahead_ref, h_hbm, wgu_ref, bgu_ref, wdn_ref, bdn_ref,
                     y_ref, buf_0, buf_1, buf_2, sem, wgu_c, wdn_c):
    f32 = jnp.float32
    i = pl.program_id(0)
    n_used = nused_ref[0]
    bufs = (buf_0, buf_1, buf_2)

    @pl.when(i == 0)
    def _():
        _row_gather_start(h_hbm, tok_ref, MOE_BLOCK, buf_0, sem.at[0])
        _row_gather_start(h_hbm, tok_1_ref, MOE_BLOCK, buf_1, sem.at[1])

    @pl.when(jnp.logical_or(i == 0, be_ref[i] != be_ref[jnp.maximum(i - 1, 0)]))
    def _():
        wgu_c[...] = wgu_ref[0, 0].astype(MXU_DTYPE)
        wdn_c[...] = wdn_ref[0, 0].astype(MXU_DTYPE)

    for phase in range(MOE_GATHER_BUFS):
        cur = bufs[phase]
        ahead = (phase + MOE_GATHER_AHEAD) % MOE_GATHER_BUFS

        @pl.when(jnp.logical_and(i < n_used, i % MOE_GATHER_BUFS == phase))
        def _():
            _row_gather_wait(h_hbm, MOE_BLOCK, cur, sem.at[phase])
            for r in range(MOE_BLOCK):
                pltpu.make_async_copy(h_hbm.at[pl.ds(tok_ahead_ref[0, 0, r], 1)], bufs[ahead].at[pl.ds(r, 1)],
                                      sem.at[ahead]).start()
            xb = _unpack_bf16_pairs(cur[...]).astype(MXU_DTYPE)
            hg = jnp.dot(xb, wgu_c[...], preferred_element_type=f32) + bgu_ref[0, 0]
            act = _clamped_swiglu(hg)
            y_ref[...] = jnp.dot(act.astype(MXU_DTYPE), wdn_c[...], preferred_element_type=f32) + bdn_ref[0, 0]

        @pl.when(jnp.logical_and(jnp.logical_and(i >= n_used, i < n_used + MOE_GATHER_AHEAD),
                                 i % MOE_GATHER_BUFS == phase))
        def _():
            _row_gather_wait(h_hbm, MOE_BLOCK, cur, sem.at[phase])

    @pl.when(i >= n_used)
    def _():
        y_ref[...] = jnp.zeros_like(y_ref)


def _moe_experts(h_packed, row_tok, block_e, n_used, layer, w_gu, b_gu, w_dn, b_dn):
    n_tok = h_packed.shape[0]
    d = 2 * h_packed.shape[1]
    n_blocks = block_e.shape[0]
    n_layers = w_gu.shape[0]
    assert (n_tok * TOP_K + N_EXPERTS * (MOE_BLOCK - 1)) // MOE_BLOCK + MOE_GATHER_AHEAD <= n_blocks
    tok3 = row_tok.reshape(n_blocks, 1, MOE_BLOCK)
    smem_blk = lambda imap: pl.BlockSpec((1, 1, MOE_BLOCK), imap, memory_space=pltpu.SMEM)
    by_expert = lambda shape: pl.BlockSpec((1, 1) + shape, lambda i, be, nu: (layer, be[i], 0, 0))
    gather_buf = pltpu.VMEM((MOE_BLOCK, d // 2), jnp.int32)
    return pl.pallas_call(
        _moe_expert_body,
        grid_spec=pltpu.PrefetchScalarGridSpec(
            num_scalar_prefetch=2,
            grid=(n_blocks,),
            in_specs=[smem_blk(lambda i, be, nu: (i, 0, 0)),
                      smem_blk(lambda i, be, nu: (1, 0, 0)),
                      smem_blk(lambda i, be, nu: (jnp.minimum(i + MOE_GATHER_AHEAD, n_blocks - 1), 0, 0)),
                      pl.BlockSpec(memory_space=pl.ANY),
                      by_expert((d, 2 * D_FF)), by_expert((1, 2 * D_FF)),
                      by_expert((D_FF, d)), by_expert((1, d))],
            out_specs=pl.BlockSpec((MOE_BLOCK, d), lambda i, be, nu: (i, 0)),
            scratch_shapes=[gather_buf] * MOE_GATHER_BUFS
            + [pltpu.SemaphoreType.DMA((MOE_GATHER_BUFS,)),
               pltpu.VMEM((d, 2 * D_FF), MXU_DTYPE), pltpu.VMEM((D_FF, d), MXU_DTYPE)]),
        out_shape=jax.ShapeDtypeStruct((n_blocks * MOE_BLOCK, d), jnp.float32),
        compiler_params=_params("arbitrary"),
        name="moe_experts",
    )(block_e, n_used, tok3, tok3, tok3, h_packed, w_gu, b_gu.reshape(n_layers, N_EXPERTS, 1, 2 * D_FF),
      w_dn, b_dn.reshape(n_layers, N_EXPERTS, 1, d))


MOE_COMBINE_ROWS = 128


def _moe_combine_ln_body(dest_ref, dest_next_ref, h_ref, tg_ref, g_ref, b_ref, y_hbm, o_ref,
                         buf_even, buf_odd, sem):
    tm = MOE_COMBINE_ROWS
    i = pl.program_id(0)
    n = pl.num_programs(0)
    bufs = (buf_even, buf_odd)

    def start(rows_smem, dst, s):
        for k in range(TOP_K):
            for r in range(tm):
                pltpu.make_async_copy(y_hbm.at[pl.ds(rows_smem[0, 0, k * tm + r], 1)],
                                      dst.at[k, pl.ds(r, 1)], sem.at[s]).start(priority=r % 2)

    def drain(dst, s):
        for k in range(TOP_K):
            _row_gather_wait(y_hbm, tm, dst.at[k], sem.at[s])

    @pl.when(i == 0)
    def _():
        start(dest_ref, buf_even, 0)

    for parity in (0, 1):
        cur, nxt = bufs[parity], bufs[1 - parity]

        @pl.when(i % 2 == parity)
        def _():
            drain(cur, parity)
            start(dest_next_ref, nxt, 1 - parity)
            tg = tg_ref[...]
            y = tg[:, 0:1] * cur[0]
            for k in range(1, TOP_K):
                y = y + tg[:, k:k + 1] * cur[k]
            o_ref[...] = _ln_rows(DEEPNORM_ALPHA * h_ref[...] + y, g_ref[...], b_ref[...])

        @pl.when(jnp.logical_and(i == n - 1, i % 2 == parity))
        def _():
            drain(nxt, 1 - parity)


def _moe_combine_ln(h2d, y_rows, dest, tg, g, b):
    n_tok, d = h2d.shape
    tm = MOE_COMBINE_ROWS
    nt = n_tok // tm
    dest3 = dest.reshape(nt, tm, TOP_K).transpose(0, 2, 1).reshape(nt, 1, TOP_K * tm)
    smem_blk = lambda imap: pl.BlockSpec((1, 1, TOP_K * tm), imap, memory_space=pltpu.SMEM)
    return pl.pallas_call(
        _moe_combine_ln_body,
        grid=(nt,),
        in_specs=[smem_blk(lambda i: (i, 0, 0)),
                  smem_blk(lambda i: (jnp.minimum(i + 1, nt - 1), 0, 0)),
                  pl.BlockSpec((tm, d), lambda i: (i, 0)),
                  pl.BlockSpec((tm, LANE), lambda i: (i, 0)),
                  pl.BlockSpec((1, d), lambda i: (0, 0)),
                  pl.BlockSpec((1, d), lambda i: (0, 0)),
                  pl.BlockSpec(memory_space=pl.ANY)],
        out_specs=pl.BlockSpec((tm, d), lambda i: (i, 0)),
        out_shape=jax.ShapeDtypeStruct((n_tok, d), jnp.float32),
        scratch_shapes=[pltpu.VMEM((TOP_K, tm, d), jnp.float32), pltpu.VMEM((TOP_K, tm, d), jnp.float32),
                        pltpu.SemaphoreType.DMA((2,))],
        compiler_params=_params("arbitrary"),
        name="moe_combine_ln",
    )(dest3, dest3, h2d, tg, g.reshape(1, d), b.reshape(1, d), y_rows)


def _moe_layer(h2d, h_packed, te, tg, layer, w_gu, b_gu, w_dn, b_dn, g, b):
    dest, row_tok, block_e, n_used = _moe_plan(te)
    y_rows = _moe_experts(h_packed, row_tok, block_e, n_used, layer, w_gu, b_gu, w_dn, b_dn)
    return _moe_combine_ln(h2d, y_rows, dest, tg, g, b)


def _mixers(proj, cos, sin, bsz, seq, lambda_init, a_kv_norm, a_w_uk, a_w_uv, c_lambda, c_subln,
            d_conv, d_a_log, d_dt_bias, d_norm):
    cdt = MXU_DTYPE
    (qT, iqT, iw, ikc, kvc, ckvT, qbd, kc, vT, rq, rkT, gatesT) = _mixer_prep(
        proj, cos, sin, a_kv_norm, a_w_uk.transpose(1, 0, 2).astype(cdt), bsz, seq)
    head_a = _dsa_attention(iqT, iw, qT, ikc, kvc, ckvT, a_w_uv.transpose(1, 0, 2).astype(cdt),
                            n_keep=min(INDEX_TOPK, seq // 4))
    head_b = _retention(rq, rkT, proj, bsz, seq)
    lam_full = (jnp.exp(jnp.sum(c_lambda[0] * c_lambda[1])) - jnp.exp(jnp.sum(c_lambda[2] * c_lambda[3]))
                + lambda_init)
    gain = (c_subln * (1.0 - lambda_init)).reshape(1, 2 * C_DH)
    head_c = _diff_attention(lam_full.reshape(1), qbd, kc, vT, gain)
    head_d = _gated_deltanet(proj, gatesT, d_conv, d_a_log, d_dt_bias, d_norm, bsz, seq)
    return [t.reshape(bsz * seq, t.shape[-1]) for t in (head_a, head_b, head_c, head_d)]


def kernel(x, positions, w_in, w_out, a_kv_norm, a_w_uk, a_w_uv, c_lambda, c_subln, d_conv, d_a_log, d_dt_bias, d_norm, ln1_g, ln1_b, w_router, b_router, w_gu, b_gu, w_dn, b_dn, ln2_g, ln2_b):
    bsz, seq, d = x.shape
    x2d = x.reshape(bsz * seq, d)
    cos, sin = _rope_tables(positions)
    for l in range(DEPTH):
        proj = _in_proj(x2d, _pack_w_in(w_in[l]))
        lambda_init = 0.8 - 0.6 * math.exp(-0.3 * l)
        heads = _mixers(proj, cos, sin, bsz, seq, lambda_init, a_kv_norm[l], a_w_uk[l], a_w_uv[l], c_lambda[l],
                        c_subln[l], d_conv[l], d_a_log[l], d_dt_bias[l], d_norm[l])
        h2d, h_packed, te, tg = _out_ln_router(heads, w_out[l].astype(MXU_DTYPE), x2d, ln1_g[l], ln1_b[l],
                                     w_router[l], b_router[l])
        x2d = _moe_layer(h2d, h_packed, te, tg, l, w_gu, b_gu, w_dn, b_dn, ln2_g[l], ln2_b[l])
    return x2d.reshape(bsz, seq, d)
```
